```python
import math, functools
import jax, jax.numpy as jnp
from jax import lax
import numpy as np

D_MODEL = 1024
BATCH = 2
SEQ = 8192
DEPTH = 1
DEC_BATCH = 32
DEC_SEQ = 4
PAST_LEN = 16384
PAGE_SIZE = 128

D_MIX = D_MODEL
D_A = D_MIX // 2
D_B = D_MIX - D_A
H_A = 4
DV_A = D_A // H_A
DK_A = DV_A // 2
H_B = 4
DH_B = D_B // H_B
CONV_W = 4
CHUNK = 64
Q_BLOCK = 128
PLE_DIM = 256
EPS = 1e-6
D_IN = 4 * D_A + 3 * D_B
SPLITS = [D_A, 2 * D_A, 3 * D_A, 4 * D_A, 4 * D_A + D_B, 4 * D_A + 2 * D_B]

kernel_name = "hymba_diffattn_mlstm_step"


def rmsnorm(x, g):
    xf = x.astype(jnp.float32)
    y = xf * lax.rsqrt(jnp.mean(xf * xf, axis=-1, keepdims=True) + EPS)
    return (y * g.astype(jnp.float32)).astype(x.dtype)


def headwise_rmsnorm(x, g):
    return rmsnorm(x, g.reshape(x.shape[-2:]))


def diff_attend(q, k, v, q_pos, k_pos, lam):
    s = jnp.einsum('bqhmd,bkhmd->bhmqk', q.astype(jnp.float32), k.astype(jnp.float32)) * (DK_A ** -0.5)
    mask = k_pos[None, :] <= q_pos[:, None]
    p = jax.nn.softmax(jnp.where(mask, s, -jnp.inf), axis=-1)
    w = p[:, :, 0] - lam * p[:, :, 1]
    return jnp.einsum('bhqk,bkhd->bqhd', w, v.astype(jnp.float32))


def attend_prompt(q, k, v, lam):
    B, S = q.shape[:2]
    nb = S // Q_BLOCK
    qb = q.reshape(B, nb, Q_BLOCK, H_A, 2, DK_A).swapaxes(0, 1)
    q_pos = jnp.arange(S).reshape(nb, Q_BLOCK)
    k_pos = jnp.arange(S)
    out = lax.map(lambda t: diff_attend(t[0], k, v, t[1], k_pos, lam), (qb, q_pos))
    return out.swapaxes(0, 1).reshape(B, S, H_A, DV_A)


def attend_paged(q, k, v, lam, cache_k, cache_v, page_table):
    DB, T = q.shape[:2]
    P = page_table.shape[1] * PAGE_SIZE
    pk = cache_k[page_table].reshape(DB, P, H_A, 2, DK_A)
    pv = cache_v[page_table].reshape(DB, P, H_A, DV_A)
    k_all = jnp.concatenate([pk, k.astype(pk.dtype)], axis=1)
    v_all = jnp.concatenate([pv, v.astype(pv.dtype)], axis=1)
    q_pos = P + jnp.arange(T)
    k_pos = jnp.arange(P + T)
    return diff_attend(q, k_all, v_all, q_pos, k_pos, lam)


def causal_conv(xb, buf, w, b):
    S = xb.shape[1]
    xpad = jnp.concatenate([buf.astype(xb.dtype), xb], axis=1)
    out = b + sum(xpad[:, j:j + S] * w[j] for j in range(CONV_W))
    return out, xpad[:, -(CONV_W - 1):]


def mlstm_scan(q, k, v, ig, lf, C0, n0, m0):
    B, S, H, DH = q.shape
    L = math.gcd(S, CHUNK)
    nc = S // L
    f32 = jnp.float32

    def chunks4(a):
        return a.astype(f32).reshape(B, nc, L, H, DH).transpose(1, 0, 3, 2, 4)

    def chunks3(a):
        return a.astype(f32).reshape(B, nc, L, H).transpose(1, 0, 3, 2)

    tril = jnp.tril(jnp.ones((L, L), dtype=bool))

    def step(carry, xs):
        C, n, m = carry
        qc, kc, vc, ic, fc = xs
        b = jnp.cumsum(fc, axis=-1)
        Dm = jnp.where(tril, b[..., :, None] - b[..., None, :] + ic[..., None, :], -jnp.inf)
        inter = b + m[..., None]
        m_t = jnp.maximum(inter, jnp.max(Dm, axis=-1))
        wD = jnp.exp(Dm - m_t[..., None])
        w_in = jnp.exp(inter - m_t)
        sqk = jnp.einsum('bhtd,bhsd->bhts', qc, kc) * wD
        num = w_in[..., None] * jnp.einsum('bhtd,bhde->bhte', qc, C) + jnp.einsum('bhts,bhse->bhte', sqk, vc)
        den = w_in * jnp.einsum('bhtd,bhd->bht', qc, n) + jnp.sum(sqk, axis=-1)
        h = num / jnp.maximum(jnp.abs(den), jnp.exp(-m_t))[..., None]
        bL = b[..., -1]
        g = bL[..., None] - b + ic
        m_new = jnp.maximum(bL + m, jnp.max(g, axis=-1))
        decay = jnp.exp(bL + m - m_new)
        wg = jnp.exp(g - m_new[..., None])
        C_new = decay[..., None, None] * C + jnp.einsum('bhs,bhsd,bhse->bhde', wg, kc, vc)
        n_new = decay[..., None] * n + jnp.einsum('bhs,bhsd->bhd', wg, kc)
        return (C_new, n_new, m_new), h

    carry0 = (C0.astype(f32), n0.astype(f32), m0.astype(f32))
    (C1, n1, m1), h = lax.scan(step, carry0, (chunks4(q), chunks4(k), chunks4(v), chunks3(ig), chunks3(lf)))
    h = h.transpose(1, 0, 3, 2, 4).reshape(B, S, H, DH)
    return h, C1, n1, m1


def hybrid_layer(x, p, attend, conv_buf, C0, n0, m0, lam_init, w):
    B, S, _ = x.shape
    h = rmsnorm(x, w['pre_g'])
    u = h @ w['w_in']
    qa, ka, va, za, xb, ob, zb = jnp.split(u, SPLITS, axis=-1)
    q = qa.reshape(B, S, H_A, 2, DK_A)
    k = ka.reshape(B, S, H_A, 2, DK_A)
    v = va.reshape(B, S, H_A, DV_A)
    f32 = jnp.float32
    lam = (jnp.exp(jnp.sum(w['lq1'].astype(f32) * w['lk1'].astype(f32)))
           - jnp.exp(jnp.sum(w['lq2'].astype(f32) * w['lk2'].astype(f32))) + lam_init)
    a = attend(q, k, v, lam)
    a = headwise_rmsnorm(a, w['attn_g']) * (1.0 - lam_init)
    a = a.reshape(B, S, D_A).astype(x.dtype) * jax.nn.silu(za)
    xc, new_buf = causal_conv(xb, conv_buf, w['conv_w'], w['conv_b'])
    xch = jax.nn.silu(xc).reshape(B, S, H_B, DH_B)
    qb = jnp.einsum('bshd,hde->bshe', xch, w['wq_b'])
    kb = jnp.einsum('bshd,hde->bshe', xch, w['wk_b']) * (DH_B ** -0.5)
    vb = jnp.einsum('bshd,hde->bshe', xb.reshape(B, S, H_B, DH_B), w['wv_b'])
    gates = jnp.concatenate([qb, kb, vb], axis=-1).reshape(B, S, 3 * D_B) @ w['w_if'] + w['b_if']
    ig = gates[..., :H_B].astype(f32)
    lf = jax.nn.log_sigmoid(gates[..., H_B:].astype(f32))
    hb, C1, n1, m1 = mlstm_scan(qb, kb, vb, ig, lf, C0, n0, m0)
    hb = hb * jax.nn.sigmoid(ob.reshape(B, S, H_B, DH_B).astype(f32))
    hb = headwise_rmsnorm(hb, w['mlstm_g']) + w['mlstm_skip'].reshape(H_B, DH_B) * xch
    bo = hb.reshape(B, S, D_B).astype(x.dtype) * jax.nn.silu(zb)
    y = jnp.concatenate([a, bo], axis=-1) @ w['w_out']
    x = x + rmsnorm(y, w['post_g'])
    gate = jax.nn.sigmoid(x @ w['ple_gate'])
    x = x + rmsnorm(gate * (p.astype(x.dtype) @ w['ple_w']), w['ple_g'])
    return x, (k.reshape(B, S, H_A, 2 * DK_A), v, new_buf, C1, n1, m1)


def setup_inputs(seed: int = 0) -> dict:
    key = jax.random.key(seed)
    ks = jax.random.split(key, 40)
    f32 = jnp.float32

    def nrm(k, shape, scale):
        return jax.random.normal(k, shape, f32) * scale

    n_pages = PAST_LEN // PAGE_SIZE
    n_used = DEC_BATCH * n_pages
    n_pool = n_used + n_used // 4
    page_table = jax.random.permutation(ks[0], n_pool)[:n_used].reshape(DEC_BATCH, n_pages).astype(jnp.int32)
    f_bias = jnp.broadcast_to(jnp.linspace(3.0, 6.0, H_B), (DEPTH, H_B)) + nrm(ks[1], (DEPTH, H_B), 0.1)
    i_bias = nrm(ks[2], (DEPTH, H_B), 0.1)
    return {
        'x_prompt': nrm(ks[3], (BATCH, SEQ, D_MODEL), 1.0),
        'x_sample': nrm(ks[4], (DEC_BATCH, DEC_SEQ, D_MODEL), 1.0),
        'cache_k': nrm(ks[5], (DEPTH, n_pool, PAGE_SIZE, H_A, 2 * DK_A), 1.0),
        'cache_v': nrm(ks[6], (DEPTH, n_pool, PAGE_SIZE, H_A, DV_A), 1.0),
        'state_conv': nrm(ks[7], (DEPTH, DEC_BATCH, CONV_W - 1, D_B), 1.0),
        'state_C': nrm(ks[8], (DEPTH, DEC_BATCH, H_B, DH_B, DH_B), DH_B ** -0.5),
        'state_n': nrm(ks[9], (DEPTH, DEC_BATCH, H_B, DH_B), 1.0),
        'state_m': nrm(ks[10], (DEPTH, DEC_BATCH, H_B), 1.0),
        'page_table': page_table,
        'p_prompt': nrm(ks[11], (DEPTH, BATCH, SEQ, PLE_DIM), 1.0),
        'p_sample': nrm(ks[12], (DEPTH, DEC_BATCH, DEC_SEQ, PLE_DIM), 1.0),
        'norm_pre_g': 1.0 + nrm(ks[13], (DEPTH, D_MODEL), 0.05),
        'norm_post_g': 1.0 + nrm(ks[14], (DEPTH, D_MODEL), 0.05),
        'w_in': nrm(ks[15], (DEPTH, D_MODEL, D_IN), D_MODEL ** -0.5),
        'lam_q1': nrm(ks[16], (DEPTH, DK_A), 0.1),
        'lam_k1': nrm(ks[17], (DEPTH, DK_A), 0.1),
        'lam_q2': nrm(ks[18], (DEPTH, DK_A), 0.1),
        'lam_k2': nrm(ks[19], (DEPTH, DK_A), 0.1),
        'attn_norm_g': 1.0 + nrm(ks[20], (DEPTH, D_A), 0.05),
        'conv_w': nrm(ks[21], (DEPTH, CONV_W, D_B), CONV_W ** -0.5),
        'conv_b': nrm(ks[22], (DEPTH, D_B), 0.02),
        'w_q_b': nrm(ks[23], (DEPTH, H_B, DH_B, DH_B), DH_B ** -0.5),
        'w_k_b': nrm(ks[24], (DEPTH, H_B, DH_B, DH_B), DH_B ** -0.5),
        'w_v_b': nrm(ks[25], (DEPTH, H_B, DH_B, DH_B), DH_B ** -0.5),
        'w_if': nrm(ks[26], (DEPTH, 3 * D_B, 2 * H_B), 0.3 * (3 * D_B) ** -0.5),
        'b_if': jnp.concatenate([i_bias, f_bias], axis=-1),
        'mlstm_norm_g': 1.0 + nrm(ks[27], (DEPTH, D_B), 0.05),
        'mlstm_skip': 1.0 + nrm(ks[28], (DEPTH, D_B), 0.05),
        'w_out': nrm(ks[29], (DEPTH, D_MIX, D_MODEL), D_MIX ** -0.5),
        'ple_w': nrm(ks[30], (DEPTH, PLE_DIM, D_MODEL), PLE_DIM ** -0.5),
        'ple_gate_w': nrm(ks[31], (DEPTH, D_MODEL, D_MODEL), D_MODEL ** -0.5),
        'ple_norm_g': 1.0 + nrm(ks[32], (DEPTH, D_MODEL), 0.05),
    }


def reference(x_prompt, x_sample, cache_k, cache_v, state_conv, state_C, state_n, state_m, page_table,
              p_prompt, p_sample, norm_pre_g, norm_post_g, w_in, lam_q1, lam_k1, lam_q2, lam_k2,
              attn_norm_g, conv_w, conv_b, w_q_b, w_k_b, w_v_b, w_if, b_if, mlstm_norm_g, mlstm_skip,
              w_out, ple_w, ple_gate_w, ple_norm_g):
    B = x_prompt.shape[0]
    xp, xs = x_prompt, x_sample
    outs_p, outs_s = [], []
    for l in range(DEPTH):
        w = {'pre_g': norm_pre_g[l], 'post_g': norm_post_g[l], 'w_in': w_in[l],
             'lq1': lam_q1[l], 'lk1': lam_k1[l], 'lq2': lam_q2[l], 'lk2': lam_k2[l],
             'attn_g': attn_norm_g[l], 'conv_w': conv_w[l], 'conv_b': conv_b[l],
             'wq_b': w_q_b[l], 'wk_b': w_k_b[l], 'wv_b': w_v_b[l], 'w_if': w_if[l], 'b_if': b_if[l],
             'mlstm_g': mlstm_norm_g[l], 'mlstm_skip': mlstm_skip[l], 'w_out': w_out[l],
             'ple_w': ple_w[l], 'ple_gate': ple_gate_w[l], 'ple_g': ple_norm_g[l]}
        lam_init = 0.8 - 0.6 * math.exp(-0.3 * l)
        conv0 = jnp.zeros((B, CONV_W - 1, D_B), xp.dtype)
        C0 = jnp.zeros((B, H_B, DH_B, DH_B), jnp.float32)
        n0 = jnp.zeros((B, H_B, DH_B), jnp.float32)
        m0 = jnp.zeros((B, H_B), jnp.float32)
        xp, st_p = hybrid_layer(xp, p_prompt[l], attend_prompt, conv0, C0, n0, m0, lam_init, w)
        attend_sample = functools.partial(attend_paged, cache_k=cache_k[l], cache_v=cache_v[l], page_table=page_table)
        xs, st_s = hybrid_layer(xs, p_sample[l], attend_sample, state_conv[l], state_C[l], state_n[l], state_m[l], lam_init, w)
        outs_p.append(st_p)
        outs_s.append(st_s)
    kp, vp, cp, Cp, np_, mp = [jnp.stack([o[i] for o in outs_p]) for i in range(6)]
    ks_, vs_, cs_, Cs_, ns_, ms_ = [jnp.stack([o[i] for o in outs_s]) for i in range(6)]
    return (xp, xs, kp, vp, cp, Cp, np_, mp, ks_, vs_, cs_, Cs_, ns_, ms_)
```

```python
import functools
import math

import jax
import jax.numpy as jnp
from jax import lax
from jax.experimental import pallas as pl
from jax.experimental.pallas import tpu as pltpu

F32 = jnp.float32
BF16 = jnp.bfloat16
EPS = 1e-6
NEG_INF = float("-inf")

N_HEADS = 4
HEAD_DIM = 128
QK_DIM = 64
GROUP_W = N_HEADS * HEAD_DIM
CONV_W = 4
SAMPLE_ROWS = 8
PROMPT_CHUNK = 128
PAGES_PER_STEP = 8

V7X_VMEM_LIMIT = 48 * 1024 * 1024


def _cparams(sem):
    return pltpu.CompilerParams(dimension_semantics=sem, vmem_limit_bytes=V7X_VMEM_LIMIT)


def _silu(x):
    return x * jax.nn.sigmoid(x)


def _rms(x):
    return x * lax.rsqrt(jnp.mean(x * x, axis=-1, keepdims=True) + EPS)


def _inproj_kernel(x_ref, g_ref, w_ref, q_ref, k_ref, v_ref, kb_ref, vb_ref, rest_ref):
    hb = (_rms(x_ref[...]) * g_ref[...]).astype(BF16)

    def proj(c):
        return jnp.dot(hb, w_ref[:, c * GROUP_W:(c + 1) * GROUP_W], preferred_element_type=F32)

    q_ref[...] = (proj(0) * (QK_DIM ** -0.5)).astype(BF16)
    u = proj(1)
    k_ref[...] = u
    kb_ref[...] = u.astype(BF16)
    u = proj(2)
    v_ref[...] = u
    vb_ref[...] = u.astype(BF16)
    for c in range(3, 7):
        rest_ref[:, (c - 3) * GROUP_W:(c - 2) * GROUP_W] = proj(c)


def _inproj(x, g, w_bf, tm):
    m, d = x.shape
    n_in = w_bf.shape[1]
    row = lambda w: pl.BlockSpec((tm, w), lambda i: (i, 0))
    return pl.pallas_call(
        _inproj_kernel,
        grid=(m // tm,),
        in_specs=[row(d), pl.BlockSpec((1, d), lambda i: (0, 0)),
                  pl.BlockSpec((d, n_in), lambda i: (0, 0))],
        out_specs=[row(GROUP_W)] * 5 + [row(4 * GROUP_W)],
        out_shape=[jax.ShapeDtypeStruct((m, GROUP_W), BF16),
                   jax.ShapeDtypeStruct((m, GROUP_W), F32),
                   jax.ShapeDtypeStruct((m, GROUP_W), F32),
                   jax.ShapeDtypeStruct((m, GROUP_W), BF16),
                   jax.ShapeDtypeStruct((m, GROUP_W), BF16),
                   jax.ShapeDtypeStruct((m, 4 * GROUP_W), F32)],
        compiler_params=_cparams(("parallel",)),
        name="inproj",
    )(x, g, w_bf)


def _split_maps(q):
    q = q.astype(F32)
    lane = lax.broadcasted_iota(jnp.int32, q.shape, 1)
    zero = jnp.zeros_like(q)
    both = jnp.concatenate([jnp.where(lane < QK_DIM, q, zero), jnp.where(lane >= QK_DIM, q, zero)], axis=0)
    return both.astype(BF16)


def _lambda(lam_ref, lam_init):
    l1 = jnp.sum(lam_ref[0:1, :] * lam_ref[1:2, :], axis=-1, keepdims=True)
    l2 = jnp.sum(lam_ref[2:3, :] * lam_ref[3:4, :], axis=-1, keepdims=True)
    return jnp.exp(l1) - jnp.exp(l2) + lam_init


def _attn_finish(acc, l, r, lam, g, za, lam_init):
    o = acc[0:r] / l[0:r] - lam * (acc[r:2 * r] / l[r:2 * r])
    o = _rms(o) * g * (1.0 - lam_init)
    return o * _silu(za)


def _flash_kernel(q_ref, k_ref, v_ref, za_ref, lam_ref, g_ref, o_ref, qq_scr, m_scr, l_scr, acc_scr,
                  *, tq, lam_init):
    i = pl.program_id(2)
    qq_scr[...] = _split_maps(q_ref[...])
    m_scr[...] = jnp.full(m_scr.shape, NEG_INF, F32)
    l_scr[...] = jnp.zeros(l_scr.shape, F32)
    acc_scr[...] = jnp.zeros(acc_scr.shape, F32)

    def step(j, masked):
        k0 = pl.multiple_of(j * tq, tq)
        k = k_ref[pl.ds(k0, tq), :]
        v = v_ref[pl.ds(k0, tq), :]
        s = lax.dot_general(qq_scr[...], k, (((1,), (1,)), ((), ())), preferred_element_type=F32)
        if masked:
            r = lax.broadcasted_iota(jnp.int32, s.shape, 0)
            c = lax.broadcasted_iota(jnp.int32, s.shape, 1)
            s = jnp.where(c <= jnp.where(r >= tq, r - tq, r), s, NEG_INF)
        m_prev = m_scr[...]
        m_new = jnp.maximum(m_prev, jnp.max(s, axis=-1, keepdims=True))
        p = jnp.exp(s - m_new)
        alpha = jnp.exp(m_prev - m_new)
        l_scr[...] = alpha * l_scr[...] + jnp.sum(p, axis=-1, keepdims=True)
        acc_scr[...] = alpha * acc_scr[...] + jnp.dot(p.astype(BF16), v, preferred_element_type=F32)
        m_scr[...] = m_new

    def body(j, carry):
        step(j, False)
        return carry

    lax.fori_loop(0, i, body, 0)
    step(i, True)
    lam = _lambda(lam_ref, lam_init)
    o_ref[...] = _attn_finish(acc_scr[...], l_scr[...], tq, lam, g_ref[...], za_ref[...], lam_init).astype(BF16)


def _flash(q, kb, vb, rest, lamp, attn_g, lam_init, tq):
    b, s, _ = q.shape
    blk = lambda: pl.BlockSpec((None, tq, HEAD_DIM), lambda bi, h, i: (bi, i, h))
    res = lambda: pl.BlockSpec((None, s, HEAD_DIM), lambda bi, h, i: (bi, 0, h))
    return pl.pallas_call(
        functools.partial(_flash_kernel, tq=tq, lam_init=lam_init),
        grid=(b, N_HEADS, s // tq),
        in_specs=[blk(), res(), res(), blk(),
                  pl.BlockSpec((4, QK_DIM), lambda bi, h, i: (0, 0)),
                  pl.BlockSpec((1, HEAD_DIM), lambda bi, h, i: (0, h))],
        out_specs=blk(),
        out_shape=jax.ShapeDtypeStruct((b, s, GROUP_W), BF16),
        scratch_shapes=[pltpu.VMEM((2 * tq, HEAD_DIM), BF16),
                        pltpu.VMEM((2 * tq, 1), F32),
                        pltpu.VMEM((2 * tq, 1), F32),
                        pltpu.VMEM((2 * tq, HEAD_DIM), F32)],
        compiler_params=_cparams(("parallel", "parallel", "arbitrary")),
        name="flash_prompt",
    )(q, kb, vb, rest, lamp, attn_g)


def _paged_kernel(pt_ref, q_ref, *refs, n_pages, n_valid, lam_init):
    del pt_ref
    kp = refs[:n_pages]
    vp = refs[n_pages:2 * n_pages]
    kn_ref, vn_ref, za_ref, lam_ref, g_ref, o_ref, qq_scr, m_scr, l_scr, acc_scr = refs[2 * n_pages:]
    r = SAMPLE_ROWS
    step = pl.program_id(1)

    @pl.when(step == 0)
    def _init():
        for h in range(N_HEADS):
            qq_scr[h] = _split_maps(q_ref[:, h * HEAD_DIM:(h + 1) * HEAD_DIM])
        m_scr[...] = jnp.full(m_scr.shape, NEG_INF, F32)
        l_scr[...] = jnp.zeros(l_scr.shape, F32)
        acc_scr[...] = jnp.zeros(acc_scr.shape, F32)

    for h in range(N_HEADS):
        cols = slice(h * HEAD_DIM, (h + 1) * HEAD_DIM)
        qq = qq_scr[h]
        s = jnp.concatenate(
            [lax.dot_general(qq, kp[j][:, cols].astype(BF16), (((1,), (1,)), ((), ())),
                             preferred_element_type=F32) for j in range(n_pages)], axis=1)
        m_prev = m_scr[h]
        m_new = jnp.maximum(m_prev, jnp.max(s, axis=-1, keepdims=True))
        p = jnp.exp(s - m_new)
        alpha = jnp.exp(m_prev - m_new)
        l_scr[h] = alpha * l_scr[h] + jnp.sum(p, axis=-1, keepdims=True)
        p = p.astype(BF16)
        pv = jnp.dot(p[:, 0:128], vp[0][:, cols].astype(BF16), preferred_element_type=F32)
        for j in range(1, n_pages):
            pv = pv + jnp.dot(p[:, j * 128:(j + 1) * 128], vp[j][:, cols].astype(BF16),
                              preferred_element_type=F32)
        acc_scr[h] = alpha * acc_scr[h] + pv
        m_scr[h] = m_new

    @pl.when(step == pl.num_programs(1) - 1)
    def _finish():
        lam = _lambda(lam_ref, lam_init)
        row = lax.broadcasted_iota(jnp.int32, (2 * r, 1), 0)
        tok = jnp.where(row >= r, row - r, row)
        for h in range(N_HEADS):
            cols = slice(h * HEAD_DIM, (h + 1) * HEAD_DIM)
            qf = qq_scr[h].astype(F32)
            kn = kn_ref[:, cols]
            vn = vn_ref[:, cols]
            s_new = [jnp.where(tok >= j, jnp.sum(qf * kn[j:j + 1, :], axis=-1, keepdims=True), NEG_INF)
                     for j in range(n_valid)]
            m_prev = m_scr[h]
            m_new = m_prev
            for sj in s_new:
                m_new = jnp.maximum(m_new, sj)
            alpha = jnp.exp(m_prev - m_new)
            l = alpha * l_scr[h]
            acc = alpha * acc_scr[h]
            for j, sj in enumerate(s_new):
                pj = jnp.exp(sj - m_new)
                l = l + pj
                acc = acc + pj * vn[j:j + 1, :]
            o_ref[:, cols] = _attn_finish(acc, l, r, lam, g_ref[:, cols], za_ref[:, cols], lam_init).astype(BF16)


def _paged(q, k_new, v_new, rest, cache_k, cache_v, page_table, lamp, attn_g, lam_init, n_valid):
    db = q.shape[0]
    r = SAMPLE_ROWS
    n_pages_total = page_table.shape[1]
    page = cache_k.shape[1]
    p = PAGES_PER_STEP
    seq = lambda w=GROUP_W: pl.BlockSpec((None, r, w), lambda b, s, pt: (b, 0, 0))

    def page_spec(j):
        return pl.BlockSpec((None, page, GROUP_W), lambda b, s, pt, j=j: (pt[b, s * p + j], 0, 0))

    grid_spec = pltpu.PrefetchScalarGridSpec(
        num_scalar_prefetch=1,
        grid=(db, n_pages_total // p),
        in_specs=[seq()] + [page_spec(j) for j in range(p)] * 2 + [seq(), seq(), seq(),
                  pl.BlockSpec((4, QK_DIM), lambda b, s, pt: (0, 0)),
                  pl.BlockSpec((1, GROUP_W), lambda b, s, pt: (0, 0))],
        out_specs=seq(),
        scratch_shapes=[pltpu.VMEM((N_HEADS, 2 * r, HEAD_DIM), BF16),
                        pltpu.VMEM((N_HEADS, 2 * r, 1), F32),
                        pltpu.VMEM((N_HEADS, 2 * r, 1), F32),
                        pltpu.VMEM((N_HEADS, 2 * r, HEAD_DIM), F32)])
    return pl.pallas_call(
        functools.partial(_paged_kernel, n_pages=p, n_valid=n_valid, lam_init=lam_init),
        grid_spec=grid_spec,
        out_shape=jax.ShapeDtypeStruct((db, r, GROUP_W), BF16),
        compiler_params=_cparams(("parallel", "arbitrary")),
        name="paged_attn",
    )(page_table, q, *([cache_k] * p), *([cache_v] * p), k_new, v_new, rest, lamp, attn_g)


def _log_sigmoid(x):
    return jnp.minimum(x, 0.0) - jnp.log1p(jnp.exp(-jnp.abs(x)))


def _mlstm_kernel(xb_ref, ob_ref, zb_ref, conv0_ref, c0_ref, n0_ref, m0_ref,
                  cw_ref, cb_ref, wq_ref, wk_ref, wv_ref, wif_ref, wift_ref, bifr_ref, bifc_ref,
                  mg_ref, skip_ref,
                  bo_ref, c1_ref, n1_ref, m1_ref,
                  xpad, xch_scr, q_scr, k_scr, v_scr, h_scr, gc_scr, gr_scr, c_scr, n_scr, m_scr,
                  *, t, chunk, n_valid):
    step = pl.program_id(1)
    n_chunks = t // chunk
    hd = HEAD_DIM

    @pl.when(step == 0)
    def _init():
        xpad[0:8, :] = jnp.zeros((8, GROUP_W), F32)
        xpad[8 - (CONV_W - 1):8, :] = conv0_ref[...]
        c_scr[...] = c0_ref[...]
        n_scr[...] = n0_ref[...]
        m_scr[...] = m0_ref[...]

    x = xb_ref[...]
    xpad[8:8 + t, :] = x
    xc = cb_ref[...] + cw_ref[CONV_W - 1:CONV_W, :] * x
    for j in range(CONV_W - 1):
        xc = xc + cw_ref[j:j + 1, :] * xpad[8 - (CONV_W - 1) + j:8 - (CONV_W - 1) + j + t, :]
    tail = xpad[t:t + 8, :]
    xpad[0:8, :] = tail
    xch = _silu(xc)
    xch_scr[...] = xch

    gcol = jnp.zeros((t, 2 * N_HEADS), F32) + bifr_ref[...]
    grow = jnp.zeros((2 * N_HEADS, t), F32) + bifc_ref[...]
    for h in range(N_HEADS):
        cols = slice(h * hd, (h + 1) * hd)
        xh = xch[:, cols].astype(BF16)
        qh = jnp.dot(xh, wq_ref[h], preferred_element_type=F32)
        kh = jnp.dot(xh, wk_ref[h], preferred_element_type=F32) * (hd ** -0.5)
        vh = jnp.dot(x[:, cols].astype(BF16), wv_ref[h], preferred_element_type=F32)
        q_scr[:, cols] = qh
        k_scr[:, cols] = kh
        v_scr[:, cols] = vh
        for part, val in enumerate((qh, kh, vh)):
            off = (3 * h + part) * hd
            vb = val.astype(BF16)
            gcol = gcol + jnp.dot(vb, wif_ref[off:off + hd, :], preferred_element_type=F32)
            grow = grow + lax.dot_general(wift_ref[:, off:off + hd], vb, (((1,), (1,)), ((), ())),
                                          preferred_element_type=F32)

    lane = lax.broadcasted_iota(jnp.int32, gcol.shape, 1)
    gcol = jnp.where(lane < N_HEADS, gcol, _log_sigmoid(gcol))
    sub = lax.broadcasted_iota(jnp.int32, grow.shape, 0)
    grow = jnp.where(sub < N_HEADS, grow, _log_sigmoid(grow))
    if n_valid < chunk:
        pos_c = lax.broadcasted_iota(jnp.int32, gcol.shape, 0) % chunk
        gcol = jnp.where(pos_c >= n_valid, jnp.where(lane < N_HEADS, NEG_INF, 0.0), gcol)
        pos_r = lax.broadcasted_iota(jnp.int32, grow.shape, 1) % chunk
        grow = jnp.where(pos_r >= n_valid, jnp.where(sub < N_HEADS, NEG_INF, 0.0), grow)
    gc_scr[...] = gcol
    for c in range(n_chunks):
        gr_scr[c] = grow[:, c * chunk:(c + 1) * chunk]

    ri = lax.broadcasted_iota(jnp.int32, (chunk, chunk), 0)
    ci = lax.broadcasted_iota(jnp.int32, (chunk, chunk), 1)
    causal = ci <= ri
    tril = jnp.where(causal, 1.0, 0.0).astype(F32)
    triu = jnp.where(ri <= ci, 1.0, 0.0).astype(F32)
    lane_c = lax.broadcasted_iota(jnp.int32, (chunk, 2 * N_HEADS), 1)
    sub_r = lax.broadcasted_iota(jnp.int32, (2 * N_HEADS, chunk), 0)

    def scan_chunk(c, carry):
        r0 = pl.multiple_of(c * chunk, chunk)
        rows = pl.ds(r0, chunk)
        gc = gc_scr[rows, :]
        gr = gr_scr[c]
        b_c = jnp.dot(tril, jnp.where(lane_c >= N_HEADS, gc, 0.0), preferred_element_type=F32,
                      precision=lax.Precision.HIGHEST)
        b_r = jnp.dot(jnp.where(sub_r >= N_HEADS, gr, 0.0), triu, preferred_element_type=F32,
                      precision=lax.Precision.HIGHEST)
        for h in range(N_HEADS):
            cols = slice(h * hd, (h + 1) * hd)
            bc = b_c[:, N_HEADS + h:N_HEADS + h + 1]
            br = b_r[N_HEADS + h:N_HEADS + h + 1, :]
            ic = gc[:, h:h + 1]
            ir = gr[h:h + 1, :]
            m = m_scr[h]
            dm = jnp.where(causal, bc - br + ir, NEG_INF)
            inter = bc + m
            m_t = jnp.maximum(inter, jnp.max(dm, axis=-1, keepdims=True))
            w_d = jnp.exp(dm - m_t)
            w_in = jnp.exp(inter - m_t)
            q = q_scr[rows, cols]
            k = k_scr[rows, cols]
            v = v_scr[rows, cols]
            qb = q.astype(BF16)
            kb = k.astype(BF16)
            sqk = lax.dot_general(qb, kb, (((1,), (1,)), ((), ())), preferred_element_type=F32) * w_d
            cmat = c_scr[h]
            nvec = n_scr[h]
            num = (w_in * jnp.dot(qb, cmat.astype(BF16), preferred_element_type=F32)
                   + jnp.dot(sqk.astype(BF16), v.astype(BF16), preferred_element_type=F32))
            den = w_in * jnp.sum(q * nvec, axis=-1, keepdims=True) + jnp.sum(sqk, axis=-1, keepdims=True)
            h_scr[rows, cols] = num / jnp.maximum(jnp.abs(den), jnp.exp(-m_t))
            b_l = bc[chunk - 1:chunk, :]
            g_r = b_l - br + ir
            m_new = jnp.maximum(b_l + m, jnp.max(g_r, axis=-1, keepdims=True))
            decay = jnp.exp(b_l + m - m_new)
            w_g = jnp.exp(b_l - bc + ic - m_new)
            c_scr[h] = decay * cmat + lax.dot_general(kb, (w_g * v).astype(BF16), (((0,), (0,)), ((), ())),
                                                      preferred_element_type=F32)
            n_scr[h] = decay * nvec + jnp.sum(w_g * k, axis=0, keepdims=True)
            m_scr[h] = m_new
        return carry

    if n_chunks == 1:
        scan_chunk(0, 0)
    else:
        lax.fori_loop(0, n_chunks, scan_chunk, 0)

    for h in range(N_HEADS):
        cols = slice(h * hd, (h + 1) * hd)
        hb = h_scr[:, cols] * jax.nn.sigmoid(ob_ref[:, cols])
        hb = _rms(hb) * mg_ref[:, cols] + skip_ref[:, cols] * xch_scr[:, cols]
        bo_ref[:, cols] = (hb * _silu(zb_ref[:, cols])).astype(BF16)

    @pl.when(step == pl.num_programs(1) - 1)
    def _state_out():
        c1_ref[...] = c_scr[...]
        n1_ref[...] = n_scr[...]
        m1_ref[...] = m_scr[...]


def _mlstm(rest, conv0, c0, n0, m0, w, t, chunk, n_valid):
    bn, s, _ = rest.shape
    hd = HEAD_DIM
    tok = lambda col: pl.BlockSpec((None, t, GROUP_W), lambda b, j, col=col: (b, j, col))
    per_b = lambda *shape: pl.BlockSpec((None,) + shape, lambda b, j: (b,) + (0,) * len(shape))
    full = lambda *shape: pl.BlockSpec(shape, lambda b, j: (0,) * len(shape))
    state_specs = [per_b(N_HEADS, hd, hd), per_b(N_HEADS, 1, hd), per_b(N_HEADS, 1, 1)]
    return pl.pallas_call(
        functools.partial(_mlstm_kernel, t=t, chunk=chunk, n_valid=n_valid),
        grid=(bn, s // t),
        in_specs=[tok(1), tok(2), tok(3), per_b(CONV_W - 1, GROUP_W)] + state_specs + [
            full(CONV_W, GROUP_W), full(1, GROUP_W),
            full(N_HEADS, hd, hd), full(N_HEADS, hd, hd), full(N_HEADS, hd, hd),
            full(3 * GROUP_W, 2 * N_HEADS), full(2 * N_HEADS, 3 * GROUP_W),
            full(1, 2 * N_HEADS), full(2 * N_HEADS, 1),
            full(1, GROUP_W), full(1, GROUP_W)],
        out_specs=[pl.BlockSpec((None, t, GROUP_W), lambda b, j: (b, j, 0))] + state_specs,
        out_shape=[jax.ShapeDtypeStruct((bn, s, GROUP_W), BF16),
                   jax.ShapeDtypeStruct((bn, N_HEADS, hd, hd), F32),
                   jax.ShapeDtypeStruct((bn, N_HEADS, 1, hd), F32),
                   jax.ShapeDtypeStruct((bn, N_HEADS, 1, 1), F32)],
        scratch_shapes=[pltpu.VMEM((t + 8, GROUP_W), F32)] + [pltpu.VMEM((t, GROUP_W), F32)] * 5 + [
            pltpu.VMEM((t, 2 * N_HEADS), F32),
            pltpu.VMEM((t // chunk, 2 * N_HEADS, chunk), F32),
            pltpu.VMEM((N_HEADS, hd, hd), F32),
            pltpu.VMEM((N_HEADS, 1, hd), F32),
            pltpu.VMEM((N_HEADS, 1, 1), F32)],
        compiler_params=_cparams(("parallel", "arbitrary")),
        name="mlstm",
    )(rest, rest, rest, conv0, c0, n0, m0,
      w["conv_w"], w["conv_b"], w["wq"], w["wk"], w["wv"], w["wif"], w["wif_t"], w["bif_row"], w["bif_col"],
      w["mlstm_g"], w["mlstm_skip"])


def _outproj_kernel(a_ref, bo_ref, x_ref, p_ref, woa_ref, wob_ref, pg_ref, wgate_ref, wple_ref, plg_ref, y_ref):
    y = (jnp.dot(a_ref[...], woa_ref[...], preferred_element_type=F32)
         + jnp.dot(bo_ref[...], wob_ref[...], preferred_element_type=F32))
    x1 = x_ref[...] + _rms(y) * pg_ref[...]
    gate = jax.nn.sigmoid(jnp.dot(x1.astype(BF16), wgate_ref[...], preferred_element_type=F32))
    pe = jnp.dot(p_ref[...].astype(BF16), wple_ref[...], preferred_element_type=F32)
    y_ref[...] = x1 + _rms(gate * pe) * plg_ref[...]


def _outproj(a, bo, x, p, w, tm):
    m, d = x.shape
    row = lambda width: pl.BlockSpec((tm, width), lambda i: (i, 0))
    full = lambda arr: pl.BlockSpec(arr.shape, lambda i: (0,) * arr.ndim)
    consts = [w["wo_a"], w["wo_b"], w["post_g"], w["ple_gate"], w["ple_w"], w["ple_g"]]
    return pl.pallas_call(
        _outproj_kernel,
        grid=(m // tm,),
        in_specs=[row(GROUP_W), row(GROUP_W), row(d), row(p.shape[1])] + [full(c) for c in consts],
        out_specs=row(d),
        out_shape=jax.ShapeDtypeStruct((m, d), F32),
        compiler_params=_cparams(("parallel",)),
        name="outproj",
    )(a, bo, x, p, *consts)


def _layer_weights(l, norm_pre_g, norm_post_g, w_in, lam_q1, lam_k1, lam_q2, lam_k2, attn_norm_g, conv_w, conv_b,
                   w_q_b, w_k_b, w_v_b, w_if, b_if, mlstm_norm_g, mlstm_skip, w_out, ple_w, ple_gate_w, ple_norm_g):
    row = lambda v: v.reshape(1, -1)
    wo = w_out[l].astype(BF16)
    return {
        "pre_g": row(norm_pre_g[l]), "w_in": w_in[l].astype(BF16),
        "lam": jnp.stack([lam_q1[l], lam_k1[l], lam_q2[l], lam_k2[l]]).astype(F32),
        "attn_g": row(attn_norm_g[l]),
        "conv_w": conv_w[l], "conv_b": row(conv_b[l]),
        "wq": w_q_b[l].astype(BF16), "wk": w_k_b[l].astype(BF16), "wv": w_v_b[l].astype(BF16),
        "wif": w_if[l].astype(BF16), "wif_t": w_if[l].T.astype(BF16),
        "bif_row": row(b_if[l]), "bif_col": b_if[l].reshape(-1, 1),
        "mlstm_g": row(mlstm_norm_g[l]), "mlstm_skip": row(mlstm_skip[l]),
        "wo_a": wo[:GROUP_W], "wo_b": wo[GROUP_W:], "post_g": row(norm_post_g[l]),
        "ple_gate": ple_gate_w[l].astype(BF16), "ple_w": ple_w[l].astype(BF16), "ple_g": row(ple_norm_g[l]),
    }


def _prompt_layer(x, p, w, lam_init):
    b, s, d = x.shape
    x2 = x.reshape(b * s, d)
    q, k, v, kb, vb, rest = _inproj(x2, w["pre_g"], w["w_in"], tm=512)
    rest3 = rest.reshape(b, s, 4 * GROUP_W)
    a = _flash(q.reshape(b, s, GROUP_W), kb.reshape(b, s, GROUP_W), vb.reshape(b, s, GROUP_W), rest3,
               w["lam"], w["attn_g"], lam_init, tq=256)
    hd = HEAD_DIM
    bo, c1, n1, m1 = _mlstm(rest3, jnp.zeros((b, CONV_W - 1, GROUP_W), F32),
                            jnp.zeros((b, N_HEADS, hd, hd), F32), jnp.zeros((b, N_HEADS, 1, hd), F32),
                            jnp.zeros((b, N_HEADS, 1, 1), F32), w, t=512, chunk=PROMPT_CHUNK, n_valid=PROMPT_CHUNK)
    y = _outproj(a.reshape(b * s, GROUP_W), bo.reshape(b * s, GROUP_W), x2, p.reshape(b * s, -1), w, tm=512)
    state = (k.reshape(b, s, N_HEADS, hd), v.reshape(b, s, N_HEADS, hd),
             rest3[:, s - (CONV_W - 1):, GROUP_W:2 * GROUP_W],
             c1, n1.reshape(b, N_HEADS, hd), m1.reshape(b, N_HEADS))
    return y.reshape(b, s, d), state


def _sample_layer(x, p, n_valid, cache_k, cache_v, page_table, conv0, c0, n0, m0, w, lam_init):
    db, r, d = x.shape
    hd = HEAD_DIM
    assert CONV_W - 1 <= n_valid <= r
    x2 = x.reshape(db * r, d)
    q, k, v, _, _, rest = _inproj(x2, w["pre_g"], w["w_in"], tm=db * r)
    rest3 = rest.reshape(db, r, 4 * GROUP_W)
    k3 = k.reshape(db, r, GROUP_W)
    v3 = v.reshape(db, r, GROUP_W)
    n_pool, page = cache_k.shape[0], cache_k.shape[1]
    a = _paged(q.reshape(db, r, GROUP_W), k3, v3, rest3, cache_k.reshape(n_pool, page, GROUP_W),
               cache_v.reshape(n_pool, page, GROUP_W), page_table, w["lam"], w["attn_g"], lam_init, n_valid)
    bo, c1, n1, m1 = _mlstm(rest3, conv0, c0, n0.reshape(db, N_HEADS, 1, hd), m0.reshape(db, N_HEADS, 1, 1),
                            w, t=r, chunk=r, n_valid=n_valid)
    y = _outproj(a.reshape(db * r, GROUP_W), bo.reshape(db * r, GROUP_W), x2, p.reshape(db * r, -1), w, tm=db * r)
    state = (k3[:, :n_valid].reshape(db, n_valid, N_HEADS, hd), v3[:, :n_valid].reshape(db, n_valid, N_HEADS, hd),
             rest3[:, n_valid - (CONV_W - 1):n_valid, GROUP_W:2 * GROUP_W],
             c1, n1.reshape(db, N_HEADS, hd), m1.reshape(db, N_HEADS))
    return y.reshape(db, r, d), state


def kernel(x_prompt, x_sample, cache_k, cache_v, state_conv, state_C, state_n, state_m, page_table,
           p_prompt, p_sample, norm_pre_g, norm_post_g, w_in, lam_q1, lam_k1, lam_q2, lam_k2,
           attn_norm_g, conv_w, conv_b, w_q_b, w_k_b, w_v_b, w_if, b_if, mlstm_norm_g, mlstm_skip,
           w_out, ple_w, ple_gate_w, ple_norm_g):
    depth = w_in.shape[0]
    t_dec = x_sample.shape[1]
    pad_rows = lambda a: jnp.pad(a, ((0, 0), (0, SAMPLE_ROWS - t_dec), (0, 0)))
    xp = x_prompt
    xs = pad_rows(x_sample)
    outs_p, outs_s = [], []
    for l in range(depth):
        w = _layer_weights(l, norm_pre_g, norm_post_g, w_in, lam_q1, lam_k1, lam_q2, lam_k2, attn_norm_g,
                           conv_w, conv_b, w_q_b, w_k_b, w_v_b, w_if, b_if, mlstm_norm_g, mlstm_skip,
                           w_out, ple_w, ple_gate_w, ple_norm_g)
        lam_init = 0.8 - 0.6 * math.exp(-0.3 * l)
        xp, st_p = _prompt_layer(xp, p_prompt[l], w, lam_init)
        outs_p.append(st_p)
        xs, st_s = _sample_layer(xs, pad_rows(p_sample[l]), t_dec, cache_k[l], cache_v[l], page_table,
                                 state_conv[l], state_C[l], state_n[l], state_m[l], w, lam_init)
        outs_s.append(st_s)
    stack = lambda outs, i: jnp.stack([o[i] for o in outs])
    return ((xp, xs[:, :t_dec]) + tuple(stack(outs_p, i) for i in range(6))
            + tuple(stack(outs_s, i) for i in range(6)))
```

```python
import functools
import math

import jax
import jax.numpy as jnp
from jax import lax
from jax.experimental import pallas as pl
from jax.experimental.pallas import tpu as pltpu

F32 = jnp.float32
BF16 = jnp.bfloat16
EPS = 1e-6
NEG_INF = float("-inf")

N_HEADS = 4
HEAD_DIM = 128
QK_DIM = 64
GROUP_W = N_HEADS * HEAD_DIM
CONV_W = 4
SAMPLE_ROWS = 8
PROMPT_CHUNK = 128
PROMPT_ROWS = 512
FLASH_BLOCK = 256
PAGES_PER_STEP = 8

V7X_VMEM_LIMIT = 48 * 1024 * 1024


def _cparams(sem):
    return pltpu.CompilerParams(dimension_semantics=sem, vmem_limit_bytes=V7X_VMEM_LIMIT)


def _silu(x):
    return x * jax.nn.sigmoid(x)


def _rms(x):
    return x * lax.rsqrt(jnp.mean(x * x, axis=-1, keepdims=True) + EPS)


def _inproj_kernel(x_ref, g_ref, w_ref, q_ref, k_ref, v_ref, rest_ref, *maybe_flash_refs, kv_block):
    hb = (_rms(x_ref[...]) * g_ref[...]).astype(BF16)

    def proj(c):
        return jnp.dot(hb, w_ref[:, c * GROUP_W:(c + 1) * GROUP_W], preferred_element_type=F32)

    q_ref[...] = (proj(0) * (QK_DIM ** -0.5)).astype(BF16)
    k = proj(1)
    k_ref[...] = k
    v = proj(2)
    v_ref[...] = v
    if kv_block:
        kb_ref, vt_ref = maybe_flash_refs
        kb_ref[...] = k.astype(BF16)
        for h in range(N_HEADS):
            for c in range(v.shape[0] // kv_block):
                blk = v[c * kv_block:(c + 1) * kv_block, h * HEAD_DIM:(h + 1) * HEAD_DIM]
                vt_ref[h, c] = blk.T.astype(BF16)
    for c in range(3, 7):
        rest_ref[:, (c - 3) * GROUP_W:(c - 2) * GROUP_W] = proj(c)


def _inproj(x, g, w_bf, tm, kv_block=0):
    m, d = x.shape
    n_in = w_bf.shape[1]
    row = lambda w: pl.BlockSpec((tm, w), lambda i: (i, 0))
    out_specs = [row(GROUP_W)] * 3 + [row(4 * GROUP_W)]
    out_shape = [jax.ShapeDtypeStruct((m, GROUP_W), BF16),
                 jax.ShapeDtypeStruct((m, GROUP_W), F32),
                 jax.ShapeDtypeStruct((m, GROUP_W), F32),
                 jax.ShapeDtypeStruct((m, 4 * GROUP_W), F32)]
    if kv_block:
        out_specs += [row(GROUP_W),
                      pl.BlockSpec((N_HEADS, tm // kv_block, HEAD_DIM, kv_block), lambda i: (0, i, 0, 0))]
        out_shape += [jax.ShapeDtypeStruct((m, GROUP_W), BF16),
                      jax.ShapeDtypeStruct((N_HEADS, m // kv_block, HEAD_DIM, kv_block), BF16)]
    return pl.pallas_call(
        functools.partial(_inproj_kernel, kv_block=kv_block),
        grid=(m // tm,),
        in_specs=[row(d), pl.BlockSpec((1, d), lambda i: (0, 0)),
                  pl.BlockSpec((d, n_in), lambda i: (0, 0))],
        out_specs=out_specs,
        out_shape=out_shape,
        compiler_params=_cparams(("parallel",)),
        name="inproj",
    )(x, g, w_bf)


def _split_maps(q):
    q = q.astype(F32)
    lane = lax.broadcasted_iota(jnp.int32, q.shape, 1)
    zero = jnp.zeros_like(q)
    both = jnp.concatenate([jnp.where(lane < QK_DIM, q, zero), jnp.where(lane >= QK_DIM, q, zero)], axis=0)
    return both.astype(BF16)


def _lambda(lam_ref, lam_init):
    l1 = jnp.sum(lam_ref[0:1, :] * lam_ref[1:2, :], axis=-1, keepdims=True)
    l2 = jnp.sum(lam_ref[2:3, :] * lam_ref[3:4, :], axis=-1, keepdims=True)
    return jnp.exp(l1) - jnp.exp(l2) + lam_init


def _attn_finish(acc, l, r, lam, g, za, lam_init):
    o = acc[0:r] / l[0:r] - lam * (acc[r:2 * r] / l[r:2 * r])
    o = _rms(o) * g * (1.0 - lam_init)
    return o * _silu(za)


def _flash_kernel(q_ref, k_ref, vt_ref, za_ref, lam_ref, g_ref, o_ref, qq_scr, m_scr, l_scr, acc_scr,
                  s0_scr, s1_scr, *, tq, lam_init):
    i = pl.program_id(2)
    qq_scr[...] = _split_maps(q_ref[...])
    m_scr[...] = jnp.full(m_scr.shape, NEG_INF, F32)
    l_scr[...] = jnp.zeros(l_scr.shape, F32)
    acc_scr[...] = jnp.zeros(acc_scr.shape, F32)

    def scores(j, dst):
        k0 = pl.multiple_of(j * tq, tq)
        dst[...] = lax.dot_general(k_ref[pl.ds(k0, tq), :], qq_scr[...], (((1,), (1,)), ((), ())),
                                   preferred_element_type=F32)

    def consume(src, j, masked):
        s = src[...]
        if masked:
            key = lax.broadcasted_iota(jnp.int32, s.shape, 0)
            qry = lax.broadcasted_iota(jnp.int32, s.shape, 1)
            s = jnp.where(key <= jnp.where(qry >= tq, qry - tq, qry), s, NEG_INF)
        m_prev = m_scr[...]
        m_new = jnp.maximum(m_prev, jnp.max(s, axis=0, keepdims=True))
        p = jnp.exp(s - m_new)
        alpha = jnp.exp(m_prev - m_new)
        l_scr[...] = alpha * l_scr[...] + jnp.sum(p, axis=0, keepdims=True)
        acc_scr[...] = alpha * acc_scr[...] + jnp.dot(vt_ref[j], p.astype(BF16), preferred_element_type=F32)
        m_scr[...] = m_new

    scores(0, s0_scr)

    def pair(jj, carry):
        j = 2 * jj
        scores(j + 1, s1_scr)
        consume(s0_scr, j, False)
        scores(j + 2, s0_scr)
        consume(s1_scr, j + 1, False)
        return carry

    lax.fori_loop(0, i // 2, pair, 0)

    @pl.when(i % 2 == 1)
    def _odd_tail():
        scores(i, s1_scr)
        consume(s0_scr, i - 1, False)
        consume(s1_scr, i, True)

    @pl.when(i % 2 == 0)
    def _even_tail():
        consume(s0_scr, i, True)

    lam = _lambda(lam_ref, lam_init)
    o_t = acc_scr[...] / l_scr[...]
    o = (o_t[:, 0:tq] - lam * o_t[:, tq:2 * tq]).T
    o = _rms(o) * g_ref[...] * (1.0 - lam_init)
    o_ref[...] = (o * _silu(za_ref[...])).astype(BF16)


def _flash(q, kb, vt, rest, lamp, attn_g, lam_init, tq):
    b, s, _ = q.shape
    nblk = s // tq
    blk = lambda: pl.BlockSpec((None, tq, HEAD_DIM), lambda bi, h, i: (bi, i, h))
    return pl.pallas_call(
        functools.partial(_flash_kernel, tq=tq, lam_init=lam_init),
        grid=(b, N_HEADS, nblk),
        in_specs=[blk(),
                  pl.BlockSpec((None, s, HEAD_DIM), lambda bi, h, i: (bi, 0, h)),
                  pl.BlockSpec((None, nblk, HEAD_DIM, tq), lambda bi, h, i: (h, bi, 0, 0)),
                  blk(),
                  pl.BlockSpec((4, QK_DIM), lambda bi, h, i: (0, 0)),
                  pl.BlockSpec((1, HEAD_DIM), lambda bi, h, i: (0, h))],
        out_specs=blk(),
        out_shape=jax.ShapeDtypeStruct((b, s, GROUP_W), BF16),
        scratch_shapes=[pltpu.VMEM((2 * tq, HEAD_DIM), BF16),
                        pltpu.VMEM((1, 2 * tq), F32),
                        pltpu.VMEM((1, 2 * tq), F32),
                        pltpu.VMEM((HEAD_DIM, 2 * tq), F32),
                        pltpu.VMEM((tq, 2 * tq), F32),
                        pltpu.VMEM((tq, 2 * tq), F32)],
        compiler_params=_cparams(("parallel", "parallel", "arbitrary")),
        name="flash_prompt",
    )(q, kb, vt, rest, lamp, attn_g)


def _paged_kernel(pt_ref, q_ref, *refs, n_pages, n_valid, lam_init):
    del pt_ref
    kp = refs[:n_pages]
    vp = refs[n_pages:2 * n_pages]
    kn_ref, vn_ref, za_ref, lam_ref, g_ref, o_ref, qq_scr, m_scr, l_scr, acc_scr = refs[2 * n_pages:]
    r = SAMPLE_ROWS
    step = pl.program_id(1)

    @pl.when(step == 0)
    def _init():
        for h in range(N_HEADS):
            qq_scr[h] = _split_maps(q_ref[:, h * HEAD_DIM:(h + 1) * HEAD_DIM])
        m_scr[...] = jnp.full(m_scr.shape, NEG_INF, F32)
        l_scr[...] = jnp.zeros(l_scr.shape, F32)
        acc_scr[...] = jnp.zeros(acc_scr.shape, F32)

    for h in range(N_HEADS):
        cols = slice(h * HEAD_DIM, (h + 1) * HEAD_DIM)
        qq = qq_scr[h]
        s = jnp.concatenate(
            [lax.dot_general(qq, kp[j][:, cols].astype(BF16), (((1,), (1,)), ((), ())),
                             preferred_element_type=F32) for j in range(n_pages)], axis=1)
        m_prev = m_scr[h]
        m_new = jnp.maximum(m_prev, jnp.max(s, axis=-1, keepdims=True))
        p = jnp.exp(s - m_new)
        alpha = jnp.exp(m_prev - m_new)
        l_scr[h] = alpha * l_scr[h] + jnp.sum(p, axis=-1, keepdims=True)
        p = p.astype(BF16)
        pv = jnp.dot(p[:, 0:128], vp[0][:, cols].astype(BF16), preferred_element_type=F32)
        for j in range(1, n_pages):
            pv = pv + jnp.dot(p[:, j * 128:(j + 1) * 128], vp[j][:, cols].astype(BF16),
                              preferred_element_type=F32)
        acc_scr[h] = alpha * acc_scr[h] + pv
        m_scr[h] = m_new

    @pl.when(step == pl.num_programs(1) - 1)
    def _finish():
        lam = _lambda(lam_ref, lam_init)
        row = lax.broadcasted_iota(jnp.int32, (2 * r, 1), 0)
        tok = jnp.where(row >= r, row - r, row)
        for h in range(N_HEADS):
            cols = slice(h * HEAD_DIM, (h + 1) * HEAD_DIM)
            qf = qq_scr[h].astype(F32)
            kn = kn_ref[:, cols]
            vn = vn_ref[:, cols]
            s_new = [jnp.where(tok >= j, jnp.sum(qf * kn[j:j + 1, :], axis=-1, keepdims=True), NEG_INF)
                     for j in range(n_valid)]
            m_prev = m_scr[h]
            m_new = m_prev
            for sj in s_new:
                m_new = jnp.maximum(m_new, sj)
            alpha = jnp.exp(m_prev - m_new)
            l = alpha * l_scr[h]
            acc = alpha * acc_scr[h]
            for j, sj in enumerate(s_new):
                pj = jnp.exp(sj - m_new)
                l = l + pj
                acc = acc + pj * vn[j:j + 1, :]
            o_ref[:, cols] = _attn_finish(acc, l, r, lam, g_ref[:, cols], za_ref[:, cols], lam_init).astype(BF16)


def _paged(q, k_new, v_new, rest, cache_k, cache_v, page_table, lamp, attn_g, lam_init, n_valid):
    db = q.shape[0]
    r = SAMPLE_ROWS
    n_pages_total = page_table.shape[1]
    page = cache_k.shape[1]
    p = PAGES_PER_STEP
    seq = lambda w=GROUP_W: pl.BlockSpec((None, r, w), lambda b, s, pt: (b, 0, 0))

    def page_spec(j):
        return pl.BlockSpec((None, page, GROUP_W), lambda b, s, pt, j=j: (pt[b, s * p + j], 0, 0))

    grid_spec = pltpu.PrefetchScalarGridSpec(
        num_scalar_prefetch=1,
        grid=(db, n_pages_total // p),
        in_specs=[seq()] + [page_spec(j) for j in range(p)] * 2 + [seq(), seq(), seq(),
                  pl.BlockSpec((4, QK_DIM), lambda b, s, pt: (0, 0)),
                  pl.BlockSpec((1, GROUP_W), lambda b, s, pt: (0, 0))],
        out_specs=seq(),
        scratch_shapes=[pltpu.VMEM((N_HEADS, 2 * r, HEAD_DIM), BF16),
                        pltpu.VMEM((N_HEADS, 2 * r, 1), F32),
                        pltpu.VMEM((N_HEADS, 2 * r, 1), F32),
                        pltpu.VMEM((N_HEADS, 2 * r, HEAD_DIM), F32)])
    return pl.pallas_call(
        functools.partial(_paged_kernel, n_pages=p, n_valid=n_valid, lam_init=lam_init),
        grid_spec=grid_spec,
        out_shape=jax.ShapeDtypeStruct((db, r, GROUP_W), BF16),
        compiler_params=_cparams(("parallel", "arbitrary")),
        name="paged_attn",
    )(page_table, q, *([cache_k] * p), *([cache_v] * p), k_new, v_new, rest, lamp, attn_g)


def _log_sigmoid(x):
    return jnp.minimum(x, 0.0) - jnp.log1p(jnp.exp(-jnp.abs(x)))


def _mlstm_kernel(xb_ref, ob_ref, zb_ref, conv0_ref, c0_ref, n0_ref, m0_ref,
                  cw_ref, cb_ref, wq_ref, wk_ref, wv_ref, wif_ref, wift_ref, bifr_ref, bifc_ref,
                  mg_ref, skip_ref,
                  bo_ref, c1_ref, n1_ref, m1_ref,
                  xpad, xch_scr, q_scr, k_scr, v_scr, h_scr, gc_scr, gr_scr, c_scr, n_scr, m_scr,
                  *, t, chunk, n_valid):
    step = pl.program_id(1)
    n_chunks = t // chunk
    hd = HEAD_DIM

    @pl.when(step == 0)
    def _init():
        xpad[0:8, :] = jnp.zeros((8, GROUP_W), F32)
        xpad[8 - (CONV_W - 1):8, :] = conv0_ref[...]
        c_scr[...] = c0_ref[...]
        n_scr[...] = n0_ref[...]
        m_scr[...] = m0_ref[...]

    x = xb_ref[...]
    xpad[8:8 + t, :] = x
    xc = cb_ref[...] + cw_ref[CONV_W - 1:CONV_W, :] * x
    for j in range(CONV_W - 1):
        xc = xc + cw_ref[j:j + 1, :] * xpad[8 - (CONV_W - 1) + j:8 - (CONV_W - 1) + j + t, :]
    tail = xpad[t:t + 8, :]
    xpad[0:8, :] = tail
    xch = _silu(xc)
    xch_scr[...] = xch

    gcol = jnp.zeros((t, 2 * N_HEADS), F32) + bifr_ref[...]
    grow = jnp.zeros((2 * N_HEADS, t), F32) + bifc_ref[...]
    for h in range(N_HEADS):
        cols = slice(h * hd, (h + 1) * hd)
        xh = xch[:, cols].astype(BF16)
        qh = jnp.dot(xh, wq_ref[h], preferred_element_type=F32)
        kh = jnp.dot(xh, wk_ref[h], preferred_element_type=F32) * (hd ** -0.5)
        vh = jnp.dot(x[:, cols].astype(BF16), wv_ref[h], preferred_element_type=F32)
        q_scr[:, cols] = qh
        k_scr[:, cols] = kh
        v_scr[:, cols] = vh
        for part, val in enumerate((qh, kh, vh)):
            off = (3 * h + part) * hd
            vb = val.astype(BF16)
            gcol = gcol + jnp.dot(vb, wif_ref[off:off + hd, :], preferred_element_type=F32)
            grow = grow + lax.dot_general(wift_ref[:, off:off + hd], vb, (((1,), (1,)), ((), ())),
                                          preferred_element_type=F32)

    lane = lax.broadcasted_iota(jnp.int32, gcol.shape, 1)
    gcol = jnp.where(lane < N_HEADS, gcol, _log_sigmoid(gcol))
    sub = lax.broadcasted_iota(jnp.int32, grow.shape, 0)
    grow = jnp.where(sub < N_HEADS, grow, _log_sigmoid(grow))
    if n_valid < chunk:
        pos_c = lax.broadcasted_iota(jnp.int32, gcol.shape, 0) % chunk
        gcol = jnp.where(pos_c >= n_valid, jnp.where(lane < N_HEADS, NEG_INF, 0.0), gcol)
        pos_r = lax.broadcasted_iota(jnp.int32, grow.shape, 1) % chunk
        grow = jnp.where(pos_r >= n_valid, jnp.where(sub < N_HEADS, NEG_INF, 0.0), grow)
    gc_scr[...] = gcol
    for c in range(n_chunks):
        gr_scr[c] = grow[:, c * chunk:(c + 1) * chunk]

    ri = lax.broadcasted_iota(jnp.int32, (chunk, chunk), 0)
    ci = lax.broadcasted_iota(jnp.int32, (chunk, chunk), 1)
    causal = ci <= ri
    tril = jnp.where(causal, 1.0, 0.0).astype(F32)
    triu = jnp.where(ri <= ci, 1.0, 0.0).astype(F32)
    lane_c = lax.broadcasted_iota(jnp.int32, (chunk, 2 * N_HEADS), 1)
    sub_r = lax.broadcasted_iota(jnp.int32, (2 * N_HEADS, chunk), 0)

    def scan_chunk(c, carry):
        r0 = pl.multiple_of(c * chunk, chunk)
        rows = pl.ds(r0, chunk)
        gc = gc_scr[rows, :]
        gr = gr_scr[c]
        b_c = jnp.dot(tril, jnp.where(lane_c >= N_HEADS, gc, 0.0), preferred_element_type=F32,
                      precision=lax.Precision.HIGHEST)
        b_r = jnp.dot(jnp.where(sub_r >= N_HEADS, gr, 0.0), triu, preferred_element_type=F32,
                      precision=lax.Precision.HIGHEST)
        for h in range(N_HEADS):
            cols = slice(h * hd, (h + 1) * hd)
            bc = b_c[:, N_HEADS + h:N_HEADS + h + 1]
            br = b_r[N_HEADS + h:N_HEADS + h + 1, :]
            ic = gc[:, h:h + 1]
            ir = gr[h:h + 1, :]
            m = m_scr[h]
            dm = jnp.where(causal, bc - br + ir, NEG_INF)
            inter = bc + m
            m_t = jnp.maximum(inter, jnp.max(dm, axis=-1, keepdims=True))
            w_d = jnp.exp(dm - m_t)
            w_in = jnp.exp(inter - m_t)
            q = q_scr[rows, cols]
            k = k_scr[rows, cols]
            v = v_scr[rows, cols]
            qb = q.astype(BF16)
            kb = k.astype(BF16)
            sqk = lax.dot_general(qb, kb, (((1,), (1,)), ((), ())), preferred_element_type=F32) * w_d
            cmat = c_scr[h]
            nvec = n_scr[h]
            num = (w_in * jnp.dot(qb, cmat.astype(BF16), preferred_element_type=F32)
                   + jnp.dot(sqk.astype(BF16), v.astype(BF16), preferred_element_type=F32))
            den = w_in * jnp.sum(q * nvec, axis=-1, keepdims=True) + jnp.sum(sqk, axis=-1, keepdims=True)
            h_scr[rows, cols] = num / jnp.maximum(jnp.abs(den), jnp.exp(-m_t))
            b_l = bc[chunk - 1:chunk, :]
            g_r = b_l - br + ir
            m_new = jnp.maximum(b_l + m, jnp.max(g_r, axis=-1, keepdims=True))
            decay = jnp.exp(b_l + m - m_new)
            w_g = jnp.exp(b_l - bc + ic - m_new)
            c_scr[h] = decay * cmat + lax.dot_general(kb, (w_g * v).astype(BF16), (((0,), (0,)), ((), ())),
                                                      preferred_element_type=F32)
            n_scr[h] = decay * nvec + jnp.sum(w_g * k, axis=0, keepdims=True)
            m_scr[h] = m_new
        return carry

    if n_chunks == 1:
        scan_chunk(0, 0)
    else:
        lax.fori_loop(0, n_chunks, scan_chunk, 0)

    for h in range(N_HEADS):
        cols = slice(h * hd, (h + 1) * hd)
        hb = h_scr[:, cols] * jax.nn.sigmoid(ob_ref[:, cols])
        hb = _rms(hb) * mg_ref[:, cols] + skip_ref[:, cols] * xch_scr[:, cols]
        bo_ref[:, cols] = (hb * _silu(zb_ref[:, cols])).astype(BF16)

    @pl.when(step == pl.num_programs(1) - 1)
    def _state_out():
        c1_ref[...] = c_scr[...]
        n1_ref[...] = n_scr[...]
        m1_ref[...] = m_scr[...]


def _mlstm(rest, conv0, c0, n0, m0, w, t, chunk, n_valid):
    bn, s, _ = rest.shape
    hd = HEAD_DIM
    tok = lambda col: pl.BlockSpec((None, t, GROUP_W), lambda b, j, col=col: (b, j, col))
    per_b = lambda *shape: pl.BlockSpec((None,) + shape, lambda b, j: (b,) + (0,) * len(shape))
    full = lambda *shape: pl.BlockSpec(shape, lambda b, j: (0,) * len(shape))
    state_specs = [per_b(N_HEADS, hd, hd), per_b(N_HEADS, 1, hd), per_b(N_HEADS, 1, 1)]
    return pl.pallas_call(
        functools.partial(_mlstm_kernel, t=t, chunk=chunk, n_valid=n_valid),
        grid=(bn, s // t),
        in_specs=[tok(1), tok(2), tok(3), per_b(CONV_W - 1, GROUP_W)] + state_specs + [
            full(CONV_W, GROUP_W), full(1, GROUP_W),
            full(N_HEADS, hd, hd), full(N_HEADS, hd, hd), full(N_HEADS, hd, hd),
            full(3 * GROUP_W, 2 * N_HEADS), full(2 * N_HEADS, 3 * GROUP_W),
            full(1, 2 * N_HEADS), full(2 * N_HEADS, 1),
            full(1, GROUP_W), full(1, GROUP_W)],
        out_specs=[pl.BlockSpec((None, t, GROUP_W), lambda b, j: (b, j, 0))] + state_specs,
        out_shape=[jax.ShapeDtypeStruct((bn, s, GROUP_W), BF16),
                   jax.ShapeDtypeStruct((bn, N_HEADS, hd, hd), F32),
                   jax.ShapeDtypeStruct((bn, N_HEADS, 1, hd), F32),
                   jax.ShapeDtypeStruct((bn, N_HEADS, 1, 1), F32)],
        scratch_shapes=[pltpu.VMEM((t + 8, GROUP_W), F32)] + [pltpu.VMEM((t, GROUP_W), F32)] * 5 + [
            pltpu.VMEM((t, 2 * N_HEADS), F32),
            pltpu.VMEM((t // chunk, 2 * N_HEADS, chunk), F32),
            pltpu.VMEM((N_HEADS, hd, hd), F32),
            pltpu.VMEM((N_HEADS, 1, hd), F32),
            pltpu.VMEM((N_HEADS, 1, 1), F32)],
        compiler_params=_cparams(("parallel", "arbitrary")),
        name="mlstm",
    )(rest, rest, rest, conv0, c0, n0, m0,
      w["conv_w"], w["conv_b"], w["wq"], w["wk"], w["wv"], w["wif"], w["wif_t"], w["bif_row"], w["bif_col"],
      w["mlstm_g"], w["mlstm_skip"])


def _outproj_kernel(a_ref, bo_ref, x_ref, p_ref, woa_ref, wob_ref, pg_ref, wgate_ref, wple_ref, plg_ref, y_ref):
    y = (jnp.dot(a_ref[...], woa_ref[...], preferred_element_type=F32)
         + jnp.dot(bo_ref[...], wob_ref[...], preferred_element_type=F32))
    x1 = x_ref[...] + _rms(y) * pg_ref[...]
    gate = jax.nn.sigmoid(jnp.dot(x1.astype(BF16), wgate_ref[...], preferred_element_type=F32))
    pe = jnp.dot(p_ref[...].astype(BF16), wple_ref[...], preferred_element_type=F32)
    y_ref[...] = x1 + _rms(gate * pe) * plg_ref[...]


def _outproj(a, bo, x, p, w, tm):
    m, d = x.shape
    row = lambda width: pl.BlockSpec((tm, width), lambda i: (i, 0))
    full = lambda arr: pl.BlockSpec(arr.shape, lambda i: (0,) * arr.ndim)
    consts = [w["wo_a"], w["wo_b"], w["post_g"], w["ple_gate"], w["ple_w"], w["ple_g"]]
    return pl.pallas_call(
        _outproj_kernel,
        grid=(m // tm,),
        in_specs=[row(GROUP_W), row(GROUP_W), row(d), row(p.shape[1])] + [full(c) for c in consts],
        out_specs=row(d),
        out_shape=jax.ShapeDtypeStruct((m, d), F32),
        compiler_params=_cparams(("parallel",)),
        name="outproj",
    )(a, bo, x, p, *consts)


def _layer_weights(l, norm_pre_g, norm_post_g, w_in, lam_q1, lam_k1, lam_q2, lam_k2, attn_norm_g, conv_w, conv_b,
                   w_q_b, w_k_b, w_v_b, w_if, b_if, mlstm_norm_g, mlstm_skip, w_out, ple_w, ple_gate_w, ple_norm_g):
    row = lambda v: v.reshape(1, -1)
    wo = w_out[l].astype(BF16)
    return {
        "pre_g": row(norm_pre_g[l]), "w_in": w_in[l].astype(BF16),
        "lam": jnp.stack([lam_q1[l], lam_k1[l], lam_q2[l], lam_k2[l]]).astype(F32),
        "attn_g": row(attn_norm_g[l]),
        "conv_w": conv_w[l], "conv_b": row(conv_b[l]),
        "wq": w_q_b[l].astype(BF16), "wk": w_k_b[l].astype(BF16), "wv": w_v_b[l].astype(BF16),
        "wif": w_if[l].astype(BF16), "wif_t": w_if[l].T.astype(BF16),
        "bif_row": row(b_if[l]), "bif_col": b_if[l].reshape(-1, 1),
        "mlstm_g": row(mlstm_norm_g[l]), "mlstm_skip": row(mlstm_skip[l]),
        "wo_a": wo[:GROUP_W], "wo_b": wo[GROUP_W:], "post_g": row(norm_post_g[l]),
        "ple_gate": ple_gate_w[l].astype(BF16), "ple_w": ple_w[l].astype(BF16), "ple_g": row(ple_norm_g[l]),
    }


def _prompt_layer(x, p, w, lam_init):
    b, s, d = x.shape
    x2 = x.reshape(b * s, d)
    q, k, v, rest, kb, vt = _inproj(x2, w["pre_g"], w["w_in"], tm=PROMPT_ROWS, kv_block=FLASH_BLOCK)
    rest3 = rest.reshape(b, s, 4 * GROUP_W)
    a = _flash(q.reshape(b, s, GROUP_W), kb.reshape(b, s, GROUP_W), vt, rest3,
               w["lam"], w["attn_g"], lam_init, tq=FLASH_BLOCK)
    hd = HEAD_DIM
    bo, c1, n1, m1 = _mlstm(rest3, jnp.zeros((b, CONV_W - 1, GROUP_W), F32),
                            jnp.zeros((b, N_HEADS, hd, hd), F32), jnp.zeros((b, N_HEADS, 1, hd), F32),
                            jnp.zeros((b, N_HEADS, 1, 1), F32), w, t=PROMPT_ROWS, chunk=PROMPT_CHUNK,
                            n_valid=PROMPT_CHUNK)
    y = _outproj(a.reshape(b * s, GROUP_W), bo.reshape(b * s, GROUP_W), x2, p.reshape(b * s, -1), w,
                 tm=PROMPT_ROWS)
    state = (k.reshape(b, s, N_HEADS, hd), v.reshape(b, s, N_HEADS, hd),
             rest3[:, s - (CONV_W - 1):, GROUP_W:2 * GROUP_W],
             c1, n1.reshape(b, N_HEADS, hd), m1.reshape(b, N_HEADS))
    return y.reshape(b, s, d), state


def _sample_layer(x, p, n_valid, cache_k, cache_v, page_table, conv0, c0, n0, m0, w, lam_init):
    db, r, d = x.shape
    hd = HEAD_DIM
    assert CONV_W - 1 <= n_valid <= r
    x2 = x.reshape(db * r, d)
    q, k, v, rest = _inproj(x2, w["pre_g"], w["w_in"], tm=db * r)
    rest3 = rest.reshape(db, r, 4 * GROUP_W)
    k3 = k.reshape(db, r, GROUP_W)
    v3 = v.reshape(db, r, GROUP_W)
    a = _paged(q.reshape(db, r, GROUP_W), k3, v3, rest3, cache_k, cache_v, page_table, w["lam"], w["attn_g"],
               lam_init, n_valid)
    bo, c1, n1, m1 = _mlstm(rest3, conv0, c0, n0.reshape(db, N_HEADS, 1, hd), m0.reshape(db, N_HEADS, 1, 1),
                            w, t=r, chunk=r, n_valid=n_valid)
    y = _outproj(a.reshape(db * r, GROUP_W), bo.reshape(db * r, GROUP_W), x2, p.reshape(db * r, -1), w, tm=db * r)
    state = (k3[:, :n_valid].reshape(db, n_valid, N_HEADS, hd), v3[:, :n_valid].reshape(db, n_valid, N_HEADS, hd),
             rest3[:, n_valid - (CONV_W - 1):n_valid, GROUP_W:2 * GROUP_W],
             c1, n1.reshape(db, N_HEADS, hd), m1.reshape(db, N_HEADS))
    return y.reshape(db, r, d), state


def kernel(x_prompt, x_sample, cache_k, cache_v, state_conv, state_C, state_n, state_m, page_table,
           p_prompt, p_sample, norm_pre_g, norm_post_g, w_in, lam_q1, lam_k1, lam_q2, lam_k2,
           attn_norm_g, conv_w, conv_b, w_q_b, w_k_b, w_v_b, w_if, b_if, mlstm_norm_g, mlstm_skip,
           w_out, ple_w, ple_gate_w, ple_norm_g):
    depth = w_in.shape[0]
    t_dec = x_sample.shape[1]
    pad_rows = lambda a: jnp.pad(a, ((0, 0), (0, SAMPLE_ROWS - t_dec), (0, 0)))
    n_pool, page = cache_k.shape[1], cache_k.shape[2]
    cache_k2 = cache_k.reshape(depth * n_pool, page, GROUP_W)
    cache_v2 = cache_v.reshape(depth * n_pool, page, GROUP_W)
    xp = x_prompt
    xs = pad_rows(x_sample)
    outs_p, outs_s = [], []
    for l in range(depth):
        w = _layer_weights(l, norm_pre_g, norm_post_g, w_in, lam_q1, lam_k1, lam_q2, lam_k2, attn_norm_g,
                           conv_w, conv_b, w_q_b, w_k_b, w_v_b, w_if, b_if, mlstm_norm_g, mlstm_skip,
                           w_out, ple_w, ple_gate_w, ple_norm_g)
        lam_init = 0.8 - 0.6 * math.exp(-0.3 * l)
        xp, st_p = _prompt_layer(xp, p_prompt[l], w, lam_init)
        outs_p.append(st_p)
        xs, st_s = _sample_layer(xs, pad_rows(p_sample[l]), t_dec, cache_k2, cache_v2, page_table + l * n_pool,
                                 state_conv[l], state_C[l], state_n[l], state_m[l], w, lam_init)
        outs_s.append(st_s)
    stack = lambda outs, i: jnp.stack([o[i] for o in outs])
    return ((xp, xs[:, :t_dec]) + tuple(stack(outs_p, i) for i in range(6))
            + tuple(stack(outs_s, i) for i in range(6)))
```

```python
import functools
import math

import jax
import jax.numpy as jnp
from jax import lax
from jax.experimental import pallas as pl
from jax.experimental.pallas import tpu as pltpu

F32 = jnp.float32
BF16 = jnp.bfloat16
EPS = 1e-6
NEG_INF = float("-inf")

N_HEADS = 4
HEAD_DIM = 128
QK_DIM = 64
GROUP_W = N_HEADS * HEAD_DIM
CONV_W = 4
SAMPLE_ROWS = 8
PROMPT_CHUNK = 128
PROMPT_ROWS = 512
FLASH_BLOCK = 256
PAGES_PER_STEP = 8

V7X_VMEM_LIMIT = 48 * 1024 * 1024


def _cparams(sem):
    return pltpu.CompilerParams(dimension_semantics=sem, vmem_limit_bytes=V7X_VMEM_LIMIT)


def _silu(x):
    return x * jax.nn.sigmoid(x)


def _rms(x):
    return x * lax.rsqrt(jnp.mean(x * x, axis=-1, keepdims=True) + EPS)


def _inproj_kernel(x_ref, g_ref, w_ref, q_ref, k_ref, v_ref, rest_ref, *maybe_flash_refs, kv_block):
    hb = (_rms(x_ref[...]) * g_ref[...]).astype(BF16)

    def proj(c):
        return jnp.dot(hb, w_ref[:, c * GROUP_W:(c + 1) * GROUP_W], preferred_element_type=F32)

    q_ref[...] = (proj(0) * (QK_DIM ** -0.5)).astype(BF16)
    k = proj(1)
    k_ref[...] = k
    v = proj(2)
    v_ref[...] = v
    if kv_block:
        kb_ref, vt_ref = maybe_flash_refs
        kb_ref[...] = k.astype(BF16)
        for h in range(N_HEADS):
            for c in range(v.shape[0] // kv_block):
                blk = v[c * kv_block:(c + 1) * kv_block, h * HEAD_DIM:(h + 1) * HEAD_DIM]
                vt_ref[h, c] = blk.T.astype(BF16)
    for c in range(3, 7):
        rest_ref[:, (c - 3) * GROUP_W:(c - 2) * GROUP_W] = proj(c)


def _inproj(x, g, w_bf, tm, kv_block=0):
    m, d = x.shape
    n_in = w_bf.shape[1]
    row = lambda w: pl.BlockSpec((tm, w), lambda i: (i, 0))
    out_specs = [row(GROUP_W)] * 3 + [row(4 * GROUP_W)]
    out_shape = [jax.ShapeDtypeStruct((m, GROUP_W), BF16),
                 jax.ShapeDtypeStruct((m, GROUP_W), F32),
                 jax.ShapeDtypeStruct((m, GROUP_W), F32),
                 jax.ShapeDtypeStruct((m, 4 * GROUP_W), F32)]
    if kv_block:
        out_specs += [row(GROUP_W),
                      pl.BlockSpec((N_HEADS, tm // kv_block, HEAD_DIM, kv_block), lambda i: (0, i, 0, 0))]
        out_shape += [jax.ShapeDtypeStruct((m, GROUP_W), BF16),
                      jax.ShapeDtypeStruct((N_HEADS, m // kv_block, HEAD_DIM, kv_block), BF16)]
    return pl.pallas_call(
        functools.partial(_inproj_kernel, kv_block=kv_block),
        grid=(m // tm,),
        in_specs=[row(d), pl.BlockSpec((1, d), lambda i: (0, 0)),
                  pl.BlockSpec((d, n_in), lambda i: (0, 0))],
        out_specs=out_specs,
        out_shape=out_shape,
        compiler_params=_cparams(("parallel",)),
        name="inproj",
    )(x, g, w_bf)


def _split_maps(q):
    q = q.astype(F32)
    lane = lax.broadcasted_iota(jnp.int32, q.shape, 1)
    zero = jnp.zeros_like(q)
    both = jnp.concatenate([jnp.where(lane < QK_DIM, q, zero), jnp.where(lane >= QK_DIM, q, zero)], axis=0)
    return both.astype(BF16)


def _lambda(lam_ref, lam_init):
    l1 = jnp.sum(lam_ref[0:1, :] * lam_ref[1:2, :], axis=-1, keepdims=True)
    l2 = jnp.sum(lam_ref[2:3, :] * lam_ref[3:4, :], axis=-1, keepdims=True)
    return jnp.exp(l1) - jnp.exp(l2) + lam_init


def _attn_finish(acc, l, r, lam, g, za, lam_init):
    o = acc[0:r] / l[0:r] - lam * (acc[r:2 * r] / l[r:2 * r])
    o = _rms(o) * g * (1.0 - lam_init)
    return o * _silu(za)


def _flash_kernel(q_ref, k_ref, vt_ref, za_ref, lam_ref, g_ref, o_ref, qq_scr, m_scr, l_scr, acc_scr,
                  s0_scr, s1_scr, *, tq, lam_init):
    i = pl.program_id(2)
    qq_scr[...] = _split_maps(q_ref[...])
    m_scr[...] = jnp.full(m_scr.shape, NEG_INF, F32)
    l_scr[...] = jnp.zeros(l_scr.shape, F32)
    acc_scr[...] = jnp.zeros(acc_scr.shape, F32)

    def scores(j, dst):
        k0 = pl.multiple_of(j * tq, tq)
        dst[...] = lax.dot_general(k_ref[pl.ds(k0, tq), :], qq_scr[...], (((1,), (1,)), ((), ())),
                                   preferred_element_type=F32)

    def consume(src, j, masked):
        s = src[...]
        if masked:
            key = lax.broadcasted_iota(jnp.int32, s.shape, 0)
            qry = lax.broadcasted_iota(jnp.int32, s.shape, 1)
            s = jnp.where(key <= jnp.where(qry >= tq, qry - tq, qry), s, NEG_INF)
        m_prev = m_scr[...]
        m_new = jnp.maximum(m_prev, jnp.max(s, axis=0, keepdims=True))
        p = jnp.exp(s - m_new)
        alpha = jnp.exp(m_prev - m_new)
        l_scr[...] = alpha * l_scr[...] + jnp.sum(p, axis=0, keepdims=True)
        acc_scr[...] = alpha * acc_scr[...] + jnp.dot(vt_ref[j], p.astype(BF16), preferred_element_type=F32)
        m_scr[...] = m_new

    scores(0, s0_scr)

    def pair(jj, carry):
        j = 2 * jj
        scores(j + 1, s1_scr)
        consume(s0_scr, j, False)
        scores(j + 2, s0_scr)
        consume(s1_scr, j + 1, False)
        return carry

    lax.fori_loop(0, i // 2, pair, 0)

    @pl.when(i % 2 == 1)
    def _odd_tail():
        scores(i, s1_scr)
        consume(s0_scr, i - 1, False)
        consume(s1_scr, i, True)

    @pl.when(i % 2 == 0)
    def _even_tail():
        consume(s0_scr, i, True)

    lam = _lambda(lam_ref, lam_init)
    o_t = acc_scr[...] / l_scr[...]
    o = (o_t[:, 0:tq] - lam * o_t[:, tq:2 * tq]).T
    o = _rms(o) * g_ref[...] * (1.0 - lam_init)
    o_ref[...] = (o * _silu(za_ref[...])).astype(BF16)


def _flash(q, kb, vt, rest, lamp, attn_g, lam_init, tq):
    b, s, _ = q.shape
    nblk = s // tq
    blk = lambda: pl.BlockSpec((None, tq, HEAD_DIM), lambda bi, h, i: (bi, i, h))
    return pl.pallas_call(
        functools.partial(_flash_kernel, tq=tq, lam_init=lam_init),
        grid=(b, N_HEADS, nblk),
        in_specs=[blk(),
                  pl.BlockSpec((None, s, HEAD_DIM), lambda bi, h, i: (bi, 0, h)),
                  pl.BlockSpec((None, nblk, HEAD_DIM, tq), lambda bi, h, i: (h, bi, 0, 0)),
                  blk(),
                  pl.BlockSpec((4, QK_DIM), lambda bi, h, i: (0, 0)),
                  pl.BlockSpec((1, HEAD_DIM), lambda bi, h, i: (0, h))],
        out_specs=blk(),
        out_shape=jax.ShapeDtypeStruct((b, s, GROUP_W), BF16),
        scratch_shapes=[pltpu.VMEM((2 * tq, HEAD_DIM), BF16),
                        pltpu.VMEM((1, 2 * tq), F32),
                        pltpu.VMEM((1, 2 * tq), F32),
                        pltpu.VMEM((HEAD_DIM, 2 * tq), F32),
                        pltpu.VMEM((tq, 2 * tq), F32),
                        pltpu.VMEM((tq, 2 * tq), F32)],
        compiler_params=_cparams(("parallel", "parallel", "arbitrary")),
        name="flash_prompt",
    )(q, kb, vt, rest, lamp, attn_g)


def _paged_kernel(pt_ref, q_ref, *refs, n_pages, n_valid, lam_init):
    del pt_ref
    kp = refs[:n_pages]
    vp = refs[n_pages:2 * n_pages]
    kn_ref, vn_ref, za_ref, lam_ref, g_ref, o_ref, qq_scr, bias_scr, m_scr, l_scr, acc_scr = refs[2 * n_pages:]
    r = SAMPLE_ROWS
    hr = 2 * r
    pw = bias_scr.shape[1]
    step = pl.program_id(1)

    @pl.when(step == 0)
    def _init():
        for h in range(N_HEADS):
            qq_scr[h * hr:(h + 1) * hr, :] = _split_maps(q_ref[:, h * HEAD_DIM:(h + 1) * HEAD_DIM])
        row_head = lax.broadcasted_iota(jnp.int32, bias_scr.shape, 0) // hr
        col_head = lax.broadcasted_iota(jnp.int32, bias_scr.shape, 1) % N_HEADS
        bias_scr[...] = jnp.where(row_head == col_head, 0.0, NEG_INF).astype(F32)
        m_scr[...] = jnp.full(m_scr.shape, NEG_INF, F32)
        l_scr[...] = jnp.zeros(l_scr.shape, F32)
        acc_scr[...] = jnp.zeros(acc_scr.shape, F32)

    qq = qq_scr[...]
    bias = bias_scr[...]
    s = jnp.concatenate(
        [lax.dot_general(qq, kp[j][...].astype(BF16), (((1,), (1,)), ((), ())),
                         preferred_element_type=F32) + bias for j in range(n_pages)], axis=1)
    m_prev = m_scr[...]
    m_new = jnp.maximum(m_prev, jnp.max(s, axis=-1, keepdims=True))
    p = jnp.exp(s - m_new)
    alpha = jnp.exp(m_prev - m_new)
    l_scr[...] = alpha * l_scr[...] + jnp.sum(p, axis=-1, keepdims=True)
    p = p.astype(BF16)
    pv = jnp.dot(p[:, 0:pw], vp[0][...].astype(BF16), preferred_element_type=F32)
    for j in range(1, n_pages):
        pv = pv + jnp.dot(p[:, j * pw:(j + 1) * pw], vp[j][...].astype(BF16), preferred_element_type=F32)
    acc_scr[...] = alpha * acc_scr[...] + pv
    m_scr[...] = m_new

    @pl.when(step == pl.num_programs(1) - 1)
    def _finish():
        lam = _lambda(lam_ref, lam_init)
        row = lax.broadcasted_iota(jnp.int32, (2 * r, 1), 0)
        tok = jnp.where(row >= r, row - r, row)
        for h in range(N_HEADS):
            cols = slice(h * HEAD_DIM, (h + 1) * HEAD_DIM)
            rows = slice(h * hr, (h + 1) * hr)
            qf = qq_scr[rows, :].astype(F32)
            kn = kn_ref[:, cols]
            vn = vn_ref[:, cols]
            s_new = [jnp.where(tok >= j, jnp.sum(qf * kn[j:j + 1, :], axis=-1, keepdims=True), NEG_INF)
                     for j in range(n_valid)]
            m_prev = m_scr[rows, :]
            m_new = m_prev
            for sj in s_new:
                m_new = jnp.maximum(m_new, sj)
            alpha = jnp.exp(m_prev - m_new)
            l = alpha * l_scr[rows, :]
            acc = alpha * acc_scr[rows, :]
            for j, sj in enumerate(s_new):
                pj = jnp.exp(sj - m_new)
                l = l + pj
                acc = acc + pj * vn[j:j + 1, :]
            o_ref[:, cols] = _attn_finish(acc, l, r, lam, g_ref[:, cols], za_ref[:, cols], lam_init).astype(BF16)


def _paged(q, k_new, v_new, rest, cache_k, cache_v, page_table, lamp, attn_g, lam_init, n_valid):
    db = q.shape[0]
    r = SAMPLE_ROWS
    n_pages_total = page_table.shape[1]
    page_rows = cache_k.shape[1]
    p = PAGES_PER_STEP
    seq = lambda w=GROUP_W: pl.BlockSpec((None, r, w), lambda b, s, pt: (b, 0, 0))

    def page_spec(j):
        return pl.BlockSpec((None, page_rows, HEAD_DIM), lambda b, s, pt, j=j: (pt[b, s * p + j], 0, 0))

    grid_spec = pltpu.PrefetchScalarGridSpec(
        num_scalar_prefetch=1,
        grid=(db, n_pages_total // p),
        in_specs=[seq()] + [page_spec(j) for j in range(p)] * 2 + [seq(), seq(), seq(),
                  pl.BlockSpec((4, QK_DIM), lambda b, s, pt: (0, 0)),
                  pl.BlockSpec((1, GROUP_W), lambda b, s, pt: (0, 0))],
        out_specs=seq(),
        scratch_shapes=[pltpu.VMEM((N_HEADS * 2 * r, HEAD_DIM), BF16),
                        pltpu.VMEM((N_HEADS * 2 * r, page_rows), F32),
                        pltpu.VMEM((N_HEADS * 2 * r, 1), F32),
                        pltpu.VMEM((N_HEADS * 2 * r, 1), F32),
                        pltpu.VMEM((N_HEADS * 2 * r, HEAD_DIM), F32)])
    return pl.pallas_call(
        functools.partial(_paged_kernel, n_pages=p, n_valid=n_valid, lam_init=lam_init),
        grid_spec=grid_spec,
        out_shape=jax.ShapeDtypeStruct((db, r, GROUP_W), BF16),
        compiler_params=_cparams(("parallel", "arbitrary")),
        name="paged_attn",
    )(page_table, q, *([cache_k] * p), *([cache_v] * p), k_new, v_new, rest, lamp, attn_g)


def _log_sigmoid(x):
    return jnp.minimum(x, 0.0) - jnp.log1p(jnp.exp(-jnp.abs(x)))


def _mlstm_kernel(xb_ref, ob_ref, zb_ref, conv0_ref, c0_ref, n0_ref, m0_ref,
                  cw_ref, cb_ref, wq_ref, wk_ref, wv_ref, wif_ref, wift_ref, bifr_ref, bifc_ref,
                  mg_ref, skip_ref,
                  bo_ref, c1_ref, n1_ref, m1_ref,
                  xpad, xch_scr, q_scr, k_scr, v_scr, h_scr, gc_scr, gr_scr, c_scr, n_scr, m_scr,
                  *, t, chunk, n_valid):
    step = pl.program_id(1)
    n_chunks = t // chunk
    hd = HEAD_DIM

    @pl.when(step == 0)
    def _init():
        xpad[0:8, :] = jnp.zeros((8, GROUP_W), F32)
        xpad[8 - (CONV_W - 1):8, :] = conv0_ref[...]
        c_scr[...] = c0_ref[...]
        n_scr[...] = n0_ref[...]
        m_scr[...] = m0_ref[...]

    x = xb_ref[...]
    xpad[8:8 + t, :] = x
    xc = cb_ref[...] + cw_ref[CONV_W - 1:CONV_W, :] * x
    for j in range(CONV_W - 1):
        xc = xc + cw_ref[j:j + 1, :] * xpad[8 - (CONV_W - 1) + j:8 - (CONV_W - 1) + j + t, :]
    tail = xpad[t:t + 8, :]
    xpad[0:8, :] = tail
    xch = _silu(xc)
    xch_scr[...] = xch

    gcol = jnp.zeros((t, 2 * N_HEADS), F32) + bifr_ref[...]
    grow = jnp.zeros((2 * N_HEADS, t), F32) + bifc_ref[...]
    for h in range(N_HEADS):
        cols = slice(h * hd, (h + 1) * hd)
        xh = xch[:, cols].astype(BF16)
        qh = jnp.dot(xh, wq_ref[h], preferred_element_type=F32)
        kh = jnp.dot(xh, wk_ref[h], preferred_element_type=F32) * (hd ** -0.5)
        vh = jnp.dot(x[:, cols].astype(BF16), wv_ref[h], preferred_element_type=F32)
        q_scr[:, cols] = qh
        k_scr[:, cols] = kh
        v_scr[:, cols] = vh
        for part, val in enumerate((qh, kh, vh)):
            off = (3 * h + part) * hd
            vb = val.astype(BF16)
            gcol = gcol + jnp.dot(vb, wif_ref[off:off + hd, :], preferred_element_type=F32)
            grow = grow + lax.dot_general(wift_ref[:, off:off + hd], vb, (((1,), (1,)), ((), ())),
                                          preferred_element_type=F32)

    lane = lax.broadcasted_iota(jnp.int32, gcol.shape, 1)
    gcol = jnp.where(lane < N_HEADS, gcol, _log_sigmoid(gcol))
    sub = lax.broadcasted_iota(jnp.int32, grow.shape, 0)
    grow = jnp.where(sub < N_HEADS, grow, _log_sigmoid(grow))
    if n_valid < chunk:
        pos_c = lax.broadcasted_iota(jnp.int32, gcol.shape, 0) % chunk
        gcol = jnp.where(pos_c >= n_valid, jnp.where(lane < N_HEADS, NEG_INF, 0.0), gcol)
        pos_r = lax.broadcasted_iota(jnp.int32, grow.shape, 1) % chunk
        grow = jnp.where(pos_r >= n_valid, jnp.where(sub < N_HEADS, NEG_INF, 0.0), grow)
    gc_scr[...] = gcol
    for c in range(n_chunks):
        gr_scr[c] = grow[:, c * chunk:(c + 1) * chunk]

    ri = lax.broadcasted_iota(jnp.int32, (chunk, chunk), 0)
    ci = lax.broadcasted_iota(jnp.int32, (chunk, chunk), 1)
    causal = ci <= ri
    tril = jnp.where(causal, 1.0, 0.0).astype(F32)
    triu = jnp.where(ri <= ci, 1.0, 0.0).astype(F32)
    lane_c = lax.broadcasted_iota(jnp.int32, (chunk, 2 * N_HEADS), 1)
    sub_r = lax.broadcasted_iota(jnp.int32, (2 * N_HEADS, chunk), 0)

    def scan_chunk(c, carry):
        r0 = pl.multiple_of(c * chunk, chunk)
        rows = pl.ds(r0, chunk)
        gc = gc_scr[rows, :]
        gr = gr_scr[c]
        b_c = jnp.dot(tril, jnp.where(lane_c >= N_HEADS, gc, 0.0), preferred_element_type=F32,
                      precision=lax.Precision.HIGHEST)
        b_r = jnp.dot(jnp.where(sub_r >= N_HEADS, gr, 0.0), triu, preferred_element_type=F32,
                      precision=lax.Precision.HIGHEST)
        for h in range(N_HEADS):
            cols = slice(h * hd, (h + 1) * hd)
            bc = b_c[:, N_HEADS + h:N_HEADS + h + 1]
            br = b_r[N_HEADS + h:N_HEADS + h + 1, :]
            ic = gc[:, h:h + 1]
            ir = gr[h:h + 1, :]
            m = m_scr[h]
            dm = jnp.where(causal, bc - br + ir, NEG_INF)
            inter = bc + m
            m_t = jnp.maximum(inter, jnp.max(dm, axis=-1, keepdims=True))
            w_d = jnp.exp(dm - m_t)
            w_in = jnp.exp(inter - m_t)
            q = q_scr[rows, cols]
            k = k_scr[rows, cols]
            v = v_scr[rows, cols]
            qb = q.astype(BF16)
            kb = k.astype(BF16)
            sqk = lax.dot_general(qb, kb, (((1,), (1,)), ((), ())), preferred_element_type=F32) * w_d
            cmat = c_scr[h]
            nvec = n_scr[h]
            num = (w_in * jnp.dot(qb, cmat.astype(BF16), preferred_element_type=F32)
                   + jnp.dot(sqk.astype(BF16), v.astype(BF16), preferred_element_type=F32))
            den = w_in * jnp.sum(q * nvec, axis=-1, keepdims=True) + jnp.sum(sqk, axis=-1, keepdims=True)
            h_scr[rows, cols] = num / jnp.maximum(jnp.abs(den), jnp.exp(-m_t))
            b_l = bc[chunk - 1:chunk, :]
            g_r = b_l - br + ir
            m_new = jnp.maximum(b_l + m, jnp.max(g_r, axis=-1, keepdims=True))
            decay = jnp.exp(b_l + m - m_new)
            w_g = jnp.exp(b_l - bc + ic - m_new)
            c_scr[h] = decay * cmat + lax.dot_general(kb, (w_g * v).astype(BF16), (((0,), (0,)), ((), ())),
                                                      preferred_element_type=F32)
            n_scr[h] = decay * nvec + jnp.sum(w_g * k, axis=0, keepdims=True)
            m_scr[h] = m_new
        return carry

    if n_chunks == 1:
        scan_chunk(0, 0)
    else:
        lax.fori_loop(0, n_chunks, scan_chunk, 0)

    for h in range(N_HEADS):
        cols = slice(h * hd, (h + 1) * hd)
        hb = h_scr[:, cols] * jax.nn.sigmoid(ob_ref[:, cols])
        hb = _rms(hb) * mg_ref[:, cols] + skip_ref[:, cols] * xch_scr[:, cols]
        bo_ref[:, cols] = (hb * _silu(zb_ref[:, cols])).astype(BF16)

    @pl.when(step == pl.num_programs(1) - 1)
    def _state_out():
        c1_ref[...] = c_scr[...]
        n1_ref[...] = n_scr[...]
        m1_ref[...] = m_scr[...]


def _mlstm(rest, conv0, c0, n0, m0, w, t, chunk, n_valid):
    bn, s, _ = rest.shape
    hd = HEAD_DIM
    tok = lambda col: pl.BlockSpec((None, t, GROUP_W), lambda b, j, col=col: (b, j, col))
    per_b = lambda *shape: pl.BlockSpec((None,) + shape, lambda b, j: (b,) + (0,) * len(shape))
    full = lambda *shape: pl.BlockSpec(shape, lambda b, j: (0,) * len(shape))
    state_specs = [per_b(N_HEADS, hd, hd), per_b(N_HEADS, 1, hd), per_b(N_HEADS, 1, 1)]
    return pl.pallas_call(
        functools.partial(_mlstm_kernel, t=t, chunk=chunk, n_valid=n_valid),
        grid=(bn, s // t),
        in_specs=[tok(1), tok(2), tok(3), per_b(CONV_W - 1, GROUP_W)] + state_specs + [
            full(CONV_W, GROUP_W), full(1, GROUP_W),
            full(N_HEADS, hd, hd), full(N_HEADS, hd, hd), full(N_HEADS, hd, hd),
            full(3 * GROUP_W, 2 * N_HEADS), full(2 * N_HEADS, 3 * GROUP_W),
            full(1, 2 * N_HEADS), full(2 * N_HEADS, 1),
            full(1, GROUP_W), full(1, GROUP_W)],
        out_specs=[pl.BlockSpec((None, t, GROUP_W), lambda b, j: (b, j, 0))] + state_specs,
        out_shape=[jax.ShapeDtypeStruct((bn, s, GROUP_W), BF16),
                   jax.ShapeDtypeStruct((bn, N_HEADS, hd, hd), F32),
                   jax.ShapeDtypeStruct((bn, N_HEADS, 1, hd), F32),
                   jax.ShapeDtypeStruct((bn, N_HEADS, 1, 1), F32)],
        scratch_shapes=[pltpu.VMEM((t + 8, GROUP_W), F32)] + [pltpu.VMEM((t, GROUP_W), F32)] * 5 + [
            pltpu.VMEM((t, 2 * N_HEADS), F32),
            pltpu.VMEM((t // chunk, 2 * N_HEADS, chunk), F32),
            pltpu.VMEM((N_HEADS, hd, hd), F32),
            pltpu.VMEM((N_HEADS, 1, hd), F32),
            pltpu.VMEM((N_HEADS, 1, 1), F32)],
        compiler_params=_cparams(("parallel", "arbitrary")),
        name="mlstm",
    )(rest, rest, rest, conv0, c0, n0, m0,
      w["conv_w"], w["conv_b"], w["wq"], w["wk"], w["wv"], w["wif"], w["wif_t"], w["bif_row"], w["bif_col"],
      w["mlstm_g"], w["mlstm_skip"])


def _outproj_kernel(a_ref, bo_ref, x_ref, p_ref, woa_ref, wob_ref, pg_ref, wgate_ref, wple_ref, plg_ref, y_ref):
    y = (jnp.dot(a_ref[...], woa_ref[...], preferred_element_type=F32)
         + jnp.dot(bo_ref[...], wob_ref[...], preferred_element_type=F32))
    x1 = x_ref[...] + _rms(y) * pg_ref[...]
    gate = jax.nn.sigmoid(jnp.dot(x1.astype(BF16), wgate_ref[...], preferred_element_type=F32))
    pe = jnp.dot(p_ref[...].astype(BF16), wple_ref[...], preferred_element_type=F32)
    y_ref[...] = x1 + _rms(gate * pe) * plg_ref[...]


def _outproj(a, bo, x, p, w, tm):
    m, d = x.shape
    row = lambda width: pl.BlockSpec((tm, width), lambda i: (i, 0))
    full = lambda arr: pl.BlockSpec(arr.shape, lambda i: (0,) * arr.ndim)
    consts = [w["wo_a"], w["wo_b"], w["post_g"], w["ple_gate"], w["ple_w"], w["ple_g"]]
    return pl.pallas_call(
        _outproj_kernel,
        grid=(m // tm,),
        in_specs=[row(GROUP_W), row(GROUP_W), row(d), row(p.shape[1])] + [full(c) for c in consts],
        out_specs=row(d),
        out_shape=jax.ShapeDtypeStruct((m, d), F32),
        compiler_params=_cparams(("parallel",)),
        name="outproj",
    )(a, bo, x, p, *consts)


def _layer_weights(l, norm_pre_g, norm_post_g, w_in, lam_q1, lam_k1, lam_q2, lam_k2, attn_norm_g, conv_w, conv_b,
                   w_q_b, w_k_b, w_v_b, w_if, b_if, mlstm_norm_g, mlstm_skip, w_out, ple_w, ple_gate_w, ple_norm_g):
    row = lambda v: v.reshape(1, -1)
    wo = w_out[l].astype(BF16)
    return {
        "pre_g": row(norm_pre_g[l]), "w_in": w_in[l].astype(BF16),
        "lam": jnp.stack([lam_q1[l], lam_k1[l], lam_q2[l], lam_k2[l]]).astype(F32),
        "attn_g": row(attn_norm_g[l]),
        "conv_w": conv_w[l], "conv_b": row(conv_b[l]),
        "wq": w_q_b[l].astype(BF16), "wk": w_k_b[l].astype(BF16), "wv": w_v_b[l].astype(BF16),
        "wif": w_if[l].astype(BF16), "wif_t": w_if[l].T.astype(BF16),
        "bif_row": row(b_if[l]), "bif_col": b_if[l].reshape(-1, 1),
        "mlstm_g": row(mlstm_norm_g[l]), "mlstm_skip": row(mlstm_skip[l]),
        "wo_a": wo[:GROUP_W], "wo_b": wo[GROUP_W:], "post_g": row(norm_post_g[l]),
        "ple_gate": ple_gate_w[l].astype(BF16), "ple_w": ple_w[l].astype(BF16), "ple_g": row(ple_norm_g[l]),
    }


def _prompt_layer(x, p, w, lam_init):
    b, s, d = x.shape
    x2 = x.reshape(b * s, d)
    q, k, v, rest, kb, vt = _inproj(x2, w["pre_g"], w["w_in"], tm=PROMPT_ROWS, kv_block=FLASH_BLOCK)
    rest3 = rest.reshape(b, s, 4 * GROUP_W)
    a = _flash(q.reshape(b, s, GROUP_W), kb.reshape(b, s, GROUP_W), vt, rest3,
               w["lam"], w["attn_g"], lam_init, tq=FLASH_BLOCK)
    hd = HEAD_DIM
    bo, c1, n1, m1 = _mlstm(rest3, jnp.zeros((b, CONV_W - 1, GROUP_W), F32),
                            jnp.zeros((b, N_HEADS, hd, hd), F32), jnp.zeros((b, N_HEADS, 1, hd), F32),
                            jnp.zeros((b, N_HEADS, 1, 1), F32), w, t=PROMPT_ROWS, chunk=PROMPT_CHUNK,
                            n_valid=PROMPT_CHUNK)
    y = _outproj(a.reshape(b * s, GROUP_W), bo.reshape(b * s, GROUP_W), x2, p.reshape(b * s, -1), w,
                 tm=PROMPT_ROWS)
    state = (k.reshape(b, s, N_HEADS, hd), v.reshape(b, s, N_HEADS, hd),
             rest3[:, s - (CONV_W - 1):, GROUP_W:2 * GROUP_W],
             c1, n1.reshape(b, N_HEADS, hd), m1.reshape(b, N_HEADS))
    return y.reshape(b, s, d), state


def _sample_layer(x, p, n_valid, cache_k, cache_v, page_table, conv0, c0, n0, m0, w, lam_init):
    db, r, d = x.shape
    hd = HEAD_DIM
    assert CONV_W - 1 <= n_valid <= r
    x2 = x.reshape(db * r, d)
    q, k, v, rest = _inproj(x2, w["pre_g"], w["w_in"], tm=db * r)
    rest3 = rest.reshape(db, r, 4 * GROUP_W)
    k3 = k.reshape(db, r, GROUP_W)
    v3 = v.reshape(db, r, GROUP_W)
    a = _paged(q.reshape(db, r, GROUP_W), k3, v3, rest3, cache_k, cache_v, page_table, w["lam"], w["attn_g"],
               lam_init, n_valid)
    bo, c1, n1, m1 = _mlstm(rest3, conv0, c0, n0.reshape(db, N_HEADS, 1, hd), m0.reshape(db, N_HEADS, 1, 1),
                            w, t=r, chunk=r, n_valid=n_valid)
    y = _outproj(a.reshape(db * r, GROUP_W), bo.reshape(db * r, GROUP_W), x2, p.reshape(db * r, -1), w, tm=db * r)
    state = (k3[:, :n_valid].reshape(db, n_valid, N_HEADS, hd), v3[:, :n_valid].reshape(db, n_valid, N_HEADS, hd),
             rest3[:, n_valid - (CONV_W - 1):n_valid, GROUP_W:2 * GROUP_W],
             c1, n1.reshape(db, N_HEADS, hd), m1.reshape(db, N_HEADS))
    return y.reshape(db, r, d), state


def kernel(x_prompt, x_sample, cache_k, cache_v, state_conv, state_C, state_n, state_m, page_table,
           p_prompt, p_sample, norm_pre_g, norm_post_g, w_in, lam_q1, lam_k1, lam_q2, lam_k2,
           attn_norm_g, conv_w, conv_b, w_q_b, w_k_b, w_v_b, w_if, b_if, mlstm_norm_g, mlstm_skip,
           w_out, ple_w, ple_gate_w, ple_norm_g):
    depth = w_in.shape[0]
    t_dec = x_sample.shape[1]
    pad_rows = lambda a: jnp.pad(a, ((0, 0), (0, SAMPLE_ROWS - t_dec), (0, 0)))
    n_pool, page = cache_k.shape[1], cache_k.shape[2]
    cache_k2 = cache_k.reshape(depth * n_pool, page * N_HEADS, HEAD_DIM)
    cache_v2 = cache_v.reshape(depth * n_pool, page * N_HEADS, HEAD_DIM)
    xp = x_prompt
    xs = pad_rows(x_sample)
    outs_p, outs_s = [], []
    for l in range(depth):
        w = _layer_weights(l, norm_pre_g, norm_post_g, w_in, lam_q1, lam_k1, lam_q2, lam_k2, attn_norm_g,
                           conv_w, conv_b, w_q_b, w_k_b, w_v_b, w_if, b_if, mlstm_norm_g, mlstm_skip,
                           w_out, ple_w, ple_gate_w, ple_norm_g)
        lam_init = 0.8 - 0.6 * math.exp(-0.3 * l)
        xp, st_p = _prompt_layer(xp, p_prompt[l], w, lam_init)
        outs_p.append(st_p)
        xs, st_s = _sample_layer(xs, pad_rows(p_sample[l]), t_dec, cache_k2, cache_v2, page_table + l * n_pool,
                                 state_conv[l], state_C[l], state_n[l], state_m[l], w, lam_init)
        outs_s.append(st_s)
    stack = lambda outs, i: jnp.stack([o[i] for o in outs])
    return ((xp, xs[:, :t_dec]) + tuple(stack(outs_p, i) for i in range(6))
            + tuple(stack(outs_s, i) for i in range(6)))
```

```python
import functools
import math

import jax
import jax.numpy as jnp
from jax import lax
from jax.experimental import pallas as pl
from jax.experimental.pallas import tpu as pltpu

F32 = jnp.float32
BF16 = jnp.bfloat16
EPS = 1e-6
NEG_INF = float("-inf")

N_HEADS = 4
HEAD_DIM = 128
QK_DIM = 64
GROUP_W = N_HEADS * HEAD_DIM
CONV_W = 4
SAMPLE_ROWS = 8
PROMPT_CHUNK = 128
PROMPT_ROWS = 512
FLASH_BLOCK = 512
VT_ROWS = HEAD_DIM + 16
Q_SCALE = (QK_DIM ** -0.5) * math.log2(math.e)
PAGES_PER_STEP = 8

V7X_VMEM_LIMIT = 48 * 1024 * 1024


def _cparams(sem):
    return pltpu.CompilerParams(dimension_semantics=sem, vmem_limit_bytes=V7X_VMEM_LIMIT)


def _silu(x):
    return x * jax.nn.sigmoid(x)


def _rms(x):
    return x * lax.rsqrt(jnp.mean(x * x, axis=-1, keepdims=True) + EPS)


def _inproj_kernel(x_ref, g_ref, w_ref, q_ref, k_ref, v_ref, rest_ref, *maybe_flash_refs, kv_block):
    hb = (_rms(x_ref[...]) * g_ref[...]).astype(BF16)

    def proj(c):
        return jnp.dot(hb, w_ref[:, c * GROUP_W:(c + 1) * GROUP_W], preferred_element_type=F32)

    q_ref[...] = (proj(0) * Q_SCALE).astype(BF16)
    k = proj(1)
    k_ref[...] = k
    v = proj(2)
    v_ref[...] = v
    if kv_block:
        kb_ref, vt_ref = maybe_flash_refs
        kb_ref[...] = k.astype(BF16)
        for h in range(N_HEADS):
            for c in range(v.shape[0] // kv_block):
                blk = v[c * kv_block:(c + 1) * kv_block, h * HEAD_DIM:(h + 1) * HEAD_DIM]
                vt_ref[h, c, 0:HEAD_DIM, :] = blk.T.astype(BF16)
                vt_ref[h, c, HEAD_DIM:VT_ROWS, :] = jnp.ones((VT_ROWS - HEAD_DIM, kv_block), BF16)
    for c in range(3, 7):
        rest_ref[:, (c - 3) * GROUP_W:(c - 2) * GROUP_W] = proj(c)


def _inproj(x, g, w_bf, tm, kv_block=0):
    m, d = x.shape
    n_in = w_bf.shape[1]
    row = lambda w: pl.BlockSpec((tm, w), lambda i: (i, 0))
    out_specs = [row(GROUP_W)] * 3 + [row(4 * GROUP_W)]
    out_shape = [jax.ShapeDtypeStruct((m, GROUP_W), BF16),
                 jax.ShapeDtypeStruct((m, GROUP_W), F32),
                 jax.ShapeDtypeStruct((m, GROUP_W), F32),
                 jax.ShapeDtypeStruct((m, 4 * GROUP_W), F32)]
    if kv_block:
        out_specs += [row(GROUP_W),
                      pl.BlockSpec((N_HEADS, tm // kv_block, VT_ROWS, kv_block), lambda i: (0, i, 0, 0))]
        out_shape += [jax.ShapeDtypeStruct((m, GROUP_W), BF16),
                      jax.ShapeDtypeStruct((N_HEADS, m // kv_block, VT_ROWS, kv_block), BF16)]
    return pl.pallas_call(
        functools.partial(_inproj_kernel, kv_block=kv_block),
        grid=(m // tm,),
        in_specs=[row(d), pl.BlockSpec((1, d), lambda i: (0, 0)),
                  pl.BlockSpec((d, n_in), lambda i: (0, 0))],
        out_specs=out_specs,
        out_shape=out_shape,
        compiler_params=_cparams(("parallel",)),
        name="inproj",
    )(x, g, w_bf)


def _split_maps(q):
    q = q.astype(F32)
    lane = lax.broadcasted_iota(jnp.int32, q.shape, 1)
    zero = jnp.zeros_like(q)
    both = jnp.concatenate([jnp.where(lane < QK_DIM, q, zero), jnp.where(lane >= QK_DIM, q, zero)], axis=0)
    return both.astype(BF16)


def _lambda(lam_ref, lam_init):
    l1 = jnp.sum(lam_ref[0:1, :] * lam_ref[1:2, :], axis=-1, keepdims=True)
    l2 = jnp.sum(lam_ref[2:3, :] * lam_ref[3:4, :], axis=-1, keepdims=True)
    return jnp.exp(l1) - jnp.exp(l2) + lam_init


def _attn_finish(acc, l, r, lam, g, za, lam_init):
    o = acc[0:r] / l[0:r] - lam * (acc[r:2 * r] / l[r:2 * r])
    o = _rms(o) * g * (1.0 - lam_init)
    return o * _silu(za)


def _flash_kernel(q_ref, k_ref, vt_ref, za_ref, lam_ref, g_ref, o_ref, qq_scr, m_scr, acc_scr,
                  s0_scr, s1_scr, *, tq, lam_init):
    i = pl.program_id(2)
    qq_scr[...] = _split_maps(q_ref[...])
    m_scr[...] = jnp.full(m_scr.shape, NEG_INF, F32)
    acc_scr[...] = jnp.zeros(acc_scr.shape, F32)

    def scores(j, dst):
        k0 = pl.multiple_of(j * tq, tq)
        dst[...] = lax.dot_general(k_ref[pl.ds(k0, tq), :], qq_scr[...], (((1,), (1,)), ((), ())),
                                   preferred_element_type=F32)

    def consume(src, j, masked):
        s = src[...]
        if masked:
            key = lax.broadcasted_iota(jnp.int32, s.shape, 0)
            qry = lax.broadcasted_iota(jnp.int32, s.shape, 1)
            s = jnp.where(key <= jnp.where(qry >= tq, qry - tq, qry), s, NEG_INF)
        m_prev = m_scr[...]
        m_new = jnp.maximum(m_prev, jnp.max(s, axis=0, keepdims=True))
        p = jnp.exp2(s - m_new)
        alpha = jnp.exp2(m_prev - m_new)
        acc_scr[...] = alpha * acc_scr[...] + jnp.dot(vt_ref[j], p.astype(BF16), preferred_element_type=F32)
        m_scr[...] = m_new

    scores(0, s0_scr)

    def pair(jj, carry):
        j = 2 * jj
        scores(j + 1, s1_scr)
        consume(s0_scr, j, False)
        scores(j + 2, s0_scr)
        consume(s1_scr, j + 1, False)
        return carry

    lax.fori_loop(0, i // 2, pair, 0)

    @pl.when(i % 2 == 1)
    def _odd_tail():
        scores(i, s1_scr)
        consume(s0_scr, i - 1, False)
        consume(s1_scr, i, True)

    @pl.when(i % 2 == 0)
    def _even_tail():
        consume(s0_scr, i, True)

    lam = _lambda(lam_ref, lam_init)
    o_t = acc_scr[0:HEAD_DIM, :] / acc_scr[HEAD_DIM:HEAD_DIM + 1, :]
    o = (o_t[:, 0:tq] - lam * o_t[:, tq:2 * tq]).T
    o = _rms(o) * g_ref[...] * (1.0 - lam_init)
    o_ref[...] = (o * _silu(za_ref[...])).astype(BF16)


def _flash(q, kb, vt, rest, lamp, attn_g, lam_init, tq):
    b, s, _ = q.shape
    nblk = s // tq
    blk = lambda: pl.BlockSpec((None, tq, HEAD_DIM), lambda bi, h, i: (bi, i, h))
    return pl.pallas_call(
        functools.partial(_flash_kernel, tq=tq, lam_init=lam_init),
        grid=(b, N_HEADS, nblk),
        in_specs=[blk(),
                  pl.BlockSpec((None, s, HEAD_DIM), lambda bi, h, i: (bi, 0, h)),
                  pl.BlockSpec((None, nblk, VT_ROWS, tq), lambda bi, h, i: (h, bi, 0, 0)),
                  blk(),
                  pl.BlockSpec((4, QK_DIM), lambda bi, h, i: (0, 0)),
                  pl.BlockSpec((1, HEAD_DIM), lambda bi, h, i: (0, h))],
        out_specs=blk(),
        out_shape=jax.ShapeDtypeStruct((b, s, GROUP_W), BF16),
        scratch_shapes=[pltpu.VMEM((2 * tq, HEAD_DIM), BF16),
                        pltpu.VMEM((1, 2 * tq), F32),
                        pltpu.VMEM((VT_ROWS, 2 * tq), F32),
                        pltpu.VMEM((tq, 2 * tq), F32),
                        pltpu.VMEM((tq, 2 * tq), F32)],
        compiler_params=_cparams(("parallel", "parallel", "arbitrary")),
        name="flash_prompt",
    )(q, kb, vt, rest, lamp, attn_g)


def _paged_kernel(pt_ref, q_ref, *refs, n_pages, n_valid, lam_init):
    del pt_ref
    kp = refs[:n_pages]
    vp = refs[n_pages:2 * n_pages]
    kn_ref, vn_ref, za_ref, lam_ref, g_ref, o_ref, qq_scr, bias_scr, m_scr, l_scr, acc_scr = refs[2 * n_pages:]
    r = SAMPLE_ROWS
    hr = 2 * r
    pw = bias_scr.shape[1]
    step = pl.program_id(1)

    @pl.when(step == 0)
    def _init():
        for h in range(N_HEADS):
            qq_scr[h * hr:(h + 1) * hr, :] = _split_maps(q_ref[:, h * HEAD_DIM:(h + 1) * HEAD_DIM])
        row_head = lax.broadcasted_iota(jnp.int32, bias_scr.shape, 0) // hr
        col_head = lax.broadcasted_iota(jnp.int32, bias_scr.shape, 1) % N_HEADS
        bias_scr[...] = jnp.where(row_head == col_head, 0.0, NEG_INF).astype(F32)
        m_scr[...] = jnp.full(m_scr.shape, NEG_INF, F32)
        l_scr[...] = jnp.zeros(l_scr.shape, F32)
        acc_scr[...] = jnp.zeros(acc_scr.shape, F32)

    qq = qq_scr[...]
    bias = bias_scr[...]
    s = jnp.concatenate(
        [lax.dot_general(qq, kp[j][...].astype(BF16), (((1,), (1,)), ((), ())),
                         preferred_element_type=F32) + bias for j in range(n_pages)], axis=1)
    m_prev = m_scr[...]
    m_new = jnp.maximum(m_prev, jnp.max(s, axis=-1, keepdims=True))
    p = jnp.exp2(s - m_new)
    alpha = jnp.exp2(m_prev - m_new)
    l_scr[...] = alpha * l_scr[...] + jnp.sum(p, axis=-1, keepdims=True)
    p = p.astype(BF16)
    pv = jnp.dot(p[:, 0:pw], vp[0][...].astype(BF16), preferred_element_type=F32)
    for j in range(1, n_pages):
        pv = pv + jnp.dot(p[:, j * pw:(j + 1) * pw], vp[j][...].astype(BF16), preferred_element_type=F32)
    acc_scr[...] = alpha * acc_scr[...] + pv
    m_scr[...] = m_new

    @pl.when(step == pl.num_programs(1) - 1)
    def _finish():
        lam = _lambda(lam_ref, lam_init)
        row = lax.broadcasted_iota(jnp.int32, (2 * r, 1), 0)
        tok = jnp.where(row >= r, row - r, row)
        for h in range(N_HEADS):
            cols = slice(h * HEAD_DIM, (h + 1) * HEAD_DIM)
            rows = slice(h * hr, (h + 1) * hr)
            qf = qq_scr[rows, :].astype(F32)
            kn = kn_ref[:, cols]
            vn = vn_ref[:, cols]
            s_new = [jnp.where(tok >= j, jnp.sum(qf * kn[j:j + 1, :], axis=-1, keepdims=True), NEG_INF)
                     for j in range(n_valid)]
            m_prev = m_scr[rows, :]
            m_new = m_prev
            for sj in s_new:
                m_new = jnp.maximum(m_new, sj)
            alpha = jnp.exp2(m_prev - m_new)
            l = alpha * l_scr[rows, :]
            acc = alpha * acc_scr[rows, :]
            for j, sj in enumerate(s_new):
                pj = jnp.exp2(sj - m_new)
                l = l + pj
                acc = acc + pj * vn[j:j + 1, :]
            o_ref[:, cols] = _attn_finish(acc, l, r, lam, g_ref[:, cols], za_ref[:, cols], lam_init).astype(BF16)


def _paged(q, k_new, v_new, rest, cache_k, cache_v, page_table, lamp, attn_g, lam_init, n_valid):
    db = q.shape[0]
    r = SAMPLE_ROWS
    n_pages_total = page_table.shape[1]
    page_rows = cache_k.shape[1]
    p = PAGES_PER_STEP
    seq = lambda w=GROUP_W: pl.BlockSpec((None, r, w), lambda b, s, pt: (b, 0, 0))

    def page_spec(j):
        return pl.BlockSpec((None, page_rows, HEAD_DIM), lambda b, s, pt, j=j: (pt[b, s * p + j], 0, 0))

    grid_spec = pltpu.PrefetchScalarGridSpec(
        num_scalar_prefetch=1,
        grid=(db, n_pages_total // p),
        in_specs=[seq()] + [page_spec(j) for j in range(p)] * 2 + [seq(), seq(), seq(),
                  pl.BlockSpec((4, QK_DIM), lambda b, s, pt: (0, 0)),
                  pl.BlockSpec((1, GROUP_W), lambda b, s, pt: (0, 0))],
        out_specs=seq(),
        scratch_shapes=[pltpu.VMEM((N_HEADS * 2 * r, HEAD_DIM), BF16),
                        pltpu.VMEM((N_HEADS * 2 * r, page_rows), F32),
                        pltpu.VMEM((N_HEADS * 2 * r, 1), F32),
                        pltpu.VMEM((N_HEADS * 2 * r, 1), F32),
                        pltpu.VMEM((N_HEADS * 2 * r, HEAD_DIM), F32)])
    return pl.pallas_call(
        functools.partial(_paged_kernel, n_pages=p, n_valid=n_valid, lam_init=lam_init),
        grid_spec=grid_spec,
        out_shape=jax.ShapeDtypeStruct((db, r, GROUP_W), BF16),
        compiler_params=_cparams(("parallel", "arbitrary")),
        name="paged_attn",
    )(page_table, q, *([cache_k] * p), *([cache_v] * p), k_new, v_new, rest, lamp, attn_g)


def _log_sigmoid(x):
    return jnp.minimum(x, 0.0) - jnp.log1p(jnp.exp(-jnp.abs(x)))


def _mlstm_kernel(xb_ref, ob_ref, zb_ref, conv0_ref, c0_ref, n0_ref, m0_ref,
                  cw_ref, cb_ref, wq_ref, wk_ref, wv_ref, wif_ref, wift_ref, bifr_ref, bifc_ref,
                  mg_ref, skip_ref,
                  bo_ref, c1_ref, n1_ref, m1_ref,
                  xpad, xch_scr, q_scr, k_scr, v_scr, h_scr, gc_scr, gr_scr, c_scr, n_scr, m_scr,
                  *, t, chunk, n_valid):
    step = pl.program_id(1)
    n_chunks = t // chunk
    hd = HEAD_DIM

    @pl.when(step == 0)
    def _init():
        xpad[0:8, :] = jnp.zeros((8, GROUP_W), F32)
        xpad[8 - (CONV_W - 1):8, :] = conv0_ref[...]
        c_scr[...] = c0_ref[...]
        n_scr[...] = n0_ref[...]
        m_scr[...] = m0_ref[...]

    x = xb_ref[...]
    xpad[8:8 + t, :] = x
    xc = cb_ref[...] + cw_ref[CONV_W - 1:CONV_W, :] * x
    for j in range(CONV_W - 1):
        xc = xc + cw_ref[j:j + 1, :] * xpad[8 - (CONV_W - 1) + j:8 - (CONV_W - 1) + j + t, :]
    tail = xpad[t:t + 8, :]
    xpad[0:8, :] = tail
    xch = _silu(xc)
    xch_scr[...] = xch

    gcol = jnp.zeros((t, 2 * N_HEADS), F32) + bifr_ref[...]
    grow = jnp.zeros((2 * N_HEADS, t), F32) + bifc_ref[...]
    for h in range(N_HEADS):
        cols = slice(h * hd, (h + 1) * hd)
        xh = xch[:, cols].astype(BF16)
        qh = jnp.dot(xh, wq_ref[h], preferred_element_type=F32)
        kh = jnp.dot(xh, wk_ref[h], preferred_element_type=F32) * (hd ** -0.5)
        vh = jnp.dot(x[:, cols].astype(BF16), wv_ref[h], preferred_element_type=F32)
        q_scr[:, cols] = qh
        k_scr[:, cols] = kh
        v_scr[:, cols] = vh
        for part, val in enumerate((qh, kh, vh)):
            off = (3 * h + part) * hd
            vb = val.astype(BF16)
            gcol = gcol + jnp.dot(vb, wif_ref[off:off + hd, :], preferred_element_type=F32)
            grow = grow + lax.dot_general(wift_ref[:, off:off + hd], vb, (((1,), (1,)), ((), ())),
                                          preferred_element_type=F32)

    lane = lax.broadcasted_iota(jnp.int32, gcol.shape, 1)
    gcol = jnp.where(lane < N_HEADS, gcol, _log_sigmoid(gcol))
    sub = lax.broadcasted_iota(jnp.int32, grow.shape, 0)
    grow = jnp.where(sub < N_HEADS, grow, _log_sigmoid(grow))
    if n_valid < chunk:
        pos_c = lax.broadcasted_iota(jnp.int32, gcol.shape, 0) % chunk
        gcol = jnp.where(pos_c >= n_valid, jnp.where(lane < N_HEADS, NEG_INF, 0.0), gcol)
        pos_r = lax.broadcasted_iota(jnp.int32, grow.shape, 1) % chunk
        grow = jnp.where(pos_r >= n_valid, jnp.where(sub < N_HEADS, NEG_INF, 0.0), grow)
    gc_scr[...] = gcol
    for c in range(n_chunks):
        gr_scr[c] = grow[:, c * chunk:(c + 1) * chunk]

    ri = lax.broadcasted_iota(jnp.int32, (chunk, chunk), 0)
    ci = lax.broadcasted_iota(jnp.int32, (chunk, chunk), 1)
    causal = ci <= ri
    tril = jnp.where(causal, 1.0, 0.0).astype(F32)
    triu = jnp.where(ri <= ci, 1.0, 0.0).astype(F32)
    lane_c = lax.broadcasted_iota(jnp.int32, (chunk, 2 * N_HEADS), 1)
    sub_r = lax.broadcasted_iota(jnp.int32, (2 * N_HEADS, chunk), 0)

    def scan_chunk(c, carry):
        r0 = pl.multiple_of(c * chunk, chunk)
        rows = pl.ds(r0, chunk)
        gc = gc_scr[rows, :]
        gr = gr_scr[c]
        b_c = jnp.dot(tril, jnp.where(lane_c >= N_HEADS, gc, 0.0), preferred_element_type=F32,
                      precision=lax.Precision.HIGHEST)
        b_r = jnp.dot(jnp.where(sub_r >= N_HEADS, gr, 0.0), triu, preferred_element_type=F32,
                      precision=lax.Precision.HIGHEST)
        for h in range(N_HEADS):
            cols = slice(h * hd, (h + 1) * hd)
            bc = b_c[:, N_HEADS + h:N_HEADS + h + 1]
            br = b_r[N_HEADS + h:N_HEADS + h + 1, :]
            ic = gc[:, h:h + 1]
            ir = gr[h:h + 1, :]
            m = m_scr[h]
            dm = jnp.where(causal, bc - br + ir, NEG_INF)
            inter = bc + m
            m_t = jnp.maximum(inter, jnp.max(dm, axis=-1, keepdims=True))
            w_d = jnp.exp(dm - m_t)
            w_in = jnp.exp(inter - m_t)
            q = q_scr[rows, cols]
            k = k_scr[rows, cols]
            v = v_scr[rows, cols]
            qb = q.astype(BF16)
            kb = k.astype(BF16)
            sqk = lax.dot_general(qb, kb, (((1,), (1,)), ((), ())), preferred_element_type=F32) * w_d
            cmat = c_scr[h]
            nvec = n_scr[h]
            num = (w_in * jnp.dot(qb, cmat.astype(BF16), preferred_element_type=F32)
                   + jnp.dot(sqk.astype(BF16), v.astype(BF16), preferred_element_type=F32))
            den = w_in * jnp.sum(q * nvec, axis=-1, keepdims=True) + jnp.sum(sqk, axis=-1, keepdims=True)
            h_scr[rows, cols] = num / jnp.maximum(jnp.abs(den), jnp.exp(-m_t))
            b_l = bc[chunk - 1:chunk, :]
            g_r = b_l - br + ir
            m_new = jnp.maximum(b_l + m, jnp.max(g_r, axis=-1, keepdims=True))
            decay = jnp.exp(b_l + m - m_new)
            w_g = jnp.exp(b_l - bc + ic - m_new)
            c_scr[h] = decay * cmat + lax.dot_general(kb, (w_g * v).astype(BF16), (((0,), (0,)), ((), ())),
                                                      preferred_element_type=F32)
            n_scr[h] = decay * nvec + jnp.sum(w_g * k, axis=0, keepdims=True)
            m_scr[h] = m_new
        return carry

    if n_chunks == 1:
        scan_chunk(0, 0)
    else:
        lax.fori_loop(0, n_chunks, scan_chunk, 0)

    for h in range(N_HEADS):
        cols = slice(h * hd, (h + 1) * hd)
        hb = h_scr[:, cols] * jax.nn.sigmoid(ob_ref[:, cols])
        hb = _rms(hb) * mg_ref[:, cols] + skip_ref[:, cols] * xch_scr[:, cols]
        bo_ref[:, cols] = (hb * _silu(zb_ref[:, cols])).astype(BF16)

    @pl.when(step == pl.num_programs(1) - 1)
    def _state_out():
        c1_ref[...] = c_scr[...]
        n1_ref[...] = n_scr[...]
        m1_ref[...] = m_scr[...]


def _mlstm(rest, conv0, c0, n0, m0, w, t, chunk, n_valid):
    bn, s, _ = rest.shape
    hd = HEAD_DIM
    tok = lambda col: pl.BlockSpec((None, t, GROUP_W), lambda b, j, col=col: (b, j, col))
    per_b = lambda *shape: pl.BlockSpec((None,) + shape, lambda b, j: (b,) + (0,) * len(shape))
    full = lambda *shape: pl.BlockSpec(shape, lambda b, j: (0,) * len(shape))
    state_specs = [per_b(N_HEADS, hd, hd), per_b(N_HEADS, 1, hd), per_b(N_HEADS, 1, 1)]
    return pl.pallas_call(
        functools.partial(_mlstm_kernel, t=t, chunk=chunk, n_valid=n_valid),
        grid=(bn, s // t),
        in_specs=[tok(1), tok(2), tok(3), per_b(CONV_W - 1, GROUP_W)] + state_specs + [
            full(CONV_W, GROUP_W), full(1, GROUP_W),
            full(N_HEADS, hd, hd), full(N_HEADS, hd, hd), full(N_HEADS, hd, hd),
            full(3 * GROUP_W, 2 * N_HEADS), full(2 * N_HEADS, 3 * GROUP_W),
            full(1, 2 * N_HEADS), full(2 * N_HEADS, 1),
            full(1, GROUP_W), full(1, GROUP_W)],
        out_specs=[pl.BlockSpec((None, t, GROUP_W), lambda b, j: (b, j, 0))] + state_specs,
        out_shape=[jax.ShapeDtypeStruct((bn, s, GROUP_W), BF16),
                   jax.ShapeDtypeStruct((bn, N_HEADS, hd, hd), F32),
                   jax.ShapeDtypeStruct((bn, N_HEADS, 1, hd), F32),
                   jax.ShapeDtypeStruct((bn, N_HEADS, 1, 1), F32)],
        scratch_shapes=[pltpu.VMEM((t + 8, GROUP_W), F32)] + [pltpu.VMEM((t, GROUP_W), F32)] * 5 + [
            pltpu.VMEM((t, 2 * N_HEADS), F32),
            pltpu.VMEM((t // chunk, 2 * N_HEADS, chunk), F32),
            pltpu.VMEM((N_HEADS, hd, hd), F32),
            pltpu.VMEM((N_HEADS, 1, hd), F32),
            pltpu.VMEM((N_HEADS, 1, 1), F32)],
        compiler_params=_cparams(("parallel", "arbitrary")),
        name="mlstm",
    )(rest, rest, rest, conv0, c0, n0, m0,
      w["conv_w"], w["conv_b"], w["wq"], w["wk"], w["wv"], w["wif"], w["wif_t"], w["bif_row"], w["bif_col"],
      w["mlstm_g"], w["mlstm_skip"])


def _outproj_kernel(a_ref, bo_ref, x_ref, p_ref, woa_ref, wob_ref, pg_ref, wgate_ref, wple_ref, plg_ref, y_ref):
    y = (jnp.dot(a_ref[...], woa_ref[...], preferred_element_type=F32)
         + jnp.dot(bo_ref[...], wob_ref[...], preferred_element_type=F32))
    x1 = x_ref[...] + _rms(y) * pg_ref[...]
    gate = jax.nn.sigmoid(jnp.dot(x1.astype(BF16), wgate_ref[...], preferred_element_type=F32))
    pe = jnp.dot(p_ref[...].astype(BF16), wple_ref[...], preferred_element_type=F32)
    y_ref[...] = x1 + _rms(gate * pe) * plg_ref[...]


def _outproj(a, bo, x, p, w, tm):
    m, d = x.shape
    row = lambda width: pl.BlockSpec((tm, width), lambda i: (i, 0))
    full = lambda arr: pl.BlockSpec(arr.shape, lambda i: (0,) * arr.ndim)
    consts = [w["wo_a"], w["wo_b"], w["post_g"], w["ple_gate"], w["ple_w"], w["ple_g"]]
    return pl.pallas_call(
        _outproj_kernel,
        grid=(m // tm,),
        in_specs=[row(GROUP_W), row(GROUP_W), row(d), row(p.shape[1])] + [full(c) for c in consts],
        out_specs=row(d),
        out_shape=jax.ShapeDtypeStruct((m, d), F32),
        compiler_params=_cparams(("parallel",)),
        name="outproj",
    )(a, bo, x, p, *consts)


def _layer_weights(l, norm_pre_g, norm_post_g, w_in, lam_q1, lam_k1, lam_q2, lam_k2, attn_norm_g, conv_w, conv_b,
                   w_q_b, w_k_b, w_v_b, w_if, b_if, mlstm_norm_g, mlstm_skip, w_out, ple_w, ple_gate_w, ple_norm_g):
    row = lambda v: v.reshape(1, -1)
    wo = w_out[l].astype(BF16)
    return {
        "pre_g": row(norm_pre_g[l]), "w_in": w_in[l].astype(BF16),
        "lam": jnp.stack([lam_q1[l], lam_k1[l], lam_q2[l], lam_k2[l]]).astype(F32),
        "attn_g": row(attn_norm_g[l]),
        "conv_w": conv_w[l], "conv_b": row(conv_b[l]),
        "wq": w_q_b[l].astype(BF16), "wk": w_k_b[l].astype(BF16), "wv": w_v_b[l].astype(BF16),
        "wif": w_if[l].astype(BF16), "wif_t": w_if[l].T.astype(BF16),
        "bif_row": row(b_if[l]), "bif_col": b_if[l].reshape(-1, 1),
        "mlstm_g": row(mlstm_norm_g[l]), "mlstm_skip": row(mlstm_skip[l]),
        "wo_a": wo[:GROUP_W], "wo_b": wo[GROUP_W:], "post_g": row(norm_post_g[l]),
        "ple_gate": ple_gate_w[l].astype(BF16), "ple_w": ple_w[l].astype(BF16), "ple_g": row(ple_norm_g[l]),
    }


def _prompt_layer(x, p, w, lam_init):
    b, s, d = x.shape
    x2 = x.reshape(b * s, d)
    q, k, v, rest, kb, vt = _inproj(x2, w["pre_g"], w["w_in"], tm=PROMPT_ROWS, kv_block=FLASH_BLOCK)
    rest3 = rest.reshape(b, s, 4 * GROUP_W)
    a = _flash(q.reshape(b, s, GROUP_W), kb.reshape(b, s, GROUP_W), vt, rest3,
               w["lam"], w["attn_g"], lam_init, tq=FLASH_BLOCK)
    hd = HEAD_DIM
    bo, c1, n1, m1 = _mlstm(rest3, jnp.zeros((b, CONV_W - 1, GROUP_W), F32),
                            jnp.zeros((b, N_HEADS, hd, hd), F32), jnp.zeros((b, N_HEADS, 1, hd), F32),
                            jnp.zeros((b, N_HEADS, 1, 1), F32), w, t=PROMPT_ROWS, chunk=PROMPT_CHUNK,
                            n_valid=PROMPT_CHUNK)
    y = _outproj(a.reshape(b * s, GROUP_W), bo.reshape(b * s, GROUP_W), x2, p.reshape(b * s, -1), w,
                 tm=PROMPT_ROWS)
    state = (k.reshape(b, s, N_HEADS, hd), v.reshape(b, s, N_HEADS, hd),
             rest3[:, s - (CONV_W - 1):, GROUP_W:2 * GROUP_W],
             c1, n1.reshape(b, N_HEADS, hd), m1.reshape(b, N_HEADS))
    return y.reshape(b, s, d), state


def _sample_layer(x, p, n_valid, cache_k, cache_v, page_table, conv0, c0, n0, m0, w, lam_init):
    db, r, d = x.shape
    hd = HEAD_DIM
    assert CONV_W - 1 <= n_valid <= r
    x2 = x.reshape(db * r, d)
    q, k, v, rest = _inproj(x2, w["pre_g"], w["w_in"], tm=db * r)
    rest3 = rest.reshape(db, r, 4 * GROUP_W)
    k3 = k.reshape(db, r, GROUP_W)
    v3 = v.reshape(db, r, GROUP_W)
    a = _paged(q.reshape(db, r, GROUP_W), k3, v3, rest3, cache_k, cache_v, page_table, w["lam"], w["attn_g"],
               lam_init, n_valid)
    bo, c1, n1, m1 = _mlstm(rest3, conv0, c0, n0.reshape(db, N_HEADS, 1, hd), m0.reshape(db, N_HEADS, 1, 1),
                            w, t=r, chunk=r, n_valid=n_valid)
    y = _outproj(a.reshape(db * r, GROUP_W), bo.reshape(db * r, GROUP_W), x2, p.reshape(db * r, -1), w, tm=db * r)
    state = (k3[:, :n_valid].reshape(db, n_valid, N_HEADS, hd), v3[:, :n_valid].reshape(db, n_valid, N_HEADS, hd),
             rest3[:, n_valid - (CONV_W - 1):n_valid, GROUP_W:2 * GROUP_W],
             c1, n1.reshape(db, N_HEADS, hd), m1.reshape(db, N_HEADS))
    return y.reshape(db, r, d), state


def kernel(x_prompt, x_sample, cache_k, cache_v, state_conv, state_C, state_n, state_m, page_table,
           p_prompt, p_sample, norm_pre_g, norm_post_g, w_in, lam_q1, lam_k1, lam_q2, lam_k2,
           attn_norm_g, conv_w, conv_b, w_q_b, w_k_b, w_v_b, w_if, b_if, mlstm_norm_g, mlstm_skip,
           w_out, ple_w, ple_gate_w, ple_norm_g):
    depth = w_in.shape[0]
    t_dec = x_sample.shape[1]
    pad_rows = lambda a: jnp.pad(a, ((0, 0), (0, SAMPLE_ROWS - t_dec), (0, 0)))
    n_pool, page = cache_k.shape[1], cache_k.shape[2]
    cache_k2 = cache_k.reshape(depth * n_pool, page * N_HEADS, HEAD_DIM)
    cache_v2 = cache_v.reshape(depth * n_pool, page * N_HEADS, HEAD_DIM)
    xp = x_prompt
    xs = pad_rows(x_sample)
    outs_p, outs_s = [], []
    for l in range(depth):
        w = _layer_weights(l, norm_pre_g, norm_post_g, w_in, lam_q1, lam_k1, lam_q2, lam_k2, attn_norm_g,
                           conv_w, conv_b, w_q_b, w_k_b, w_v_b, w_if, b_if, mlstm_norm_g, mlstm_skip,
                           w_out, ple_w, ple_gate_w, ple_norm_g)
        lam_init = 0.8 - 0.6 * math.exp(-0.3 * l)
        xp, st_p = _prompt_layer(xp, p_prompt[l], w, lam_init)
        outs_p.append(st_p)
        xs, st_s = _sample_layer(xs, pad_rows(p_sample[l]), t_dec, cache_k2, cache_v2, page_table + l * n_pool,
                                 state_conv[l], state_C[l], state_n[l], state_m[l], w, lam_init)
        outs_s.append(st_s)
    stack = lambda outs, i: jnp.stack([o[i] for o in outs])
    return ((xp, xs[:, :t_dec]) + tuple(stack(outs_p, i) for i in range(6))
            + tuple(stack(outs_s, i) for i in range(6)))
```

```python
import functools
import math

import jax
import jax.numpy as jnp
from jax import lax
from jax.experimental import pallas as pl
from jax.experimental.pallas import tpu as pltpu

F32 = jnp.float32
BF16 = jnp.bfloat16
EPS = 1e-6
NEG_INF = float("-inf")

N_HEADS = 4
HEAD_DIM = 128
QK_DIM = 64
GROUP_W = N_HEADS * HEAD_DIM
CONV_W = 4
SAMPLE_ROWS = 8
PROMPT_CHUNK = 128
PROMPT_ROWS = 512
FLASH_BLOCK = 512
VT_ROWS = HEAD_DIM + 16
Q_SCALE = (QK_DIM ** -0.5) * math.log2(math.e)
PAGES_PER_STEP = 16

V7X_VMEM_LIMIT = 48 * 1024 * 1024


def _cparams(sem):
    return pltpu.CompilerParams(dimension_semantics=sem, vmem_limit_bytes=V7X_VMEM_LIMIT)


def _silu(x):
    return x * jax.nn.sigmoid(x)


def _rms(x):
    return x * lax.rsqrt(jnp.mean(x * x, axis=-1, keepdims=True) + EPS)


def _inproj_kernel(x_ref, g_ref, w_ref, q_ref, k_ref, v_ref, rest_ref, *maybe_flash_refs, kv_block):
    hb = (_rms(x_ref[...]) * g_ref[...]).astype(BF16)

    def proj(c):
        return jnp.dot(hb, w_ref[:, c * GROUP_W:(c + 1) * GROUP_W], preferred_element_type=F32)

    q_ref[...] = (proj(0) * Q_SCALE).astype(BF16)
    k = proj(1)
    k_ref[...] = k
    v = proj(2)
    v_ref[...] = v
    if kv_block:
        kb_ref, vt_ref = maybe_flash_refs
        kb_ref[...] = k.astype(BF16)
        for h in range(N_HEADS):
            for c in range(v.shape[0] // kv_block):
                blk = v[c * kv_block:(c + 1) * kv_block, h * HEAD_DIM:(h + 1) * HEAD_DIM]
                vt_ref[h, c, 0:HEAD_DIM, :] = blk.T.astype(BF16)
                vt_ref[h, c, HEAD_DIM:VT_ROWS, :] = jnp.ones((VT_ROWS - HEAD_DIM, kv_block), BF16)
    for c in range(3, 7):
        rest_ref[:, (c - 3) * GROUP_W:(c - 2) * GROUP_W] = proj(c)


def _inproj(x, g, w_bf, tm, kv_block=0):
    m, d = x.shape
    n_in = w_bf.shape[1]
    row = lambda w: pl.BlockSpec((tm, w), lambda i: (i, 0))
    out_specs = [row(GROUP_W)] * 3 + [row(4 * GROUP_W)]
    out_shape = [jax.ShapeDtypeStruct((m, GROUP_W), BF16),
                 jax.ShapeDtypeStruct((m, GROUP_W), F32),
                 jax.ShapeDtypeStruct((m, GROUP_W), F32),
                 jax.ShapeDtypeStruct((m, 4 * GROUP_W), F32)]
    if kv_block:
        out_specs += [row(GROUP_W),
                      pl.BlockSpec((N_HEADS, tm // kv_block, VT_ROWS, kv_block), lambda i: (0, i, 0, 0))]
        out_shape += [jax.ShapeDtypeStruct((m, GROUP_W), BF16),
                      jax.ShapeDtypeStruct((N_HEADS, m // kv_block, VT_ROWS, kv_block), BF16)]
    return pl.pallas_call(
        functools.partial(_inproj_kernel, kv_block=kv_block),
        grid=(m // tm,),
        in_specs=[row(d), pl.BlockSpec((1, d), lambda i: (0, 0)),
                  pl.BlockSpec((d, n_in), lambda i: (0, 0))],
        out_specs=out_specs,
        out_shape=out_shape,
        compiler_params=_cparams(("parallel",)),
        name="inproj",
    )(x, g, w_bf)


def _split_maps(q):
    q = q.astype(F32)
    lane = lax.broadcasted_iota(jnp.int32, q.shape, 1)
    zero = jnp.zeros_like(q)
    both = jnp.concatenate([jnp.where(lane < QK_DIM, q, zero), jnp.where(lane >= QK_DIM, q, zero)], axis=0)
    return both.astype(BF16)


def _lambda(lam_ref, lam_init):
    l1 = jnp.sum(lam_ref[0:1, :] * lam_ref[1:2, :], axis=-1, keepdims=True)
    l2 = jnp.sum(lam_ref[2:3, :] * lam_ref[3:4, :], axis=-1, keepdims=True)
    return jnp.exp(l1) - jnp.exp(l2) + lam_init


def _flash_kernel(q_ref, k_ref, vt_ref, za_ref, lam_ref, g_ref, o_ref, qq_scr, m_scr, acc_scr,
                  s0_scr, s1_scr, *, tq, lam_init):
    i = pl.program_id(2)
    qq_scr[...] = _split_maps(q_ref[...])
    m_scr[...] = jnp.full(m_scr.shape, NEG_INF, F32)
    acc_scr[...] = jnp.zeros(acc_scr.shape, F32)

    def scores(j, dst):
        k0 = pl.multiple_of(j * tq, tq)
        dst[...] = lax.dot_general(k_ref[pl.ds(k0, tq), :], qq_scr[...], (((1,), (1,)), ((), ())),
                                   preferred_element_type=F32)

    def consume(src, j, masked):
        s = src[...]
        if masked:
            key = lax.broadcasted_iota(jnp.int32, s.shape, 0)
            qry = lax.broadcasted_iota(jnp.int32, s.shape, 1)
            s = jnp.where(key <= jnp.where(qry >= tq, qry - tq, qry), s, NEG_INF)
        m_prev = m_scr[...]
        m_new = jnp.maximum(m_prev, jnp.max(s, axis=0, keepdims=True))
        p = jnp.exp2(s - m_new)
        alpha = jnp.exp2(m_prev - m_new)
        acc_scr[...] = alpha * acc_scr[...] + jnp.dot(vt_ref[j], p.astype(BF16), preferred_element_type=F32)
        m_scr[...] = m_new

    scores(0, s0_scr)

    def pair(jj, carry):
        j = 2 * jj
        scores(j + 1, s1_scr)
        consume(s0_scr, j, False)
        scores(j + 2, s0_scr)
        consume(s1_scr, j + 1, False)
        return carry

    lax.fori_loop(0, i // 2, pair, 0)

    @pl.when(i % 2 == 1)
    def _odd_tail():
        scores(i, s1_scr)
        consume(s0_scr, i - 1, False)
        consume(s1_scr, i, True)

    @pl.when(i % 2 == 0)
    def _even_tail():
        consume(s0_scr, i, True)

    lam = _lambda(lam_ref, lam_init)
    o_t = acc_scr[0:HEAD_DIM, :] / acc_scr[HEAD_DIM:HEAD_DIM + 1, :]
    o = (o_t[:, 0:tq] - lam * o_t[:, tq:2 * tq]).T
    o = _rms(o) * g_ref[...] * (1.0 - lam_init)
    o_ref[...] = (o * _silu(za_ref[...])).astype(BF16)


def _flash(q, kb, vt, rest, lamp, attn_g, lam_init, tq):
    b, s, _ = q.shape
    nblk = s // tq
    blk = lambda: pl.BlockSpec((None, tq, HEAD_DIM), lambda bi, h, i: (bi, i, h))
    return pl.pallas_call(
        functools.partial(_flash_kernel, tq=tq, lam_init=lam_init),
        grid=(b, N_HEADS, nblk),
        in_specs=[blk(),
                  pl.BlockSpec((None, s, HEAD_DIM), lambda bi, h, i: (bi, 0, h)),
                  pl.BlockSpec((None, nblk, VT_ROWS, tq), lambda bi, h, i: (h, bi, 0, 0)),
                  blk(),
                  pl.BlockSpec((4, QK_DIM), lambda bi, h, i: (0, 0)),
                  pl.BlockSpec((1, HEAD_DIM), lambda bi, h, i: (0, h))],
        out_specs=blk(),
        out_shape=jax.ShapeDtypeStruct((b, s, GROUP_W), BF16),
        scratch_shapes=[pltpu.VMEM((2 * tq, HEAD_DIM), BF16),
                        pltpu.VMEM((1, 2 * tq), F32),
                        pltpu.VMEM((VT_ROWS, 2 * tq), F32),
                        pltpu.VMEM((tq, 2 * tq), F32),
                        pltpu.VMEM((tq, 2 * tq), F32)],
        compiler_params=_cparams(("parallel", "parallel", "arbitrary")),
        name="flash_prompt",
    )(q, kb, vt, rest, lamp, attn_g)


def _paged_kernel(pt_ref, q_ref, *refs, n_pages, n_valid, lam_init):
    del pt_ref
    kp = refs[:n_pages]
    vp = refs[n_pages:2 * n_pages]
    kn_ref, vn_ref, za_ref, lam_ref, g_ref, o_ref, qq_scr, bias_scr, m_scr, l_scr, acc_scr = refs[2 * n_pages:]
    r = SAMPLE_ROWS
    half = r // 2
    pw = bias_scr.shape[1]
    step = pl.program_id(1)

    @pl.when(step == 0)
    def _init():
        row = lax.broadcasted_iota(jnp.int32, (r, HEAD_DIM), 0)
        lane = lax.broadcasted_iota(jnp.int32, (r, HEAD_DIM), 1)
        for h in range(N_HEADS):
            q = q_ref[:, h * HEAD_DIM:(h + 1) * HEAD_DIM].astype(F32)
            q_map1 = jnp.where(lane < QK_DIM, q, 0.0)
            q_map2 = pltpu.roll(jnp.where(lane >= QK_DIM, q, 0.0), half, axis=0)
            qq_scr[h * r:(h + 1) * r, :] = jnp.where(row < half, q_map1, q_map2)
        row_head = lax.broadcasted_iota(jnp.int32, bias_scr.shape, 0) // r
        col_head = lax.broadcasted_iota(jnp.int32, bias_scr.shape, 1) % N_HEADS
        bias_scr[...] = jnp.where(row_head == col_head, 0.0, NEG_INF).astype(F32)
        m_scr[...] = jnp.full(m_scr.shape, NEG_INF, F32)
        l_scr[...] = jnp.zeros(l_scr.shape, F32)
        acc_scr[...] = jnp.zeros(acc_scr.shape, F32)

    qq = qq_scr[...].astype(BF16)
    bias = bias_scr[...]
    s = jnp.concatenate(
        [lax.dot_general(qq, kp[j][...].astype(BF16), (((1,), (1,)), ((), ())),
                         preferred_element_type=F32) + bias for j in range(n_pages)], axis=1)
    m_prev = m_scr[...]
    m_new = jnp.maximum(m_prev, jnp.max(s, axis=-1, keepdims=True))
    p = jnp.exp2(s - m_new)
    alpha = jnp.exp2(m_prev - m_new)
    l_scr[...] = alpha * l_scr[...] + jnp.sum(p, axis=-1, keepdims=True)
    p = p.astype(BF16)
    pv = jnp.dot(p[:, 0:pw], vp[0][...].astype(BF16), preferred_element_type=F32)
    for j in range(1, n_pages):
        pv = pv + jnp.dot(p[:, j * pw:(j + 1) * pw], vp[j][...].astype(BF16), preferred_element_type=F32)
    acc_scr[...] = alpha * acc_scr[...] + pv
    m_scr[...] = m_new

    @pl.when(step == pl.num_programs(1) - 1)
    def _finish():
        lam = _lambda(lam_ref, lam_init)
        row = lax.broadcasted_iota(jnp.int32, (r, 1), 0)
        tok = jnp.where(row >= half, row - half, row)
        for h in range(N_HEADS):
            cols = slice(h * HEAD_DIM, (h + 1) * HEAD_DIM)
            rows = slice(h * r, (h + 1) * r)
            qf = qq_scr[rows, :]
            kn = kn_ref[:, cols]
            vn = vn_ref[:, cols]
            s_new = [jnp.where(tok >= j, jnp.sum(qf * kn[j:j + 1, :], axis=-1, keepdims=True), NEG_INF)
                     for j in range(n_valid)]
            m_prev = m_scr[rows, :]
            m_new = m_prev
            for sj in s_new:
                m_new = jnp.maximum(m_new, sj)
            alpha = jnp.exp2(m_prev - m_new)
            l = alpha * l_scr[rows, :]
            acc = alpha * acc_scr[rows, :]
            for j, sj in enumerate(s_new):
                pj = jnp.exp2(sj - m_new)
                l = l + pj
                acc = acc + pj * vn[j:j + 1, :]
            o_maps = acc / l
            o = o_maps - lam * pltpu.roll(o_maps, half, axis=0)
            o = _rms(o) * g_ref[:, cols] * (1.0 - lam_init)
            o_ref[:, cols] = (o * _silu(za_ref[:, cols])).astype(BF16)


def _paged(q, k_new, v_new, rest, cache_k, cache_v, page_table, lamp, attn_g, lam_init, n_valid):
    db = q.shape[0]
    r = SAMPLE_ROWS
    assert 2 * n_valid <= r, "each head packs the tokens of both maps into SAMPLE_ROWS query rows"
    n_pages_total = page_table.shape[1]
    page_rows = cache_k.shape[1]
    p = PAGES_PER_STEP
    seq = lambda w=GROUP_W: pl.BlockSpec((None, r, w), lambda b, s, pt: (b, 0, 0))

    def page_spec(j):
        return pl.BlockSpec((None, page_rows, HEAD_DIM), lambda b, s, pt, j=j: (pt[b, s * p + j], 0, 0))

    grid_spec = pltpu.PrefetchScalarGridSpec(
        num_scalar_prefetch=1,
        grid=(db, n_pages_total // p),
        in_specs=[seq()] + [page_spec(j) for j in range(p)] * 2 + [seq(), seq(), seq(),
                  pl.BlockSpec((4, QK_DIM), lambda b, s, pt: (0, 0)),
                  pl.BlockSpec((1, GROUP_W), lambda b, s, pt: (0, 0))],
        out_specs=seq(),
        scratch_shapes=[pltpu.VMEM((N_HEADS * r, HEAD_DIM), F32),
                        pltpu.VMEM((N_HEADS * r, page_rows), F32),
                        pltpu.VMEM((N_HEADS * r, 1), F32),
                        pltpu.VMEM((N_HEADS * r, 1), F32),
                        pltpu.VMEM((N_HEADS * r, HEAD_DIM), F32)])
    return pl.pallas_call(
        functools.partial(_paged_kernel, n_pages=p, n_valid=n_valid, lam_init=lam_init),
        grid_spec=grid_spec,
        out_shape=jax.ShapeDtypeStruct((db, r, GROUP_W), BF16),
        compiler_params=_cparams(("parallel", "arbitrary")),
        name="paged_attn",
    )(page_table, q, *([cache_k] * p), *([cache_v] * p), k_new, v_new, rest, lamp, attn_g)


def _log_sigmoid(x):
    return jnp.minimum(x, 0.0) - jnp.log1p(jnp.exp(-jnp.abs(x)))


def _mlstm_kernel(xb_ref, ob_ref, zb_ref, conv0_ref, c0_ref, n0_ref, m0_ref,
                  cw_ref, cb_ref, wq_ref, wk_ref, wv_ref, wif_ref, wift_ref, bifr_ref, bifc_ref,
                  mg_ref, skip_ref,
                  bo_ref, c1_ref, n1_ref, m1_ref,
                  xpad, xch_scr, q_scr, k_scr, v_scr, h_scr, gc_scr, gr_scr, c_scr, n_scr, m_scr,
                  *, t, chunk, n_valid):
    step = pl.program_id(1)
    n_chunks = t // chunk
    hd = HEAD_DIM

    @pl.when(step == 0)
    def _init():
        xpad[0:8, :] = jnp.zeros((8, GROUP_W), F32)
        xpad[8 - (CONV_W - 1):8, :] = conv0_ref[...]
        c_scr[...] = c0_ref[...]
        n_scr[...] = n0_ref[...]
        m_scr[...] = m0_ref[...]

    x = xb_ref[...]
    xpad[8:8 + t, :] = x
    xc = cb_ref[...] + cw_ref[CONV_W - 1:CONV_W, :] * x
    for j in range(CONV_W - 1):
        xc = xc + cw_ref[j:j + 1, :] * xpad[8 - (CONV_W - 1) + j:8 - (CONV_W - 1) + j + t, :]
    tail = xpad[t:t + 8, :]
    xpad[0:8, :] = tail
    xch = _silu(xc)
    xch_scr[...] = xch

    gcol = jnp.zeros((t, 2 * N_HEADS), F32) + bifr_ref[...]
    grow = jnp.zeros((2 * N_HEADS, t), F32) + bifc_ref[...]
    for h in range(N_HEADS):
        cols = slice(h * hd, (h + 1) * hd)
        xh = xch[:, cols].astype(BF16)
        qh = jnp.dot(xh, wq_ref[h], preferred_element_type=F32)
        kh = jnp.dot(xh, wk_ref[h], preferred_element_type=F32) * (hd ** -0.5)
        vh = jnp.dot(x[:, cols].astype(BF16), wv_ref[h], preferred_element_type=F32)
        q_scr[:, cols] = qh
        k_scr[:, cols] = kh
        v_scr[:, cols] = vh
        for part, val in enumerate((qh, kh, vh)):
            off = (3 * h + part) * hd
            vb = val.astype(BF16)
            gcol = gcol + jnp.dot(vb, wif_ref[off:off + hd, :], preferred_element_type=F32)
            grow = grow + lax.dot_general(wift_ref[:, off:off + hd], vb, (((1,), (1,)), ((), ())),
                                          preferred_element_type=F32)

    lane = lax.broadcasted_iota(jnp.int32, gcol.shape, 1)
    gcol = jnp.where(lane < N_HEADS, gcol, _log_sigmoid(gcol))
    sub = lax.broadcasted_iota(jnp.int32, grow.shape, 0)
    grow = jnp.where(sub < N_HEADS, grow, _log_sigmoid(grow))
    if n_valid < chunk:
        pos_c = lax.broadcasted_iota(jnp.int32, gcol.shape, 0) % chunk
        gcol = jnp.where(pos_c >= n_valid, jnp.where(lane < N_HEADS, NEG_INF, 0.0), gcol)
        pos_r = lax.broadcasted_iota(jnp.int32, grow.shape, 1) % chunk
        grow = jnp.where(pos_r >= n_valid, jnp.where(sub < N_HEADS, NEG_INF, 0.0), grow)
    gc_scr[...] = gcol
    for c in range(n_chunks):
        gr_scr[c] = grow[:, c * chunk:(c + 1) * chunk]

    ri = lax.broadcasted_iota(jnp.int32, (chunk, chunk), 0)
    ci = lax.broadcasted_iota(jnp.int32, (chunk, chunk), 1)
    causal = ci <= ri
    tril = jnp.where(causal, 1.0, 0.0).astype(F32)
    triu = jnp.where(ri <= ci, 1.0, 0.0).astype(F32)
    lane_c = lax.broadcasted_iota(jnp.int32, (chunk, 2 * N_HEADS), 1)
    sub_r = lax.broadcasted_iota(jnp.int32, (2 * N_HEADS, chunk), 0)

    def scan_chunk(c, carry):
        r0 = pl.multiple_of(c * chunk, chunk)
        rows = pl.ds(r0, chunk)
        gc = gc_scr[rows, :]
        gr = gr_scr[c]
        b_c = jnp.dot(tril, jnp.where(lane_c >= N_HEADS, gc, 0.0), preferred_element_type=F32,
                      precision=lax.Precision.HIGHEST)
        b_r = jnp.dot(jnp.where(sub_r >= N_HEADS, gr, 0.0), triu, preferred_element_type=F32,
                      precision=lax.Precision.HIGHEST)
        for h in range(N_HEADS):
            cols = slice(h * hd, (h + 1) * hd)
            bc = b_c[:, N_HEADS + h:N_HEADS + h + 1]
            br = b_r[N_HEADS + h:N_HEADS + h + 1, :]
            ic = gc[:, h:h + 1]
            ir = gr[h:h + 1, :]
            m = m_scr[h]
            dm = jnp.where(causal, bc - br + ir, NEG_INF)
            inter = bc + m
            m_t = jnp.maximum(inter, jnp.max(dm, axis=-1, keepdims=True))
            w_d = jnp.exp(dm - m_t)
            w_in = jnp.exp(inter - m_t)
            q = q_scr[rows, cols]
            k = k_scr[rows, cols]
            v = v_scr[rows, cols]
            qb = q.astype(BF16)
            kb = k.astype(BF16)
            sqk = lax.dot_general(qb, kb, (((1,), (1,)), ((), ())), preferred_element_type=F32) * w_d
            cmat = c_scr[h]
            nvec = n_scr[h]
            num = (w_in * jnp.dot(qb, cmat.astype(BF16), preferred_element_type=F32)
                   + jnp.dot(sqk.astype(BF16), v.astype(BF16), preferred_element_type=F32))
            den = w_in * jnp.sum(q * nvec, axis=-1, keepdims=True) + jnp.sum(sqk, axis=-1, keepdims=True)
            h_scr[rows, cols] = num / jnp.maximum(jnp.abs(den), jnp.exp(-m_t))
            b_l = bc[chunk - 1:chunk, :]
            g_r = b_l - br + ir
            m_new = jnp.maximum(b_l + m, jnp.max(g_r, axis=-1, keepdims=True))
            decay = jnp.exp(b_l + m - m_new)
            w_g = jnp.exp(b_l - bc + ic - m_new)
            c_scr[h] = decay * cmat + lax.dot_general(kb, (w_g * v).astype(BF16), (((0,), (0,)), ((), ())),
                                                      preferred_element_type=F32)
            n_scr[h] = decay * nvec + jnp.sum(w_g * k, axis=0, keepdims=True)
            m_scr[h] = m_new
        return carry

    if n_chunks == 1:
        scan_chunk(0, 0)
    else:
        lax.fori_loop(0, n_chunks, scan_chunk, 0)

    for h in range(N_HEADS):
        cols = slice(h * hd, (h + 1) * hd)
        hb = h_scr[:, cols] * jax.nn.sigmoid(ob_ref[:, cols])
        hb = _rms(hb) * mg_ref[:, cols] + skip_ref[:, cols] * xch_scr[:, cols]
        bo_ref[:, cols] = (hb * _silu(zb_ref[:, cols])).astype(BF16)

    @pl.when(step == pl.num_programs(1) - 1)
    def _state_out():
        c1_ref[...] = c_scr[...]
        n1_ref[...] = n_scr[...]
        m1_ref[...] = m_scr[...]


def _mlstm(rest, conv0, c0, n0, m0, w, t, chunk, n_valid):
    bn, s, _ = rest.shape
    hd = HEAD_DIM
    tok = lambda col: pl.BlockSpec((None, t, GROUP_W), lambda b, j, col=col: (b, j, col))
    per_b = lambda *shape: pl.BlockSpec((None,) + shape, lambda b, j: (b,) + (0,) * len(shape))
    full = lambda *shape: pl.BlockSpec(shape, lambda b, j: (0,) * len(shape))
    state_specs = [per_b(N_HEADS, hd, hd), per_b(N_HEADS, 1, hd), per_b(N_HEADS, 1, 1)]
    return pl.pallas_call(
        functools.partial(_mlstm_kernel, t=t, chunk=chunk, n_valid=n_valid),
        grid=(bn, s // t),
        in_specs=[tok(1), tok(2), tok(3), per_b(CONV_W - 1, GROUP_W)] + state_specs + [
            full(CONV_W, GROUP_W), full(1, GROUP_W),
            full(N_HEADS, hd, hd), full(N_HEADS, hd, hd), full(N_HEADS, hd, hd),
            full(3 * GROUP_W, 2 * N_HEADS), full(2 * N_HEADS, 3 * GROUP_W),
            full(1, 2 * N_HEADS), full(2 * N_HEADS, 1),
            full(1, GROUP_W), full(1, GROUP_W)],
        out_specs=[pl.BlockSpec((None, t, GROUP_W), lambda b, j: (b, j, 0))] + state_specs,
        out_shape=[jax.ShapeDtypeStruct((bn, s, GROUP_W), BF16),
                   jax.ShapeDtypeStruct((bn, N_HEADS, hd, hd), F32),
                   jax.ShapeDtypeStruct((bn, N_HEADS, 1, hd), F32),
                   jax.ShapeDtypeStruct((bn, N_HEADS, 1, 1), F32)],
        scratch_shapes=[pltpu.VMEM((t + 8, GROUP_W), F32)] + [pltpu.VMEM((t, GROUP_W), F32)] * 5 + [
            pltpu.VMEM((t, 2 * N_HEADS), F32),
            pltpu.VMEM((t // chunk, 2 * N_HEADS, chunk), F32),
            pltpu.VMEM((N_HEADS, hd, hd), F32),
            pltpu.VMEM((N_HEADS, 1, hd), F32),
            pltpu.VMEM((N_HEADS, 1, 1), F32)],
        compiler_params=_cparams(("parallel", "arbitrary")),
        name="mlstm",
    )(rest, rest, rest, conv0, c0, n0, m0,
      w["conv_w"], w["conv_b"], w["wq"], w["wk"], w["wv"], w["wif"], w["wif_t"], w["bif_row"], w["bif_col"],
      w["mlstm_g"], w["mlstm_skip"])


def _outproj_kernel(a_ref, bo_ref, x_ref, p_ref, woa_ref, wob_ref, pg_ref, wgate_ref, wple_ref, plg_ref, y_ref):
    y = (jnp.dot(a_ref[...], woa_ref[...], preferred_element_type=F32)
         + jnp.dot(bo_ref[...], wob_ref[...], preferred_element_type=F32))
    x1 = x_ref[...] + _rms(y) * pg_ref[...]
    gate = jax.nn.sigmoid(jnp.dot(x1.astype(BF16), wgate_ref[...], preferred_element_type=F32))
    pe = jnp.dot(p_ref[...].astype(BF16), wple_ref[...], preferred_element_type=F32)
    y_ref[...] = x1 + _rms(gate * pe) * plg_ref[...]


def _outproj(a, bo, x, p, w, tm):
    m, d = x.shape
    row = lambda width: pl.BlockSpec((tm, width), lambda i: (i, 0))
    full = lambda arr: pl.BlockSpec(arr.shape, lambda i: (0,) * arr.ndim)
    consts = [w["wo_a"], w["wo_b"], w["post_g"], w["ple_gate"], w["ple_w"], w["ple_g"]]
    return pl.pallas_call(
        _outproj_kernel,
        grid=(m // tm,),
        in_specs=[row(GROUP_W), row(GROUP_W), row(d), row(p.shape[1])] + [full(c) for c in consts],
        out_specs=row(d),
        out_shape=jax.ShapeDtypeStruct((m, d), F32),
        compiler_params=_cparams(("parallel",)),
        name="outproj",
    )(a, bo, x, p, *consts)


def _layer_weights(l, norm_pre_g, norm_post_g, w_in, lam_q1, lam_k1, lam_q2, lam_k2, attn_norm_g, conv_w, conv_b,
                   w_q_b, w_k_b, w_v_b, w_if, b_if, mlstm_norm_g, mlstm_skip, w_out, ple_w, ple_gate_w, ple_norm_g):
    row = lambda v: v.reshape(1, -1)
    wo = w_out[l].astype(BF16)
    return {
        "pre_g": row(norm_pre_g[l]), "w_in": w_in[l].astype(BF16),
        "lam": jnp.stack([lam_q1[l], lam_k1[l], lam_q2[l], lam_k2[l]]).astype(F32),
        "attn_g": row(attn_norm_g[l]),
        "conv_w": conv_w[l], "conv_b": row(conv_b[l]),
        "wq": w_q_b[l].astype(BF16), "wk": w_k_b[l].astype(BF16), "wv": w_v_b[l].astype(BF16),
        "wif": w_if[l].astype(BF16), "wif_t": w_if[l].T.astype(BF16),
        "bif_row": row(b_if[l]), "bif_col": b_if[l].reshape(-1, 1),
        "mlstm_g": row(mlstm_norm_g[l]), "mlstm_skip": row(mlstm_skip[l]),
        "wo_a": wo[:GROUP_W], "wo_b": wo[GROUP_W:], "post_g": row(norm_post_g[l]),
        "ple_gate": ple_gate_w[l].astype(BF16), "ple_w": ple_w[l].astype(BF16), "ple_g": row(ple_norm_g[l]),
    }


def _prompt_layer(x, p, w, lam_init):
    b, s, d = x.shape
    x2 = x.reshape(b * s, d)
    q, k, v, rest, kb, vt = _inproj(x2, w["pre_g"], w["w_in"], tm=PROMPT_ROWS, kv_block=FLASH_BLOCK)
    rest3 = rest.reshape(b, s, 4 * GROUP_W)
    a = _flash(q.reshape(b, s, GROUP_W), kb.reshape(b, s, GROUP_W), vt, rest3,
               w["lam"], w["attn_g"], lam_init, tq=FLASH_BLOCK)
    hd = HEAD_DIM
    bo, c1, n1, m1 = _mlstm(rest3, jnp.zeros((b, CONV_W - 1, GROUP_W), F32),
                            jnp.zeros((b, N_HEADS, hd, hd), F32), jnp.zeros((b, N_HEADS, 1, hd), F32),
                            jnp.zeros((b, N_HEADS, 1, 1), F32), w, t=PROMPT_ROWS, chunk=PROMPT_CHUNK,
                            n_valid=PROMPT_CHUNK)
    y = _outproj(a.reshape(b * s, GROUP_W), bo.reshape(b * s, GROUP_W), x2, p.reshape(b * s, -1), w,
                 tm=PROMPT_ROWS)
    state = (k.reshape(b, s, N_HEADS, hd), v.reshape(b, s, N_HEADS, hd),
             rest3[:, s - (CONV_W - 1):, GROUP_W:2 * GROUP_W],
             c1, n1.reshape(b, N_HEADS, hd), m1.reshape(b, N_HEADS))
    return y.reshape(b, s, d), state


def _sample_layer(x, p, n_valid, cache_k, cache_v, page_table, conv0, c0, n0, m0, w, lam_init):
    db, r, d = x.shape
    hd = HEAD_DIM
    assert CONV_W - 1 <= n_valid <= r
    x2 = x.reshape(db * r, d)
    q, k, v, rest = _inproj(x2, w["pre_g"], w["w_in"], tm=db * r)
    rest3 = rest.reshape(db, r, 4 * GROUP_W)
    k3 = k.reshape(db, r, GROUP_W)
    v3 = v.reshape(db, r, GROUP_W)
    a = _paged(q.reshape(db, r, GROUP_W), k3, v3, rest3, cache_k, cache_v, page_table, w["lam"], w["attn_g"],
               lam_init, n_valid)
    bo, c1, n1, m1 = _mlstm(rest3, conv0, c0, n0.reshape(db, N_HEADS, 1, hd), m0.reshape(db, N_HEADS, 1, 1),
                            w, t=r, chunk=r, n_valid=n_valid)
    y = _outproj(a.reshape(db * r, GROUP_W), bo.reshape(db * r, GROUP_W), x2, p.reshape(db * r, -1), w, tm=db * r)
    state = (k3[:, :n_valid].reshape(db, n_valid, N_HEADS, hd), v3[:, :n_valid].reshape(db, n_valid, N_HEADS, hd),
             rest3[:, n_valid - (CONV_W - 1):n_valid, GROUP_W:2 * GROUP_W],
             c1, n1.reshape(db, N_HEADS, hd), m1.reshape(db, N_HEADS))
    return y.reshape(db, r, d), state


def kernel(x_prompt, x_sample, cache_k, cache_v, state_conv, state_C, state_n, state_m, page_table,
           p_prompt, p_sample, norm_pre_g, norm_post_g, w_in, lam_q1, lam_k1, lam_q2, lam_k2,
           attn_norm_g, conv_w, conv_b, w_q_b, w_k_b, w_v_b, w_if, b_if, mlstm_norm_g, mlstm_skip,
           w_out, ple_w, ple_gate_w, ple_norm_g):
    depth = w_in.shape[0]
    t_dec = x_sample.shape[1]
    pad_rows = lambda a: jnp.pad(a, ((0, 0), (0, SAMPLE_ROWS - t_dec), (0, 0)))
    n_pool, page = cache_k.shape[1], cache_k.shape[2]
    cache_k2 = cache_k.reshape(depth * n_pool, page * N_HEADS, HEAD_DIM)
    cache_v2 = cache_v.reshape(depth * n_pool, page * N_HEADS, HEAD_DIM)
    xp = x_prompt
    xs = pad_rows(x_sample)
    outs_p, outs_s = [], []
    for l in range(depth):
        w = _layer_weights(l, norm_pre_g, norm_post_g, w_in, lam_q1, lam_k1, lam_q2, lam_k2, attn_norm_g,
                           conv_w, conv_b, w_q_b, w_k_b, w_v_b, w_if, b_if, mlstm_norm_g, mlstm_skip,
                           w_out, ple_w, ple_gate_w, ple_norm_g)
        lam_init = 0.8 - 0.6 * math.exp(-0.3 * l)
        xp, st_p = _prompt_layer(xp, p_prompt[l], w, lam_init)
        outs_p.append(st_p)
        xs, st_s = _sample_layer(xs, pad_rows(p_sample[l]), t_dec, cache_k2, cache_v2, page_table + l * n_pool,
                                 state_conv[l], state_C[l], state_n[l], state_m[l], w, lam_init)
        outs_s.append(st_s)
    stack = lambda outs, i: jnp.stack([o[i] for o in outs])
    return ((xp, xs[:, :t_dec]) + tuple(stack(outs_p, i) for i in range(6))
            + tuple(stack(outs_s, i) for i in range(6)))
```

```python
import functools
import math

import jax
import jax.numpy as jnp
from jax import lax
from jax.experimental import pallas as pl
from jax.experimental.pallas import tpu as pltpu

F32 = jnp.float32
BF16 = jnp.bfloat16
EPS = 1e-6
NEG_INF = float("-inf")

N_HEADS = 4
HEAD_DIM = 128
QK_DIM = 64
GROUP_W = N_HEADS * HEAD_DIM
CONV_W = 4
SAMPLE_ROWS = 8
PROMPT_CHUNK = 128
PROMPT_ROWS = 512
FLASH_BLOCK = 512
PROMPT_GROUP = 1
SAMPLE_GROUP = 4
VT_ROWS = HEAD_DIM + 16
Q_SCALE = (QK_DIM ** -0.5) * math.log2(math.e)
PAGES_PER_STEP = 16

V7X_VMEM_LIMIT = 48 * 1024 * 1024


def _cparams(sem):
    return pltpu.CompilerParams(dimension_semantics=sem, vmem_limit_bytes=V7X_VMEM_LIMIT)


def _silu(x):
    return x * jax.nn.sigmoid(x)


def _rms(x):
    return x * lax.rsqrt(jnp.mean(x * x, axis=-1, keepdims=True) + EPS)


def _inproj_kernel(x_ref, g_ref, w_ref, q_ref, k_ref, v_ref, rest_ref, *maybe_flash_refs, kv_block):
    hb = (_rms(x_ref[...]) * g_ref[...]).astype(BF16)

    def proj(c):
        return jnp.dot(hb, w_ref[:, c * GROUP_W:(c + 1) * GROUP_W], preferred_element_type=F32)

    q_ref[...] = (proj(0) * Q_SCALE).astype(BF16)
    k = proj(1)
    v = proj(2)
    if not kv_block:
        k_ref[...] = k
        v_ref[...] = v
    else:
        for h in range(N_HEADS):
            cols = slice(h * HEAD_DIM, (h + 1) * HEAD_DIM)
            k_ref[pl.ds(h, k.shape[0], stride=N_HEADS), :] = k[:, cols]
            v_ref[pl.ds(h, v.shape[0], stride=N_HEADS), :] = v[:, cols]
        kb_ref, vt_ref = maybe_flash_refs
        kb_ref[...] = k.astype(BF16)
        for h in range(N_HEADS):
            for c in range(v.shape[0] // kv_block):
                blk = v[c * kv_block:(c + 1) * kv_block, h * HEAD_DIM:(h + 1) * HEAD_DIM]
                vt_ref[h, c, 0:HEAD_DIM, :] = blk.T.astype(BF16)
                vt_ref[h, c, HEAD_DIM:VT_ROWS, :] = jnp.ones((VT_ROWS - HEAD_DIM, kv_block), BF16)
    for c in range(3, 7):
        rest_ref[:, (c - 3) * GROUP_W:(c - 2) * GROUP_W] = proj(c)


def _inproj(x, g, w_bf, tm, kv_block=0):
    m, d = x.shape
    n_in = w_bf.shape[1]
    row = lambda w: pl.BlockSpec((tm, w), lambda i: (i, 0))
    out_specs = [row(GROUP_W)] * 3 + [row(4 * GROUP_W)]
    out_shape = [jax.ShapeDtypeStruct((m, GROUP_W), BF16),
                 jax.ShapeDtypeStruct((m, GROUP_W), F32),
                 jax.ShapeDtypeStruct((m, GROUP_W), F32),
                 jax.ShapeDtypeStruct((m, 4 * GROUP_W), F32)]
    if kv_block:
        for i_out in (1, 2):
            out_specs[i_out] = pl.BlockSpec((tm * N_HEADS, HEAD_DIM), lambda i: (i, 0))
            out_shape[i_out] = jax.ShapeDtypeStruct((m * N_HEADS, HEAD_DIM), F32)
        out_specs += [row(GROUP_W),
                      pl.BlockSpec((N_HEADS, tm // kv_block, VT_ROWS, kv_block), lambda i: (0, i, 0, 0))]
        out_shape += [jax.ShapeDtypeStruct((m, GROUP_W), BF16),
                      jax.ShapeDtypeStruct((N_HEADS, m // kv_block, VT_ROWS, kv_block), BF16)]
    return pl.pallas_call(
        functools.partial(_inproj_kernel, kv_block=kv_block),
        grid=(m // tm,),
        in_specs=[row(d), pl.BlockSpec((1, d), lambda i: (0, 0)),
                  pl.BlockSpec((d, n_in), lambda i: (0, 0))],
        out_specs=out_specs,
        out_shape=out_shape,
        compiler_params=_cparams(("parallel",)),
        name="inproj",
    )(x, g, w_bf)


def _split_maps(q):
    q = q.astype(F32)
    lane = lax.broadcasted_iota(jnp.int32, q.shape, 1)
    zero = jnp.zeros_like(q)
    both = jnp.concatenate([jnp.where(lane < QK_DIM, q, zero), jnp.where(lane >= QK_DIM, q, zero)], axis=0)
    return both.astype(BF16)


def _lambda(lam_ref, lam_init):
    l1 = jnp.sum(lam_ref[0:1, :] * lam_ref[1:2, :], axis=-1, keepdims=True)
    l2 = jnp.sum(lam_ref[2:3, :] * lam_ref[3:4, :], axis=-1, keepdims=True)
    return jnp.exp(l1) - jnp.exp(l2) + lam_init


def _flash_kernel(q_ref, k_ref, vt_ref, za_ref, lam_ref, g_ref, o_ref, qq_scr, m_scr, acc_scr,
                  s0_scr, s1_scr, *, tq, lam_init):
    i = pl.program_id(2)
    qq_scr[...] = _split_maps(q_ref[...])
    m_scr[...] = jnp.full(m_scr.shape, NEG_INF, F32)
    acc_scr[...] = jnp.zeros(acc_scr.shape, F32)

    def scores(j, dst):
        k0 = pl.multiple_of(j * tq, tq)
        dst[...] = lax.dot_general(k_ref[pl.ds(k0, tq), :], qq_scr[...], (((1,), (1,)), ((), ())),
                                   preferred_element_type=F32)

    def consume(src, j, masked):
        s = src[...]
        if masked:
            key = lax.broadcasted_iota(jnp.int32, s.shape, 0)
            qry = lax.broadcasted_iota(jnp.int32, s.shape, 1)
            s = jnp.where(key <= jnp.where(qry >= tq, qry - tq, qry), s, NEG_INF)
        m_prev = m_scr[...]
        m_new = jnp.maximum(m_prev, jnp.max(s, axis=0, keepdims=True))
        p = jnp.exp2(s - m_new)
        alpha = jnp.exp2(m_prev - m_new)
        acc_scr[...] = alpha * acc_scr[...] + jnp.dot(vt_ref[j], p.astype(BF16), preferred_element_type=F32)
        m_scr[...] = m_new

    scores(0, s0_scr)

    def pair(jj, carry):
        j = 2 * jj
        scores(j + 1, s1_scr)
        consume(s0_scr, j, False)
        scores(j + 2, s0_scr)
        consume(s1_scr, j + 1, False)
        return carry

    lax.fori_loop(0, i // 2, pair, 0)

    @pl.when(i % 2 == 1)
    def _odd_tail():
        scores(i, s1_scr)
        consume(s0_scr, i - 1, False)
        consume(s1_scr, i, True)

    @pl.when(i % 2 == 0)
    def _even_tail():
        consume(s0_scr, i, True)

    lam = _lambda(lam_ref, lam_init)
    o_t = acc_scr[0:HEAD_DIM, :] / acc_scr[HEAD_DIM:HEAD_DIM + 1, :]
    o = (o_t[:, 0:tq] - lam * o_t[:, tq:2 * tq]).T
    o = _rms(o) * g_ref[...] * (1.0 - lam_init)
    o_ref[...] = (o * _silu(za_ref[...])).astype(BF16)


def _flash(q, kb, vt, rest, lamp, attn_g, lam_init, tq):
    b, s, _ = q.shape
    nblk = s // tq
    blk = lambda: pl.BlockSpec((None, tq, HEAD_DIM), lambda bi, h, i: (bi, i, h))
    return pl.pallas_call(
        functools.partial(_flash_kernel, tq=tq, lam_init=lam_init),
        grid=(b, N_HEADS, nblk),
        in_specs=[blk(),
                  pl.BlockSpec((None, s, HEAD_DIM), lambda bi, h, i: (bi, 0, h)),
                  pl.BlockSpec((None, nblk, VT_ROWS, tq), lambda bi, h, i: (h, bi, 0, 0)),
                  blk(),
                  pl.BlockSpec((4, QK_DIM), lambda bi, h, i: (0, 0)),
                  pl.BlockSpec((1, HEAD_DIM), lambda bi, h, i: (0, h))],
        out_specs=blk(),
        out_shape=jax.ShapeDtypeStruct((b, s, GROUP_W), BF16),
        scratch_shapes=[pltpu.VMEM((2 * tq, HEAD_DIM), BF16),
                        pltpu.VMEM((1, 2 * tq), F32),
                        pltpu.VMEM((VT_ROWS, 2 * tq), F32),
                        pltpu.VMEM((tq, 2 * tq), F32),
                        pltpu.VMEM((tq, 2 * tq), F32)],
        compiler_params=_cparams(("parallel", "parallel", "arbitrary")),
        name="flash_prompt",
    )(q, kb, vt, rest, lamp, attn_g)


def _paged_kernel(pt_ref, q_ref, *refs, n_pages, n_valid, lam_init):
    del pt_ref
    kp = refs[:n_pages]
    vp = refs[n_pages:2 * n_pages]
    kn_ref, vn_ref, za_ref, lam_ref, g_ref, o_ref, qq_scr, bias_scr, m_scr, l_scr, acc_scr = refs[2 * n_pages:]
    r = SAMPLE_ROWS
    half = r // 2
    pw = bias_scr.shape[1]
    step = pl.program_id(1)

    @pl.when(step == 0)
    def _init():
        row = lax.broadcasted_iota(jnp.int32, (r, HEAD_DIM), 0)
        lane = lax.broadcasted_iota(jnp.int32, (r, HEAD_DIM), 1)
        for h in range(N_HEADS):
            q = q_ref[:, h * HEAD_DIM:(h + 1) * HEAD_DIM].astype(F32)
            q_map1 = jnp.where(lane < QK_DIM, q, 0.0)
            q_map2 = pltpu.roll(jnp.where(lane >= QK_DIM, q, 0.0), half, axis=0)
            qq_scr[h * r:(h + 1) * r, :] = jnp.where(row < half, q_map1, q_map2)
        row_head = lax.broadcasted_iota(jnp.int32, bias_scr.shape, 0) // r
        col_head = lax.broadcasted_iota(jnp.int32, bias_scr.shape, 1) % N_HEADS
        bias_scr[...] = jnp.where(row_head == col_head, 0.0, NEG_INF).astype(F32)
        m_scr[...] = jnp.full(m_scr.shape, NEG_INF, F32)
        l_scr[...] = jnp.zeros(l_scr.shape, F32)
        acc_scr[...] = jnp.zeros(acc_scr.shape, F32)

    qq = qq_scr[...].astype(BF16)
    bias = bias_scr[...]
    s = jnp.concatenate(
        [lax.dot_general(qq, kp[j][...].astype(BF16), (((1,), (1,)), ((), ())),
                         preferred_element_type=F32) + bias for j in range(n_pages)], axis=1)
    m_prev = m_scr[...]
    m_new = jnp.maximum(m_prev, jnp.max(s, axis=-1, keepdims=True))
    p = jnp.exp2(s - m_new)
    alpha = jnp.exp2(m_prev - m_new)
    l_scr[...] = alpha * l_scr[...] + jnp.sum(p, axis=-1, keepdims=True)
    p = p.astype(BF16)
    pv = jnp.dot(p[:, 0:pw], vp[0][...].astype(BF16), preferred_element_type=F32)
    for j in range(1, n_pages):
        pv = pv + jnp.dot(p[:, j * pw:(j + 1) * pw], vp[j][...].astype(BF16), preferred_element_type=F32)
    acc_scr[...] = alpha * acc_scr[...] + pv
    m_scr[...] = m_new

    @pl.when(step == pl.num_programs(1) - 1)
    def _finish():
        lam = _lambda(lam_ref, lam_init)
        row = lax.broadcasted_iota(jnp.int32, (r, 1), 0)
        tok = jnp.where(row >= half, row - half, row)
        for h in range(N_HEADS):
            cols = slice(h * HEAD_DIM, (h + 1) * HEAD_DIM)
            rows = slice(h * r, (h + 1) * r)
            qf = qq_scr[rows, :]
            kn = kn_ref[:, cols]
            vn = vn_ref[:, cols]
            s_new = [jnp.where(tok >= j, jnp.sum(qf * kn[j:j + 1, :], axis=-1, keepdims=True), NEG_INF)
                     for j in range(n_valid)]
            m_prev = m_scr[rows, :]
            m_new = m_prev
            for sj in s_new:
                m_new = jnp.maximum(m_new, sj)
            alpha = jnp.exp2(m_prev - m_new)
            l = alpha * l_scr[rows, :]
            acc = alpha * acc_scr[rows, :]
            for j, sj in enumerate(s_new):
                pj = jnp.exp2(sj - m_new)
                l = l + pj
                acc = acc + pj * vn[j:j + 1, :]
            o_maps = acc / l
            o = o_maps - lam * pltpu.roll(o_maps, half, axis=0)
            o = _rms(o) * g_ref[:, cols] * (1.0 - lam_init)
            o_ref[:, cols] = (o * _silu(za_ref[:, cols])).astype(BF16)


def _paged(q, k_new, v_new, rest, cache_k, cache_v, page_table, lamp, attn_g, lam_init, n_valid):
    db = q.shape[0]
    r = SAMPLE_ROWS
    assert 2 * n_valid <= r, "each head packs the tokens of both maps into SAMPLE_ROWS query rows"
    n_pages_total = page_table.shape[1]
    page_rows = cache_k.shape[1]
    p = PAGES_PER_STEP
    seq = lambda w=GROUP_W: pl.BlockSpec((None, r, w), lambda b, s, pt: (b, 0, 0))

    def page_spec(j):
        return pl.BlockSpec((None, page_rows, HEAD_DIM), lambda b, s, pt, j=j: (pt[b, s * p + j], 0, 0))

    grid_spec = pltpu.PrefetchScalarGridSpec(
        num_scalar_prefetch=1,
        grid=(db, n_pages_total // p),
        in_specs=[seq()] + [page_spec(j) for j in range(p)] * 2 + [seq(), seq(), seq(),
                  pl.BlockSpec((4, QK_DIM), lambda b, s, pt: (0, 0)),
                  pl.BlockSpec((1, GROUP_W), lambda b, s, pt: (0, 0))],
        out_specs=seq(),
        scratch_shapes=[pltpu.VMEM((N_HEADS * r, HEAD_DIM), F32),
                        pltpu.VMEM((N_HEADS * r, page_rows), F32),
                        pltpu.VMEM((N_HEADS * r, 1), F32),
                        pltpu.VMEM((N_HEADS * r, 1), F32),
                        pltpu.VMEM((N_HEADS * r, HEAD_DIM), F32)])
    return pl.pallas_call(
        functools.partial(_paged_kernel, n_pages=p, n_valid=n_valid, lam_init=lam_init),
        grid_spec=grid_spec,
        out_shape=jax.ShapeDtypeStruct((db, r, GROUP_W), BF16),
        compiler_params=_cparams(("parallel", "arbitrary")),
        name="paged_attn",
    )(page_table, q, *([cache_k] * p), *([cache_v] * p), k_new, v_new, rest, lamp, attn_g)


def _log_sigmoid(x):
    return jnp.minimum(x, 0.0) - jnp.log1p(jnp.exp(-jnp.abs(x)))


def _mlstm_kernel(xb_ref, ob_ref, zb_ref, conv0_ref, c0_ref, n0_ref, m0_ref,
                  cw_ref, cb_ref, wq_ref, wk_ref, wv_ref, wif_ref, wift_ref, bifr_ref, bifc_ref,
                  mg_ref, skip_ref,
                  bo_ref, c1_ref, n1_ref, m1_ref,
                  xpad, xch_scr, q_scr, k_scr, v_scr, h_scr, gc_scr, gr_scr, c_scr, n_scr, m_scr,
                  *, t, chunk, n_valid, group):
    step = pl.program_id(1)
    n_chunks = t // chunk
    hd = HEAD_DIM

    @pl.when(step == 0)
    def _init():
        for g in range(group):
            xpad[g, 0:8, :] = jnp.zeros((8, GROUP_W), F32)
            xpad[g, 8 - (CONV_W - 1):8, :] = conv0_ref[g]
        c_scr[...] = c0_ref[...]
        n_scr[...] = n0_ref[...]
        m_scr[...] = m0_ref[...]

    for g in range(group):
        x = xb_ref[g]
        xpad[g, 8:8 + t, :] = x
        xc = cb_ref[...] + cw_ref[CONV_W - 1:CONV_W, :] * x
        for j in range(CONV_W - 1):
            xc = xc + cw_ref[j:j + 1, :] * xpad[g, 8 - (CONV_W - 1) + j:8 - (CONV_W - 1) + j + t, :]
        tail = xpad[g, t:t + 8, :]
        xpad[g, 0:8, :] = tail
        xch = _silu(xc)
        xch_scr[g] = xch

        gcol = jnp.zeros((t, 2 * N_HEADS), F32) + bifr_ref[...]
        grow = jnp.zeros((2 * N_HEADS, t), F32) + bifc_ref[...]
        for h in range(N_HEADS):
            cols = slice(h * hd, (h + 1) * hd)
            xh = xch[:, cols].astype(BF16)
            qh = jnp.dot(xh, wq_ref[h], preferred_element_type=F32)
            kh = jnp.dot(xh, wk_ref[h], preferred_element_type=F32) * (hd ** -0.5)
            vh = jnp.dot(x[:, cols].astype(BF16), wv_ref[h], preferred_element_type=F32)
            q_scr[g, :, cols] = qh
            k_scr[g, :, cols] = kh
            v_scr[g, :, cols] = vh
            for part, val in enumerate((qh, kh, vh)):
                off = (3 * h + part) * hd
                vb = val.astype(BF16)
                gcol = gcol + jnp.dot(vb, wif_ref[off:off + hd, :], preferred_element_type=F32)
                grow = grow + lax.dot_general(wift_ref[:, off:off + hd], vb, (((1,), (1,)), ((), ())),
                                              preferred_element_type=F32)

        lane = lax.broadcasted_iota(jnp.int32, gcol.shape, 1)
        gcol = jnp.where(lane < N_HEADS, gcol, _log_sigmoid(gcol))
        sub = lax.broadcasted_iota(jnp.int32, grow.shape, 0)
        grow = jnp.where(sub < N_HEADS, grow, _log_sigmoid(grow))
        if n_valid < chunk:
            pos_c = lax.broadcasted_iota(jnp.int32, gcol.shape, 0) % chunk
            gcol = jnp.where(pos_c >= n_valid, jnp.where(lane < N_HEADS, NEG_INF, 0.0), gcol)
            pos_r = lax.broadcasted_iota(jnp.int32, grow.shape, 1) % chunk
            grow = jnp.where(pos_r >= n_valid, jnp.where(sub < N_HEADS, NEG_INF, 0.0), grow)
        gc_scr[g] = gcol
        for c in range(n_chunks):
            gr_scr[g, c] = grow[:, c * chunk:(c + 1) * chunk]

    ri = lax.broadcasted_iota(jnp.int32, (chunk, chunk), 0)
    ci = lax.broadcasted_iota(jnp.int32, (chunk, chunk), 1)
    causal = ci <= ri
    tril = jnp.where(causal, 1.0, 0.0).astype(F32)
    triu = jnp.where(ri <= ci, 1.0, 0.0).astype(F32)
    lane_c = lax.broadcasted_iota(jnp.int32, (chunk, 2 * N_HEADS), 1)
    sub_r = lax.broadcasted_iota(jnp.int32, (2 * N_HEADS, chunk), 0)

    def scan_head(g, h, rows, gc, gr, b_c, b_r):
        cols = slice(h * hd, (h + 1) * hd)
        bc = b_c[:, N_HEADS + h:N_HEADS + h + 1]
        br = b_r[N_HEADS + h:N_HEADS + h + 1, :]
        ic = gc[:, h:h + 1]
        ir = gr[h:h + 1, :]
        m = m_scr[g, h]
        dm = jnp.where(causal, bc - br + ir, NEG_INF)
        inter = bc + m
        m_t = jnp.maximum(inter, jnp.max(dm, axis=-1, keepdims=True))
        w_d = jnp.exp(dm - m_t)
        w_in = jnp.exp(inter - m_t)
        q = q_scr[g, rows, cols]
        k = k_scr[g, rows, cols]
        v = v_scr[g, rows, cols]
        qb = q.astype(BF16)
        kb = k.astype(BF16)
        sqk = lax.dot_general(qb, kb, (((1,), (1,)), ((), ())), preferred_element_type=F32) * w_d
        cmat = c_scr[g, h]
        nvec = n_scr[g, h]
        num = (w_in * jnp.dot(qb, cmat.astype(BF16), preferred_element_type=F32)
               + jnp.dot(sqk.astype(BF16), v.astype(BF16), preferred_element_type=F32))
        den = w_in * jnp.sum(q * nvec, axis=-1, keepdims=True) + jnp.sum(sqk, axis=-1, keepdims=True)
        h_scr[g, rows, cols] = num / jnp.maximum(jnp.abs(den), jnp.exp(-m_t))
        b_l = bc[chunk - 1:chunk, :]
        g_r = b_l - br + ir
        m_new = jnp.maximum(b_l + m, jnp.max(g_r, axis=-1, keepdims=True))
        decay = jnp.exp(b_l + m - m_new)
        w_g = jnp.exp(b_l - bc + ic - m_new)
        c_scr[g, h] = decay * cmat + lax.dot_general(kb, (w_g * v).astype(BF16), (((0,), (0,)), ((), ())),
                                                     preferred_element_type=F32)
        n_scr[g, h] = decay * nvec + jnp.sum(w_g * k, axis=0, keepdims=True)
        m_scr[g, h] = m_new

    def scan_chunk(c, carry):
        r0 = pl.multiple_of(c * chunk, chunk)
        rows = pl.ds(r0, chunk)
        for g in range(group):
            gc = gc_scr[g, rows, :]
            gr = gr_scr[g, c]
            b_c = jnp.dot(tril, jnp.where(lane_c >= N_HEADS, gc, 0.0), preferred_element_type=F32,
                          precision=lax.Precision.HIGHEST)
            b_r = jnp.dot(jnp.where(sub_r >= N_HEADS, gr, 0.0), triu, preferred_element_type=F32,
                          precision=lax.Precision.HIGHEST)
            for h in range(N_HEADS):
                scan_head(g, h, rows, gc, gr, b_c, b_r)
        return carry

    if n_chunks == 1:
        scan_chunk(0, 0)
    else:
        lax.fori_loop(0, n_chunks, scan_chunk, 0)

    for g in range(group):
        for h in range(N_HEADS):
            cols = slice(h * hd, (h + 1) * hd)
            hb = h_scr[g, :, cols] * jax.nn.sigmoid(ob_ref[g, :, cols])
            hb = _rms(hb) * mg_ref[:, cols] + skip_ref[:, cols] * xch_scr[g, :, cols]
            bo_ref[g, :, cols] = (hb * _silu(zb_ref[g, :, cols])).astype(BF16)

    @pl.when(step == pl.num_programs(1) - 1)
    def _state_out():
        c1_ref[...] = c_scr[...]
        n1_ref[...] = n_scr[...]
        m1_ref[...] = m_scr[...]


def _mlstm(rest, conv0, c0, n0, m0, w, t, chunk, n_valid, group):
    bn, s, _ = rest.shape
    hd = HEAD_DIM
    tok = lambda col: pl.BlockSpec((group, t, GROUP_W), lambda b, j, col=col: (b, j, col))
    per_b = lambda *shape: pl.BlockSpec((group,) + shape, lambda b, j: (b,) + (0,) * len(shape))
    full = lambda *shape: pl.BlockSpec(shape, lambda b, j: (0,) * len(shape))
    state_shapes = [(N_HEADS, hd, hd), (N_HEADS, 1, hd), (N_HEADS, 1, 1)]
    state_specs = [per_b(*shape) for shape in state_shapes]
    return pl.pallas_call(
        functools.partial(_mlstm_kernel, t=t, chunk=chunk, n_valid=n_valid, group=group),
        grid=(bn // group, s // t),
        in_specs=[tok(1), tok(2), tok(3), per_b(CONV_W - 1, GROUP_W)] + state_specs + [
            full(CONV_W, GROUP_W), full(1, GROUP_W),
            full(N_HEADS, hd, hd), full(N_HEADS, hd, hd), full(N_HEADS, hd, hd),
            full(3 * GROUP_W, 2 * N_HEADS), full(2 * N_HEADS, 3 * GROUP_W),
            full(1, 2 * N_HEADS), full(2 * N_HEADS, 1),
            full(1, GROUP_W), full(1, GROUP_W)],
        out_specs=[tok(0)] + state_specs,
        out_shape=[jax.ShapeDtypeStruct((bn, s, GROUP_W), BF16)]
                  + [jax.ShapeDtypeStruct((bn,) + shape, F32) for shape in state_shapes],
        scratch_shapes=[pltpu.VMEM((group, t + 8, GROUP_W), F32)]
                       + [pltpu.VMEM((group, t, GROUP_W), F32)] * 5
                       + [pltpu.VMEM((group, t, 2 * N_HEADS), F32),
                          pltpu.VMEM((group, t // chunk, 2 * N_HEADS, chunk), F32)]
                       + [pltpu.VMEM((group,) + shape, F32) for shape in state_shapes],
        compiler_params=_cparams(("parallel", "arbitrary")),
        name="mlstm",
    )(rest, rest, rest, conv0, c0, n0, m0,
      w["conv_w"], w["conv_b"], w["wq"], w["wk"], w["wv"], w["wif"], w["wif_t"], w["bif_row"], w["bif_col"],
      w["mlstm_g"], w["mlstm_skip"])


def _outproj_kernel(a_ref, bo_ref, x_ref, p_ref, woa_ref, wob_ref, pg_ref, wgate_ref, wple_ref, plg_ref, y_ref):
    y = (jnp.dot(a_ref[...], woa_ref[...], preferred_element_type=F32)
         + jnp.dot(bo_ref[...], wob_ref[...], preferred_element_type=F32))
    x1 = x_ref[...] + _rms(y) * pg_ref[...]
    gate = jax.nn.sigmoid(jnp.dot(x1.astype(BF16), wgate_ref[...], preferred_element_type=F32))
    pe = jnp.dot(p_ref[...].astype(BF16), wple_ref[...], preferred_element_type=F32)
    y_ref[...] = x1 + _rms(gate * pe) * plg_ref[...]


def _outproj(a, bo, x, p, w, tm):
    m, d = x.shape
    row = lambda width: pl.BlockSpec((tm, width), lambda i: (i, 0))
    full = lambda arr: pl.BlockSpec(arr.shape, lambda i: (0,) * arr.ndim)
    consts = [w["wo_a"], w["wo_b"], w["post_g"], w["ple_gate"], w["ple_w"], w["ple_g"]]
    return pl.pallas_call(
        _outproj_kernel,
        grid=(m // tm,),
        in_specs=[row(GROUP_W), row(GROUP_W), row(d), row(p.shape[1])] + [full(c) for c in consts],
        out_specs=row(d),
        out_shape=jax.ShapeDtypeStruct((m, d), F32),
        compiler_params=_cparams(("parallel",)),
        name="outproj",
    )(a, bo, x, p, *consts)


def _layer_weights(l, norm_pre_g, norm_post_g, w_in, lam_q1, lam_k1, lam_q2, lam_k2, attn_norm_g, conv_w, conv_b,
                   w_q_b, w_k_b, w_v_b, w_if, b_if, mlstm_norm_g, mlstm_skip, w_out, ple_w, ple_gate_w, ple_norm_g):
    row = lambda v: v.reshape(1, -1)
    wo = w_out[l].astype(BF16)
    return {
        "pre_g": row(norm_pre_g[l]), "w_in": w_in[l].astype(BF16),
        "lam": jnp.stack([lam_q1[l], lam_k1[l], lam_q2[l], lam_k2[l]]).astype(F32),
        "attn_g": row(attn_norm_g[l]),
        "conv_w": conv_w[l], "conv_b": row(conv_b[l]),
        "wq": w_q_b[l].astype(BF16), "wk": w_k_b[l].astype(BF16), "wv": w_v_b[l].astype(BF16),
        "wif": w_if[l].astype(BF16), "wif_t": w_if[l].T.astype(BF16),
        "bif_row": row(b_if[l]), "bif_col": b_if[l].reshape(-1, 1),
        "mlstm_g": row(mlstm_norm_g[l]), "mlstm_skip": row(mlstm_skip[l]),
        "wo_a": wo[:GROUP_W], "wo_b": wo[GROUP_W:], "post_g": row(norm_post_g[l]),
        "ple_gate": ple_gate_w[l].astype(BF16), "ple_w": ple_w[l].astype(BF16), "ple_g": row(ple_norm_g[l]),
    }


def _prompt_layer(x, p, w, lam_init):
    b, s, d = x.shape
    x2 = x.reshape(b * s, d)
    q, k, v, rest, kb, vt = _inproj(x2, w["pre_g"], w["w_in"], tm=PROMPT_ROWS, kv_block=FLASH_BLOCK)
    rest3 = rest.reshape(b, s, 4 * GROUP_W)
    a = _flash(q.reshape(b, s, GROUP_W), kb.reshape(b, s, GROUP_W), vt, rest3,
               w["lam"], w["attn_g"], lam_init, tq=FLASH_BLOCK)
    hd = HEAD_DIM
    bo, c1, n1, m1 = _mlstm(rest3, jnp.zeros((b, CONV_W - 1, GROUP_W), F32),
                            jnp.zeros((b, N_HEADS, hd, hd), F32), jnp.zeros((b, N_HEADS, 1, hd), F32),
                            jnp.zeros((b, N_HEADS, 1, 1), F32), w, t=PROMPT_ROWS, chunk=PROMPT_CHUNK,
                            n_valid=PROMPT_CHUNK, group=PROMPT_GROUP)
    y = _outproj(a.reshape(b * s, GROUP_W), bo.reshape(b * s, GROUP_W), x2, p.reshape(b * s, -1), w,
                 tm=PROMPT_ROWS)
    state = (k.reshape(b, s, N_HEADS, hd), v.reshape(b, s, N_HEADS, hd),
             rest3[:, s - (CONV_W - 1):, GROUP_W:2 * GROUP_W],
             c1, n1.reshape(b, N_HEADS, hd), m1.reshape(b, N_HEADS))
    return y.reshape(b, s, d), state


def _sample_layer(x, p, n_valid, cache_k, cache_v, page_table, conv0, c0, n0, m0, w, lam_init):
    db, r, d = x.shape
    hd = HEAD_DIM
    assert CONV_W - 1 <= n_valid <= r
    x2 = x.reshape(db * r, d)
    q, k, v, rest = _inproj(x2, w["pre_g"], w["w_in"], tm=db * r)
    rest3 = rest.reshape(db, r, 4 * GROUP_W)
    k3 = k.reshape(db, r, GROUP_W)
    v3 = v.reshape(db, r, GROUP_W)
    a = _paged(q.reshape(db, r, GROUP_W), k3, v3, rest3, cache_k, cache_v, page_table, w["lam"], w["attn_g"],
               lam_init, n_valid)
    bo, c1, n1, m1 = _mlstm(rest3, conv0, c0, n0.reshape(db, N_HEADS, 1, hd), m0.reshape(db, N_HEADS, 1, 1),
                            w, t=r, chunk=r, n_valid=n_valid, group=SAMPLE_GROUP)
    y = _outproj(a.reshape(db * r, GROUP_W), bo.reshape(db * r, GROUP_W), x2, p.reshape(db * r, -1), w, tm=db * r)
    state = (k3[:, :n_valid].reshape(db, n_valid, N_HEADS, hd), v3[:, :n_valid].reshape(db, n_valid, N_HEADS, hd),
             rest3[:, n_valid - (CONV_W - 1):n_valid, GROUP_W:2 * GROUP_W],
             c1, n1.reshape(db, N_HEADS, hd), m1.reshape(db, N_HEADS))
    return y.reshape(db, r, d), state


def kernel(x_prompt, x_sample, cache_k, cache_v, state_conv, state_C, state_n, state_m, page_table,
           p_prompt, p_sample, norm_pre_g, norm_post_g, w_in, lam_q1, lam_k1, lam_q2, lam_k2,
           attn_norm_g, conv_w, conv_b, w_q_b, w_k_b, w_v_b, w_if, b_if, mlstm_norm_g, mlstm_skip,
           w_out, ple_w, ple_gate_w, ple_norm_g):
    depth = w_in.shape[0]
    t_dec = x_sample.shape[1]
    pad_rows = lambda a: jnp.pad(a, ((0, 0), (0, SAMPLE_ROWS - t_dec), (0, 0)))
    n_pool, page = cache_k.shape[1], cache_k.shape[2]
    cache_k2 = cache_k.reshape(depth * n_pool, page * N_HEADS, HEAD_DIM)
    cache_v2 = cache_v.reshape(depth * n_pool, page * N_HEADS, HEAD_DIM)
    xp = x_prompt
    xs = pad_rows(x_sample)
    outs_p, outs_s = [], []
    for l in range(depth):
        w = _layer_weights(l, norm_pre_g, norm_post_g, w_in, lam_q1, lam_k1, lam_q2, lam_k2, attn_norm_g,
                           conv_w, conv_b, w_q_b, w_k_b, w_v_b, w_if, b_if, mlstm_norm_g, mlstm_skip,
                           w_out, ple_w, ple_gate_w, ple_norm_g)
        lam_init = 0.8 - 0.6 * math.exp(-0.3 * l)
        xp, st_p = _prompt_layer(xp, p_prompt[l], w, lam_init)
        outs_p.append(st_p)
        xs, st_s = _sample_layer(xs, pad_rows(p_sample[l]), t_dec, cache_k2, cache_v2, page_table + l * n_pool,
                                 state_conv[l], state_C[l], state_n[l], state_m[l], w, lam_init)
        outs_s.append(st_s)
    stack = lambda outs, i: jnp.stack([o[i] for o in outs])
    return ((xp, xs[:, :t_dec]) + tuple(stack(outs_p, i) for i in range(6))
            + tuple(stack(outs_s, i) for i in range(6)))
```

```python
import functools
import math

import jax
import jax.numpy as jnp
from jax import lax
from jax.experimental import pallas as pl
from jax.experimental.pallas import tpu as pltpu

F32 = jnp.float32
BF16 = jnp.bfloat16
EPS = 1e-6
NEG_INF = float("-inf")

N_HEADS = 4
HEAD_DIM = 128
QK_DIM = 64
GROUP_W = N_HEADS * HEAD_DIM
CONV_W = 4
SAMPLE_ROWS = 8
PROMPT_CHUNK = 128
PROMPT_ROWS = 512
FLASH_BLOCK = 512
PROMPT_GROUP = 1
SAMPLE_GROUP = 4
VT_ROWS = HEAD_DIM + 16
Q_SCALE = (QK_DIM ** -0.5) * math.log2(math.e)
PAGES_PER_STEP = 16
V7X_VMEM_LIMIT = 48 * 1024 * 1024


def _cparams(sem):
    return pltpu.CompilerParams(dimension_semantics=sem, vmem_limit_bytes=V7X_VMEM_LIMIT)


def _silu(x):
    return x * jax.nn.sigmoid(x)


def _rms(x):
    return x * lax.rsqrt(jnp.mean(x * x, axis=-1, keepdims=True) + EPS)


def _inproj_kernel(x_ref, g_ref, w_ref, q_ref, k_ref, v_ref, rest_ref, *maybe_flash_refs, kv_block):
    hb = (_rms(x_ref[...]) * g_ref[...]).astype(BF16)

    def proj(c):
        return jnp.dot(hb, w_ref[:, c * GROUP_W:(c + 1) * GROUP_W], preferred_element_type=F32)

    q_ref[...] = (proj(0) * Q_SCALE).astype(BF16)
    k = proj(1)
    v = proj(2)
    if not kv_block:
        k_ref[...] = k
        v_ref[...] = v
    else:
        for h in range(N_HEADS):
            cols = slice(h * HEAD_DIM, (h + 1) * HEAD_DIM)
            k_ref[pl.ds(h, k.shape[0], stride=N_HEADS), :] = k[:, cols]
            v_ref[pl.ds(h, v.shape[0], stride=N_HEADS), :] = v[:, cols]
        kb_ref, vt_ref = maybe_flash_refs
        kb_ref[...] = k.astype(BF16)
        for h in range(N_HEADS):
            for c in range(v.shape[0] // kv_block):
                blk = v[c * kv_block:(c + 1) * kv_block, h * HEAD_DIM:(h + 1) * HEAD_DIM]
                vt_ref[h, c, 0:HEAD_DIM, :] = blk.T.astype(BF16)
                vt_ref[h, c, HEAD_DIM:VT_ROWS, :] = jnp.ones((VT_ROWS - HEAD_DIM, kv_block), BF16)
    for c in range(3, 7):
        rest_ref[:, (c - 3) * GROUP_W:(c - 2) * GROUP_W] = proj(c)


def _inproj(x, g, w_bf, tm, kv_block=0):
    m, d = x.shape
    n_in = w_bf.shape[1]
    row = lambda w: pl.BlockSpec((tm, w), lambda i: (i, 0))
    out_specs = [row(GROUP_W)] * 3 + [row(4 * GROUP_W)]
    out_shape = [jax.ShapeDtypeStruct((m, GROUP_W), BF16),
                 jax.ShapeDtypeStruct((m, GROUP_W), F32),
                 jax.ShapeDtypeStruct((m, GROUP_W), F32),
                 jax.ShapeDtypeStruct((m, 4 * GROUP_W), F32)]
    if kv_block:
        for i_out in (1, 2):
            out_specs[i_out] = pl.BlockSpec((tm * N_HEADS, HEAD_DIM), lambda i: (i, 0))
            out_shape[i_out] = jax.ShapeDtypeStruct((m * N_HEADS, HEAD_DIM), F32)
        out_specs += [row(GROUP_W),
                      pl.BlockSpec((N_HEADS, tm // kv_block, VT_ROWS, kv_block), lambda i: (0, i, 0, 0))]
        out_shape += [jax.ShapeDtypeStruct((m, GROUP_W), BF16),
                      jax.ShapeDtypeStruct((N_HEADS, m // kv_block, VT_ROWS, kv_block), BF16)]
    return pl.pallas_call(
        functools.partial(_inproj_kernel, kv_block=kv_block),
        grid=(m // tm,),
        in_specs=[row(d), pl.BlockSpec((1, d), lambda i: (0, 0)),
                  pl.BlockSpec((d, n_in), lambda i: (0, 0))],
        out_specs=out_specs,
        out_shape=out_shape,
        compiler_params=_cparams(("parallel",)),
        name="inproj",
    )(x, g, w_bf)


def _split_maps(q):
    q = q.astype(F32)
    lane = lax.broadcasted_iota(jnp.int32, q.shape, 1)
    zero = jnp.zeros_like(q)
    return jnp.concatenate([jnp.where(lane < QK_DIM, q, zero), jnp.where(lane >= QK_DIM, q, zero)], axis=0)


def _lambda(lam_ref, lam_init):
    l1 = jnp.sum(lam_ref[0:1, :] * lam_ref[1:2, :], axis=-1, keepdims=True)
    l2 = jnp.sum(lam_ref[2:3, :] * lam_ref[3:4, :], axis=-1, keepdims=True)
    return jnp.exp(l1) - jnp.exp(l2) + lam_init


def _flash_kernel(q_ref, k_ref, vt_ref, za_ref, lam_ref, g_ref, o_ref, qq_scr, m_scr, acc_scr,
                  s0_scr, s1_scr, *, tq, lam_init):
    i = pl.program_id(2)
    qq_scr[...] = _split_maps(q_ref[...]).T.astype(BF16)
    m_scr[...] = jnp.full(m_scr.shape, NEG_INF, F32)
    acc_scr[...] = jnp.zeros(acc_scr.shape, F32)

    def scores(j, dst):
        k0 = pl.multiple_of(j * tq, tq)
        dst[...] = jnp.dot(k_ref[pl.ds(k0, tq), :], qq_scr[...], preferred_element_type=F32)

    def consume(src, j, masked):
        s = src[...]
        if masked:
            key = lax.broadcasted_iota(jnp.int32, s.shape, 0)
            qry = lax.broadcasted_iota(jnp.int32, s.shape, 1)
            s = jnp.where(key <= jnp.where(qry >= tq, qry - tq, qry), s, NEG_INF)
        m_prev = m_scr[...]
        m_new = jnp.maximum(m_prev, jnp.max(s, axis=0, keepdims=True))
        p = jnp.exp2(s - m_new)
        alpha = jnp.exp2(m_prev - m_new)
        acc_scr[...] = alpha * acc_scr[...] + jnp.dot(vt_ref[j], p.astype(BF16), preferred_element_type=F32)
        m_scr[...] = m_new

    scores(0, s0_scr)

    def pair(jj, carry):
        j = 2 * jj
        scores(j + 1, s1_scr)
        consume(s0_scr, j, False)
        scores(j + 2, s0_scr)
        consume(s1_scr, j + 1, False)
        return carry

    lax.fori_loop(0, i // 2, pair, 0)

    @pl.when(i % 2 == 1)
    def _odd_tail():
        scores(i, s1_scr)
        consume(s0_scr, i - 1, False)
        consume(s1_scr, i, True)

    @pl.when(i % 2 == 0)
    def _even_tail():
        consume(s0_scr, i, True)

    lam = _lambda(lam_ref, lam_init)
    o_t = acc_scr[0:HEAD_DIM, :] / acc_scr[HEAD_DIM:HEAD_DIM + 1, :]
    o = (o_t[:, 0:tq] - lam * o_t[:, tq:2 * tq]).T
    o = _rms(o) * g_ref[...] * (1.0 - lam_init)
    o_ref[...] = (o * _silu(za_ref[...])).astype(BF16)


def _flash(q, kb, vt, rest, lamp, attn_g, lam_init, tq):
    b, s, _ = q.shape
    nblk = s // tq
    blk = lambda: pl.BlockSpec((None, tq, HEAD_DIM), lambda bi, h, i: (bi, i, h))
    return pl.pallas_call(
        functools.partial(_flash_kernel, tq=tq, lam_init=lam_init),
        grid=(b, N_HEADS, nblk),
        in_specs=[blk(),
                  pl.BlockSpec((None, s, HEAD_DIM), lambda bi, h, i: (bi, 0, h)),
                  pl.BlockSpec((None, nblk, VT_ROWS, tq), lambda bi, h, i: (h, bi, 0, 0)),
                  blk(),
                  pl.BlockSpec((4, QK_DIM), lambda bi, h, i: (0, 0)),
                  pl.BlockSpec((1, HEAD_DIM), lambda bi, h, i: (0, h))],
        out_specs=blk(),
        out_shape=jax.ShapeDtypeStruct((b, s, GROUP_W), BF16),
        scratch_shapes=[pltpu.VMEM((HEAD_DIM, 2 * tq), BF16),
                        pltpu.VMEM((1, 2 * tq), F32),
                        pltpu.VMEM((VT_ROWS, 2 * tq), F32),
                        pltpu.VMEM((tq, 2 * tq), F32),
                        pltpu.VMEM((tq, 2 * tq), F32)],
        compiler_params=_cparams(("parallel", "parallel", "arbitrary")),
        name="flash_prompt",
    )(q, kb, vt, rest, lamp, attn_g)


def _paged_kernel(pt_ref, q_ref, *refs, n_pages, n_valid, lam_init):
    del pt_ref
    kp = refs[:n_pages]
    vp = refs[n_pages:2 * n_pages]
    kn_ref, vn_ref, za_ref, lam_ref, g_ref, o_ref, qq_scr, bias_scr, m_scr, l_scr, acc_scr = refs[2 * n_pages:]
    r = SAMPLE_ROWS
    half = r // 2
    pw = bias_scr.shape[1]
    step = pl.program_id(1)

    @pl.when(step == 0)
    def _init():
        row = lax.broadcasted_iota(jnp.int32, (r, HEAD_DIM), 0)
        lane = lax.broadcasted_iota(jnp.int32, (r, HEAD_DIM), 1)
        for h in range(N_HEADS):
            q = q_ref[:, h * HEAD_DIM:(h + 1) * HEAD_DIM].astype(F32)
            q_map1 = jnp.where(lane < QK_DIM, q, 0.0)
            q_map2 = pltpu.roll(jnp.where(lane >= QK_DIM, q, 0.0), half, axis=0)
            qq_scr[h * r:(h + 1) * r, :] = jnp.where(row < half, q_map1, q_map2)
        row_head = lax.broadcasted_iota(jnp.int32, bias_scr.shape, 0) // r
        col_head = lax.broadcasted_iota(jnp.int32, bias_scr.shape, 1) % N_HEADS
        bias_scr[...] = jnp.where(row_head == col_head, 0.0, NEG_INF).astype(F32)
        m_scr[...] = jnp.full(m_scr.shape, NEG_INF, F32)
        l_scr[...] = jnp.zeros(l_scr.shape, F32)
        acc_scr[...] = jnp.zeros(acc_scr.shape, F32)

    qq = qq_scr[...].astype(BF16)
    bias = bias_scr[...]
    s = jnp.concatenate(
        [lax.dot_general(qq, kp[j][...].astype(BF16), (((1,), (1,)), ((), ())),
                         preferred_element_type=F32) + bias for j in range(n_pages)], axis=1)
    m_prev = m_scr[...]
    m_new = jnp.maximum(m_prev, jnp.max(s, axis=-1, keepdims=True))
    p = jnp.exp2(s - m_new)
    alpha = jnp.exp2(m_prev - m_new)
    l_scr[...] = alpha * l_scr[...] + jnp.sum(p, axis=-1, keepdims=True)
    p = p.astype(BF16)
    pv = jnp.dot(p[:, 0:pw], vp[0][...].astype(BF16), preferred_element_type=F32)
    for j in range(1, n_pages):
        pv = pv + jnp.dot(p[:, j * pw:(j + 1) * pw], vp[j][...].astype(BF16), preferred_element_type=F32)
    acc_scr[...] = alpha * acc_scr[...] + pv
    m_scr[...] = m_new

    @pl.when(step == pl.num_programs(1) - 1)
    def _finish():
        lam = _lambda(lam_ref, lam_init)
        row = lax.broadcasted_iota(jnp.int32, (r, 1), 0)
        tok = jnp.where(row >= half, row - half, row)
        for h in range(N_HEADS):
            cols = slice(h * HEAD_DIM, (h + 1) * HEAD_DIM)
            rows = slice(h * r, (h + 1) * r)
            qf = qq_scr[rows, :]
            kn = kn_ref[:, cols]
            vn = vn_ref[:, cols]
            s_new = [jnp.where(tok >= j, jnp.sum(qf * kn[j:j + 1, :], axis=-1, keepdims=True), NEG_INF)
                     for j in range(n_valid)]
            m_prev = m_scr[rows, :]
            m_new = m_prev
            for sj in s_new:
                m_new = jnp.maximum(m_new, sj)
            alpha = jnp.exp2(m_prev - m_new)
            l = alpha * l_scr[rows, :]
            acc = alpha * acc_scr[rows, :]
            for j, sj in enumerate(s_new):
                pj = jnp.exp2(sj - m_new)
                l = l + pj
                acc = acc + pj * vn[j:j + 1, :]
            o_maps = acc / l
            o = o_maps - lam * pltpu.roll(o_maps, half, axis=0)
            o = _rms(o) * g_ref[:, cols] * (1.0 - lam_init)
            o_ref[:, cols] = (o * _silu(za_ref[:, cols])).astype(BF16)


def _paged(q, k_new, v_new, rest, cache_k, cache_v, page_table, lamp, attn_g, lam_init, n_valid):
    db = q.shape[0]
    r = SAMPLE_ROWS
    assert 2 * n_valid <= r, "each head packs the tokens of both maps into SAMPLE_ROWS query rows"
    n_pages_total = page_table.shape[1]
    page_rows = cache_k.shape[1]
    p = PAGES_PER_STEP
    seq = lambda w=GROUP_W: pl.BlockSpec((None, r, w), lambda b, s, pt: (b, 0, 0))

    def page_spec(j):
        return pl.BlockSpec((None, page_rows, HEAD_DIM), lambda b, s, pt, j=j: (pt[b, s * p + j], 0, 0))

    grid_spec = pltpu.PrefetchScalarGridSpec(
        num_scalar_prefetch=1,
        grid=(db, n_pages_total // p),
        in_specs=[seq()] + [page_spec(j) for j in range(p)] * 2 + [seq(), seq(), seq(),
                  pl.BlockSpec((4, QK_DIM), lambda b, s, pt: (0, 0)),
                  pl.BlockSpec((1, GROUP_W), lambda b, s, pt: (0, 0))],
        out_specs=seq(),
        scratch_shapes=[pltpu.VMEM((N_HEADS * r, HEAD_DIM), F32),
                        pltpu.VMEM((N_HEADS * r, page_rows), F32),
                        pltpu.VMEM((N_HEADS * r, 1), F32),
                        pltpu.VMEM((N_HEADS * r, 1), F32),
                        pltpu.VMEM((N_HEADS * r, HEAD_DIM), F32)])
    return pl.pallas_call(
        functools.partial(_paged_kernel, n_pages=p, n_valid=n_valid, lam_init=lam_init),
        grid_spec=grid_spec,
        out_shape=jax.ShapeDtypeStruct((db, r, GROUP_W), BF16),
        compiler_params=_cparams(("parallel", "arbitrary")),
        name="paged_attn",
    )(page_table, q, *([cache_k] * p), *([cache_v] * p), k_new, v_new, rest, lamp, attn_g)


def _log_sigmoid(x):
    return jnp.minimum(x, 0.0) - jnp.log1p(jnp.exp(-jnp.abs(x)))


def _bf16_terms(x):
    hi = x.astype(BF16)
    rest = x - hi.astype(F32)
    mid = rest.astype(BF16)
    lo = (rest - mid.astype(F32)).astype(BF16)
    return hi, mid, lo


def _mlstm_kernel(xb_ref, ob_ref, zb_ref, conv0_ref, c0_ref, n0_ref, m0_ref,
                  cw_ref, cb_ref, wq_ref, wk_ref, wv_ref, wif_ref, wift_ref, bifr_ref, bifc_ref,
                  mg_ref, skip_ref,
                  bo_ref, c1_ref, n1_ref, m1_ref,
                  xpad, xch_scr, q_scr, k_scr, v_scr, h_scr, gc_scr, gr_scr, bc_scr, br_scr, c_scr, n_scr, m_scr,
                  *, t, chunk, n_valid, group):
    step = pl.program_id(1)
    n_chunks = t // chunk
    hd = HEAD_DIM
    ri = lax.broadcasted_iota(jnp.int32, (chunk, chunk), 0)
    ci = lax.broadcasted_iota(jnp.int32, (chunk, chunk), 1)
    causal = ci <= ri
    tril = jnp.where(causal, 1.0, 0.0).astype(BF16)
    triu = jnp.where(ri <= ci, 1.0, 0.0).astype(BF16)

    @pl.when(step == 0)
    def _init():
        for g in range(group):
            xpad[g, 0:8, :] = jnp.zeros((8, GROUP_W), F32)
            xpad[g, 8 - (CONV_W - 1):8, :] = conv0_ref[g]
        c_scr[...] = c0_ref[...]
        n_scr[...] = n0_ref[...]
        m_scr[...] = m0_ref[...]

    for g in range(group):
        x = xb_ref[g]
        xpad[g, 8:8 + t, :] = x
        xc = cb_ref[...] + cw_ref[CONV_W - 1:CONV_W, :] * x
        for j in range(CONV_W - 1):
            xc = xc + cw_ref[j:j + 1, :] * xpad[g, 8 - (CONV_W - 1) + j:8 - (CONV_W - 1) + j + t, :]
        tail = xpad[g, t:t + 8, :]
        xpad[g, 0:8, :] = tail
        xch = _silu(xc)
        xch_scr[g] = xch

        gcol = jnp.zeros((t, 2 * N_HEADS), F32) + bifr_ref[...]
        grow = jnp.zeros((2 * N_HEADS, t), F32) + bifc_ref[...]
        for h in range(N_HEADS):
            cols = slice(h * hd, (h + 1) * hd)
            xh = xch[:, cols].astype(BF16)
            qh = jnp.dot(xh, wq_ref[h], preferred_element_type=F32)
            kh = jnp.dot(xh, wk_ref[h], preferred_element_type=F32) * (hd ** -0.5)
            vh = jnp.dot(x[:, cols].astype(BF16), wv_ref[h], preferred_element_type=F32)
            q_scr[g, :, cols] = qh
            k_scr[g, :, cols] = kh
            v_scr[g, :, cols] = vh
            for part, val in enumerate((qh, kh, vh)):
                off = (3 * h + part) * hd
                vb = val.astype(BF16)
                gcol = gcol + jnp.dot(vb, wif_ref[off:off + hd, :], preferred_element_type=F32)
                grow = grow + lax.dot_general(wift_ref[:, off:off + hd], vb, (((1,), (1,)), ((), ())),
                                              preferred_element_type=F32)

        lane = lax.broadcasted_iota(jnp.int32, gcol.shape, 1)
        gcol = jnp.where(lane < N_HEADS, gcol, _log_sigmoid(gcol))
        sub = lax.broadcasted_iota(jnp.int32, grow.shape, 0)
        grow = jnp.where(sub < N_HEADS, grow, _log_sigmoid(grow))
        if n_valid < chunk:
            pos_c = lax.broadcasted_iota(jnp.int32, gcol.shape, 0) % chunk
            gcol = jnp.where(pos_c >= n_valid, jnp.where(lane < N_HEADS, NEG_INF, 0.0), gcol)
            pos_r = lax.broadcasted_iota(jnp.int32, grow.shape, 1) % chunk
            grow = jnp.where(pos_r >= n_valid, jnp.where(sub < N_HEADS, NEG_INF, 0.0), grow)
        gc_scr[g] = gcol
        lf_col = jnp.where(lane >= N_HEADS, gcol, 0.0)
        lf_row = jnp.where(sub >= N_HEADS, grow, 0.0)
        for c in range(n_chunks):
            span = slice(c * chunk, (c + 1) * chunk)
            gr_scr[g, c] = grow[:, span]
            bc_scr[g, span, :] = sum(jnp.dot(tril, part, preferred_element_type=F32)
                                     for part in _bf16_terms(lf_col[span, :]))
            br_scr[g, c] = sum(jnp.dot(part, triu, preferred_element_type=F32)
                               for part in _bf16_terms(lf_row[:, span]))

    def scan_head(g, h, rows, gc, gr, b_c, b_r):
        cols = slice(h * hd, (h + 1) * hd)
        bc = b_c[:, N_HEADS + h:N_HEADS + h + 1]
        br = b_r[N_HEADS + h:N_HEADS + h + 1, :]
        ic = gc[:, h:h + 1]
        ir = gr[h:h + 1, :]
        m = m_scr[g, h]
        dm = jnp.where(causal, bc - br + ir, NEG_INF)
        inter = bc + m
        m_t = jnp.maximum(inter, jnp.max(dm, axis=-1, keepdims=True))
        w_d = jnp.exp(dm - m_t)
        w_in = jnp.exp(inter - m_t)
        q = q_scr[g, rows, cols]
        k = k_scr[g, rows, cols]
        v = v_scr[g, rows, cols]
        qb = q.astype(BF16)
        kb = k.astype(BF16)
        sqk = lax.dot_general(qb, kb, (((1,), (1,)), ((), ())), preferred_element_type=F32) * w_d
        cmat = c_scr[g, h]
        nvec = n_scr[g, h]
        num = (w_in * jnp.dot(qb, cmat.astype(BF16), preferred_element_type=F32)
               + jnp.dot(sqk.astype(BF16), v.astype(BF16), preferred_element_type=F32))
        den = w_in * jnp.sum(q * nvec, axis=-1, keepdims=True) + jnp.sum(sqk, axis=-1, keepdims=True)
        h_scr[g, rows, cols] = num / jnp.maximum(jnp.abs(den), jnp.exp(-m_t))
        b_l = bc[chunk - 1:chunk, :]
        g_r = b_l - br + ir
        m_new = jnp.maximum(b_l + m, jnp.max(g_r, axis=-1, keepdims=True))
        decay = jnp.exp(b_l + m - m_new)
        w_g = jnp.exp(b_l - bc + ic - m_new)
        c_scr[g, h] = decay * cmat + lax.dot_general(kb, (w_g * v).astype(BF16), (((0,), (0,)), ((), ())),
                                                     preferred_element_type=F32)
        n_scr[g, h] = decay * nvec + jnp.sum(w_g * k, axis=0, keepdims=True)
        m_scr[g, h] = m_new

    def scan_chunk(c, carry):
        r0 = pl.multiple_of(c * chunk, chunk)
        rows = pl.ds(r0, chunk)
        for g in range(group):
            gc = gc_scr[g, rows, :]
            gr = gr_scr[g, c]
            b_c = bc_scr[g, rows, :]
            b_r = br_scr[g, c]
            for h in range(N_HEADS):
                scan_head(g, h, rows, gc, gr, b_c, b_r)
        return carry

    if n_chunks == 1:
        scan_chunk(0, 0)
    else:
        lax.fori_loop(0, n_chunks, scan_chunk, 0)

    for g in range(group):
        for h in range(N_HEADS):
            cols = slice(h * hd, (h + 1) * hd)
            hb = h_scr[g, :, cols] * jax.nn.sigmoid(ob_ref[g, :, cols])
            hb = _rms(hb) * mg_ref[:, cols] + skip_ref[:, cols] * xch_scr[g, :, cols]
            bo_ref[g, :, cols] = (hb * _silu(zb_ref[g, :, cols])).astype(BF16)

    @pl.when(step == pl.num_programs(1) - 1)
    def _state_out():
        c1_ref[...] = c_scr[...]
        n1_ref[...] = n_scr[...]
        m1_ref[...] = m_scr[...]


def _mlstm(rest, conv0, c0, n0, m0, w, t, chunk, n_valid, group):
    bn, s, _ = rest.shape
    hd = HEAD_DIM
    tok = lambda col: pl.BlockSpec((group, t, GROUP_W), lambda b, j, col=col: (b, j, col))
    per_b = lambda *shape: pl.BlockSpec((group,) + shape, lambda b, j: (b,) + (0,) * len(shape))
    full = lambda *shape: pl.BlockSpec(shape, lambda b, j: (0,) * len(shape))
    state_shapes = [(N_HEADS, hd, hd), (N_HEADS, 1, hd), (N_HEADS, 1, 1)]
    state_specs = [per_b(*shape) for shape in state_shapes]
    return pl.pallas_call(
        functools.partial(_mlstm_kernel, t=t, chunk=chunk, n_valid=n_valid, group=group),
        grid=(bn // group, s // t),
        in_specs=[tok(1), tok(2), tok(3), per_b(CONV_W - 1, GROUP_W)] + state_specs + [
            full(CONV_W, GROUP_W), full(1, GROUP_W),
            full(N_HEADS, hd, hd), full(N_HEADS, hd, hd), full(N_HEADS, hd, hd),
            full(3 * GROUP_W, 2 * N_HEADS), full(2 * N_HEADS, 3 * GROUP_W),
            full(1, 2 * N_HEADS), full(2 * N_HEADS, 1),
            full(1, GROUP_W), full(1, GROUP_W)],
        out_specs=[tok(0)] + state_specs,
        out_shape=[jax.ShapeDtypeStruct((bn, s, GROUP_W), BF16)]
                  + [jax.ShapeDtypeStruct((bn,) + shape, F32) for shape in state_shapes],
        scratch_shapes=[pltpu.VMEM((group, t + 8, GROUP_W), F32)]
                       + [pltpu.VMEM((group, t, GROUP_W), F32)] * 5
                       + [pltpu.VMEM((group, t, 2 * N_HEADS), F32),
                          pltpu.VMEM((group, t // chunk, 2 * N_HEADS, chunk), F32)] * 2
                       + [pltpu.VMEM((group,) + shape, F32) for shape in state_shapes],
        compiler_params=_cparams(("parallel", "arbitrary")),
        name="mlstm",
    )(rest, rest, rest, conv0, c0, n0, m0,
      w["conv_w"], w["conv_b"], w["wq"], w["wk"], w["wv"], w["wif"], w["wif_t"], w["bif_row"], w["bif_col"],
      w["mlstm_g"], w["mlstm_skip"])


def _outproj_kernel(a_ref, bo_ref, x_ref, p_ref, woa_ref, wob_ref, pg_ref, wgate_ref, wple_ref, plg_ref, y_ref):
    y = (jnp.dot(a_ref[...], woa_ref[...], preferred_element_type=F32)
         + jnp.dot(bo_ref[...], wob_ref[...], preferred_element_type=F32))
    x1 = x_ref[...] + _rms(y) * pg_ref[...]
    gate = jax.nn.sigmoid(jnp.dot(x1.astype(BF16), wgate_ref[...], preferred_element_type=F32))
    pe = jnp.dot(p_ref[...].astype(BF16), wple_ref[...], preferred_element_type=F32)
    y_ref[...] = x1 + _rms(gate * pe) * plg_ref[...]


def _outproj(a, bo, x, p, w, tm):
    m, d = x.shape
    row = lambda width: pl.BlockSpec((tm, width), lambda i: (i, 0))
    full = lambda arr: pl.BlockSpec(arr.shape, lambda i: (0,) * arr.ndim)
    consts = [w["wo_a"], w["wo_b"], w["post_g"], w["ple_gate"], w["ple_w"], w["ple_g"]]
    return pl.pallas_call(
        _outproj_kernel,
        grid=(m // tm,),
        in_specs=[row(GROUP_W), row(GROUP_W), row(d), row(p.shape[1])] + [full(c) for c in consts],
        out_specs=row(d),
        out_shape=jax.ShapeDtypeStruct((m, d), F32),
        compiler_params=_cparams(("parallel",)),
        name="outproj",
    )(a, bo, x, p, *consts)


def _layer_weights(l, norm_pre_g, norm_post_g, w_in, lam_q1, lam_k1, lam_q2, lam_k2, attn_norm_g, conv_w, conv_b,
                   w_q_b, w_k_b, w_v_b, w_if, b_if, mlstm_norm_g, mlstm_skip, w_out, ple_w, ple_gate_w, ple_norm_g):
    row = lambda v: v.reshape(1, -1)
    wo = w_out[l].astype(BF16)
    return {
        "pre_g": row(norm_pre_g[l]), "w_in": w_in[l].astype(BF16),
        "lam": jnp.stack([lam_q1[l], lam_k1[l], lam_q2[l], lam_k2[l]]).astype(F32),
        "attn_g": row(attn_norm_g[l]),
        "conv_w": conv_w[l], "conv_b": row(conv_b[l]),
        "wq": w_q_b[l].astype(BF16), "wk": w_k_b[l].astype(BF16), "wv": w_v_b[l].astype(BF16),
        "wif": w_if[l].astype(BF16), "wif_t": w_if[l].T.astype(BF16),
        "bif_row": row(b_if[l]), "bif_col": b_if[l].reshape(-1, 1),
        "mlstm_g": row(mlstm_norm_g[l]), "mlstm_skip": row(mlstm_skip[l]),
        "wo_a": wo[:GROUP_W], "wo_b": wo[GROUP_W:], "post_g": row(norm_post_g[l]),
        "ple_gate": ple_gate_w[l].astype(BF16), "ple_w": ple_w[l].astype(BF16), "ple_g": row(ple_norm_g[l]),
    }


def _prompt_layer(x, p, w, lam_init):
    b, s, d = x.shape
    x2 = x.reshape(b * s, d)
    q, k, v, rest, kb, vt = _inproj(x2, w["pre_g"], w["w_in"], tm=PROMPT_ROWS, kv_block=FLASH_BLOCK)
    rest3 = rest.reshape(b, s, 4 * GROUP_W)
    a = _flash(q.reshape(b, s, GROUP_W), kb.reshape(b, s, GROUP_W), vt, rest3,
               w["lam"], w["attn_g"], lam_init, tq=FLASH_BLOCK)
    hd = HEAD_DIM
    bo, c1, n1, m1 = _mlstm(rest3, jnp.zeros((b, CONV_W - 1, GROUP_W), F32),
                            jnp.zeros((b, N_HEADS, hd, hd), F32), jnp.zeros((b, N_HEADS, 1, hd), F32),
                            jnp.zeros((b, N_HEADS, 1, 1), F32), w, t=PROMPT_ROWS, chunk=PROMPT_CHUNK,
                            n_valid=PROMPT_CHUNK, group=PROMPT_GROUP)
    y = _outproj(a.reshape(b * s, GROUP_W), bo.reshape(b * s, GROUP_W), x2, p.reshape(b * s, -1), w,
                 tm=PROMPT_ROWS)
    state = (k.reshape(b, s, N_HEADS, hd), v.reshape(b, s, N_HEADS, hd),
             rest3[:, s - (CONV_W - 1):, GROUP_W:2 * GROUP_W],
             c1, n1.reshape(b, N_HEADS, hd), m1.reshape(b, N_HEADS))
    return y.reshape(b, s, d), state


def _sample_layer(x, p, n_valid, cache_k, cache_v, page_table, conv0, c0, n0, m0, w, lam_init):
    db, r, d = x.shape
    hd = HEAD_DIM
    assert CONV_W - 1 <= n_valid <= r
    x2 = x.reshape(db * r, d)
    q, k, v, rest = _inproj(x2, w["pre_g"], w["w_in"], tm=db * r)
    rest3 = rest.reshape(db, r, 4 * GROUP_W)
    k3 = k.reshape(db, r, GROUP_W)
    v3 = v.reshape(db, r, GROUP_W)
    a = _paged(q.reshape(db, r, GROUP_W), k3, v3, rest3, cache_k, cache_v, page_table, w["lam"], w["attn_g"],
               lam_init, n_valid)
    bo, c1, n1, m1 = _mlstm(rest3, conv0, c0, n0.reshape(db, N_HEADS, 1, hd), m0.reshape(db, N_HEADS, 1, 1),
                            w, t=r, chunk=r, n_valid=n_valid, group=SAMPLE_GROUP)
    y = _outproj(a.reshape(db * r, GROUP_W), bo.reshape(db * r, GROUP_W), x2, p.reshape(db * r, -1), w, tm=db * r)
    state = (k3[:, :n_valid].reshape(db, n_valid, N_HEADS, hd), v3[:, :n_valid].reshape(db, n_valid, N_HEADS, hd),
             rest3[:, n_valid - (CONV_W - 1):n_valid, GROUP_W:2 * GROUP_W],
             c1, n1.reshape(db, N_HEADS, hd), m1.reshape(db, N_HEADS))
    return y.reshape(db, r, d), state


def kernel(x_prompt, x_sample, cache_k, cache_v, state_conv, state_C, state_n, state_m, page_table,
           p_prompt, p_sample, norm_pre_g, norm_post_g, w_in, lam_q1, lam_k1, lam_q2, lam_k2,
           attn_norm_g, conv_w, conv_b, w_q_b, w_k_b, w_v_b, w_if, b_if, mlstm_norm_g, mlstm_skip,
           w_out, ple_w, ple_gate_w, ple_norm_g):
    depth = w_in.shape[0]
    t_dec = x_sample.shape[1]
    pad_rows = lambda a: jnp.pad(a, ((0, 0), (0, SAMPLE_ROWS - t_dec), (0, 0)))
    n_pool, page = cache_k.shape[1], cache_k.shape[2]
    cache_k2 = cache_k.reshape(depth * n_pool, page * N_HEADS, HEAD_DIM)
    cache_v2 = cache_v.reshape(depth * n_pool, page * N_HEADS, HEAD_DIM)
    xp = x_prompt
    xs = pad_rows(x_sample)
    outs_p, outs_s = [], []
    for l in range(depth):
        w = _layer_weights(l, norm_pre_g, norm_post_g, w_in, lam_q1, lam_k1, lam_q2, lam_k2, attn_norm_g,
                           conv_w, conv_b, w_q_b, w_k_b, w_v_b, w_if, b_if, mlstm_norm_g, mlstm_skip,
                           w_out, ple_w, ple_gate_w, ple_norm_g)
        lam_init = 0.8 - 0.6 * math.exp(-0.3 * l)
        xp, st_p = _prompt_layer(xp, p_prompt[l], w, lam_init)
        outs_p.append(st_p)
        xs, st_s = _sample_layer(xs, pad_rows(p_sample[l]), t_dec, cache_k2, cache_v2, page_table + l * n_pool,
                                 state_conv[l], state_C[l], state_n[l], state_m[l], w, lam_init)
        outs_s.append(st_s)
    stack = lambda outs, i: jnp.stack([o[i] for o in outs])
    return ((xp, xs[:, :t_dec]) + tuple(stack(outs_p, i) for i in range(6))
            + tuple(stack(outs_s, i) for i in range(6)))
```

```python
import functools
import math

import jax
import jax.numpy as jnp
from jax import lax
from jax.experimental import pallas as pl
from jax.experimental.pallas import tpu as pltpu

F32 = jnp.float32
BF16 = jnp.bfloat16
EPS = 1e-6
NEG_INF = float("-inf")

N_HEADS = 4
HEAD_DIM = 128
QK_DIM = 64
GROUP_W = N_HEADS * HEAD_DIM
CONV_W = 4
SAMPLE_ROWS = 8
PROMPT_CHUNK = 128
PROMPT_ROWS = 512
FLASH_BLOCK = 512
PROMPT_GROUP = 1
SAMPLE_GROUP = 4
VT_ROWS = HEAD_DIM + 16
Q_SCALE = (QK_DIM ** -0.5) * math.log2(math.e)
PAGES_PER_STEP = 32
V7X_VMEM_LIMIT = 48 * 1024 * 1024


def _cparams(sem):
    return pltpu.CompilerParams(dimension_semantics=sem, vmem_limit_bytes=V7X_VMEM_LIMIT)


def _silu(x):
    return x * jax.nn.sigmoid(x)


def _rms(x):
    return x * lax.rsqrt(jnp.mean(x * x, axis=-1, keepdims=True) + EPS)


def _inproj_kernel(x_ref, g_ref, w_ref, q_ref, k_ref, v_ref, rest_ref, *maybe_flash_refs, kv_block):
    hb = (_rms(x_ref[...]) * g_ref[...]).astype(BF16)

    def proj(c):
        return jnp.dot(hb, w_ref[:, c * GROUP_W:(c + 1) * GROUP_W], preferred_element_type=F32)

    q_ref[...] = (proj(0) * Q_SCALE).astype(BF16)
    k = proj(1)
    v = proj(2)
    if not kv_block:
        k_ref[...] = k
        v_ref[...] = v
    else:
        for h in range(N_HEADS):
            cols = slice(h * HEAD_DIM, (h + 1) * HEAD_DIM)
            k_ref[pl.ds(h, k.shape[0], stride=N_HEADS), :] = k[:, cols]
            v_ref[pl.ds(h, v.shape[0], stride=N_HEADS), :] = v[:, cols]
        kb_ref, vt_ref = maybe_flash_refs
        kb_ref[...] = k.astype(BF16)
        for h in range(N_HEADS):
            for c in range(v.shape[0] // kv_block):
                blk = v[c * kv_block:(c + 1) * kv_block, h * HEAD_DIM:(h + 1) * HEAD_DIM]
                vt_ref[h, c, 0:HEAD_DIM, :] = blk.T.astype(BF16)
                vt_ref[h, c, HEAD_DIM:VT_ROWS, :] = jnp.ones((VT_ROWS - HEAD_DIM, kv_block), BF16)
    for c in range(3, 7):
        rest_ref[:, (c - 3) * GROUP_W:(c - 2) * GROUP_W] = proj(c)


def _inproj(x, g, w_bf, tm, kv_block=0):
    m, d = x.shape
    n_in = w_bf.shape[1]
    row = lambda w: pl.BlockSpec((tm, w), lambda i: (i, 0))
    out_specs = [row(GROUP_W)] * 3 + [row(4 * GROUP_W)]
    out_shape = [jax.ShapeDtypeStruct((m, GROUP_W), BF16),
                 jax.ShapeDtypeStruct((m, GROUP_W), F32),
                 jax.ShapeDtypeStruct((m, GROUP_W), F32),
                 jax.ShapeDtypeStruct((m, 4 * GROUP_W), F32)]
    if kv_block:
        for i_out in (1, 2):
            out_specs[i_out] = pl.BlockSpec((tm * N_HEADS, HEAD_DIM), lambda i: (i, 0))
            out_shape[i_out] = jax.ShapeDtypeStruct((m * N_HEADS, HEAD_DIM), F32)
        out_specs += [row(GROUP_W),
                      pl.BlockSpec((N_HEADS, tm // kv_block, VT_ROWS, kv_block), lambda i: (0, i, 0, 0))]
        out_shape += [jax.ShapeDtypeStruct((m, GROUP_W), BF16),
                      jax.ShapeDtypeStruct((N_HEADS, m // kv_block, VT_ROWS, kv_block), BF16)]
    return pl.pallas_call(
        functools.partial(_inproj_kernel, kv_block=kv_block),
        grid=(m // tm,),
        in_specs=[row(d), pl.BlockSpec((1, d), lambda i: (0, 0)),
                  pl.BlockSpec((d, n_in), lambda i: (0, 0))],
        out_specs=out_specs,
        out_shape=out_shape,
        compiler_params=_cparams(("parallel",)),
        name="inproj",
    )(x, g, w_bf)


def _split_maps(q):
    q = q.astype(F32)
    lane = lax.broadcasted_iota(jnp.int32, q.shape, 1)
    zero = jnp.zeros_like(q)
    return jnp.concatenate([jnp.where(lane < QK_DIM, q, zero), jnp.where(lane >= QK_DIM, q, zero)], axis=0)


def _lambda(lam_ref, lam_init):
    l1 = jnp.sum(lam_ref[0:1, :] * lam_ref[1:2, :], axis=-1, keepdims=True)
    l2 = jnp.sum(lam_ref[2:3, :] * lam_ref[3:4, :], axis=-1, keepdims=True)
    return jnp.exp(l1) - jnp.exp(l2) + lam_init


def _flash_kernel(q_ref, k_ref, vt_ref, za_ref, lam_ref, g_ref, o_ref, qq_scr, m_scr, acc_scr,
                  s0_scr, s1_scr, *, tq, lam_init):
    i = pl.program_id(2)
    qq_scr[...] = _split_maps(q_ref[...]).T.astype(BF16)
    m_scr[...] = jnp.full(m_scr.shape, NEG_INF, F32)
    acc_scr[...] = jnp.zeros(acc_scr.shape, F32)

    def scores(j, dst):
        k0 = pl.multiple_of(j * tq, tq)
        dst[...] = jnp.dot(k_ref[pl.ds(k0, tq), :], qq_scr[...], preferred_element_type=F32)

    def consume(src, j, masked):
        s = src[...]
        if masked:
            key = lax.broadcasted_iota(jnp.int32, s.shape, 0)
            qry = lax.broadcasted_iota(jnp.int32, s.shape, 1)
            s = jnp.where(key <= jnp.where(qry >= tq, qry - tq, qry), s, NEG_INF)
        m_prev = m_scr[...]
        m_new = jnp.maximum(m_prev, jnp.max(s, axis=0, keepdims=True))
        p = jnp.exp2(s - m_new)
        alpha = jnp.exp2(m_prev - m_new)
        acc_scr[...] = alpha * acc_scr[...] + jnp.dot(vt_ref[j], p.astype(BF16), preferred_element_type=F32)
        m_scr[...] = m_new

    scores(0, s0_scr)

    def pair(jj, carry):
        j = 2 * jj
        scores(j + 1, s1_scr)
        consume(s0_scr, j, False)
        scores(j + 2, s0_scr)
        consume(s1_scr, j + 1, False)
        return carry

    lax.fori_loop(0, i // 2, pair, 0)

    @pl.when(i % 2 == 1)
    def _odd_tail():
        scores(i, s1_scr)
        consume(s0_scr, i - 1, False)
        consume(s1_scr, i, True)

    @pl.when(i % 2 == 0)
    def _even_tail():
        consume(s0_scr, i, True)

    lam = _lambda(lam_ref, lam_init)
    o_t = acc_scr[0:HEAD_DIM, :] / acc_scr[HEAD_DIM:HEAD_DIM + 1, :]
    o = (o_t[:, 0:tq] - lam * o_t[:, tq:2 * tq]).T
    o = _rms(o) * g_ref[...] * (1.0 - lam_init)
    o_ref[...] = (o * _silu(za_ref[...])).astype(BF16)


def _flash(q, kb, vt, rest, lamp, attn_g, lam_init, tq):
    b, s, _ = q.shape
    nblk = s // tq
    blk = lambda: pl.BlockSpec((None, tq, HEAD_DIM), lambda bi, h, i: (bi, i, h))
    return pl.pallas_call(
        functools.partial(_flash_kernel, tq=tq, lam_init=lam_init),
        grid=(b, N_HEADS, nblk),
        in_specs=[blk(),
                  pl.BlockSpec((None, s, HEAD_DIM), lambda bi, h, i: (bi, 0, h)),
                  pl.BlockSpec((None, nblk, VT_ROWS, tq), lambda bi, h, i: (h, bi, 0, 0)),
                  blk(),
                  pl.BlockSpec((4, QK_DIM), lambda bi, h, i: (0, 0)),
                  pl.BlockSpec((1, HEAD_DIM), lambda bi, h, i: (0, h))],
        out_specs=blk(),
        out_shape=jax.ShapeDtypeStruct((b, s, GROUP_W), BF16),
        scratch_shapes=[pltpu.VMEM((HEAD_DIM, 2 * tq), BF16),
                        pltpu.VMEM((1, 2 * tq), F32),
                        pltpu.VMEM((VT_ROWS, 2 * tq), F32),
                        pltpu.VMEM((tq, 2 * tq), F32),
                        pltpu.VMEM((tq, 2 * tq), F32)],
        compiler_params=_cparams(("parallel", "parallel", "arbitrary")),
        name="flash_prompt",
    )(q, kb, vt, rest, lamp, attn_g)


def _paged_kernel(pt_ref, q_ref, *refs, n_pages, n_valid, lam_init):
    del pt_ref
    kp = refs[:n_pages]
    vp = refs[n_pages:2 * n_pages]
    kn_ref, vn_ref, za_ref, lam_ref, g_ref, o_ref, qq_scr, bias_scr, m_scr, l_scr, acc_scr = refs[2 * n_pages:]
    r = SAMPLE_ROWS
    half = r // 2
    pw = bias_scr.shape[1]
    step = pl.program_id(1)

    @pl.when(step == 0)
    def _init():
        row = lax.broadcasted_iota(jnp.int32, (r, HEAD_DIM), 0)
        lane = lax.broadcasted_iota(jnp.int32, (r, HEAD_DIM), 1)
        for h in range(N_HEADS):
            q = q_ref[:, h * HEAD_DIM:(h + 1) * HEAD_DIM].astype(F32)
            q_map1 = jnp.where(lane < QK_DIM, q, 0.0)
            q_map2 = pltpu.roll(jnp.where(lane >= QK_DIM, q, 0.0), half, axis=0)
            qq_scr[h * r:(h + 1) * r, :] = jnp.where(row < half, q_map1, q_map2)
        row_head = lax.broadcasted_iota(jnp.int32, bias_scr.shape, 0) // r
        col_head = lax.broadcasted_iota(jnp.int32, bias_scr.shape, 1) % N_HEADS
        bias_scr[...] = jnp.where(row_head == col_head, 0.0, NEG_INF).astype(F32)
        m_scr[...] = jnp.full(m_scr.shape, NEG_INF, F32)
        l_scr[...] = jnp.zeros(l_scr.shape, F32)
        acc_scr[...] = jnp.zeros(acc_scr.shape, F32)

    qq = qq_scr[...].astype(BF16)
    bias = bias_scr[...]
    s = jnp.concatenate(
        [lax.dot_general(qq, kp[j][...].astype(BF16), (((1,), (1,)), ((), ())),
                         preferred_element_type=F32) + bias for j in range(n_pages)], axis=1)
    m_prev = m_scr[...]
    m_new = jnp.maximum(m_prev, jnp.max(s, axis=-1, keepdims=True))
    p = jnp.exp2(s - m_new)
    alpha = jnp.exp2(m_prev - m_new)
    l_scr[...] = alpha * l_scr[...] + jnp.sum(p, axis=-1, keepdims=True)
    p = p.astype(BF16)
    pv = jnp.dot(p[:, 0:pw], vp[0][...].astype(BF16), preferred_element_type=F32)
    for j in range(1, n_pages):
        pv = pv + jnp.dot(p[:, j * pw:(j + 1) * pw], vp[j][...].astype(BF16), preferred_element_type=F32)
    acc_scr[...] = alpha * acc_scr[...] + pv
    m_scr[...] = m_new

    @pl.when(step == pl.num_programs(1) - 1)
    def _finish():
        lam = _lambda(lam_ref, lam_init)
        row = lax.broadcasted_iota(jnp.int32, (r, 1), 0)
        tok = jnp.where(row >= half, row - half, row)
        for h in range(N_HEADS):
            cols = slice(h * HEAD_DIM, (h + 1) * HEAD_DIM)
            rows = slice(h * r, (h + 1) * r)
            qf = qq_scr[rows, :]
            kn = kn_ref[:, cols]
            vn = vn_ref[:, cols]
            s_new = [jnp.where(tok >= j, jnp.sum(qf * kn[j:j + 1, :], axis=-1, keepdims=True), NEG_INF)
                     for j in range(n_valid)]
            m_prev = m_scr[rows, :]
            m_new = m_prev
            for sj in s_new:
                m_new = jnp.maximum(m_new, sj)
            alpha = jnp.exp2(m_prev - m_new)
            l = alpha * l_scr[rows, :]
            acc = alpha * acc_scr[rows, :]
            for j, sj in enumerate(s_new):
                pj = jnp.exp2(sj - m_new)
                l = l + pj
                acc = acc + pj * vn[j:j + 1, :]
            o_maps = acc / l
            o = o_maps - lam * pltpu.roll(o_maps, half, axis=0)
            o = _rms(o) * g_ref[:, cols] * (1.0 - lam_init)
            o_ref[:, cols] = (o * _silu(za_ref[:, cols])).astype(BF16)


def _paged(q, k_new, v_new, rest, cache_k, cache_v, page_table, lamp, attn_g, lam_init, n_valid):
    db = q.shape[0]
    r = SAMPLE_ROWS
    assert 2 * n_valid <= r, "each head packs the tokens of both maps into SAMPLE_ROWS query rows"
    n_pages_total = page_table.shape[1]
    page_rows = cache_k.shape[1]
    p = PAGES_PER_STEP
    seq = lambda w=GROUP_W: pl.BlockSpec((None, r, w), lambda b, s, pt: (b, 0, 0))

    def page_spec(j):
        return pl.BlockSpec((None, page_rows, HEAD_DIM), lambda b, s, pt, j=j: (pt[b, s * p + j], 0, 0))

    grid_spec = pltpu.PrefetchScalarGridSpec(
        num_scalar_prefetch=1,
        grid=(db, n_pages_total // p),
        in_specs=[seq()] + [page_spec(j) for j in range(p)] * 2 + [seq(), seq(), seq(),
                  pl.BlockSpec((4, QK_DIM), lambda b, s, pt: (0, 0)),
                  pl.BlockSpec((1, GROUP_W), lambda b, s, pt: (0, 0))],
        out_specs=seq(),
        scratch_shapes=[pltpu.VMEM((N_HEADS * r, HEAD_DIM), F32),
                        pltpu.VMEM((N_HEADS * r, page_rows), F32),
                        pltpu.VMEM((N_HEADS * r, 1), F32),
                        pltpu.VMEM((N_HEADS * r, 1), F32),
                        pltpu.VMEM((N_HEADS * r, HEAD_DIM), F32)])
    return pl.pallas_call(
        functools.partial(_paged_kernel, n_pages=p, n_valid=n_valid, lam_init=lam_init),
        grid_spec=grid_spec,
        out_shape=jax.ShapeDtypeStruct((db, r, GROUP_W), BF16),
        compiler_params=_cparams(("parallel", "arbitrary")),
        name="paged_attn",
    )(page_table, q, *([cache_k] * p), *([cache_v] * p), k_new, v_new, rest, lamp, attn_g)


def _log_sigmoid(x):
    return jnp.minimum(x, 0.0) - jnp.log1p(jnp.exp(-jnp.abs(x)))


def _bf16_terms(x):
    hi = x.astype(BF16)
    rest = x - hi.astype(F32)
    mid = rest.astype(BF16)
    lo = (rest - mid.astype(F32)).astype(BF16)
    return hi, mid, lo


def _mlstm_kernel(xb_ref, ob_ref, zb_ref, conv0_ref, c0_ref, n0_ref, m0_ref,
                  cw_ref, cb_ref, wq_ref, wk_ref, wv_ref, wif_ref, wift_ref, bifr_ref, bifc_ref,
                  mg_ref, skip_ref,
                  bo_ref, c1_ref, n1_ref, m1_ref,
                  xpad, xch_scr, q_scr, k_scr, v_scr, h_scr, gc_scr, gr_scr, bc_scr, br_scr, c_scr, n_scr, m_scr,
                  *, t, chunk, n_valid, group):
    step = pl.program_id(1)
    n_chunks = t // chunk
    hd = HEAD_DIM
    ri = lax.broadcasted_iota(jnp.int32, (chunk, chunk), 0)
    ci = lax.broadcasted_iota(jnp.int32, (chunk, chunk), 1)
    causal = ci <= ri
    tril = jnp.where(causal, 1.0, 0.0).astype(BF16)
    triu = jnp.where(ri <= ci, 1.0, 0.0).astype(BF16)

    @pl.when(step == 0)
    def _init():
        for g in range(group):
            xpad[g, 0:8, :] = jnp.zeros((8, GROUP_W), F32)
            xpad[g, 8 - (CONV_W - 1):8, :] = conv0_ref[g]
        c_scr[...] = c0_ref[...]
        n_scr[...] = n0_ref[...]
        m_scr[...] = m0_ref[...]

    for g in range(group):
        x = xb_ref[g]
        xpad[g, 8:8 + t, :] = x
        xc = cb_ref[...] + cw_ref[CONV_W - 1:CONV_W, :] * x
        for j in range(CONV_W - 1):
            xc = xc + cw_ref[j:j + 1, :] * xpad[g, 8 - (CONV_W - 1) + j:8 - (CONV_W - 1) + j + t, :]
        tail = xpad[g, t:t + 8, :]
        xpad[g, 0:8, :] = tail
        xch = _silu(xc)
        xch_scr[g] = xch

        gcol = jnp.zeros((t, 2 * N_HEADS), F32) + bifr_ref[...]
        grow = jnp.zeros((2 * N_HEADS, t), F32) + bifc_ref[...]
        for h in range(N_HEADS):
            cols = slice(h * hd, (h + 1) * hd)
            xh = xch[:, cols].astype(BF16)
            qh = jnp.dot(xh, wq_ref[h], preferred_element_type=F32)
            kh = jnp.dot(xh, wk_ref[h], preferred_element_type=F32) * (hd ** -0.5)
            vh = jnp.dot(x[:, cols].astype(BF16), wv_ref[h], preferred_element_type=F32)
            q_scr[g, :, cols] = qh
            k_scr[g, :, cols] = kh
            v_scr[g, :, cols] = vh
            for part, val in enumerate((qh, kh, vh)):
                off = (3 * h + part) * hd
                vb = val.astype(BF16)
                gcol = gcol + jnp.dot(vb, wif_ref[off:off + hd, :], preferred_element_type=F32)
                grow = grow + lax.dot_general(wift_ref[:, off:off + hd], vb, (((1,), (1,)), ((), ())),
                                              preferred_element_type=F32)

        lane = lax.broadcasted_iota(jnp.int32, gcol.shape, 1)
        gcol = jnp.where(lane < N_HEADS, gcol, _log_sigmoid(gcol))
        sub = lax.broadcasted_iota(jnp.int32, grow.shape, 0)
        grow = jnp.where(sub < N_HEADS, grow, _log_sigmoid(grow))
        if n_valid < chunk:
            pos_c = lax.broadcasted_iota(jnp.int32, gcol.shape, 0) % chunk
            gcol = jnp.where(pos_c >= n_valid, jnp.where(lane < N_HEADS, NEG_INF, 0.0), gcol)
            pos_r = lax.broadcasted_iota(jnp.int32, grow.shape, 1) % chunk
            grow = jnp.where(pos_r >= n_valid, jnp.where(sub < N_HEADS, NEG_INF, 0.0), grow)
        gc_scr[g] = gcol
        lf_col = jnp.where(lane >= N_HEADS, gcol, 0.0)
        lf_row = jnp.where(sub >= N_HEADS, grow, 0.0)
        for c in range(n_chunks):
            span = slice(c * chunk, (c + 1) * chunk)
            gr_scr[g, c] = grow[:, span]
            bc_scr[g, span, :] = sum(jnp.dot(tril, part, preferred_element_type=F32)
                                     for part in _bf16_terms(lf_col[span, :]))
            br_scr[g, c] = sum(jnp.dot(part, triu, preferred_element_type=F32)
                               for part in _bf16_terms(lf_row[:, span]))

    def scan_head(g, h, rows, gc, gr, b_c, b_r):
        cols = slice(h * hd, (h + 1) * hd)
        bc = b_c[:, N_HEADS + h:N_HEADS + h + 1]
        br = b_r[N_HEADS + h:N_HEADS + h + 1, :]
        ic = gc[:, h:h + 1]
        ir = gr[h:h + 1, :]
        m = m_scr[g, h]
        dm = jnp.where(causal, bc - br + ir, NEG_INF)
        inter = bc + m
        m_t = jnp.maximum(inter, jnp.max(dm, axis=-1, keepdims=True))
        w_d = jnp.exp(dm - m_t)
        w_in = jnp.exp(inter - m_t)
        q = q_scr[g, rows, cols]
        k = k_scr[g, rows, cols]
        v = v_scr[g, rows, cols]
        qb = q.astype(BF16)
        kb = k.astype(BF16)
        sqk = lax.dot_general(qb, kb, (((1,), (1,)), ((), ())), preferred_element_type=F32) * w_d
        cmat = c_scr[g, h]
        nvec = n_scr[g, h]
        num = (w_in * jnp.dot(qb, cmat.astype(BF16), preferred_element_type=F32)
               + jnp.dot(sqk.astype(BF16), v.astype(BF16), preferred_element_type=F32))
        den = w_in * jnp.sum(q * nvec, axis=-1, keepdims=True) + jnp.sum(sqk, axis=-1, keepdims=True)
        h_scr[g, rows, cols] = num / jnp.maximum(jnp.abs(den), jnp.exp(-m_t))
        b_l = bc[chunk - 1:chunk, :]
        g_r = b_l - br + ir
        m_new = jnp.maximum(b_l + m, jnp.max(g_r, axis=-1, keepdims=True))
        decay = jnp.exp(b_l + m - m_new)
        w_g = jnp.exp(b_l - bc + ic - m_new)
        c_scr[g, h] = decay * cmat + lax.dot_general(kb, (w_g * v).astype(BF16), (((0,), (0,)), ((), ())),
                                                     preferred_element_type=F32)
        n_scr[g, h] = decay * nvec + jnp.sum(w_g * k, axis=0, keepdims=True)
        m_scr[g, h] = m_new

    def scan_chunk(c, carry):
        r0 = pl.multiple_of(c * chunk, chunk)
        rows = pl.ds(r0, chunk)
        for g in range(group):
            gc = gc_scr[g, rows, :]
            gr = gr_scr[g, c]
            b_c = bc_scr[g, rows, :]
            b_r = br_scr[g, c]
            for h in range(N_HEADS):
                scan_head(g, h, rows, gc, gr, b_c, b_r)
        return carry

    if n_chunks == 1:
        scan_chunk(0, 0)
    else:
        lax.fori_loop(0, n_chunks, scan_chunk, 0)

    for g in range(group):
        for h in range(N_HEADS):
            cols = slice(h * hd, (h + 1) * hd)
            hb = h_scr[g, :, cols] * jax.nn.sigmoid(ob_ref[g, :, cols])
            hb = _rms(hb) * mg_ref[:, cols] + skip_ref[:, cols] * xch_scr[g, :, cols]
            bo_ref[g, :, cols] = (hb * _silu(zb_ref[g, :, cols])).astype(BF16)

    @pl.when(step == pl.num_programs(1) - 1)
    def _state_out():
        c1_ref[...] = c_scr[...]
        n1_ref[...] = n_scr[...]
        m1_ref[...] = m_scr[...]


def _mlstm(rest, conv0, c0, n0, m0, w, t, chunk, n_valid, group):
    bn, s, _ = rest.shape
    hd = HEAD_DIM
    tok = lambda col: pl.BlockSpec((group, t, GROUP_W), lambda b, j, col=col: (b, j, col))
    per_b = lambda *shape: pl.BlockSpec((group,) + shape, lambda b, j: (b,) + (0,) * len(shape))
    full = lambda *shape: pl.BlockSpec(shape, lambda b, j: (0,) * len(shape))
    state_shapes = [(N_HEADS, hd, hd), (N_HEADS, 1, hd), (N_HEADS, 1, 1)]
    state_specs = [per_b(*shape) for shape in state_shapes]
    return pl.pallas_call(
        functools.partial(_mlstm_kernel, t=t, chunk=chunk, n_valid=n_valid, group=group),
        grid=(bn // group, s // t),
        in_specs=[tok(1), tok(2), tok(3), per_b(CONV_W - 1, GROUP_W)] + state_specs + [
            full(CONV_W, GROUP_W), full(1, GROUP_W),
            full(N_HEADS, hd, hd), full(N_HEADS, hd, hd), full(N_HEADS, hd, hd),
            full(3 * GROUP_W, 2 * N_HEADS), full(2 * N_HEADS, 3 * GROUP_W),
            full(1, 2 * N_HEADS), full(2 * N_HEADS, 1),
            full(1, GROUP_W), full(1, GROUP_W)],
        out_specs=[tok(0)] + state_specs,
        out_shape=[jax.ShapeDtypeStruct((bn, s, GROUP_W), BF16)]
                  + [jax.ShapeDtypeStruct((bn,) + shape, F32) for shape in state_shapes],
        scratch_shapes=[pltpu.VMEM((group, t + 8, GROUP_W), F32)]
                       + [pltpu.VMEM((group, t, GROUP_W), F32)] * 5
                       + [pltpu.VMEM((group, t, 2 * N_HEADS), F32),
                          pltpu.VMEM((group, t // chunk, 2 * N_HEADS, chunk), F32)] * 2
                       + [pltpu.VMEM((group,) + shape, F32) for shape in state_shapes],
        compiler_params=_cparams(("parallel", "arbitrary")),
        name="mlstm",
    )(rest, rest, rest, conv0, c0, n0, m0,
      w["conv_w"], w["conv_b"], w["wq"], w["wk"], w["wv"], w["wif"], w["wif_t"], w["bif_row"], w["bif_col"],
      w["mlstm_g"], w["mlstm_skip"])


def _outproj_kernel(a_ref, bo_ref, x_ref, p_ref, woa_ref, wob_ref, pg_ref, wgate_ref, wple_ref, plg_ref, y_ref):
    y = (jnp.dot(a_ref[...], woa_ref[...], preferred_element_type=F32)
         + jnp.dot(bo_ref[...], wob_ref[...], preferred_element_type=F32))
    x1 = x_ref[...] + _rms(y) * pg_ref[...]
    gate = jax.nn.sigmoid(jnp.dot(x1.astype(BF16), wgate_ref[...], preferred_element_type=F32))
    pe = jnp.dot(p_ref[...].astype(BF16), wple_ref[...], preferred_element_type=F32)
    y_ref[...] = x1 + _rms(gate * pe) * plg_ref[...]


def _outproj(a, bo, x, p, w, tm):
    m, d = x.shape
    row = lambda width: pl.BlockSpec((tm, width), lambda i: (i, 0))
    full = lambda arr: pl.BlockSpec(arr.shape, lambda i: (0,) * arr.ndim)
    consts = [w["wo_a"], w["wo_b"], w["post_g"], w["ple_gate"], w["ple_w"], w["ple_g"]]
    return pl.pallas_call(
        _outproj_kernel,
        grid=(m // tm,),
        in_specs=[row(GROUP_W), row(GROUP_W), row(d), row(p.shape[1])] + [full(c) for c in consts],
        out_specs=row(d),
        out_shape=jax.ShapeDtypeStruct((m, d), F32),
        compiler_params=_cparams(("parallel",)),
        name="outproj",
    )(a, bo, x, p, *consts)


def _layer_weights(l, norm_pre_g, norm_post_g, w_in, lam_q1, lam_k1, lam_q2, lam_k2, attn_norm_g, conv_w, conv_b,
                   w_q_b, w_k_b, w_v_b, w_if, b_if, mlstm_norm_g, mlstm_skip, w_out, ple_w, ple_gate_w, ple_norm_g):
    row = lambda v: v.reshape(1, -1)
    wo = w_out[l].astype(BF16)
    return {
        "pre_g": row(norm_pre_g[l]), "w_in": w_in[l].astype(BF16),
        "lam": jnp.stack([lam_q1[l], lam_k1[l], lam_q2[l], lam_k2[l]]).astype(F32),
        "attn_g": row(attn_norm_g[l]),
        "conv_w": conv_w[l], "conv_b": row(conv_b[l]),
        "wq": w_q_b[l].astype(BF16), "wk": w_k_b[l].astype(BF16), "wv": w_v_b[l].astype(BF16),
        "wif": w_if[l].astype(BF16), "wif_t": w_if[l].T.astype(BF16),
        "bif_row": row(b_if[l]), "bif_col": b_if[l].reshape(-1, 1),
        "mlstm_g": row(mlstm_norm_g[l]), "mlstm_skip": row(mlstm_skip[l]),
        "wo_a": wo[:GROUP_W], "wo_b": wo[GROUP_W:], "post_g": row(norm_post_g[l]),
        "ple_gate": ple_gate_w[l].astype(BF16), "ple_w": ple_w[l].astype(BF16), "ple_g": row(ple_norm_g[l]),
    }


def _prompt_layer(x, p, w, lam_init):
    b, s, d = x.shape
    x2 = x.reshape(b * s, d)
    q, k, v, rest, kb, vt = _inproj(x2, w["pre_g"], w["w_in"], tm=PROMPT_ROWS, kv_block=FLASH_BLOCK)
    rest3 = rest.reshape(b, s, 4 * GROUP_W)
    a = _flash(q.reshape(b, s, GROUP_W), kb.reshape(b, s, GROUP_W), vt, rest3,
               w["lam"], w["attn_g"], lam_init, tq=FLASH_BLOCK)
    hd = HEAD_DIM
    bo, c1, n1, m1 = _mlstm(rest3, jnp.zeros((b, CONV_W - 1, GROUP_W), F32),
                            jnp.zeros((b, N_HEADS, hd, hd), F32), jnp.zeros((b, N_HEADS, 1, hd), F32),
                            jnp.zeros((b, N_HEADS, 1, 1), F32), w, t=PROMPT_ROWS, chunk=PROMPT_CHUNK,
                            n_valid=PROMPT_CHUNK, group=PROMPT_GROUP)
    y = _outproj(a.reshape(b * s, GROUP_W), bo.reshape(b * s, GROUP_W), x2, p.reshape(b * s, -1), w,
                 tm=PROMPT_ROWS)
    state = (k.reshape(b, s, N_HEADS, hd), v.reshape(b, s, N_HEADS, hd),
             rest3[:, s - (CONV_W - 1):, GROUP_W:2 * GROUP_W],
             c1, n1.reshape(b, N_HEADS, hd), m1.reshape(b, N_HEADS))
    return y.reshape(b, s, d), state


def _sample_layer(x, p, n_valid, cache_k, cache_v, page_table, conv0, c0, n0, m0, w, lam_init):
    db, r, d = x.shape
    hd = HEAD_DIM
    assert CONV_W - 1 <= n_valid <= r
    x2 = x.reshape(db * r, d)
    q, k, v, rest = _inproj(x2, w["pre_g"], w["w_in"], tm=db * r)
    rest3 = rest.reshape(db, r, 4 * GROUP_W)
    k3 = k.reshape(db, r, GROUP_W)
    v3 = v.reshape(db, r, GROUP_W)
    a = _paged(q.reshape(db, r, GROUP_W), k3, v3, rest3, cache_k, cache_v, page_table, w["lam"], w["attn_g"],
               lam_init, n_valid)
    bo, c1, n1, m1 = _mlstm(rest3, conv0, c0, n0.reshape(db, N_HEADS, 1, hd), m0.reshape(db, N_HEADS, 1, 1),
                            w, t=r, chunk=r, n_valid=n_valid, group=SAMPLE_GROUP)
    y = _outproj(a.reshape(db * r, GROUP_W), bo.reshape(db * r, GROUP_W), x2, p.reshape(db * r, -1), w, tm=db * r)
    state = (k3[:, :n_valid].reshape(db, n_valid, N_HEADS, hd), v3[:, :n_valid].reshape(db, n_valid, N_HEADS, hd),
             rest3[:, n_valid - (CONV_W - 1):n_valid, GROUP_W:2 * GROUP_W],
             c1, n1.reshape(db, N_HEADS, hd), m1.reshape(db, N_HEADS))
    return y.reshape(db, r, d), state


def kernel(x_prompt, x_sample, cache_k, cache_v, state_conv, state_C, state_n, state_m, page_table,
           p_prompt, p_sample, norm_pre_g, norm_post_g, w_in, lam_q1, lam_k1, lam_q2, lam_k2,
           attn_norm_g, conv_w, conv_b, w_q_b, w_k_b, w_v_b, w_if, b_if, mlstm_norm_g, mlstm_skip,
           w_out, ple_w, ple_gate_w, ple_norm_g):
    depth = w_in.shape[0]
    t_dec = x_sample.shape[1]
    pad_rows = lambda a: jnp.pad(a, ((0, 0), (0, SAMPLE_ROWS - t_dec), (0, 0)))
    n_pool, page = cache_k.shape[1], cache_k.shape[2]
    cache_k2 = cache_k.reshape(depth * n_pool, page * N_HEADS, HEAD_DIM)
    cache_v2 = cache_v.reshape(depth * n_pool, page * N_HEADS, HEAD_DIM)
    xp = x_prompt
    xs = pad_rows(x_sample)
    outs_p, outs_s = [], []
    for l in range(depth):
        w = _layer_weights(l, norm_pre_g, norm_post_g, w_in, lam_q1, lam_k1, lam_q2, lam_k2, attn_norm_g,
                           conv_w, conv_b, w_q_b, w_k_b, w_v_b, w_if, b_if, mlstm_norm_g, mlstm_skip,
                           w_out, ple_w, ple_gate_w, ple_norm_g)
        lam_init = 0.8 - 0.6 * math.exp(-0.3 * l)
        xp, st_p = _prompt_layer(xp, p_prompt[l], w, lam_init)
        outs_p.append(st_p)
        xs, st_s = _sample_layer(xs, pad_rows(p_sample[l]), t_dec, cache_k2, cache_v2, page_table + l * n_pool,
                                 state_conv[l], state_C[l], state_n[l], state_m[l], w, lam_init)
        outs_s.append(st_s)
    stack = lambda outs, i: jnp.stack([o[i] for o in outs])
    return ((xp, xs[:, :t_dec]) + tuple(stack(outs_p, i) for i in range(6))
            + tuple(stack(outs_s, i) for i in range(6)))
```

```python
import functools
import math

import jax
import jax.numpy as jnp
from jax import lax
from jax.experimental import pallas as pl
from jax.experimental.pallas import tpu as pltpu

F32 = jnp.float32
BF16 = jnp.bfloat16
EPS = 1e-6
NEG_INF = float("-inf")

N_HEADS = 4
HEAD_DIM = 128
QK_DIM = 64
GROUP_W = N_HEADS * HEAD_DIM
CONV_W = 4
SAMPLE_ROWS = 8
PROMPT_CHUNK = 128
PROMPT_ROWS = 512
FLASH_BLOCK = 512
FLASH_UNROLL = 4
PROMPT_GROUP = 1
SAMPLE_GROUP = 4
VT_ROWS = HEAD_DIM + 16
Q_SCALE = (QK_DIM ** -0.5) * math.log2(math.e)
PAGES_PER_STEP = 32
V7X_VMEM_LIMIT = 48 * 1024 * 1024


def _cparams(sem):
    return pltpu.CompilerParams(dimension_semantics=sem, vmem_limit_bytes=V7X_VMEM_LIMIT)


def _silu(x):
    return x * jax.nn.sigmoid(x)


def _rms(x):
    return x * lax.rsqrt(jnp.mean(x * x, axis=-1, keepdims=True) + EPS)


def _inproj_kernel(x_ref, g_ref, w_ref, q_ref, k_ref, v_ref, rest_ref, *maybe_flash_refs, kv_block):
    hb = (_rms(x_ref[...]) * g_ref[...]).astype(BF16)

    def proj(c):
        return jnp.dot(hb, w_ref[:, c * GROUP_W:(c + 1) * GROUP_W], preferred_element_type=F32)

    q_ref[...] = (proj(0) * Q_SCALE).astype(BF16)
    k = proj(1)
    v = proj(2)
    if not kv_block:
        k_ref[...] = k
        v_ref[...] = v
    else:
        for h in range(N_HEADS):
            cols = slice(h * HEAD_DIM, (h + 1) * HEAD_DIM)
            k_ref[pl.ds(h, k.shape[0], stride=N_HEADS), :] = k[:, cols]
            v_ref[pl.ds(h, v.shape[0], stride=N_HEADS), :] = v[:, cols]
        kb_ref, vt_ref = maybe_flash_refs
        kb_ref[...] = k.astype(BF16)
        for h in range(N_HEADS):
            for c in range(v.shape[0] // kv_block):
                blk = v[c * kv_block:(c + 1) * kv_block, h * HEAD_DIM:(h + 1) * HEAD_DIM]
                vt_ref[h, c, 0:HEAD_DIM, :] = blk.T.astype(BF16)
                vt_ref[h, c, HEAD_DIM:VT_ROWS, :] = jnp.ones((VT_ROWS - HEAD_DIM, kv_block), BF16)
    for c in range(3, 7):
        rest_ref[:, (c - 3) * GROUP_W:(c - 2) * GROUP_W] = proj(c)


def _inproj(x, g, w_bf, tm, kv_block=0):
    m, d = x.shape
    n_in = w_bf.shape[1]
    row = lambda w: pl.BlockSpec((tm, w), lambda i: (i, 0))
    out_specs = [row(GROUP_W)] * 3 + [row(4 * GROUP_W)]
    out_shape = [jax.ShapeDtypeStruct((m, GROUP_W), BF16),
                 jax.ShapeDtypeStruct((m, GROUP_W), F32),
                 jax.ShapeDtypeStruct((m, GROUP_W), F32),
                 jax.ShapeDtypeStruct((m, 4 * GROUP_W), F32)]
    if kv_block:
        for i_out in (1, 2):
            out_specs[i_out] = pl.BlockSpec((tm * N_HEADS, HEAD_DIM), lambda i: (i, 0))
            out_shape[i_out] = jax.ShapeDtypeStruct((m * N_HEADS, HEAD_DIM), F32)
        out_specs += [row(GROUP_W),
                      pl.BlockSpec((N_HEADS, tm // kv_block, VT_ROWS, kv_block), lambda i: (0, i, 0, 0))]
        out_shape += [jax.ShapeDtypeStruct((m, GROUP_W), BF16),
                      jax.ShapeDtypeStruct((N_HEADS, m // kv_block, VT_ROWS, kv_block), BF16)]
    return pl.pallas_call(
        functools.partial(_inproj_kernel, kv_block=kv_block),
        grid=(m // tm,),
        in_specs=[row(d), pl.BlockSpec((1, d), lambda i: (0, 0)),
                  pl.BlockSpec((d, n_in), lambda i: (0, 0))],
        out_specs=out_specs,
        out_shape=out_shape,
        compiler_params=_cparams(("parallel",)),
        name="inproj",
    )(x, g, w_bf)


def _split_maps(q):
    q = q.astype(F32)
    lane = lax.broadcasted_iota(jnp.int32, q.shape, 1)
    zero = jnp.zeros_like(q)
    return jnp.concatenate([jnp.where(lane < QK_DIM, q, zero), jnp.where(lane >= QK_DIM, q, zero)], axis=0)


def _lambda(lam_ref, lam_init):
    l1 = jnp.sum(lam_ref[0:1, :] * lam_ref[1:2, :], axis=-1, keepdims=True)
    l2 = jnp.sum(lam_ref[2:3, :] * lam_ref[3:4, :], axis=-1, keepdims=True)
    return jnp.exp(l1) - jnp.exp(l2) + lam_init


def _flash_kernel(q_ref, k_ref, vt_ref, za_ref, lam_ref, g_ref, o_ref, qq_scr, m_scr, acc_scr,
                  s0_scr, s1_scr, *, tq, lam_init):
    i = pl.program_id(2)
    qq_scr[...] = _split_maps(q_ref[...]).T.astype(BF16)
    m_scr[...] = jnp.full(m_scr.shape, NEG_INF, F32)
    acc_scr[...] = jnp.zeros(acc_scr.shape, F32)

    def scores(j, dst):
        k0 = pl.multiple_of(j * tq, tq)
        dst[...] = jnp.dot(k_ref[pl.ds(k0, tq), :], qq_scr[...], preferred_element_type=F32)

    def consume(src, j, masked):
        s = src[...]
        if masked:
            key = lax.broadcasted_iota(jnp.int32, s.shape, 0)
            qry = lax.broadcasted_iota(jnp.int32, s.shape, 1)
            s = jnp.where(key <= jnp.where(qry >= tq, qry - tq, qry), s, NEG_INF)
        m_prev = m_scr[...]
        m_new = jnp.maximum(m_prev, jnp.max(s, axis=0, keepdims=True))
        p = jnp.exp2(s - m_new)
        alpha = jnp.exp2(m_prev - m_new)
        acc_scr[...] = alpha * acc_scr[...] + jnp.dot(vt_ref[j], p.astype(BF16), preferred_element_type=F32)
        m_scr[...] = m_new

    bufs = (s0_scr, s1_scr)
    scores(0, s0_scr)

    def run(first, count, masked_last):
        for u in range(count):
            last = u == count - 1
            if not (last and masked_last):
                scores(first + u + 1, bufs[(u + 1) % 2])
            consume(bufs[u % 2], first + u, last and masked_last)

    def full_group(jj, carry):
        run(FLASH_UNROLL * jj, FLASH_UNROLL, False)
        return carry

    lax.fori_loop(0, i // FLASH_UNROLL, full_group, 0)

    for rem in range(FLASH_UNROLL):
        @pl.when(i % FLASH_UNROLL == rem)
        def _tail(rem=rem):
            run(i - rem, rem + 1, True)

    lam = _lambda(lam_ref, lam_init)
    o_t = acc_scr[0:HEAD_DIM, :] / acc_scr[HEAD_DIM:HEAD_DIM + 1, :]
    o = (o_t[:, 0:tq] - lam * o_t[:, tq:2 * tq]).T
    o = _rms(o) * g_ref[...] * (1.0 - lam_init)
    o_ref[...] = (o * _silu(za_ref[...])).astype(BF16)


def _flash(q, kb, vt, rest, lamp, attn_g, lam_init, tq):
    b, s, _ = q.shape
    nblk = s // tq
    blk = lambda: pl.BlockSpec((None, tq, HEAD_DIM), lambda bi, h, i: (bi, i, h))
    return pl.pallas_call(
        functools.partial(_flash_kernel, tq=tq, lam_init=lam_init),
        grid=(b, N_HEADS, nblk),
        in_specs=[blk(),
                  pl.BlockSpec((None, s, HEAD_DIM), lambda bi, h, i: (bi, 0, h)),
                  pl.BlockSpec((None, nblk, VT_ROWS, tq), lambda bi, h, i: (h, bi, 0, 0)),
                  blk(),
                  pl.BlockSpec((4, QK_DIM), lambda bi, h, i: (0, 0)),
                  pl.BlockSpec((1, HEAD_DIM), lambda bi, h, i: (0, h))],
        out_specs=blk(),
        out_shape=jax.ShapeDtypeStruct((b, s, GROUP_W), BF16),
        scratch_shapes=[pltpu.VMEM((HEAD_DIM, 2 * tq), BF16),
                        pltpu.VMEM((1, 2 * tq), F32),
                        pltpu.VMEM((VT_ROWS, 2 * tq), F32),
                        pltpu.VMEM((tq, 2 * tq), F32),
                        pltpu.VMEM((tq, 2 * tq), F32)],
        compiler_params=_cparams(("parallel", "parallel", "arbitrary")),
        name="flash_prompt",
    )(q, kb, vt, rest, lamp, attn_g)


def _paged_kernel(pt_ref, q_ref, *refs, n_pages, n_valid, lam_init):
    del pt_ref
    kp = refs[:n_pages]
    vp = refs[n_pages:2 * n_pages]
    kn_ref, vn_ref, za_ref, lam_ref, g_ref, o_ref, qq_scr, bias_scr, m_scr, l_scr, acc_scr = refs[2 * n_pages:]
    r = SAMPLE_ROWS
    half = r // 2
    pw = bias_scr.shape[1]
    step = pl.program_id(1)

    @pl.when(step == 0)
    def _init():
        row = lax.broadcasted_iota(jnp.int32, (r, HEAD_DIM), 0)
        lane = lax.broadcasted_iota(jnp.int32, (r, HEAD_DIM), 1)
        for h in range(N_HEADS):
            q = q_ref[:, h * HEAD_DIM:(h + 1) * HEAD_DIM].astype(F32)
            q_map1 = jnp.where(lane < QK_DIM, q, 0.0)
            q_map2 = pltpu.roll(jnp.where(lane >= QK_DIM, q, 0.0), half, axis=0)
            qq_scr[h * r:(h + 1) * r, :] = jnp.where(row < half, q_map1, q_map2)
        row_head = lax.broadcasted_iota(jnp.int32, bias_scr.shape, 0) // r
        col_head = lax.broadcasted_iota(jnp.int32, bias_scr.shape, 1) % N_HEADS
        bias_scr[...] = jnp.where(row_head == col_head, 0.0, NEG_INF).astype(F32)
        m_scr[...] = jnp.full(m_scr.shape, NEG_INF, F32)
        l_scr[...] = jnp.zeros(l_scr.shape, F32)
        acc_scr[...] = jnp.zeros(acc_scr.shape, F32)

    qq = qq_scr[...].astype(BF16)
    bias = bias_scr[...]
    s = jnp.concatenate(
        [lax.dot_general(qq, kp[j][...].astype(BF16), (((1,), (1,)), ((), ())),
                         preferred_element_type=F32) + bias for j in range(n_pages)], axis=1)
    m_prev = m_scr[...]
    m_new = jnp.maximum(m_prev, jnp.max(s, axis=-1, keepdims=True))
    p = jnp.exp2(s - m_new)
    alpha = jnp.exp2(m_prev - m_new)
    l_scr[...] = alpha * l_scr[...] + jnp.sum(p, axis=-1, keepdims=True)
    p = p.astype(BF16)
    pv = jnp.dot(p[:, 0:pw], vp[0][...].astype(BF16), preferred_element_type=F32)
    for j in range(1, n_pages):
        pv = pv + jnp.dot(p[:, j * pw:(j + 1) * pw], vp[j][...].astype(BF16), preferred_element_type=F32)
    acc_scr[...] = alpha * acc_scr[...] + pv
    m_scr[...] = m_new

    @pl.when(step == pl.num_programs(1) - 1)
    def _finish():
        lam = _lambda(lam_ref, lam_init)
        row = lax.broadcasted_iota(jnp.int32, (r, 1), 0)
        tok = jnp.where(row >= half, row - half, row)
        for h in range(N_HEADS):
            cols = slice(h * HEAD_DIM, (h + 1) * HEAD_DIM)
            rows = slice(h * r, (h + 1) * r)
            qf = qq_scr[rows, :]
            kn = kn_ref[:, cols]
            vn = vn_ref[:, cols]
            s_new = [jnp.where(tok >= j, jnp.sum(qf * kn[j:j + 1, :], axis=-1, keepdims=True), NEG_INF)
                     for j in range(n_valid)]
            m_prev = m_scr[rows, :]
            m_new = m_prev
            for sj in s_new:
                m_new = jnp.maximum(m_new, sj)
            alpha = jnp.exp2(m_prev - m_new)
            l = alpha * l_scr[rows, :]
            acc = alpha * acc_scr[rows, :]
            for j, sj in enumerate(s_new):
                pj = jnp.exp2(sj - m_new)
                l = l + pj
                acc = acc + pj * vn[j:j + 1, :]
            o_maps = acc / l
            o = o_maps - lam * pltpu.roll(o_maps, half, axis=0)
            o = _rms(o) * g_ref[:, cols] * (1.0 - lam_init)
            o_ref[:, cols] = (o * _silu(za_ref[:, cols])).astype(BF16)


def _paged(q, k_new, v_new, rest, cache_k, cache_v, page_table, lamp, attn_g, lam_init, n_valid):
    db = q.shape[0]
    r = SAMPLE_ROWS
    assert 2 * n_valid <= r, "each head packs the tokens of both maps into SAMPLE_ROWS query rows"
    n_pages_total = page_table.shape[1]
    page_rows = cache_k.shape[1]
    p = PAGES_PER_STEP
    seq = lambda w=GROUP_W: pl.BlockSpec((None, r, w), lambda b, s, pt: (b, 0, 0))

    def page_spec(j):
        return pl.BlockSpec((None, page_rows, HEAD_DIM), lambda b, s, pt, j=j: (pt[b, s * p + j], 0, 0))

    grid_spec = pltpu.PrefetchScalarGridSpec(
        num_scalar_prefetch=1,
        grid=(db, n_pages_total // p),
        in_specs=[seq()] + [page_spec(j) for j in range(p)] * 2 + [seq(), seq(), seq(),
                  pl.BlockSpec((4, QK_DIM), lambda b, s, pt: (0, 0)),
                  pl.BlockSpec((1, GROUP_W), lambda b, s, pt: (0, 0))],
        out_specs=seq(),
        scratch_shapes=[pltpu.VMEM((N_HEADS * r, HEAD_DIM), F32),
                        pltpu.VMEM((N_HEADS * r, page_rows), F32),
                        pltpu.VMEM((N_HEADS * r, 1), F32),
                        pltpu.VMEM((N_HEADS * r, 1), F32),
                        pltpu.VMEM((N_HEADS * r, HEAD_DIM), F32)])
    return pl.pallas_call(
        functools.partial(_paged_kernel, n_pages=p, n_valid=n_valid, lam_init=lam_init),
        grid_spec=grid_spec,
        out_shape=jax.ShapeDtypeStruct((db, r, GROUP_W), BF16),
        compiler_params=_cparams(("parallel", "arbitrary")),
        name="paged_attn",
    )(page_table, q, *([cache_k] * p), *([cache_v] * p), k_new, v_new, rest, lamp, attn_g)


def _log_sigmoid(x):
    return jnp.minimum(x, 0.0) - jnp.log1p(jnp.exp(-jnp.abs(x)))


def _bf16_terms(x):
    hi = x.astype(BF16)
    rest = x - hi.astype(F32)
    mid = rest.astype(BF16)
    lo = (rest - mid.astype(F32)).astype(BF16)
    return hi, mid, lo


def _mlstm_kernel(xb_ref, ob_ref, zb_ref, conv0_ref, c0_ref, n0_ref, m0_ref,
                  cw_ref, cb_ref, wq_ref, wk_ref, wv_ref, wif_ref, wift_ref, bifr_ref, bifc_ref,
                  mg_ref, skip_ref,
                  bo_ref, c1_ref, n1_ref, m1_ref,
                  xpad, xch_scr, q_scr, k_scr, v_scr, h_scr, gc_scr, gr_scr, bc_scr, br_scr, c_scr, n_scr, m_scr,
                  *, t, chunk, n_valid, group):
    step = pl.program_id(1)
    n_chunks = t // chunk
    hd = HEAD_DIM
    ri = lax.broadcasted_iota(jnp.int32, (chunk, chunk), 0)
    ci = lax.broadcasted_iota(jnp.int32, (chunk, chunk), 1)
    causal = ci <= ri
    tril = jnp.where(causal, 1.0, 0.0).astype(BF16)
    triu = jnp.where(ri <= ci, 1.0, 0.0).astype(BF16)

    @pl.when(step == 0)
    def _init():
        for g in range(group):
            xpad[g, 0:8, :] = jnp.zeros((8, GROUP_W), F32)
            xpad[g, 8 - (CONV_W - 1):8, :] = conv0_ref[g]
        c_scr[...] = c0_ref[...]
        n_scr[...] = n0_ref[...]
        m_scr[...] = m0_ref[...]

    for g in range(group):
        x = xb_ref[g]
        xpad[g, 8:8 + t, :] = x
        xc = cb_ref[...] + cw_ref[CONV_W - 1:CONV_W, :] * x
        for j in range(CONV_W - 1):
            xc = xc + cw_ref[j:j + 1, :] * xpad[g, 8 - (CONV_W - 1) + j:8 - (CONV_W - 1) + j + t, :]
        tail = xpad[g, t:t + 8, :]
        xpad[g, 0:8, :] = tail
        xch = _silu(xc)
        xch_scr[g] = xch

        gcol = jnp.zeros((t, 2 * N_HEADS), F32) + bifr_ref[...]
        grow = jnp.zeros((2 * N_HEADS, t), F32) + bifc_ref[...]
        for h in range(N_HEADS):
            cols = slice(h * hd, (h + 1) * hd)
            xh = xch[:, cols].astype(BF16)
            qh = jnp.dot(xh, wq_ref[h], preferred_element_type=F32)
            kh = jnp.dot(xh, wk_ref[h], preferred_element_type=F32) * (hd ** -0.5)
            vh = jnp.dot(x[:, cols].astype(BF16), wv_ref[h], preferred_element_type=F32)
            q_scr[g, :, cols] = qh
            k_scr[g, :, cols] = kh
            v_scr[g, :, cols] = vh
            for part, val in enumerate((qh, kh, vh)):
                off = (3 * h + part) * hd
                vb = val.astype(BF16)
                gcol = gcol + jnp.dot(vb, wif_ref[off:off + hd, :], preferred_element_type=F32)
                grow = grow + lax.dot_general(wift_ref[:, off:off + hd], vb, (((1,), (1,)), ((), ())),
                                              preferred_element_type=F32)

        lane = lax.broadcasted_iota(jnp.int32, gcol.shape, 1)
        gcol = jnp.where(lane < N_HEADS, gcol, _log_sigmoid(gcol))
        sub = lax.broadcasted_iota(jnp.int32, grow.shape, 0)
        grow = jnp.where(sub < N_HEADS, grow, _log_sigmoid(grow))
        if n_valid < chunk:
            pos_c = lax.broadcasted_iota(jnp.int32, gcol.shape, 0) % chunk
            gcol = jnp.where(pos_c >= n_valid, jnp.where(lane < N_HEADS, NEG_INF, 0.0), gcol)
            pos_r = lax.broadcasted_iota(jnp.int32, grow.shape, 1) % chunk
            grow = jnp.where(pos_r >= n_valid, jnp.where(sub < N_HEADS, NEG_INF, 0.0), grow)
        gc_scr[g] = gcol
        lf_col = jnp.where(lane >= N_HEADS, gcol, 0.0)
        lf_row = jnp.where(sub >= N_HEADS, grow, 0.0)
        for c in range(n_chunks):
            span = slice(c * chunk, (c + 1) * chunk)
            gr_scr[g, c] = grow[:, span]
            bc_scr[g, span, :] = sum(jnp.dot(tril, part, preferred_element_type=F32)
                                     for part in _bf16_terms(lf_col[span, :]))
            br_scr[g, c] = sum(jnp.dot(part, triu, preferred_element_type=F32)
                               for part in _bf16_terms(lf_row[:, span]))

    def scan_chunk(c, carry):
        r0 = pl.multiple_of(c * chunk, chunk)
        rows = pl.ds(r0, chunk)
        chains = [(g, h) for g in range(group) for h in range(N_HEADS)]
        gates = {g: (gc_scr[g, rows, :], gr_scr[g, c], bc_scr[g, rows, :], br_scr[g, c]) for g in range(group)}
        st = []
        for g, h in chains:
            gc, gr, b_c, b_r = gates[g]
            bc = b_c[:, N_HEADS + h:N_HEADS + h + 1]
            br = b_r[N_HEADS + h:N_HEADS + h + 1, :]
            ic = gc[:, h:h + 1]
            ir = gr[h:h + 1, :]
            m = m_scr[g, h]
            dm = jnp.where(causal, bc - br + ir, NEG_INF)
            st.append(dict(bc=bc, br=br, ic=ic, ir=ir, m=m, dm=dm, inter=bc + m,
                           cols=slice(h * hd, (h + 1) * hd)))
        for s in st:
            s["m_t"] = jnp.maximum(s["inter"], jnp.max(s["dm"], axis=-1, keepdims=True))
        for s in st:
            s["w_d"] = jnp.exp(s["dm"] - s["m_t"])
            s["w_in"] = jnp.exp(s["inter"] - s["m_t"])
        for (g, h), s in zip(chains, st):
            s["q"] = q_scr[g, rows, s["cols"]]
            s["k"] = k_scr[g, rows, s["cols"]]
            s["v"] = v_scr[g, rows, s["cols"]]
            s["qb"] = s["q"].astype(BF16)
            s["kb"] = s["k"].astype(BF16)
            s["sqk"] = lax.dot_general(s["qb"], s["kb"], (((1,), (1,)), ((), ())),
                                       preferred_element_type=F32) * s["w_d"]
        for (g, h), s in zip(chains, st):
            s["cmat"] = c_scr[g, h]
            s["nvec"] = n_scr[g, h]
            s["num"] = (s["w_in"] * jnp.dot(s["qb"], s["cmat"].astype(BF16), preferred_element_type=F32)
                        + jnp.dot(s["sqk"].astype(BF16), s["v"].astype(BF16), preferred_element_type=F32))
        for s in st:
            s["den"] = (s["w_in"] * jnp.sum(s["q"] * s["nvec"], axis=-1, keepdims=True)
                        + jnp.sum(s["sqk"], axis=-1, keepdims=True))
        for (g, h), s in zip(chains, st):
            h_scr[g, rows, s["cols"]] = s["num"] / jnp.maximum(jnp.abs(s["den"]), jnp.exp(-s["m_t"]))
        for (g, h), s in zip(chains, st):
            b_l = s["bc"][chunk - 1:chunk, :]
            g_r = b_l - s["br"] + s["ir"]
            m_new = jnp.maximum(b_l + s["m"], jnp.max(g_r, axis=-1, keepdims=True))
            decay = jnp.exp(b_l + s["m"] - m_new)
            w_g = jnp.exp(b_l - s["bc"] + s["ic"] - m_new)
            c_scr[g, h] = decay * s["cmat"] + lax.dot_general(
                s["kb"], (w_g * s["v"]).astype(BF16), (((0,), (0,)), ((), ())), preferred_element_type=F32)
            n_scr[g, h] = decay * s["nvec"] + jnp.sum(w_g * s["k"], axis=0, keepdims=True)
            m_scr[g, h] = m_new
        return carry

    if n_chunks == 1:
        scan_chunk(0, 0)
    else:
        lax.fori_loop(0, n_chunks, scan_chunk, 0)

    for g in range(group):
        for h in range(N_HEADS):
            cols = slice(h * hd, (h + 1) * hd)
            hb = h_scr[g, :, cols] * jax.nn.sigmoid(ob_ref[g, :, cols])
            hb = _rms(hb) * mg_ref[:, cols] + skip_ref[:, cols] * xch_scr[g, :, cols]
            bo_ref[g, :, cols] = (hb * _silu(zb_ref[g, :, cols])).astype(BF16)

    @pl.when(step == pl.num_programs(1) - 1)
    def _state_out():
        c1_ref[...] = c_scr[...]
        n1_ref[...] = n_scr[...]
        m1_ref[...] = m_scr[...]


def _mlstm(rest, conv0, c0, n0, m0, w, t, chunk, n_valid, group):
    bn, s, _ = rest.shape
    hd = HEAD_DIM
    tok = lambda col: pl.BlockSpec((group, t, GROUP_W), lambda b, j, col=col: (b, j, col))
    per_b = lambda *shape: pl.BlockSpec((group,) + shape, lambda b, j: (b,) + (0,) * len(shape))
    full = lambda *shape: pl.BlockSpec(shape, lambda b, j: (0,) * len(shape))
    state_shapes = [(N_HEADS, hd, hd), (N_HEADS, 1, hd), (N_HEADS, 1, 1)]
    state_specs = [per_b(*shape) for shape in state_shapes]
    return pl.pallas_call(
        functools.partial(_mlstm_kernel, t=t, chunk=chunk, n_valid=n_valid, group=group),
        grid=(bn // group, s // t),
        in_specs=[tok(1), tok(2), tok(3), per_b(CONV_W - 1, GROUP_W)] + state_specs + [
            full(CONV_W, GROUP_W), full(1, GROUP_W),
            full(N_HEADS, hd, hd), full(N_HEADS, hd, hd), full(N_HEADS, hd, hd),
            full(3 * GROUP_W, 2 * N_HEADS), full(2 * N_HEADS, 3 * GROUP_W),
            full(1, 2 * N_HEADS), full(2 * N_HEADS, 1),
            full(1, GROUP_W), full(1, GROUP_W)],
        out_specs=[tok(0)] + state_specs,
        out_shape=[jax.ShapeDtypeStruct((bn, s, GROUP_W), BF16)]
                  + [jax.ShapeDtypeStruct((bn,) + shape, F32) for shape in state_shapes],
        scratch_shapes=[pltpu.VMEM((group, t + 8, GROUP_W), F32)]
                       + [pltpu.VMEM((group, t, GROUP_W), F32)] * 5
                       + [pltpu.VMEM((group, t, 2 * N_HEADS), F32),
                          pltpu.VMEM((group, t // chunk, 2 * N_HEADS, chunk), F32)] * 2
                       + [pltpu.VMEM((group,) + shape, F32) for shape in state_shapes],
        compiler_params=_cparams(("parallel", "arbitrary")),
        name="mlstm",
    )(rest, rest, rest, conv0, c0, n0, m0,
      w["conv_w"], w["conv_b"], w["wq"], w["wk"], w["wv"], w["wif"], w["wif_t"], w["bif_row"], w["bif_col"],
      w["mlstm_g"], w["mlstm_skip"])


def _outproj_kernel(a_ref, bo_ref, x_ref, p_ref, woa_ref, wob_ref, pg_ref, wgate_ref, wple_ref, plg_ref, y_ref):
    y = (jnp.dot(a_ref[...], woa_ref[...], preferred_element_type=F32)
         + jnp.dot(bo_ref[...], wob_ref[...], preferred_element_type=F32))
    x1 = x_ref[...] + _rms(y) * pg_ref[...]
    gate = jax.nn.sigmoid(jnp.dot(x1.astype(BF16), wgate_ref[...], preferred_element_type=F32))
    pe = jnp.dot(p_ref[...].astype(BF16), wple_ref[...], preferred_element_type=F32)
    y_ref[...] = x1 + _rms(gate * pe) * plg_ref[...]


def _outproj(a, bo, x, p, w, tm):
    m, d = x.shape
    row = lambda width: pl.BlockSpec((tm, width), lambda i: (i, 0))
    full = lambda arr: pl.BlockSpec(arr.shape, lambda i: (0,) * arr.ndim)
    consts = [w["wo_a"], w["wo_b"], w["post_g"], w["ple_gate"], w["ple_w"], w["ple_g"]]
    return pl.pallas_call(
        _outproj_kernel,
        grid=(m // tm,),
        in_specs=[row(GROUP_W), row(GROUP_W), row(d), row(p.shape[1])] + [full(c) for c in consts],
        out_specs=row(d),
        out_shape=jax.ShapeDtypeStruct((m, d), F32),
        compiler_params=_cparams(("parallel",)),
        name="outproj",
    )(a, bo, x, p, *consts)


def _layer_weights(l, norm_pre_g, norm_post_g, w_in, lam_q1, lam_k1, lam_q2, lam_k2, attn_norm_g, conv_w, conv_b,
                   w_q_b, w_k_b, w_v_b, w_if, b_if, mlstm_norm_g, mlstm_skip, w_out, ple_w, ple_gate_w, ple_norm_g):
    row = lambda v: v.reshape(1, -1)
    wo = w_out[l].astype(BF16)
    return {
        "pre_g": row(norm_pre_g[l]), "w_in": w_in[l].astype(BF16),
        "lam": jnp.stack([lam_q1[l], lam_k1[l], lam_q2[l], lam_k2[l]]).astype(F32),
        "attn_g": row(attn_norm_g[l]),
        "conv_w": conv_w[l], "conv_b": row(conv_b[l]),
        "wq": w_q_b[l].astype(BF16), "wk": w_k_b[l].astype(BF16), "wv": w_v_b[l].astype(BF16),
        "wif": w_if[l].astype(BF16), "wif_t": w_if[l].T.astype(BF16),
        "bif_row": row(b_if[l]), "bif_col": b_if[l].reshape(-1, 1),
        "mlstm_g": row(mlstm_norm_g[l]), "mlstm_skip": row(mlstm_skip[l]),
        "wo_a": wo[:GROUP_W], "wo_b": wo[GROUP_W:], "post_g": row(norm_post_g[l]),
        "ple_gate": ple_gate_w[l].astype(BF16), "ple_w": ple_w[l].astype(BF16), "ple_g": row(ple_norm_g[l]),
    }


def _prompt_layer(x, p, w, lam_init):
    b, s, d = x.shape
    x2 = x.reshape(b * s, d)
    q, k, v, rest, kb, vt = _inproj(x2, w["pre_g"], w["w_in"], tm=PROMPT_ROWS, kv_block=FLASH_BLOCK)
    rest3 = rest.reshape(b, s, 4 * GROUP_W)
    a = _flash(q.reshape(b, s, GROUP_W), kb.reshape(b, s, GROUP_W), vt, rest3,
               w["lam"], w["attn_g"], lam_init, tq=FLASH_BLOCK)
    hd = HEAD_DIM
    bo, c1, n1, m1 = _mlstm(rest3, jnp.zeros((b, CONV_W - 1, GROUP_W), F32),
                            jnp.zeros((b, N_HEADS, hd, hd), F32), jnp.zeros((b, N_HEADS, 1, hd), F32),
                            jnp.zeros((b, N_HEADS, 1, 1), F32), w, t=PROMPT_ROWS, chunk=PROMPT_CHUNK,
                            n_valid=PROMPT_CHUNK, group=PROMPT_GROUP)
    y = _outproj(a.reshape(b * s, GROUP_W), bo.reshape(b * s, GROUP_W), x2, p.reshape(b * s, -1), w,
                 tm=PROMPT_ROWS)
    state = (k.reshape(b, s, N_HEADS, hd), v.reshape(b, s, N_HEADS, hd),
             rest3[:, s - (CONV_W - 1):, GROUP_W:2 * GROUP_W],
             c1, n1.reshape(b, N_HEADS, hd), m1.reshape(b, N_HEADS))
    return y.reshape(b, s, d), state


def _sample_layer(x, p, n_valid, cache_k, cache_v, page_table, conv0, c0, n0, m0, w, lam_init):
    db, r, d = x.shape
    hd = HEAD_DIM
    assert CONV_W - 1 <= n_valid <= r
    x2 = x.reshape(db * r, d)
    q, k, v, rest = _inproj(x2, w["pre_g"], w["w_in"], tm=db * r)
    rest3 = rest.reshape(db, r, 4 * GROUP_W)
    k3 = k.reshape(db, r, GROUP_W)
    v3 = v.reshape(db, r, GROUP_W)
    a = _paged(q.reshape(db, r, GROUP_W), k3, v3, rest3, cache_k, cache_v, page_table, w["lam"], w["attn_g"],
               lam_init, n_valid)
    bo, c1, n1, m1 = _mlstm(rest3, conv0, c0, n0.reshape(db, N_HEADS, 1, hd), m0.reshape(db, N_HEADS, 1, 1),
                            w, t=r, chunk=r, n_valid=n_valid, group=SAMPLE_GROUP)
    y = _outproj(a.reshape(db * r, GROUP_W), bo.reshape(db * r, GROUP_W), x2, p.reshape(db * r, -1), w, tm=db * r)
    state = (k3[:, :n_valid].reshape(db, n_valid, N_HEADS, hd), v3[:, :n_valid].reshape(db, n_valid, N_HEADS, hd),
             rest3[:, n_valid - (CONV_W - 1):n_valid, GROUP_W:2 * GROUP_W],
             c1, n1.reshape(db, N_HEADS, hd), m1.reshape(db, N_HEADS))
    return y.reshape(db, r, d), state


def kernel(x_prompt, x_sample, cache_k, cache_v, state_conv, state_C, state_n, state_m, page_table,
           p_prompt, p_sample, norm_pre_g, norm_post_g, w_in, lam_q1, lam_k1, lam_q2, lam_k2,
           attn_norm_g, conv_w, conv_b, w_q_b, w_k_b, w_v_b, w_if, b_if, mlstm_norm_g, mlstm_skip,
           w_out, ple_w, ple_gate_w, ple_norm_g):
    depth = w_in.shape[0]
    t_dec = x_sample.shape[1]
    pad_rows = lambda a: jnp.pad(a, ((0, 0), (0, SAMPLE_ROWS - t_dec), (0, 0)))
    n_pool, page = cache_k.shape[1], cache_k.shape[2]
    cache_k2 = cache_k.reshape(depth * n_pool, page * N_HEADS, HEAD_DIM)
    cache_v2 = cache_v.reshape(depth * n_pool, page * N_HEADS, HEAD_DIM)
    xp = x_prompt
    xs = pad_rows(x_sample)
    outs_p, outs_s = [], []
    for l in range(depth):
        w = _layer_weights(l, norm_pre_g, norm_post_g, w_in, lam_q1, lam_k1, lam_q2, lam_k2, attn_norm_g,
                           conv_w, conv_b, w_q_b, w_k_b, w_v_b, w_if, b_if, mlstm_norm_g, mlstm_skip,
                           w_out, ple_w, ple_gate_w, ple_norm_g)
        lam_init = 0.8 - 0.6 * math.exp(-0.3 * l)
        xp, st_p = _prompt_layer(xp, p_prompt[l], w, lam_init)
        outs_p.append(st_p)
        xs, st_s = _sample_layer(xs, pad_rows(p_sample[l]), t_dec, cache_k2, cache_v2, page_table + l * n_pool,
                                 state_conv[l], state_C[l], state_n[l], state_m[l], w, lam_init)
        outs_s.append(st_s)
    stack = lambda outs, i: jnp.stack([o[i] for o in outs])
    return ((xp, xs[:, :t_dec]) + tuple(stack(outs_p, i) for i in range(6))
            + tuple(stack(outs_s, i) for i in range(6)))
```

```python
import functools
import math

import jax
import jax.numpy as jnp
from jax import lax
from jax.experimental import pallas as pl
from jax.experimental.pallas import tpu as pltpu

F32 = jnp.float32
BF16 = jnp.bfloat16
EPS = 1e-6
NEG_INF = float("-inf")

N_HEADS = 4
HEAD_DIM = 128
QK_DIM = 64
GROUP_W = N_HEADS * HEAD_DIM
CONV_W = 4
SAMPLE_ROWS = 8
PROMPT_CHUNK = 128
PROMPT_ROWS = 512
FLASH_BLOCK = 512
FLASH_UNROLL = 4
OUTPROJ_SLABS = 2
PROMPT_GROUP = 1
SAMPLE_GROUP = 4
VT_ROWS = HEAD_DIM + 16
Q_SCALE = (QK_DIM ** -0.5) * math.log2(math.e)
PAGES_PER_STEP = 32
V7X_VMEM_LIMIT = 48 * 1024 * 1024


def _cparams(sem):
    return pltpu.CompilerParams(dimension_semantics=sem, vmem_limit_bytes=V7X_VMEM_LIMIT)


def _silu(x):
    return x * jax.nn.sigmoid(x)


def _rms(x):
    return x * lax.rsqrt(jnp.mean(x * x, axis=-1, keepdims=True) + EPS)


def _inproj_kernel(x_ref, g_ref, w_ref, q_ref, k_ref, v_ref, rest_ref, *maybe_flash_refs, kv_block):
    hb = (_rms(x_ref[...]) * g_ref[...]).astype(BF16)

    def proj(c):
        return jnp.dot(hb, w_ref[:, c * GROUP_W:(c + 1) * GROUP_W], preferred_element_type=F32)

    q_ref[...] = (proj(0) * Q_SCALE).astype(BF16)
    k = proj(1)
    v = proj(2)
    if not kv_block:
        k_ref[...] = k
        v_ref[...] = v
    else:
        for h in range(N_HEADS):
            cols = slice(h * HEAD_DIM, (h + 1) * HEAD_DIM)
            k_ref[pl.ds(h, k.shape[0], stride=N_HEADS), :] = k[:, cols]
            v_ref[pl.ds(h, v.shape[0], stride=N_HEADS), :] = v[:, cols]
        kb_ref, vt_ref = maybe_flash_refs
        kb_ref[...] = k.astype(BF16)
        for h in range(N_HEADS):
            for c in range(v.shape[0] // kv_block):
                blk = v[c * kv_block:(c + 1) * kv_block, h * HEAD_DIM:(h + 1) * HEAD_DIM]
                vt_ref[h, c, 0:HEAD_DIM, :] = blk.T.astype(BF16)
                vt_ref[h, c, HEAD_DIM:VT_ROWS, :] = jnp.ones((VT_ROWS - HEAD_DIM, kv_block), BF16)
    for c in range(3, 7):
        rest_ref[:, (c - 3) * GROUP_W:(c - 2) * GROUP_W] = proj(c)


def _inproj(x, g, w_bf, tm, kv_block=0):
    m, d = x.shape
    n_in = w_bf.shape[1]
    row = lambda w: pl.BlockSpec((tm, w), lambda i: (i, 0))
    out_specs = [row(GROUP_W)] * 3 + [row(4 * GROUP_W)]
    out_shape = [jax.ShapeDtypeStruct((m, GROUP_W), BF16),
                 jax.ShapeDtypeStruct((m, GROUP_W), F32),
                 jax.ShapeDtypeStruct((m, GROUP_W), F32),
                 jax.ShapeDtypeStruct((m, 4 * GROUP_W), F32)]
    if kv_block:
        for i_out in (1, 2):
            out_specs[i_out] = pl.BlockSpec((tm * N_HEADS, HEAD_DIM), lambda i: (i, 0))
            out_shape[i_out] = jax.ShapeDtypeStruct((m * N_HEADS, HEAD_DIM), F32)
        out_specs += [row(GROUP_W),
                      pl.BlockSpec((N_HEADS, tm // kv_block, VT_ROWS, kv_block), lambda i: (0, i, 0, 0))]
        out_shape += [jax.ShapeDtypeStruct((m, GROUP_W), BF16),
                      jax.ShapeDtypeStruct((N_HEADS, m // kv_block, VT_ROWS, kv_block), BF16)]
    return pl.pallas_call(
        functools.partial(_inproj_kernel, kv_block=kv_block),
        grid=(m // tm,),
        in_specs=[row(d), pl.BlockSpec((1, d), lambda i: (0, 0)),
                  pl.BlockSpec((d, n_in), lambda i: (0, 0))],
        out_specs=out_specs,
        out_shape=out_shape,
        compiler_params=_cparams(("parallel",)),
        name="inproj",
    )(x, g, w_bf)


def _split_maps(q):
    q = q.astype(F32)
    lane = lax.broadcasted_iota(jnp.int32, q.shape, 1)
    zero = jnp.zeros_like(q)
    return jnp.concatenate([jnp.where(lane < QK_DIM, q, zero), jnp.where(lane >= QK_DIM, q, zero)], axis=0)


def _lambda(lam_ref, lam_init):
    l1 = jnp.sum(lam_ref[0:1, :] * lam_ref[1:2, :], axis=-1, keepdims=True)
    l2 = jnp.sum(lam_ref[2:3, :] * lam_ref[3:4, :], axis=-1, keepdims=True)
    return jnp.exp(l1) - jnp.exp(l2) + lam_init


def _flash_kernel(q_ref, k_ref, vt_ref, za_ref, lam_ref, g_ref, o_ref, qq_scr, m_scr, acc_scr,
                  s0_scr, s1_scr, *, tq, lam_init):
    i = pl.program_id(2)
    qq_scr[...] = _split_maps(q_ref[...]).T.astype(BF16)
    m_scr[...] = jnp.full(m_scr.shape, NEG_INF, F32)
    acc_scr[...] = jnp.zeros(acc_scr.shape, F32)

    def scores(j, dst):
        k0 = pl.multiple_of(j * tq, tq)
        dst[...] = jnp.dot(k_ref[pl.ds(k0, tq), :], qq_scr[...], preferred_element_type=F32)

    def consume(src, j, masked):
        s = src[...]
        if masked:
            key = lax.broadcasted_iota(jnp.int32, s.shape, 0)
            qry = lax.broadcasted_iota(jnp.int32, s.shape, 1)
            s = jnp.where(key <= jnp.where(qry >= tq, qry - tq, qry), s, NEG_INF)
        m_prev = m_scr[...]
        m_new = jnp.maximum(m_prev, jnp.max(s, axis=0, keepdims=True))
        p = jnp.exp2(s - m_new)
        alpha = jnp.exp2(m_prev - m_new)
        acc_scr[...] = alpha * acc_scr[...] + jnp.dot(vt_ref[j], p.astype(BF16), preferred_element_type=F32)
        m_scr[...] = m_new

    bufs = (s0_scr, s1_scr)
    scores(0, s0_scr)

    def run(first, count, masked_last):
        for u in range(count):
            last = u == count - 1
            if not (last and masked_last):
                scores(first + u + 1, bufs[(u + 1) % 2])
            consume(bufs[u % 2], first + u, last and masked_last)

    def full_group(jj, carry):
        run(FLASH_UNROLL * jj, FLASH_UNROLL, False)
        return carry

    lax.fori_loop(0, i // FLASH_UNROLL, full_group, 0)

    for rem in range(FLASH_UNROLL):
        @pl.when(i % FLASH_UNROLL == rem)
        def _tail(rem=rem):
            run(i - rem, rem + 1, True)

    lam = _lambda(lam_ref, lam_init)
    o_t = acc_scr[0:HEAD_DIM, :] * (1.0 / acc_scr[HEAD_DIM:HEAD_DIM + 1, :])
    o = (o_t[:, 0:tq] - lam * o_t[:, tq:2 * tq]).T
    o = _rms(o) * g_ref[...] * (1.0 - lam_init)
    o_ref[...] = (o * _silu(za_ref[...])).astype(BF16)


def _flash(q, kb, vt, rest, lamp, attn_g, lam_init, tq):
    b, s, _ = q.shape
    nblk = s // tq
    blk = lambda: pl.BlockSpec((None, tq, HEAD_DIM), lambda bi, h, i: (bi, i, h))
    return pl.pallas_call(
        functools.partial(_flash_kernel, tq=tq, lam_init=lam_init),
        grid=(b, N_HEADS, nblk),
        in_specs=[blk(),
                  pl.BlockSpec((None, s, HEAD_DIM), lambda bi, h, i: (bi, 0, h)),
                  pl.BlockSpec((None, nblk, VT_ROWS, tq), lambda bi, h, i: (h, bi, 0, 0)),
                  blk(),
                  pl.BlockSpec((4, QK_DIM), lambda bi, h, i: (0, 0)),
                  pl.BlockSpec((1, HEAD_DIM), lambda bi, h, i: (0, h))],
        out_specs=blk(),
        out_shape=jax.ShapeDtypeStruct((b, s, GROUP_W), BF16),
        scratch_shapes=[pltpu.VMEM((HEAD_DIM, 2 * tq), BF16),
                        pltpu.VMEM((1, 2 * tq), F32),
                        pltpu.VMEM((VT_ROWS, 2 * tq), F32),
                        pltpu.VMEM((tq, 2 * tq), F32),
                        pltpu.VMEM((tq, 2 * tq), F32)],
        compiler_params=_cparams(("parallel", "parallel", "arbitrary")),
        name="flash_prompt",
    )(q, kb, vt, rest, lamp, attn_g)


def _paged_kernel(pt_ref, q_ref, *refs, n_pages, n_valid, lam_init):
    del pt_ref
    kp = refs[:n_pages]
    vp = refs[n_pages:2 * n_pages]
    kn_ref, vn_ref, za_ref, lam_ref, g_ref, o_ref, qq_scr, bias_scr, m_scr, l_scr, acc_scr = refs[2 * n_pages:]
    r = SAMPLE_ROWS
    half = r // 2
    pw = bias_scr.shape[1]
    step = pl.program_id(1)

    @pl.when(step == 0)
    def _init():
        row = lax.broadcasted_iota(jnp.int32, (r, HEAD_DIM), 0)
        lane = lax.broadcasted_iota(jnp.int32, (r, HEAD_DIM), 1)
        for h in range(N_HEADS):
            q = q_ref[:, h * HEAD_DIM:(h + 1) * HEAD_DIM].astype(F32)
            q_map1 = jnp.where(lane < QK_DIM, q, 0.0)
            q_map2 = pltpu.roll(jnp.where(lane >= QK_DIM, q, 0.0), half, axis=0)
            qq_scr[h * r:(h + 1) * r, :] = jnp.where(row < half, q_map1, q_map2)
        row_head = lax.broadcasted_iota(jnp.int32, bias_scr.shape, 0) // r
        col_head = lax.broadcasted_iota(jnp.int32, bias_scr.shape, 1) % N_HEADS
        bias_scr[...] = jnp.where(row_head == col_head, 0.0, NEG_INF).astype(F32)
        m_scr[...] = jnp.full(m_scr.shape, NEG_INF, F32)
        l_scr[...] = jnp.zeros(l_scr.shape, F32)
        acc_scr[...] = jnp.zeros(acc_scr.shape, F32)

    qq = qq_scr[...].astype(BF16)
    bias = bias_scr[...]
    s = jnp.concatenate(
        [lax.dot_general(qq, kp[j][...].astype(BF16), (((1,), (1,)), ((), ())),
                         preferred_element_type=F32) + bias for j in range(n_pages)], axis=1)
    m_prev = m_scr[...]
    m_new = jnp.maximum(m_prev, jnp.max(s, axis=-1, keepdims=True))
    p = jnp.exp2(s - m_new)
    alpha = jnp.exp2(m_prev - m_new)
    l_scr[...] = alpha * l_scr[...] + jnp.sum(p, axis=-1, keepdims=True)
    p = p.astype(BF16)
    pv = jnp.dot(p[:, 0:pw], vp[0][...].astype(BF16), preferred_element_type=F32)
    for j in range(1, n_pages):
        pv = pv + jnp.dot(p[:, j * pw:(j + 1) * pw], vp[j][...].astype(BF16), preferred_element_type=F32)
    acc_scr[...] = alpha * acc_scr[...] + pv
    m_scr[...] = m_new

    @pl.when(step == pl.num_programs(1) - 1)
    def _finish():
        lam = _lambda(lam_ref, lam_init)
        row = lax.broadcasted_iota(jnp.int32, (r, 1), 0)
        tok = jnp.where(row >= half, row - half, row)
        for h in range(N_HEADS):
            cols = slice(h * HEAD_DIM, (h + 1) * HEAD_DIM)
            rows = slice(h * r, (h + 1) * r)
            qf = qq_scr[rows, :]
            kn = kn_ref[:, cols]
            vn = vn_ref[:, cols]
            s_new = [jnp.where(tok >= j, jnp.sum(qf * kn[j:j + 1, :], axis=-1, keepdims=True), NEG_INF)
                     for j in range(n_valid)]
            m_prev = m_scr[rows, :]
            m_new = m_prev
            for sj in s_new:
                m_new = jnp.maximum(m_new, sj)
            alpha = jnp.exp2(m_prev - m_new)
            l = alpha * l_scr[rows, :]
            acc = alpha * acc_scr[rows, :]
            for j, sj in enumerate(s_new):
                pj = jnp.exp2(sj - m_new)
                l = l + pj
                acc = acc + pj * vn[j:j + 1, :]
            o_maps = acc / l
            o = o_maps - lam * pltpu.roll(o_maps, half, axis=0)
            o = _rms(o) * g_ref[:, cols] * (1.0 - lam_init)
            o_ref[:, cols] = (o * _silu(za_ref[:, cols])).astype(BF16)


def _paged(q, k_new, v_new, rest, cache_k, cache_v, page_table, lamp, attn_g, lam_init, n_valid):
    db = q.shape[0]
    r = SAMPLE_ROWS
    assert 2 * n_valid <= r, "each head packs the tokens of both maps into SAMPLE_ROWS query rows"
    n_pages_total = page_table.shape[1]
    page_rows = cache_k.shape[1]
    p = PAGES_PER_STEP
    seq = lambda w=GROUP_W: pl.BlockSpec((None, r, w), lambda b, s, pt: (b, 0, 0))

    def page_spec(j):
        return pl.BlockSpec((None, page_rows, HEAD_DIM), lambda b, s, pt, j=j: (pt[b, s * p + j], 0, 0))

    grid_spec = pltpu.PrefetchScalarGridSpec(
        num_scalar_prefetch=1,
        grid=(db, n_pages_total // p),
        in_specs=[seq()] + [page_spec(j) for j in range(p)] * 2 + [seq(), seq(), seq(),
                  pl.BlockSpec((4, QK_DIM), lambda b, s, pt: (0, 0)),
                  pl.BlockSpec((1, GROUP_W), lambda b, s, pt: (0, 0))],
        out_specs=seq(),
        scratch_shapes=[pltpu.VMEM((N_HEADS * r, HEAD_DIM), F32),
                        pltpu.VMEM((N_HEADS * r, page_rows), F32),
                        pltpu.VMEM((N_HEADS * r, 1), F32),
                        pltpu.VMEM((N_HEADS * r, 1), F32),
                        pltpu.VMEM((N_HEADS * r, HEAD_DIM), F32)])
    return pl.pallas_call(
        functools.partial(_paged_kernel, n_pages=p, n_valid=n_valid, lam_init=lam_init),
        grid_spec=grid_spec,
        out_shape=jax.ShapeDtypeStruct((db, r, GROUP_W), BF16),
        compiler_params=_cparams(("parallel", "arbitrary")),
        name="paged_attn",
    )(page_table, q, *([cache_k] * p), *([cache_v] * p), k_new, v_new, rest, lamp, attn_g)


def _log_sigmoid(x):
    return jnp.minimum(x, 0.0) - jnp.log1p(jnp.exp(-jnp.abs(x)))


def _bf16_terms(x):
    hi = x.astype(BF16)
    rest = x - hi.astype(F32)
    mid = rest.astype(BF16)
    lo = (rest - mid.astype(F32)).astype(BF16)
    return hi, mid, lo


def _mlstm_kernel(xb_ref, ob_ref, zb_ref, conv0_ref, c0_ref, n0_ref, m0_ref,
                  cw_ref, cb_ref, wq_ref, wk_ref, wv_ref, wif_ref, wift_ref, bifr_ref, bifc_ref,
                  mg_ref, skip_ref,
                  bo_ref, c1_ref, n1_ref, m1_ref,
                  xpad, xch_scr, q_scr, k_scr, v_scr, h_scr, gc_scr, gr_scr, bc_scr, br_scr, c_scr, n_scr, m_scr,
                  *, t, chunk, n_valid, group):
    step = pl.program_id(1)
    n_chunks = t // chunk
    hd = HEAD_DIM
    ri = lax.broadcasted_iota(jnp.int32, (chunk, chunk), 0)
    ci = lax.broadcasted_iota(jnp.int32, (chunk, chunk), 1)
    causal = ci <= ri
    tril = jnp.where(causal, 1.0, 0.0).astype(BF16)
    triu = jnp.where(ri <= ci, 1.0, 0.0).astype(BF16)

    @pl.when(step == 0)
    def _init():
        for g in range(group):
            xpad[g, 0:8, :] = jnp.zeros((8, GROUP_W), F32)
            xpad[g, 8 - (CONV_W - 1):8, :] = conv0_ref[g]
        c_scr[...] = c0_ref[...]
        n_scr[...] = n0_ref[...]
        m_scr[...] = m0_ref[...]

    for g in range(group):
        x = xb_ref[g]
        xpad[g, 8:8 + t, :] = x
        xc = cb_ref[...] + cw_ref[CONV_W - 1:CONV_W, :] * x
        for j in range(CONV_W - 1):
            xc = xc + cw_ref[j:j + 1, :] * xpad[g, 8 - (CONV_W - 1) + j:8 - (CONV_W - 1) + j + t, :]
        tail = xpad[g, t:t + 8, :]
        xpad[g, 0:8, :] = tail
        xch = _silu(xc)
        xch_scr[g] = xch

        gcol = jnp.zeros((t, 2 * N_HEADS), F32) + bifr_ref[...]
        grow = jnp.zeros((2 * N_HEADS, t), F32) + bifc_ref[...]
        for h in range(N_HEADS):
            cols = slice(h * hd, (h + 1) * hd)
            xh = xch[:, cols].astype(BF16)
            qh = jnp.dot(xh, wq_ref[h], preferred_element_type=F32)
            kh = jnp.dot(xh, wk_ref[h], preferred_element_type=F32) * (hd ** -0.5)
            vh = jnp.dot(x[:, cols].astype(BF16), wv_ref[h], preferred_element_type=F32)
            q_scr[g, :, cols] = qh
            k_scr[g, :, cols] = kh
            v_scr[g, :, cols] = vh
            for part, val in enumerate((qh, kh, vh)):
                off = (3 * h + part) * hd
                vb = val.astype(BF16)
                gcol = gcol + jnp.dot(vb, wif_ref[off:off + hd, :], preferred_element_type=F32)
                grow = grow + lax.dot_general(wift_ref[:, off:off + hd], vb, (((1,), (1,)), ((), ())),
                                              preferred_element_type=F32)

        lane = lax.broadcasted_iota(jnp.int32, gcol.shape, 1)
        gcol = jnp.where(lane < N_HEADS, gcol, _log_sigmoid(gcol))
        sub = lax.broadcasted_iota(jnp.int32, grow.shape, 0)
        grow = jnp.where(sub < N_HEADS, grow, _log_sigmoid(grow))
        if n_valid < chunk:
            pos_c = lax.broadcasted_iota(jnp.int32, gcol.shape, 0) % chunk
            gcol = jnp.where(pos_c >= n_valid, jnp.where(lane < N_HEADS, NEG_INF, 0.0), gcol)
            pos_r = lax.broadcasted_iota(jnp.int32, grow.shape, 1) % chunk
            grow = jnp.where(pos_r >= n_valid, jnp.where(sub < N_HEADS, NEG_INF, 0.0), grow)
        gc_scr[g] = gcol
        lf_col = jnp.where(lane >= N_HEADS, gcol, 0.0)
        lf_row = jnp.where(sub >= N_HEADS, grow, 0.0)
        for c in range(n_chunks):
            span = slice(c * chunk, (c + 1) * chunk)
            gr_scr[g, c] = grow[:, span]
            bc_scr[g, span, :] = sum(jnp.dot(tril, part, preferred_element_type=F32)
                                     for part in _bf16_terms(lf_col[span, :]))
            br_scr[g, c] = sum(jnp.dot(part, triu, preferred_element_type=F32)
                               for part in _bf16_terms(lf_row[:, span]))

    def scan_chunk(c, carry):
        r0 = pl.multiple_of(c * chunk, chunk)
        rows = pl.ds(r0, chunk)
        chains = [(g, h) for g in range(group) for h in range(N_HEADS)]
        gates = {g: (gc_scr[g, rows, :], gr_scr[g, c], bc_scr[g, rows, :], br_scr[g, c]) for g in range(group)}
        st = []
        for g, h in chains:
            gc, gr, b_c, b_r = gates[g]
            bc = b_c[:, N_HEADS + h:N_HEADS + h + 1]
            br = b_r[N_HEADS + h:N_HEADS + h + 1, :]
            ic = gc[:, h:h + 1]
            ir = gr[h:h + 1, :]
            m = m_scr[g, h]
            dm = jnp.where(causal, bc - br + ir, NEG_INF)
            st.append(dict(bc=bc, br=br, ic=ic, ir=ir, m=m, dm=dm, inter=bc + m,
                           cols=slice(h * hd, (h + 1) * hd)))
        for s in st:
            s["m_t"] = jnp.maximum(s["inter"], jnp.max(s["dm"], axis=-1, keepdims=True))
        for s in st:
            s["w_d"] = jnp.exp(s["dm"] - s["m_t"])
            s["w_in"] = jnp.exp(s["inter"] - s["m_t"])
        for (g, h), s in zip(chains, st):
            s["q"] = q_scr[g, rows, s["cols"]]
            s["k"] = k_scr[g, rows, s["cols"]]
            s["v"] = v_scr[g, rows, s["cols"]]
            s["qb"] = s["q"].astype(BF16)
            s["kb"] = s["k"].astype(BF16)
            s["sqk"] = lax.dot_general(s["qb"], s["kb"], (((1,), (1,)), ((), ())),
                                       preferred_element_type=F32) * s["w_d"]
        for (g, h), s in zip(chains, st):
            s["cmat"] = c_scr[g, h]
            s["nvec"] = n_scr[g, h]
            s["num"] = (s["w_in"] * jnp.dot(s["qb"], s["cmat"].astype(BF16), preferred_element_type=F32)
                        + jnp.dot(s["sqk"].astype(BF16), s["v"].astype(BF16), preferred_element_type=F32))
        for s in st:
            s["den"] = (s["w_in"] * jnp.sum(s["q"] * s["nvec"], axis=-1, keepdims=True)
                        + jnp.sum(s["sqk"], axis=-1, keepdims=True))
        for (g, h), s in zip(chains, st):
            h_scr[g, rows, s["cols"]] = s["num"] / jnp.maximum(jnp.abs(s["den"]), jnp.exp(-s["m_t"]))
        for (g, h), s in zip(chains, st):
            b_l = s["bc"][chunk - 1:chunk, :]
            g_r = b_l - s["br"] + s["ir"]
            m_new = jnp.maximum(b_l + s["m"], jnp.max(g_r, axis=-1, keepdims=True))
            decay = jnp.exp(b_l + s["m"] - m_new)
            w_g = jnp.exp(b_l - s["bc"] + s["ic"] - m_new)
            c_scr[g, h] = decay * s["cmat"] + lax.dot_general(
                s["kb"], (w_g * s["v"]).astype(BF16), (((0,), (0,)), ((), ())), preferred_element_type=F32)
            n_scr[g, h] = decay * s["nvec"] + jnp.sum(w_g * s["k"], axis=0, keepdims=True)
            m_scr[g, h] = m_new
        return carry

    if n_chunks == 1:
        scan_chunk(0, 0)
    else:
        lax.fori_loop(0, n_chunks, scan_chunk, 0)

    for g in range(group):
        for h in range(N_HEADS):
            cols = slice(h * hd, (h + 1) * hd)
            hb = h_scr[g, :, cols] * jax.nn.sigmoid(ob_ref[g, :, cols])
            hb = _rms(hb) * mg_ref[:, cols] + skip_ref[:, cols] * xch_scr[g, :, cols]
            bo_ref[g, :, cols] = (hb * _silu(zb_ref[g, :, cols])).astype(BF16)

    @pl.when(step == pl.num_programs(1) - 1)
    def _state_out():
        c1_ref[...] = c_scr[...]
        n1_ref[...] = n_scr[...]
        m1_ref[...] = m_scr[...]


def _mlstm(rest, conv0, c0, n0, m0, w, t, chunk, n_valid, group):
    bn, s, _ = rest.shape
    hd = HEAD_DIM
    tok = lambda col: pl.BlockSpec((group, t, GROUP_W), lambda b, j, col=col: (b, j, col))
    per_b = lambda *shape: pl.BlockSpec((group,) + shape, lambda b, j: (b,) + (0,) * len(shape))
    full = lambda *shape: pl.BlockSpec(shape, lambda b, j: (0,) * len(shape))
    state_shapes = [(N_HEADS, hd, hd), (N_HEADS, 1, hd), (N_HEADS, 1, 1)]
    state_specs = [per_b(*shape) for shape in state_shapes]
    return pl.pallas_call(
        functools.partial(_mlstm_kernel, t=t, chunk=chunk, n_valid=n_valid, group=group),
        grid=(bn // group, s // t),
        in_specs=[tok(1), tok(2), tok(3), per_b(CONV_W - 1, GROUP_W)] + state_specs + [
            full(CONV_W, GROUP_W), full(1, GROUP_W),
            full(N_HEADS, hd, hd), full(N_HEADS, hd, hd), full(N_HEADS, hd, hd),
            full(3 * GROUP_W, 2 * N_HEADS), full(2 * N_HEADS, 3 * GROUP_W),
            full(1, 2 * N_HEADS), full(2 * N_HEADS, 1),
            full(1, GROUP_W), full(1, GROUP_W)],
        out_specs=[tok(0)] + state_specs,
        out_shape=[jax.ShapeDtypeStruct((bn, s, GROUP_W), BF16)]
                  + [jax.ShapeDtypeStruct((bn,) + shape, F32) for shape in state_shapes],
        scratch_shapes=[pltpu.VMEM((group, t + 8, GROUP_W), F32)]
                       + [pltpu.VMEM((group, t, GROUP_W), F32)] * 5
                       + [pltpu.VMEM((group, t, 2 * N_HEADS), F32),
                          pltpu.VMEM((group, t // chunk, 2 * N_HEADS, chunk), F32)] * 2
                       + [pltpu.VMEM((group,) + shape, F32) for shape in state_shapes],
        compiler_params=_cparams(("parallel", "arbitrary")),
        name="mlstm",
    )(rest, rest, rest, conv0, c0, n0, m0,
      w["conv_w"], w["conv_b"], w["wq"], w["wk"], w["wv"], w["wif"], w["wif_t"], w["bif_row"], w["bif_col"],
      w["mlstm_g"], w["mlstm_skip"])


def _outproj_kernel(a_ref, bo_ref, x_ref, p_ref, woa_ref, wob_ref, pg_ref, wgate_ref, wple_ref, plg_ref, y_ref):
    tm = x_ref.shape[0]
    n_slabs = OUTPROJ_SLABS if tm % (8 * OUTPROJ_SLABS) == 0 else 1
    slabs = [slice(c * tm // n_slabs, (c + 1) * tm // n_slabs) for c in range(n_slabs)]
    y = [jnp.dot(a_ref[r, :], woa_ref[...], preferred_element_type=F32)
         + jnp.dot(bo_ref[r, :], wob_ref[...], preferred_element_type=F32) for r in slabs]
    pe = [jnp.dot(p_ref[r, :].astype(BF16), wple_ref[...], preferred_element_type=F32) for r in slabs]
    x1 = [x_ref[r, :] + _rms(y_r) * pg_ref[...] for r, y_r in zip(slabs, y)]
    gate = [jax.nn.sigmoid(jnp.dot(x.astype(BF16), wgate_ref[...], preferred_element_type=F32)) for x in x1]
    for r, x, g, e in zip(slabs, x1, gate, pe):
        y_ref[r, :] = x + _rms(g * e) * plg_ref[...]


def _outproj(a, bo, x, p, w, tm):
    m, d = x.shape
    row = lambda width: pl.BlockSpec((tm, width), lambda i: (i, 0))
    full = lambda arr: pl.BlockSpec(arr.shape, lambda i: (0,) * arr.ndim)
    consts = [w["wo_a"], w["wo_b"], w["post_g"], w["ple_gate"], w["ple_w"], w["ple_g"]]
    return pl.pallas_call(
        _outproj_kernel,
        grid=(m // tm,),
        in_specs=[row(GROUP_W), row(GROUP_W), row(d), row(p.shape[1])] + [full(c) for c in consts],
        out_specs=row(d),
        out_shape=jax.ShapeDtypeStruct((m, d), F32),
        compiler_params=_cparams(("parallel",)),
        name="outproj",
    )(a, bo, x, p, *consts)


def _layer_weights(l, norm_pre_g, norm_post_g, w_in, lam_q1, lam_k1, lam_q2, lam_k2, attn_norm_g, conv_w, conv_b,
                   w_q_b, w_k_b, w_v_b, w_if, b_if, mlstm_norm_g, mlstm_skip, w_out, ple_w, ple_gate_w, ple_norm_g):
    row = lambda v: v.reshape(1, -1)
    wo = w_out[l].astype(BF16)
    return {
        "pre_g": row(norm_pre_g[l]), "w_in": w_in[l].astype(BF16),
        "lam": jnp.stack([lam_q1[l], lam_k1[l], lam_q2[l], lam_k2[l]]).astype(F32),
        "attn_g": row(attn_norm_g[l]),
        "conv_w": conv_w[l], "conv_b": row(conv_b[l]),
        "wq": w_q_b[l].astype(BF16), "wk": w_k_b[l].astype(BF16), "wv": w_v_b[l].astype(BF16),
        "wif": w_if[l].astype(BF16), "wif_t": w_if[l].T.astype(BF16),
        "bif_row": row(b_if[l]), "bif_col": b_if[l].reshape(-1, 1),
        "mlstm_g": row(mlstm_norm_g[l]), "mlstm_skip": row(mlstm_skip[l]),
        "wo_a": wo[:GROUP_W], "wo_b": wo[GROUP_W:], "post_g": row(norm_post_g[l]),
        "ple_gate": ple_gate_w[l].astype(BF16), "ple_w": ple_w[l].astype(BF16), "ple_g": row(ple_norm_g[l]),
    }


def _prompt_layer(x, p, w, lam_init):
    b, s, d = x.shape
    x2 = x.reshape(b * s, d)
    q, k, v, rest, kb, vt = _inproj(x2, w["pre_g"], w["w_in"], tm=PROMPT_ROWS, kv_block=FLASH_BLOCK)
    rest3 = rest.reshape(b, s, 4 * GROUP_W)
    a = _flash(q.reshape(b, s, GROUP_W), kb.reshape(b, s, GROUP_W), vt, rest3,
               w["lam"], w["attn_g"], lam_init, tq=FLASH_BLOCK)
    hd = HEAD_DIM
    bo, c1, n1, m1 = _mlstm(rest3, jnp.zeros((b, CONV_W - 1, GROUP_W), F32),
                            jnp.zeros((b, N_HEADS, hd, hd), F32), jnp.zeros((b, N_HEADS, 1, hd), F32),
                            jnp.zeros((b, N_HEADS, 1, 1), F32), w, t=PROMPT_ROWS, chunk=PROMPT_CHUNK,
                            n_valid=PROMPT_CHUNK, group=PROMPT_GROUP)
    y = _outproj(a.reshape(b * s, GROUP_W), bo.reshape(b * s, GROUP_W), x2, p.reshape(b * s, -1), w,
                 tm=PROMPT_ROWS)
    state = (k.reshape(b, s, N_HEADS, hd), v.reshape(b, s, N_HEADS, hd),
             rest3[:, s - (CONV_W - 1):, GROUP_W:2 * GROUP_W],
             c1, n1.reshape(b, N_HEADS, hd), m1.reshape(b, N_HEADS))
    return y.reshape(b, s, d), state


def _sample_layer(x, p, n_valid, cache_k, cache_v, page_table, conv0, c0, n0, m0, w, lam_init):
    db, r, d = x.shape
    hd = HEAD_DIM
    assert CONV_W - 1 <= n_valid <= r
    x2 = x.reshape(db * r, d)
    q, k, v, rest = _inproj(x2, w["pre_g"], w["w_in"], tm=db * r)
    rest3 = rest.reshape(db, r, 4 * GROUP_W)
    k3 = k.reshape(db, r, GROUP_W)
    v3 = v.reshape(db, r, GROUP_W)
    a = _paged(q.reshape(db, r, GROUP_W), k3, v3, rest3, cache_k, cache_v, page_table, w["lam"], w["attn_g"],
               lam_init, n_valid)
    bo, c1, n1, m1 = _mlstm(rest3, conv0, c0, n0.reshape(db, N_HEADS, 1, hd), m0.reshape(db, N_HEADS, 1, 1),
                            w, t=r, chunk=r, n_valid=n_valid, group=SAMPLE_GROUP)
    y = _outproj(a.reshape(db * r, GROUP_W), bo.reshape(db * r, GROUP_W), x2, p.reshape(db * r, -1), w, tm=db * r)
    state = (k3[:, :n_valid].reshape(db, n_valid, N_HEADS, hd), v3[:, :n_valid].reshape(db, n_valid, N_HEADS, hd),
             rest3[:, n_valid - (CONV_W - 1):n_valid, GROUP_W:2 * GROUP_W],
             c1, n1.reshape(db, N_HEADS, hd), m1.reshape(db, N_HEADS))
    return y.reshape(db, r, d), state


def kernel(x_prompt, x_sample, cache_k, cache_v, state_conv, state_C, state_n, state_m, page_table,
           p_prompt, p_sample, norm_pre_g, norm_post_g, w_in, lam_q1, lam_k1, lam_q2, lam_k2,
           attn_norm_g, conv_w, conv_b, w_q_b, w_k_b, w_v_b, w_if, b_if, mlstm_norm_g, mlstm_skip,
           w_out, ple_w, ple_gate_w, ple_norm_g):
    depth = w_in.shape[0]
    t_dec = x_sample.shape[1]
    pad_rows = lambda a: jnp.pad(a, ((0, 0), (0, SAMPLE_ROWS - t_dec), (0, 0)))
    n_pool, page = cache_k.shape[1], cache_k.shape[2]
    cache_k2 = cache_k.reshape(depth * n_pool, page * N_HEADS, HEAD_DIM)
    cache_v2 = cache_v.reshape(depth * n_pool, page * N_HEADS, HEAD_DIM)
    xp = x_prompt
    xs = pad_rows(x_sample)
    outs_p, outs_s = [], []
    for l in range(depth):
        w = _layer_weights(l, norm_pre_g, norm_post_g, w_in, lam_q1, lam_k1, lam_q2, lam_k2, attn_norm_g,
                           conv_w, conv_b, w_q_b, w_k_b, w_v_b, w_if, b_if, mlstm_norm_g, mlstm_skip,
                           w_out, ple_w, ple_gate_w, ple_norm_g)
        lam_init = 0.8 - 0.6 * math.exp(-0.3 * l)
        xp, st_p = _prompt_layer(xp, p_prompt[l], w, lam_init)
        outs_p.append(st_p)
        xs, st_s = _sample_layer(xs, pad_rows(p_sample[l]), t_dec, cache_k2, cache_v2, page_table + l * n_pool,
                                 state_conv[l], state_C[l], state_n[l], state_m[l], w, lam_init)
        outs_s.append(st_s)
    stack = lambda outs, i: jnp.stack([o[i] for o in outs])
    return ((xp, xs[:, :t_dec]) + tuple(stack(outs_p, i) for i in range(6))
            + tuple(stack(outs_s, i) for i in range(6)))
```

```python
import functools
import math

import jax
import jax.numpy as jnp
from jax import lax
from jax.experimental import pallas as pl
from jax.experimental.pallas import tpu as pltpu

F32 = jnp.float32
BF16 = jnp.bfloat16
EPS = 1e-6
NEG_INF = float("-inf")

N_HEADS = 4
HEAD_DIM = 128
QK_DIM = 64
GROUP_W = N_HEADS * HEAD_DIM
CONV_W = 4
SAMPLE_ROWS = 8
PROMPT_CHUNK = 128
PROMPT_ROWS = 512
FLASH_BLOCK = 512
FLASH_UNROLL = 4
OUTPROJ_SLABS = 2
PROMPT_GROUP = 1
SAMPLE_GROUP = 4
VT_ROWS = HEAD_DIM + 16
Q_SCALE = (QK_DIM ** -0.5) * math.log2(math.e)
PAGES_PER_STEP = 32
V7X_VMEM_LIMIT = 48 * 1024 * 1024


def _cparams(sem):
    return pltpu.CompilerParams(dimension_semantics=sem, vmem_limit_bytes=V7X_VMEM_LIMIT)


def _sigmoid(x):
    return 0.5 * jnp.tanh(0.5 * x) + 0.5


def _silu(x):
    return x * _sigmoid(x)


def _rms(x):
    return x * lax.rsqrt(jnp.mean(x * x, axis=-1, keepdims=True) + EPS)


def _inproj_kernel(x_ref, g_ref, w_ref, q_ref, k_ref, v_ref, rest_ref, *maybe_flash_refs, kv_block):
    hb = (_rms(x_ref[...]) * g_ref[...]).astype(BF16)

    def proj(c):
        return jnp.dot(hb, w_ref[:, c * GROUP_W:(c + 1) * GROUP_W], preferred_element_type=F32)

    q = proj(0) * Q_SCALE
    k = proj(1)
    v = proj(2)
    if not kv_block:
        q_ref[...] = q.astype(BF16)
        k_ref[...] = k
        v_ref[...] = v
    else:
        for h in range(N_HEADS):
            for c in range(q.shape[0] // kv_block):
                blk = q[c * kv_block:(c + 1) * kv_block, h * HEAD_DIM:(h + 1) * HEAD_DIM]
                q_ref[h, c] = _split_maps(blk).T.astype(BF16)
        for h in range(N_HEADS):
            cols = slice(h * HEAD_DIM, (h + 1) * HEAD_DIM)
            k_ref[pl.ds(h, k.shape[0], stride=N_HEADS), :] = k[:, cols]
            v_ref[pl.ds(h, v.shape[0], stride=N_HEADS), :] = v[:, cols]
        kb_ref, vt_ref = maybe_flash_refs
        kb_ref[...] = k.astype(BF16)
        for h in range(N_HEADS):
            for c in range(v.shape[0] // kv_block):
                blk = v[c * kv_block:(c + 1) * kv_block, h * HEAD_DIM:(h + 1) * HEAD_DIM]
                vt_ref[h, c, 0:HEAD_DIM, :] = blk.T.astype(BF16)
                vt_ref[h, c, HEAD_DIM:VT_ROWS, :] = jnp.ones((VT_ROWS - HEAD_DIM, kv_block), BF16)
    for c in range(3, 7):
        rest_ref[:, (c - 3) * GROUP_W:(c - 2) * GROUP_W] = proj(c)


def _inproj(x, g, w_bf, tm, kv_block=0):
    m, d = x.shape
    n_in = w_bf.shape[1]
    row = lambda w: pl.BlockSpec((tm, w), lambda i: (i, 0))
    out_specs = [row(GROUP_W)] * 3 + [row(4 * GROUP_W)]
    out_shape = [jax.ShapeDtypeStruct((m, GROUP_W), BF16),
                 jax.ShapeDtypeStruct((m, GROUP_W), F32),
                 jax.ShapeDtypeStruct((m, GROUP_W), F32),
                 jax.ShapeDtypeStruct((m, 4 * GROUP_W), F32)]
    if kv_block:
        out_specs[0] = pl.BlockSpec((N_HEADS, tm // kv_block, HEAD_DIM, 2 * kv_block), lambda i: (0, i, 0, 0))
        out_shape[0] = jax.ShapeDtypeStruct((N_HEADS, m // kv_block, HEAD_DIM, 2 * kv_block), BF16)
        for i_out in (1, 2):
            out_specs[i_out] = pl.BlockSpec((tm * N_HEADS, HEAD_DIM), lambda i: (i, 0))
            out_shape[i_out] = jax.ShapeDtypeStruct((m * N_HEADS, HEAD_DIM), F32)
        out_specs += [row(GROUP_W),
                      pl.BlockSpec((N_HEADS, tm // kv_block, VT_ROWS, kv_block), lambda i: (0, i, 0, 0))]
        out_shape += [jax.ShapeDtypeStruct((m, GROUP_W), BF16),
                      jax.ShapeDtypeStruct((N_HEADS, m // kv_block, VT_ROWS, kv_block), BF16)]
    return pl.pallas_call(
        functools.partial(_inproj_kernel, kv_block=kv_block),
        grid=(m // tm,),
        in_specs=[row(d), pl.BlockSpec((1, d), lambda i: (0, 0)),
                  pl.BlockSpec((d, n_in), lambda i: (0, 0))],
        out_specs=out_specs,
        out_shape=out_shape,
        compiler_params=_cparams(("parallel",)),
        name="inproj",
    )(x, g, w_bf)


def _split_maps(q):
    lane = lax.broadcasted_iota(jnp.int32, q.shape, 1)
    zero = jnp.zeros_like(q)
    return jnp.concatenate([jnp.where(lane < QK_DIM, q, zero), jnp.where(lane >= QK_DIM, q, zero)], axis=0)


def _lambda(lam_ref, lam_init):
    l1 = jnp.sum(lam_ref[0:1, :] * lam_ref[1:2, :], axis=-1, keepdims=True)
    l2 = jnp.sum(lam_ref[2:3, :] * lam_ref[3:4, :], axis=-1, keepdims=True)
    return jnp.exp(l1) - jnp.exp(l2) + lam_init


def _flash_kernel(qt_ref, k_ref, vt_ref, za_ref, lam_ref, g_ref, o_ref, m_scr, acc_scr,
                  s0_scr, s1_scr, *, tq, lam_init):
    i = pl.program_id(2)
    m_scr[...] = jnp.full(m_scr.shape, NEG_INF, F32)
    acc_scr[...] = jnp.zeros(acc_scr.shape, F32)

    def scores(j, dst):
        k0 = pl.multiple_of(j * tq, tq)
        dst[...] = jnp.dot(k_ref[pl.ds(k0, tq), :], qt_ref[...], preferred_element_type=F32)

    def consume(src, j, masked):
        s = src[...]
        if masked:
            key = lax.broadcasted_iota(jnp.int32, s.shape, 0)
            qry = lax.broadcasted_iota(jnp.int32, s.shape, 1)
            s = jnp.where(key <= jnp.where(qry >= tq, qry - tq, qry), s, NEG_INF)
        m_prev = m_scr[...]
        m_new = jnp.maximum(m_prev, jnp.max(s, axis=0, keepdims=True))
        p = jnp.exp2(s - m_new)
        alpha = jnp.exp2(m_prev - m_new)
        acc_scr[...] = alpha * acc_scr[...] + jnp.dot(vt_ref[j], p.astype(BF16), preferred_element_type=F32)
        m_scr[...] = m_new

    bufs = (s0_scr, s1_scr)
    scores(0, s0_scr)

    def run(first, count, masked_last):
        for u in range(count):
            last = u == count - 1
            if not (last and masked_last):
                scores(first + u + 1, bufs[(u + 1) % 2])
            consume(bufs[u % 2], first + u, last and masked_last)

    def full_group(jj, carry):
        run(FLASH_UNROLL * jj, FLASH_UNROLL, False)
        return carry

    lax.fori_loop(0, i // FLASH_UNROLL, full_group, 0)

    for rem in range(FLASH_UNROLL):
        @pl.when(i % FLASH_UNROLL == rem)
        def _tail(rem=rem):
            run(i - rem, rem + 1, True)

    lam = _lambda(lam_ref, lam_init)
    o_t = acc_scr[0:HEAD_DIM, :] * (1.0 / acc_scr[HEAD_DIM:HEAD_DIM + 1, :])
    o = (o_t[:, 0:tq] - lam * o_t[:, tq:2 * tq]).T
    o = _rms(o) * g_ref[...] * (1.0 - lam_init)
    o_ref[...] = (o * _silu(za_ref[...])).astype(BF16)


def _flash(qt, kb, vt, rest, lamp, attn_g, lam_init, tq):
    b, s, _ = kb.shape
    nblk = s // tq
    blk = lambda: pl.BlockSpec((None, tq, HEAD_DIM), lambda bi, h, i: (bi, i, h))
    return pl.pallas_call(
        functools.partial(_flash_kernel, tq=tq, lam_init=lam_init),
        grid=(b, N_HEADS, nblk),
        in_specs=[pl.BlockSpec((None, None, HEAD_DIM, 2 * tq), lambda bi, h, i: (h, bi * nblk + i, 0, 0)),
                  pl.BlockSpec((None, s, HEAD_DIM), lambda bi, h, i: (bi, 0, h)),
                  pl.BlockSpec((None, nblk, VT_ROWS, tq), lambda bi, h, i: (h, bi, 0, 0)),
                  blk(),
                  pl.BlockSpec((4, QK_DIM), lambda bi, h, i: (0, 0)),
                  pl.BlockSpec((1, HEAD_DIM), lambda bi, h, i: (0, h))],
        out_specs=blk(),
        out_shape=jax.ShapeDtypeStruct((b, s, GROUP_W), BF16),
        scratch_shapes=[pltpu.VMEM((1, 2 * tq), F32),
                        pltpu.VMEM((VT_ROWS, 2 * tq), F32),
                        pltpu.VMEM((tq, 2 * tq), F32),
                        pltpu.VMEM((tq, 2 * tq), F32)],
        compiler_params=_cparams(("parallel", "parallel", "arbitrary")),
        name="flash_prompt",
    )(qt, kb, vt, rest, lamp, attn_g)


def _paged_kernel(pt_ref, q_ref, *refs, n_pages, n_valid, lam_init):
    del pt_ref
    kp = refs[:n_pages]
    vp = refs[n_pages:2 * n_pages]
    kn_ref, vn_ref, za_ref, lam_ref, g_ref, o_ref, qq_scr, bias_scr, m_scr, l_scr, acc_scr = refs[2 * n_pages:]
    r = SAMPLE_ROWS
    half = r // 2
    pw = bias_scr.shape[1]
    step = pl.program_id(1)

    @pl.when(step == 0)
    def _init():
        row = lax.broadcasted_iota(jnp.int32, (r, HEAD_DIM), 0)
        lane = lax.broadcasted_iota(jnp.int32, (r, HEAD_DIM), 1)
        for h in range(N_HEADS):
            q = q_ref[:, h * HEAD_DIM:(h + 1) * HEAD_DIM].astype(F32)
            q_map1 = jnp.where(lane < QK_DIM, q, 0.0)
            q_map2 = pltpu.roll(jnp.where(lane >= QK_DIM, q, 0.0), half, axis=0)
            qq_scr[h * r:(h + 1) * r, :] = jnp.where(row < half, q_map1, q_map2)
        row_head = lax.broadcasted_iota(jnp.int32, bias_scr.shape, 0) // r
        col_head = lax.broadcasted_iota(jnp.int32, bias_scr.shape, 1) % N_HEADS
        bias_scr[...] = jnp.where(row_head == col_head, 0.0, NEG_INF).astype(F32)
        m_scr[...] = jnp.full(m_scr.shape, NEG_INF, F32)
        l_scr[...] = jnp.zeros(l_scr.shape, F32)
        acc_scr[...] = jnp.zeros(acc_scr.shape, F32)

    qq = qq_scr[...].astype(BF16)
    bias = bias_scr[...]
    s = jnp.concatenate(
        [lax.dot_general(qq, kp[j][...].astype(BF16), (((1,), (1,)), ((), ())),
                         preferred_element_type=F32) + bias for j in range(n_pages)], axis=1)
    m_prev = m_scr[...]
    m_new = jnp.maximum(m_prev, jnp.max(s, axis=-1, keepdims=True))
    p = jnp.exp2(s - m_new)
    alpha = jnp.exp2(m_prev - m_new)
    l_scr[...] = alpha * l_scr[...] + jnp.sum(p, axis=-1, keepdims=True)
    p = p.astype(BF16)
    pv = jnp.dot(p[:, 0:pw], vp[0][...].astype(BF16), preferred_element_type=F32)
    for j in range(1, n_pages):
        pv = pv + jnp.dot(p[:, j * pw:(j + 1) * pw], vp[j][...].astype(BF16), preferred_element_type=F32)
    acc_scr[...] = alpha * acc_scr[...] + pv
    m_scr[...] = m_new

    @pl.when(step == pl.num_programs(1) - 1)
    def _finish():
        lam = _lambda(lam_ref, lam_init)
        row = lax.broadcasted_iota(jnp.int32, (r, 1), 0)
        tok = jnp.where(row >= half, row - half, row)
        for h in range(N_HEADS):
            cols = slice(h * HEAD_DIM, (h + 1) * HEAD_DIM)
            rows = slice(h * r, (h + 1) * r)
            qf = qq_scr[rows, :]
            kn = kn_ref[:, cols]
            vn = vn_ref[:, cols]
            s_new = [jnp.where(tok >= j, jnp.sum(qf * kn[j:j + 1, :], axis=-1, keepdims=True), NEG_INF)
                     for j in range(n_valid)]
            m_prev = m_scr[rows, :]
            m_new = m_prev
            for sj in s_new:
                m_new = jnp.maximum(m_new, sj)
            alpha = jnp.exp2(m_prev - m_new)
            l = alpha * l_scr[rows, :]
            acc = alpha * acc_scr[rows, :]
            for j, sj in enumerate(s_new):
                pj = jnp.exp2(sj - m_new)
                l = l + pj
                acc = acc + pj * vn[j:j + 1, :]
            o_maps = acc / l
            o = o_maps - lam * pltpu.roll(o_maps, half, axis=0)
            o = _rms(o) * g_ref[:, cols] * (1.0 - lam_init)
            o_ref[:, cols] = (o * _silu(za_ref[:, cols])).astype(BF16)


def _paged(q, k_new, v_new, rest, cache_k, cache_v, page_table, lamp, attn_g, lam_init, n_valid):
    db = q.shape[0]
    r = SAMPLE_ROWS
    assert 2 * n_valid <= r, "each head packs the tokens of both maps into SAMPLE_ROWS query rows"
    n_pages_total = page_table.shape[1]
    page_rows = cache_k.shape[1]
    p = PAGES_PER_STEP
    seq = lambda w=GROUP_W: pl.BlockSpec((None, r, w), lambda b, s, pt: (b, 0, 0))

    def page_spec(j):
        return pl.BlockSpec((None, page_rows, HEAD_DIM), lambda b, s, pt, j=j: (pt[b, s * p + j], 0, 0))

    grid_spec = pltpu.PrefetchScalarGridSpec(
        num_scalar_prefetch=1,
        grid=(db, n_pages_total // p),
        in_specs=[seq()] + [page_spec(j) for j in range(p)] * 2 + [seq(), seq(), seq(),
                  pl.BlockSpec((4, QK_DIM), lambda b, s, pt: (0, 0)),
                  pl.BlockSpec((1, GROUP_W), lambda b, s, pt: (0, 0))],
        out_specs=seq(),
        scratch_shapes=[pltpu.VMEM((N_HEADS * r, HEAD_DIM), F32),
                        pltpu.VMEM((N_HEADS * r, page_rows), F32),
                        pltpu.VMEM((N_HEADS * r, 1), F32),
                        pltpu.VMEM((N_HEADS * r, 1), F32),
                        pltpu.VMEM((N_HEADS * r, HEAD_DIM), F32)])
    return pl.pallas_call(
        functools.partial(_paged_kernel, n_pages=p, n_valid=n_valid, lam_init=lam_init),
        grid_spec=grid_spec,
        out_shape=jax.ShapeDtypeStruct((db, r, GROUP_W), BF16),
        compiler_params=_cparams(("parallel", "arbitrary")),
        name="paged_attn",
    )(page_table, q, *([cache_k] * p), *([cache_v] * p), k_new, v_new, rest, lamp, attn_g)


def _log_sigmoid(x):
    return jnp.minimum(x, 0.0) - jnp.log1p(jnp.exp(-jnp.abs(x)))


def _bf16_terms(x):
    hi = x.astype(BF16)
    rest = x - hi.astype(F32)
    mid = rest.astype(BF16)
    lo = (rest - mid.astype(F32)).astype(BF16)
    return hi, mid, lo


def _mlstm_kernel(xb_ref, ob_ref, zb_ref, conv0_ref, c0_ref, n0_ref, m0_ref,
                  cw_ref, cb_ref, wq_ref, wk_ref, wv_ref, wif_ref, wift_ref, bifr_ref, bifc_ref,
                  mg_ref, skip_ref,
                  bo_ref, c1_ref, n1_ref, m1_ref,
                  xpad, xch_scr, q_scr, k_scr, v_scr, h_scr, gc_scr, gr_scr, bc_scr, br_scr, c_scr, n_scr, m_scr,
                  *, t, chunk, n_valid, group):
    step = pl.program_id(1)
    n_chunks = t // chunk
    hd = HEAD_DIM
    ri = lax.broadcasted_iota(jnp.int32, (chunk, chunk), 0)
    ci = lax.broadcasted_iota(jnp.int32, (chunk, chunk), 1)
    causal = ci <= ri
    tril = jnp.where(causal, 1.0, 0.0).astype(BF16)
    triu = jnp.where(ri <= ci, 1.0, 0.0).astype(BF16)

    @pl.when(step == 0)
    def _init():
        for g in range(group):
            xpad[g, 0:8, :] = jnp.zeros((8, GROUP_W), F32)
            xpad[g, 8 - (CONV_W - 1):8, :] = conv0_ref[g]
        c_scr[...] = c0_ref[...]
        n_scr[...] = n0_ref[...]
        m_scr[...] = m0_ref[...]

    for g in range(group):
        x = xb_ref[g]
        xpad[g, 8:8 + t, :] = x
        xc = cb_ref[...] + cw_ref[CONV_W - 1:CONV_W, :] * x
        for j in range(CONV_W - 1):
            xc = xc + cw_ref[j:j + 1, :] * xpad[g, 8 - (CONV_W - 1) + j:8 - (CONV_W - 1) + j + t, :]
        tail = xpad[g, t:t + 8, :]
        xpad[g, 0:8, :] = tail
        xch = _silu(xc)
        xch_scr[g] = xch

        gcol = jnp.zeros((t, 2 * N_HEADS), F32) + bifr_ref[...]
        grow = jnp.zeros((2 * N_HEADS, t), F32) + bifc_ref[...]
        for h in range(N_HEADS):
            cols = slice(h * hd, (h + 1) * hd)
            xh = xch[:, cols].astype(BF16)
            qh = jnp.dot(xh, wq_ref[h], preferred_element_type=F32)
            kh = jnp.dot(xh, wk_ref[h], preferred_element_type=F32) * (hd ** -0.5)
            vh = jnp.dot(x[:, cols].astype(BF16), wv_ref[h], preferred_element_type=F32)
            q_scr[g, :, cols] = qh
            k_scr[g, :, cols] = kh
            v_scr[g, :, cols] = vh
            for part, val in enumerate((qh, kh, vh)):
                off = (3 * h + part) * hd
                vb = val.astype(BF16)
                gcol = gcol + jnp.dot(vb, wif_ref[off:off + hd, :], preferred_element_type=F32)
                grow = grow + lax.dot_general(wift_ref[:, off:off + hd], vb, (((1,), (1,)), ((), ())),
                                              preferred_element_type=F32)

        lane = lax.broadcasted_iota(jnp.int32, gcol.shape, 1)
        gcol = jnp.where(lane < N_HEADS, gcol, _log_sigmoid(gcol))
        sub = lax.broadcasted_iota(jnp.int32, grow.shape, 0)
        grow = jnp.where(sub < N_HEADS, grow, _log_sigmoid(grow))
        if n_valid < chunk:
            pos_c = lax.broadcasted_iota(jnp.int32, gcol.shape, 0) % chunk
            gcol = jnp.where(pos_c >= n_valid, jnp.where(lane < N_HEADS, NEG_INF, 0.0), gcol)
            pos_r = lax.broadcasted_iota(jnp.int32, grow.shape, 1) % chunk
            grow = jnp.where(pos_r >= n_valid, jnp.where(sub < N_HEADS, NEG_INF, 0.0), grow)
        gc_scr[g] = gcol
        lf_col = jnp.where(lane >= N_HEADS, gcol, 0.0)
        lf_row = jnp.where(sub >= N_HEADS, grow, 0.0)
        for c in range(n_chunks):
            span = slice(c * chunk, (c + 1) * chunk)
            gr_scr[g, c] = grow[:, span]
            bc_scr[g, span, :] = sum(jnp.dot(tril, part, preferred_element_type=F32)
                                     for part in _bf16_terms(lf_col[span, :]))
            br_scr[g, c] = sum(jnp.dot(part, triu, preferred_element_type=F32)
                               for part in _bf16_terms(lf_row[:, span]))

    def scan_chunk(c, carry):
        r0 = pl.multiple_of(c * chunk, chunk)
        rows = pl.ds(r0, chunk)
        chains = [(g, h) for g in range(group) for h in range(N_HEADS)]
        gates = {g: (gc_scr[g, rows, :], gr_scr[g, c], bc_scr[g, rows, :], br_scr[g, c]) for g in range(group)}
        st = []
        for g, h in chains:
            gc, gr, b_c, b_r = gates[g]
            bc = b_c[:, N_HEADS + h:N_HEADS + h + 1]
            br = b_r[N_HEADS + h:N_HEADS + h + 1, :]
            ic = gc[:, h:h + 1]
            ir = gr[h:h + 1, :]
            m = m_scr[g, h]
            dm = jnp.where(causal, bc - br + ir, NEG_INF)
            st.append(dict(bc=bc, br=br, ic=ic, ir=ir, m=m, dm=dm, inter=bc + m,
                           cols=slice(h * hd, (h + 1) * hd)))
        for s in st:
            s["m_t"] = jnp.maximum(s["inter"], jnp.max(s["dm"], axis=-1, keepdims=True))
        for s in st:
            s["w_d"] = jnp.exp(s["dm"] - s["m_t"])
            s["w_in"] = jnp.exp(s["inter"] - s["m_t"])
        for (g, h), s in zip(chains, st):
            s["q"] = q_scr[g, rows, s["cols"]]
            s["k"] = k_scr[g, rows, s["cols"]]
            s["v"] = v_scr[g, rows, s["cols"]]
            s["qb"] = s["q"].astype(BF16)
            s["kb"] = s["k"].astype(BF16)
            s["sqk"] = lax.dot_general(s["qb"], s["kb"], (((1,), (1,)), ((), ())),
                                       preferred_element_type=F32) * s["w_d"]
        for (g, h), s in zip(chains, st):
            s["cmat"] = c_scr[g, h]
            s["nvec"] = n_scr[g, h]
            s["num"] = (s["w_in"] * jnp.dot(s["qb"], s["cmat"].astype(BF16), preferred_element_type=F32)
                        + jnp.dot(s["sqk"].astype(BF16), s["v"].astype(BF16), preferred_element_type=F32))
        for s in st:
            s["den"] = (s["w_in"] * jnp.sum(s["q"] * s["nvec"], axis=-1, keepdims=True)
                        + jnp.sum(s["sqk"], axis=-1, keepdims=True))
        for (g, h), s in zip(chains, st):
            h_scr[g, rows, s["cols"]] = s["num"] / jnp.maximum(jnp.abs(s["den"]), jnp.exp(-s["m_t"]))
        for (g, h), s in zip(chains, st):
            b_l = s["bc"][chunk - 1:chunk, :]
            g_r = b_l - s["br"] + s["ir"]
            m_new = jnp.maximum(b_l + s["m"], jnp.max(g_r, axis=-1, keepdims=True))
            decay = jnp.exp(b_l + s["m"] - m_new)
            w_g = jnp.exp(b_l - s["bc"] + s["ic"] - m_new)
            c_scr[g, h] = decay * s["cmat"] + lax.dot_general(
                s["kb"], (w_g * s["v"]).astype(BF16), (((0,), (0,)), ((), ())), preferred_element_type=F32)
            n_scr[g, h] = decay * s["nvec"] + jnp.sum(w_g * s["k"], axis=0, keepdims=True)
            m_scr[g, h] = m_new
        return carry

    if n_chunks == 1:
        scan_chunk(0, 0)
    else:
        lax.fori_loop(0, n_chunks, scan_chunk, 0)

    for g in range(group):
        for h in range(N_HEADS):
            cols = slice(h * hd, (h + 1) * hd)
            hb = h_scr[g, :, cols] * _sigmoid(ob_ref[g, :, cols])
            hb = _rms(hb) * mg_ref[:, cols] + skip_ref[:, cols] * xch_scr[g, :, cols]
            bo_ref[g, :, cols] = (hb * _silu(zb_ref[g, :, cols])).astype(BF16)

    @pl.when(step == pl.num_programs(1) - 1)
    def _state_out():
        c1_ref[...] = c_scr[...]
        n1_ref[...] = n_scr[...]
        m1_ref[...] = m_scr[...]


def _mlstm(rest, conv0, c0, n0, m0, w, t, chunk, n_valid, group):
    bn, s, _ = rest.shape
    hd = HEAD_DIM
    tok = lambda col: pl.BlockSpec((group, t, GROUP_W), lambda b, j, col=col: (b, j, col))
    per_b = lambda *shape: pl.BlockSpec((group,) + shape, lambda b, j: (b,) + (0,) * len(shape))
    full = lambda *shape: pl.BlockSpec(shape, lambda b, j: (0,) * len(shape))
    state_shapes = [(N_HEADS, hd, hd), (N_HEADS, 1, hd), (N_HEADS, 1, 1)]
    state_specs = [per_b(*shape) for shape in state_shapes]
    return pl.pallas_call(
        functools.partial(_mlstm_kernel, t=t, chunk=chunk, n_valid=n_valid, group=group),
        grid=(bn // group, s // t),
        in_specs=[tok(1), tok(2), tok(3), per_b(CONV_W - 1, GROUP_W)] + state_specs + [
            full(CONV_W, GROUP_W), full(1, GROUP_W),
            full(N_HEADS, hd, hd), full(N_HEADS, hd, hd), full(N_HEADS, hd, hd),
            full(3 * GROUP_W, 2 * N_HEADS), full(2 * N_HEADS, 3 * GROUP_W),
            full(1, 2 * N_HEADS), full(2 * N_HEADS, 1),
            full(1, GROUP_W), full(1, GROUP_W)],
        out_specs=[tok(0)] + state_specs,
        out_shape=[jax.ShapeDtypeStruct((bn, s, GROUP_W), BF16)]
                  + [jax.ShapeDtypeStruct((bn,) + shape, F32) for shape in state_shapes],
        scratch_shapes=[pltpu.VMEM((group, t + 8, GROUP_W), F32)]
                       + [pltpu.VMEM((group, t, GROUP_W), F32)] * 5
                       + [pltpu.VMEM((group, t, 2 * N_HEADS), F32),
                          pltpu.VMEM((group, t // chunk, 2 * N_HEADS, chunk), F32)] * 2
                       + [pltpu.VMEM((group,) + shape, F32) for shape in state_shapes],
        compiler_params=_cparams(("parallel", "arbitrary")),
        name="mlstm",
    )(rest, rest, rest, conv0, c0, n0, m0,
      w["conv_w"], w["conv_b"], w["wq"], w["wk"], w["wv"], w["wif"], w["wif_t"], w["bif_row"], w["bif_col"],
      w["mlstm_g"], w["mlstm_skip"])


def _outproj_kernel(a_ref, bo_ref, x_ref, p_ref, woa_ref, wob_ref, pg_ref, wgate_ref, wple_ref, plg_ref, y_ref):
    tm = x_ref.shape[0]
    n_slabs = OUTPROJ_SLABS if tm % (8 * OUTPROJ_SLABS) == 0 else 1
    slabs = [slice(c * tm // n_slabs, (c + 1) * tm // n_slabs) for c in range(n_slabs)]
    y = [jnp.dot(a_ref[r, :], woa_ref[...], preferred_element_type=F32)
         + jnp.dot(bo_ref[r, :], wob_ref[...], preferred_element_type=F32) for r in slabs]
    pe = [jnp.dot(p_ref[r, :].astype(BF16), wple_ref[...], preferred_element_type=F32) for r in slabs]
    x1 = [x_ref[r, :] + _rms(y_r) * pg_ref[...] for r, y_r in zip(slabs, y)]
    gate = [_sigmoid(jnp.dot(x.astype(BF16), wgate_ref[...], preferred_element_type=F32)) for x in x1]
    for r, x, g, e in zip(slabs, x1, gate, pe):
        y_ref[r, :] = x + _rms(g * e) * plg_ref[...]


def _outproj(a, bo, x, p, w, tm):
    m, d = x.shape
    row = lambda width: pl.BlockSpec((tm, width), lambda i: (i, 0))
    full = lambda arr: pl.BlockSpec(arr.shape, lambda i: (0,) * arr.ndim)
    consts = [w["wo_a"], w["wo_b"], w["post_g"], w["ple_gate"], w["ple_w"], w["ple_g"]]
    return pl.pallas_call(
        _outproj_kernel,
        grid=(m // tm,),
        in_specs=[row(GROUP_W), row(GROUP_W), row(d), row(p.shape[1])] + [full(c) for c in consts],
        out_specs=row(d),
        out_shape=jax.ShapeDtypeStruct((m, d), F32),
        compiler_params=_cparams(("parallel",)),
        name="outproj",
    )(a, bo, x, p, *consts)


def _layer_weights(l, norm_pre_g, norm_post_g, w_in, lam_q1, lam_k1, lam_q2, lam_k2, attn_norm_g, conv_w, conv_b,
                   w_q_b, w_k_b, w_v_b, w_if, b_if, mlstm_norm_g, mlstm_skip, w_out, ple_w, ple_gate_w, ple_norm_g):
    row = lambda v: v.reshape(1, -1)
    wo = w_out[l].astype(BF16)
    return {
        "pre_g": row(norm_pre_g[l]), "w_in": w_in[l].astype(BF16),
        "lam": jnp.stack([lam_q1[l], lam_k1[l], lam_q2[l], lam_k2[l]]).astype(F32),
        "attn_g": row(attn_norm_g[l]),
        "conv_w": conv_w[l], "conv_b": row(conv_b[l]),
        "wq": w_q_b[l].astype(BF16), "wk": w_k_b[l].astype(BF16), "wv": w_v_b[l].astype(BF16),
        "wif": w_if[l].astype(BF16), "wif_t": w_if[l].T.astype(BF16),
        "bif_row": row(b_if[l]), "bif_col": b_if[l].reshape(-1, 1),
        "mlstm_g": row(mlstm_norm_g[l]), "mlstm_skip": row(mlstm_skip[l]),
        "wo_a": wo[:GROUP_W], "wo_b": wo[GROUP_W:], "post_g": row(norm_post_g[l]),
        "ple_gate": ple_gate_w[l].astype(BF16), "ple_w": ple_w[l].astype(BF16), "ple_g": row(ple_norm_g[l]),
    }


def _prompt_layer(x, p, w, lam_init):
    b, s, d = x.shape
    x2 = x.reshape(b * s, d)
    qt, k, v, rest, kb, vt = _inproj(x2, w["pre_g"], w["w_in"], tm=PROMPT_ROWS, kv_block=FLASH_BLOCK)
    rest3 = rest.reshape(b, s, 4 * GROUP_W)
    a = _flash(qt, kb.reshape(b, s, GROUP_W), vt, rest3, w["lam"], w["attn_g"], lam_init, tq=FLASH_BLOCK)
    hd = HEAD_DIM
    bo, c1, n1, m1 = _mlstm(rest3, jnp.zeros((b, CONV_W - 1, GROUP_W), F32),
                            jnp.zeros((b, N_HEADS, hd, hd), F32), jnp.zeros((b, N_HEADS, 1, hd), F32),
                            jnp.zeros((b, N_HEADS, 1, 1), F32), w, t=PROMPT_ROWS, chunk=PROMPT_CHUNK,
                            n_valid=PROMPT_CHUNK, group=PROMPT_GROUP)
    y = _outproj(a.reshape(b * s, GROUP_W), bo.reshape(b * s, GROUP_W), x2, p.reshape(b * s, -1), w,
                 tm=PROMPT_ROWS)
    state = (k.reshape(b, s, N_HEADS, hd), v.reshape(b, s, N_HEADS, hd),
             rest3[:, s - (CONV_W - 1):, GROUP_W:2 * GROUP_W],
             c1, n1.reshape(b, N_HEADS, hd), m1.reshape(b, N_HEADS))
    return y.reshape(b, s, d), state


def _sample_layer(x, p, n_valid, cache_k, cache_v, page_table, conv0, c0, n0, m0, w, lam_init):
    db, r, d = x.shape
    hd = HEAD_DIM
    assert CONV_W - 1 <= n_valid <= r
    x2 = x.reshape(db * r, d)
    q, k, v, rest = _inproj(x2, w["pre_g"], w["w_in"], tm=db * r)
    rest3 = rest.reshape(db, r, 4 * GROUP_W)
    k3 = k.reshape(db, r, GROUP_W)
    v3 = v.reshape(db, r, GROUP_W)
    a = _paged(q.reshape(db, r, GROUP_W), k3, v3, rest3, cache_k, cache_v, page_table, w["lam"], w["attn_g"],
               lam_init, n_valid)
    bo, c1, n1, m1 = _mlstm(rest3, conv0, c0, n0.reshape(db, N_HEADS, 1, hd), m0.reshape(db, N_HEADS, 1, 1),
                            w, t=r, chunk=r, n_valid=n_valid, group=SAMPLE_GROUP)
    y = _outproj(a.reshape(db * r, GROUP_W), bo.reshape(db * r, GROUP_W), x2, p.reshape(db * r, -1), w, tm=db * r)
    state = (k3[:, :n_valid].reshape(db, n_valid, N_HEADS, hd), v3[:, :n_valid].reshape(db, n_valid, N_HEADS, hd),
             rest3[:, n_valid - (CONV_W - 1):n_valid, GROUP_W:2 * GROUP_W],
             c1, n1.reshape(db, N_HEADS, hd), m1.reshape(db, N_HEADS))
    return y.reshape(db, r, d), state


def kernel(x_prompt, x_sample, cache_k, cache_v, state_conv, state_C, state_n, state_m, page_table,
           p_prompt, p_sample, norm_pre_g, norm_post_g, w_in, lam_q1, lam_k1, lam_q2, lam_k2,
           attn_norm_g, conv_w, conv_b, w_q_b, w_k_b, w_v_b, w_if, b_if, mlstm_norm_g, mlstm_skip,
           w_out, ple_w, ple_gate_w, ple_norm_g):
    depth = w_in.shape[0]
    t_dec = x_sample.shape[1]
    pad_rows = lambda a: jnp.pad(a, ((0, 0), (0, SAMPLE_ROWS - t_dec), (0, 0)))
    n_pool, page = cache_k.shape[1], cache_k.shape[2]
    cache_k2 = cache_k.reshape(depth * n_pool, page * N_HEADS, HEAD_DIM)
    cache_v2 = cache_v.reshape(depth * n_pool, page * N_HEADS, HEAD_DIM)
    xp = x_prompt
    xs = pad_rows(x_sample)
    outs_p, outs_s = [], []
    for l in range(depth):
        w = _layer_weights(l, norm_pre_g, norm_post_g, w_in, lam_q1, lam_k1, lam_q2, lam_k2, attn_norm_g,
                           conv_w, conv_b, w_q_b, w_k_b, w_v_b, w_if, b_if, mlstm_norm_g, mlstm_skip,
                           w_out, ple_w, ple_gate_w, ple_norm_g)
        lam_init = 0.8 - 0.6 * math.exp(-0.3 * l)
        xp, st_p = _prompt_layer(xp, p_prompt[l], w, lam_init)
        outs_p.append(st_p)
        xs, st_s = _sample_layer(xs, pad_rows(p_sample[l]), t_dec, cache_k2, cache_v2, page_table + l * n_pool,
                                 state_conv[l], state_C[l], state_n[l], state_m[l], w, lam_init)
        outs_s.append(st_s)
    stack = lambda outs, i: jnp.stack([o[i] for o in outs])
    return ((xp, xs[:, :t_dec]) + tuple(stack(outs_p, i) for i in range(6))
            + tuple(stack(outs_s, i) for i in range(6)))
```

```python
import functools
import math

import jax
import jax.numpy as jnp
from jax import lax
from jax.experimental import pallas as pl
from jax.experimental.pallas import tpu as pltpu

F32 = jnp.float32
BF16 = jnp.bfloat16
EPS = 1e-6
NEG_INF = float("-inf")

N_HEADS = 4
HEAD_DIM = 128
QK_DIM = 64
GROUP_W = N_HEADS * HEAD_DIM
CONV_W = 4
SAMPLE_ROWS = 8
PROMPT_CHUNK = 128
PROMPT_ROWS = 512
FLASH_BLOCK = 512
FLASH_UNROLL = 4
OUTPROJ_SLABS = 2
PROMPT_GROUP = 2
SAMPLE_GROUP = 4
VT_ROWS = HEAD_DIM + 16
Q_SCALE = (QK_DIM ** -0.5) * math.log2(math.e)
PAGES_PER_STEP = 32
V7X_VMEM_LIMIT = 48 * 1024 * 1024


def _cparams(sem):
    return pltpu.CompilerParams(dimension_semantics=sem, vmem_limit_bytes=V7X_VMEM_LIMIT)


def _sigmoid(x):
    return 0.5 * jnp.tanh(0.5 * x) + 0.5


def _silu(x):
    return x * _sigmoid(x)


def _rms(x):
    return x * lax.rsqrt(jnp.mean(x * x, axis=-1, keepdims=True) + EPS)


def _inproj_kernel(x_ref, g_ref, w_ref, q_ref, k_ref, v_ref, rest_ref, *maybe_flash_refs, kv_block):
    hb = (_rms(x_ref[...]) * g_ref[...]).astype(BF16)

    def proj(c):
        return jnp.dot(hb, w_ref[:, c * GROUP_W:(c + 1) * GROUP_W], preferred_element_type=F32)

    q = proj(0) * Q_SCALE
    k = proj(1)
    v = proj(2)
    if not kv_block:
        q_ref[...] = q.astype(BF16)
        k_ref[...] = k
        v_ref[...] = v
    else:
        for h in range(N_HEADS):
            for c in range(q.shape[0] // kv_block):
                blk = q[c * kv_block:(c + 1) * kv_block, h * HEAD_DIM:(h + 1) * HEAD_DIM]
                q_ref[h, c] = _split_maps(blk).T.astype(BF16)
        for h in range(N_HEADS):
            cols = slice(h * HEAD_DIM, (h + 1) * HEAD_DIM)
            k_ref[pl.ds(h, k.shape[0], stride=N_HEADS), :] = k[:, cols]
            v_ref[pl.ds(h, v.shape[0], stride=N_HEADS), :] = v[:, cols]
        kb_ref, vt_ref = maybe_flash_refs
        kb_ref[...] = k.astype(BF16)
        for h in range(N_HEADS):
            for c in range(v.shape[0] // kv_block):
                blk = v[c * kv_block:(c + 1) * kv_block, h * HEAD_DIM:(h + 1) * HEAD_DIM]
                vt_ref[h, c, 0:HEAD_DIM, :] = blk.T.astype(BF16)
                vt_ref[h, c, HEAD_DIM:VT_ROWS, :] = jnp.ones((VT_ROWS - HEAD_DIM, kv_block), BF16)
    for c in range(3, 7):
        rest_ref[:, (c - 3) * GROUP_W:(c - 2) * GROUP_W] = proj(c)


def _inproj(x, g, w_bf, tm, kv_block=0):
    m, d = x.shape
    n_in = w_bf.shape[1]
    row = lambda w: pl.BlockSpec((tm, w), lambda i: (i, 0))
    out_specs = [row(GROUP_W)] * 3 + [row(4 * GROUP_W)]
    out_shape = [jax.ShapeDtypeStruct((m, GROUP_W), BF16),
                 jax.ShapeDtypeStruct((m, GROUP_W), F32),
                 jax.ShapeDtypeStruct((m, GROUP_W), F32),
                 jax.ShapeDtypeStruct((m, 4 * GROUP_W), F32)]
    if kv_block:
        out_specs[0] = pl.BlockSpec((N_HEADS, tm // kv_block, HEAD_DIM, 2 * kv_block), lambda i: (0, i, 0, 0))
        out_shape[0] = jax.ShapeDtypeStruct((N_HEADS, m // kv_block, HEAD_DIM, 2 * kv_block), BF16)
        for i_out in (1, 2):
            out_specs[i_out] = pl.BlockSpec((tm * N_HEADS, HEAD_DIM), lambda i: (i, 0))
            out_shape[i_out] = jax.ShapeDtypeStruct((m * N_HEADS, HEAD_DIM), F32)
        out_specs += [row(GROUP_W),
                      pl.BlockSpec((N_HEADS, tm // kv_block, VT_ROWS, kv_block), lambda i: (0, i, 0, 0))]
        out_shape += [jax.ShapeDtypeStruct((m, GROUP_W), BF16),
                      jax.ShapeDtypeStruct((N_HEADS, m // kv_block, VT_ROWS, kv_block), BF16)]
    return pl.pallas_call(
        functools.partial(_inproj_kernel, kv_block=kv_block),
        grid=(m // tm,),
        in_specs=[row(d), pl.BlockSpec((1, d), lambda i: (0, 0)),
                  pl.BlockSpec((d, n_in), lambda i: (0, 0))],
        out_specs=out_specs,
        out_shape=out_shape,
        compiler_params=_cparams(("parallel",)),
        name="inproj",
    )(x, g, w_bf)


def _split_maps(q):
    lane = lax.broadcasted_iota(jnp.int32, q.shape, 1)
    zero = jnp.zeros_like(q)
    return jnp.concatenate([jnp.where(lane < QK_DIM, q, zero), jnp.where(lane >= QK_DIM, q, zero)], axis=0)


def _lambda(lam_ref, lam_init):
    l1 = jnp.sum(lam_ref[0:1, :] * lam_ref[1:2, :], axis=-1, keepdims=True)
    l2 = jnp.sum(lam_ref[2:3, :] * lam_ref[3:4, :], axis=-1, keepdims=True)
    return jnp.exp(l1) - jnp.exp(l2) + lam_init


def _flash_kernel(qt_ref, k_ref, vt_ref, za_ref, lam_ref, g_ref, o_ref, m_scr, acc_scr,
                  s0_scr, s1_scr, *, tq, lam_init):
    i = pl.program_id(2)
    m_scr[...] = jnp.full(m_scr.shape, NEG_INF, F32)
    acc_scr[...] = jnp.zeros(acc_scr.shape, F32)

    def scores(j, dst):
        k0 = pl.multiple_of(j * tq, tq)
        dst[...] = jnp.dot(k_ref[pl.ds(k0, tq), :], qt_ref[...], preferred_element_type=F32)

    def consume(src, j, masked):
        s = src[...]
        if masked:
            key = lax.broadcasted_iota(jnp.int32, s.shape, 0)
            qry = lax.broadcasted_iota(jnp.int32, s.shape, 1)
            s = jnp.where(key <= jnp.where(qry >= tq, qry - tq, qry), s, NEG_INF)
        m_prev = m_scr[...]
        m_new = jnp.maximum(m_prev, jnp.max(s, axis=0, keepdims=True))
        p = jnp.exp2(s - m_new)
        alpha = jnp.exp2(m_prev - m_new)
        acc_scr[...] = alpha * acc_scr[...] + jnp.dot(vt_ref[j], p.astype(BF16), preferred_element_type=F32)
        m_scr[...] = m_new

    bufs = (s0_scr, s1_scr)
    scores(0, s0_scr)

    def run(first, count, masked_last):
        for u in range(count):
            last = u == count - 1
            if not (last and masked_last):
                scores(first + u + 1, bufs[(u + 1) % 2])
            consume(bufs[u % 2], first + u, last and masked_last)

    def full_group(jj, carry):
        run(FLASH_UNROLL * jj, FLASH_UNROLL, False)
        return carry

    lax.fori_loop(0, i // FLASH_UNROLL, full_group, 0)

    for rem in range(FLASH_UNROLL):
        @pl.when(i % FLASH_UNROLL == rem)
        def _tail(rem=rem):
            run(i - rem, rem + 1, True)

    lam = _lambda(lam_ref, lam_init)
    o_t = acc_scr[0:HEAD_DIM, :] * (1.0 / acc_scr[HEAD_DIM:HEAD_DIM + 1, :])
    o_t = o_t[:, 0:tq] - lam * o_t[:, tq:2 * tq]
    o_t = o_t * lax.rsqrt(jnp.mean(o_t * o_t, axis=0, keepdims=True) + EPS)
    o = o_t.T * g_ref[...] * (1.0 - lam_init)
    o_ref[...] = (o * _silu(za_ref[...])).astype(BF16)


def _flash(qt, kb, vt, rest, lamp, attn_g, lam_init, tq):
    b, s, _ = kb.shape
    nblk = s // tq
    blk = lambda: pl.BlockSpec((None, tq, HEAD_DIM), lambda bi, h, i: (bi, i, h))
    return pl.pallas_call(
        functools.partial(_flash_kernel, tq=tq, lam_init=lam_init),
        grid=(b, N_HEADS, nblk),
        in_specs=[pl.BlockSpec((None, None, HEAD_DIM, 2 * tq), lambda bi, h, i: (h, bi * nblk + i, 0, 0)),
                  pl.BlockSpec((None, s, HEAD_DIM), lambda bi, h, i: (bi, 0, h)),
                  pl.BlockSpec((None, nblk, VT_ROWS, tq), lambda bi, h, i: (h, bi, 0, 0)),
                  blk(),
                  pl.BlockSpec((4, QK_DIM), lambda bi, h, i: (0, 0)),
                  pl.BlockSpec((1, HEAD_DIM), lambda bi, h, i: (0, h))],
        out_specs=blk(),
        out_shape=jax.ShapeDtypeStruct((b, s, GROUP_W), BF16),
        scratch_shapes=[pltpu.VMEM((1, 2 * tq), F32),
                        pltpu.VMEM((VT_ROWS, 2 * tq), F32),
                        pltpu.VMEM((tq, 2 * tq), F32),
                        pltpu.VMEM((tq, 2 * tq), F32)],
        compiler_params=_cparams(("parallel", "parallel", "arbitrary")),
        name="flash_prompt",
    )(qt, kb, vt, rest, lamp, attn_g)


def _paged_kernel(pt_ref, q_ref, *refs, n_pages, n_valid, lam_init):
    del pt_ref
    kp = refs[:n_pages]
    vp = refs[n_pages:2 * n_pages]
    kn_ref, vn_ref, za_ref, lam_ref, g_ref, o_ref, qq_scr, bias_scr, m_scr, l_scr, acc_scr = refs[2 * n_pages:]
    r = SAMPLE_ROWS
    half = r // 2
    pw = bias_scr.shape[1]
    step = pl.program_id(1)

    @pl.when(step == 0)
    def _init():
        row = lax.broadcasted_iota(jnp.int32, (r, HEAD_DIM), 0)
        lane = lax.broadcasted_iota(jnp.int32, (r, HEAD_DIM), 1)
        for h in range(N_HEADS):
            q = q_ref[:, h * HEAD_DIM:(h + 1) * HEAD_DIM].astype(F32)
            q_map1 = jnp.where(lane < QK_DIM, q, 0.0)
            q_map2 = pltpu.roll(jnp.where(lane >= QK_DIM, q, 0.0), half, axis=0)
            qq_scr[h * r:(h + 1) * r, :] = jnp.where(row < half, q_map1, q_map2)
        row_head = lax.broadcasted_iota(jnp.int32, bias_scr.shape, 0) // r
        col_head = lax.broadcasted_iota(jnp.int32, bias_scr.shape, 1) % N_HEADS
        bias_scr[...] = jnp.where(row_head == col_head, 0.0, NEG_INF).astype(F32)
        m_scr[...] = jnp.full(m_scr.shape, NEG_INF, F32)
        l_scr[...] = jnp.zeros(l_scr.shape, F32)
        acc_scr[...] = jnp.zeros(acc_scr.shape, F32)

    qq = qq_scr[...].astype(BF16)
    bias = bias_scr[...]
    s = jnp.concatenate(
        [lax.dot_general(qq, kp[j][...].astype(BF16), (((1,), (1,)), ((), ())),
                         preferred_element_type=F32) + bias for j in range(n_pages)], axis=1)
    m_prev = m_scr[...]
    m_new = jnp.maximum(m_prev, jnp.max(s, axis=-1, keepdims=True))
    p = jnp.exp2(s - m_new)
    alpha = jnp.exp2(m_prev - m_new)
    l_scr[...] = alpha * l_scr[...] + jnp.sum(p, axis=-1, keepdims=True)
    p = p.astype(BF16)
    pv = jnp.dot(p[:, 0:pw], vp[0][...].astype(BF16), preferred_element_type=F32)
    for j in range(1, n_pages):
        pv = pv + jnp.dot(p[:, j * pw:(j + 1) * pw], vp[j][...].astype(BF16), preferred_element_type=F32)
    acc_scr[...] = alpha * acc_scr[...] + pv
    m_scr[...] = m_new

    @pl.when(step == pl.num_programs(1) - 1)
    def _finish():
        lam = _lambda(lam_ref, lam_init)
        row = lax.broadcasted_iota(jnp.int32, (r, 1), 0)
        tok = jnp.where(row >= half, row - half, row)
        for h in range(N_HEADS):
            cols = slice(h * HEAD_DIM, (h + 1) * HEAD_DIM)
            rows = slice(h * r, (h + 1) * r)
            qf = qq_scr[rows, :]
            kn = kn_ref[:, cols]
            vn = vn_ref[:, cols]
            s_new = [jnp.where(tok >= j, jnp.sum(qf * kn[j:j + 1, :], axis=-1, keepdims=True), NEG_INF)
                     for j in range(n_valid)]
            m_prev = m_scr[rows, :]
            m_new = m_prev
            for sj in s_new:
                m_new = jnp.maximum(m_new, sj)
            alpha = jnp.exp2(m_prev - m_new)
            l = alpha * l_scr[rows, :]
            acc = alpha * acc_scr[rows, :]
            for j, sj in enumerate(s_new):
                pj = jnp.exp2(sj - m_new)
                l = l + pj
                acc = acc + pj * vn[j:j + 1, :]
            o_maps = acc / l
            o = o_maps - lam * pltpu.roll(o_maps, half, axis=0)
            o = _rms(o) * g_ref[:, cols] * (1.0 - lam_init)
            o_ref[:, cols] = (o * _silu(za_ref[:, cols])).astype(BF16)


def _paged(q, k_new, v_new, rest, cache_k, cache_v, page_table, lamp, attn_g, lam_init, n_valid):
    db = q.shape[0]
    r = SAMPLE_ROWS
    assert 2 * n_valid <= r, "each head packs the tokens of both maps into SAMPLE_ROWS query rows"
    n_pages_total = page_table.shape[1]
    page_rows = cache_k.shape[1]
    p = PAGES_PER_STEP
    seq = lambda w=GROUP_W: pl.BlockSpec((None, r, w), lambda b, s, pt: (b, 0, 0))

    def page_spec(j):
        return pl.BlockSpec((None, page_rows, HEAD_DIM), lambda b, s, pt, j=j: (pt[b, s * p + j], 0, 0))

    grid_spec = pltpu.PrefetchScalarGridSpec(
        num_scalar_prefetch=1,
        grid=(db, n_pages_total // p),
        in_specs=[seq()] + [page_spec(j) for j in range(p)] * 2 + [seq(), seq(), seq(),
                  pl.BlockSpec((4, QK_DIM), lambda b, s, pt: (0, 0)),
                  pl.BlockSpec((1, GROUP_W), lambda b, s, pt: (0, 0))],
        out_specs=seq(),
        scratch_shapes=[pltpu.VMEM((N_HEADS * r, HEAD_DIM), F32),
                        pltpu.VMEM((N_HEADS * r, page_rows), F32),
                        pltpu.VMEM((N_HEADS * r, 1), F32),
                        pltpu.VMEM((N_HEADS * r, 1), F32),
                        pltpu.VMEM((N_HEADS * r, HEAD_DIM), F32)])
    return pl.pallas_call(
        functools.partial(_paged_kernel, n_pages=p, n_valid=n_valid, lam_init=lam_init),
        grid_spec=grid_spec,
        out_shape=jax.ShapeDtypeStruct((db, r, GROUP_W), BF16),
        compiler_params=_cparams(("parallel", "arbitrary")),
        name="paged_attn",
    )(page_table, q, *([cache_k] * p), *([cache_v] * p), k_new, v_new, rest, lamp, attn_g)


def _log_sigmoid(x):
    return jnp.minimum(x, 0.0) - jnp.log1p(jnp.exp(-jnp.abs(x)))


def _bf16_terms(x):
    hi = x.astype(BF16)
    rest = x - hi.astype(F32)
    mid = rest.astype(BF16)
    lo = (rest - mid.astype(F32)).astype(BF16)
    return hi, mid, lo


def _mlstm_kernel(xb_ref, ob_ref, zb_ref, conv0_ref, c0_ref, n0_ref, m0_ref,
                  cw_ref, cb_ref, wq_ref, wk_ref, wv_ref, wif_ref, wift_ref, bifr_ref, bifc_ref,
                  mg_ref, skip_ref,
                  bo_ref, c1_ref, n1_ref, m1_ref,
                  xpad, xch_scr, q_scr, k_scr, v_scr, h_scr, gc_scr, gr_scr, bc_scr, br_scr, c_scr, n_scr, m_scr,
                  *, t, chunk, n_valid, group):
    step = pl.program_id(1)
    n_chunks = t // chunk
    hd = HEAD_DIM
    ri = lax.broadcasted_iota(jnp.int32, (chunk, chunk), 0)
    ci = lax.broadcasted_iota(jnp.int32, (chunk, chunk), 1)
    causal = ci <= ri
    tril = jnp.where(causal, 1.0, 0.0).astype(BF16)
    triu = jnp.where(ri <= ci, 1.0, 0.0).astype(BF16)

    @pl.when(step == 0)
    def _init():
        for g in range(group):
            xpad[g, 0:8, :] = jnp.zeros((8, GROUP_W), F32)
            xpad[g, 8 - (CONV_W - 1):8, :] = conv0_ref[g]
        c_scr[...] = c0_ref[...]
        n_scr[...] = n0_ref[...]
        m_scr[...] = m0_ref[...]

    for g in range(group):
        x = xb_ref[g]
        xpad[g, 8:8 + t, :] = x
        xc = cb_ref[...] + cw_ref[CONV_W - 1:CONV_W, :] * x
        for j in range(CONV_W - 1):
            xc = xc + cw_ref[j:j + 1, :] * xpad[g, 8 - (CONV_W - 1) + j:8 - (CONV_W - 1) + j + t, :]
        tail = xpad[g, t:t + 8, :]
        xpad[g, 0:8, :] = tail
        xch = _silu(xc)
        xch_scr[g] = xch

        gcol = jnp.zeros((t, 2 * N_HEADS), F32) + bifr_ref[...]
        grow = jnp.zeros((2 * N_HEADS, t), F32) + bifc_ref[...]
        for h in range(N_HEADS):
            cols = slice(h * hd, (h + 1) * hd)
            xh = xch[:, cols].astype(BF16)
            qh = jnp.dot(xh, wq_ref[h], preferred_element_type=F32)
            kh = jnp.dot(xh, wk_ref[h], preferred_element_type=F32) * (hd ** -0.5)
            vh = jnp.dot(x[:, cols].astype(BF16), wv_ref[h], preferred_element_type=F32)
            q_scr[g, :, cols] = qh
            k_scr[g, :, cols] = kh
            v_scr[g, :, cols] = vh
            for part, val in enumerate((qh, kh, vh)):
                off = (3 * h + part) * hd
                vb = val.astype(BF16)
                gcol = gcol + jnp.dot(vb, wif_ref[off:off + hd, :], preferred_element_type=F32)
                grow = grow + lax.dot_general(wift_ref[:, off:off + hd], vb, (((1,), (1,)), ((), ())),
                                              preferred_element_type=F32)

        lane = lax.broadcasted_iota(jnp.int32, gcol.shape, 1)
        gcol = jnp.where(lane < N_HEADS, gcol, _log_sigmoid(gcol))
        sub = lax.broadcasted_iota(jnp.int32, grow.shape, 0)
        grow = jnp.where(sub < N_HEADS, grow, _log_sigmoid(grow))
        if n_valid < chunk:
            pos_c = lax.broadcasted_iota(jnp.int32, gcol.shape, 0) % chunk
            gcol = jnp.where(pos_c >= n_valid, jnp.where(lane < N_HEADS, NEG_INF, 0.0), gcol)
            pos_r = lax.broadcasted_iota(jnp.int32, grow.shape, 1) % chunk
            grow = jnp.where(pos_r >= n_valid, jnp.where(sub < N_HEADS, NEG_INF, 0.0), grow)
        gc_scr[g] = gcol
        lf_col = jnp.where(lane >= N_HEADS, gcol, 0.0)
        lf_row = jnp.where(sub >= N_HEADS, grow, 0.0)
        for c in range(n_chunks):
            span = slice(c * chunk, (c + 1) * chunk)
            gr_scr[g, c] = grow[:, span]
            bc_scr[g, span, :] = sum(jnp.dot(tril, part, preferred_element_type=F32)
                                     for part in _bf16_terms(lf_col[span, :]))
            br_scr[g, c] = sum(jnp.dot(part, triu, preferred_element_type=F32)
                               for part in _bf16_terms(lf_row[:, span]))

    def scan_chunk(c, carry):
        r0 = pl.multiple_of(c * chunk, chunk)
        rows = pl.ds(r0, chunk)
        chains = [(g, h) for g in range(group) for h in range(N_HEADS)]
        gates = {g: (gc_scr[g, rows, :], gr_scr[g, c], bc_scr[g, rows, :], br_scr[g, c]) for g in range(group)}
        st = []
        for g, h in chains:
            gc, gr, b_c, b_r = gates[g]
            bc = b_c[:, N_HEADS + h:N_HEADS + h + 1]
            br = b_r[N_HEADS + h:N_HEADS + h + 1, :]
            ic = gc[:, h:h + 1]
            ir = gr[h:h + 1, :]
            m = m_scr[g, h]
            dm = jnp.where(causal, bc - br + ir, NEG_INF)
            st.append(dict(bc=bc, br=br, ic=ic, ir=ir, m=m, dm=dm, inter=bc + m,
                           cols=slice(h * hd, (h + 1) * hd)))
        for s in st:
            s["m_t"] = jnp.maximum(s["inter"], jnp.max(s["dm"], axis=-1, keepdims=True))
        for s in st:
            s["w_d"] = jnp.exp(s["dm"] - s["m_t"])
            s["w_in"] = jnp.exp(s["inter"] - s["m_t"])
        for (g, h), s in zip(chains, st):
            s["q"] = q_scr[g, rows, s["cols"]]
            s["k"] = k_scr[g, rows, s["cols"]]
            s["v"] = v_scr[g, rows, s["cols"]]
            s["qb"] = s["q"].astype(BF16)
            s["kb"] = s["k"].astype(BF16)
            s["sqk"] = lax.dot_general(s["qb"], s["kb"], (((1,), (1,)), ((), ())),
                                       preferred_element_type=F32) * s["w_d"]
        for (g, h), s in zip(chains, st):
            s["cmat"] = c_scr[g, h]
            s["nvec"] = n_scr[g, h]
            s["num"] = (s["w_in"] * jnp.dot(s["qb"], s["cmat"].astype(BF16), preferred_element_type=F32)
                        + jnp.dot(s["sqk"].astype(BF16), s["v"].astype(BF16), preferred_element_type=F32))
        for s in st:
            s["den"] = (s["w_in"] * jnp.sum(s["q"] * s["nvec"], axis=-1, keepdims=True)
                        + jnp.sum(s["sqk"], axis=-1, keepdims=True))
        for (g, h), s in zip(chains, st):
            h_scr[g, rows, s["cols"]] = s["num"] / jnp.maximum(jnp.abs(s["den"]), jnp.exp(-s["m_t"]))
        for (g, h), s in zip(chains, st):
            b_l = s["bc"][chunk - 1:chunk, :]
            g_r = b_l - s["br"] + s["ir"]
            m_new = jnp.maximum(b_l + s["m"], jnp.max(g_r, axis=-1, keepdims=True))
            decay = jnp.exp(b_l + s["m"] - m_new)
            w_g = jnp.exp(b_l - s["bc"] + s["ic"] - m_new)
            c_scr[g, h] = decay * s["cmat"] + lax.dot_general(
                s["kb"], (w_g * s["v"]).astype(BF16), (((0,), (0,)), ((), ())), preferred_element_type=F32)
            n_scr[g, h] = decay * s["nvec"] + jnp.sum(w_g * s["k"], axis=0, keepdims=True)
            m_scr[g, h] = m_new
        return carry

    if n_chunks == 1:
        scan_chunk(0, 0)
    else:
        lax.fori_loop(0, n_chunks, scan_chunk, 0)

    for g in range(group):
        for h in range(N_HEADS):
            cols = slice(h * hd, (h + 1) * hd)
            hb = h_scr[g, :, cols] * _sigmoid(ob_ref[g, :, cols])
            hb = _rms(hb) * mg_ref[:, cols] + skip_ref[:, cols] * xch_scr[g, :, cols]
            bo_ref[g, :, cols] = (hb * _silu(zb_ref[g, :, cols])).astype(BF16)

    @pl.when(step == pl.num_programs(1) - 1)
    def _state_out():
        c1_ref[...] = c_scr[...]
        n1_ref[...] = n_scr[...]
        m1_ref[...] = m_scr[...]


def _mlstm(rest, conv0, c0, n0, m0, w, t, chunk, n_valid, group):
    bn, s, _ = rest.shape
    hd = HEAD_DIM
    tok = lambda col: pl.BlockSpec((group, t, GROUP_W), lambda b, j, col=col: (b, j, col))
    per_b = lambda *shape: pl.BlockSpec((group,) + shape, lambda b, j: (b,) + (0,) * len(shape))
    full = lambda *shape: pl.BlockSpec(shape, lambda b, j: (0,) * len(shape))
    state_shapes = [(N_HEADS, hd, hd), (N_HEADS, 1, hd), (N_HEADS, 1, 1)]
    state_specs = [per_b(*shape) for shape in state_shapes]
    return pl.pallas_call(
        functools.partial(_mlstm_kernel, t=t, chunk=chunk, n_valid=n_valid, group=group),
        grid=(bn // group, s // t),
        in_specs=[tok(1), tok(2), tok(3), per_b(CONV_W - 1, GROUP_W)] + state_specs + [
            full(CONV_W, GROUP_W), full(1, GROUP_W),
            full(N_HEADS, hd, hd), full(N_HEADS, hd, hd), full(N_HEADS, hd, hd),
            full(3 * GROUP_W, 2 * N_HEADS), full(2 * N_HEADS, 3 * GROUP_W),
            full(1, 2 * N_HEADS), full(2 * N_HEADS, 1),
            full(1, GROUP_W), full(1, GROUP_W)],
        out_specs=[tok(0)] + state_specs,
        out_shape=[jax.ShapeDtypeStruct((bn, s, GROUP_W), BF16)]
                  + [jax.ShapeDtypeStruct((bn,) + shape, F32) for shape in state_shapes],
        scratch_shapes=[pltpu.VMEM((group, t + 8, GROUP_W), F32)]
                       + [pltpu.VMEM((group, t, GROUP_W), F32)] * 5
                       + [pltpu.VMEM((group, t, 2 * N_HEADS), F32),
                          pltpu.VMEM((group, t // chunk, 2 * N_HEADS, chunk), F32)] * 2
                       + [pltpu.VMEM((group,) + shape, F32) for shape in state_shapes],
        compiler_params=_cparams(("parallel", "arbitrary")),
        name="mlstm",
    )(rest, rest, rest, conv0, c0, n0, m0,
      w["conv_w"], w["conv_b"], w["wq"], w["wk"], w["wv"], w["wif"], w["wif_t"], w["bif_row"], w["bif_col"],
      w["mlstm_g"], w["mlstm_skip"])


def _outproj_kernel(a_ref, bo_ref, x_ref, p_ref, woa_ref, wob_ref, pg_ref, wgate_ref, wple_ref, plg_ref, y_ref):
    tm = x_ref.shape[0]
    n_slabs = OUTPROJ_SLABS if tm % (8 * OUTPROJ_SLABS) == 0 else 1
    slabs = [slice(c * tm // n_slabs, (c + 1) * tm // n_slabs) for c in range(n_slabs)]
    y = [jnp.dot(a_ref[r, :], woa_ref[...], preferred_element_type=F32)
         + jnp.dot(bo_ref[r, :], wob_ref[...], preferred_element_type=F32) for r in slabs]
    pe = [jnp.dot(p_ref[r, :].astype(BF16), wple_ref[...], preferred_element_type=F32) for r in slabs]
    x1 = [x_ref[r, :] + _rms(y_r) * pg_ref[...] for r, y_r in zip(slabs, y)]
    gate = [_sigmoid(jnp.dot(x.astype(BF16), wgate_ref[...], preferred_element_type=F32)) for x in x1]
    for r, x, g, e in zip(slabs, x1, gate, pe):
        y_ref[r, :] = x + _rms(g * e) * plg_ref[...]


def _outproj(a, bo, x, p, w, tm):
    m, d = x.shape
    row = lambda width: pl.BlockSpec((tm, width), lambda i: (i, 0))
    full = lambda arr: pl.BlockSpec(arr.shape, lambda i: (0,) * arr.ndim)
    consts = [w["wo_a"], w["wo_b"], w["post_g"], w["ple_gate"], w["ple_w"], w["ple_g"]]
    return pl.pallas_call(
        _outproj_kernel,
        grid=(m // tm,),
        in_specs=[row(GROUP_W), row(GROUP_W), row(d), row(p.shape[1])] + [full(c) for c in consts],
        out_specs=row(d),
        out_shape=jax.ShapeDtypeStruct((m, d), F32),
        compiler_params=_cparams(("parallel",)),
        name="outproj",
    )(a, bo, x, p, *consts)


def _layer_weights(l, norm_pre_g, norm_post_g, w_in, lam_q1, lam_k1, lam_q2, lam_k2, attn_norm_g, conv_w, conv_b,
                   w_q_b, w_k_b, w_v_b, w_if, b_if, mlstm_norm_g, mlstm_skip, w_out, ple_w, ple_gate_w, ple_norm_g):
    row = lambda v: v.reshape(1, -1)
    wo = w_out[l].astype(BF16)
    return {
        "pre_g": row(norm_pre_g[l]), "w_in": w_in[l].astype(BF16),
        "lam": jnp.stack([lam_q1[l], lam_k1[l], lam_q2[l], lam_k2[l]]).astype(F32),
        "attn_g": row(attn_norm_g[l]),
        "conv_w": conv_w[l], "conv_b": row(conv_b[l]),
        "wq": w_q_b[l].astype(BF16), "wk": w_k_b[l].astype(BF16), "wv": w_v_b[l].astype(BF16),
        "wif": w_if[l].astype(BF16), "wif_t": w_if[l].T.astype(BF16),
        "bif_row": row(b_if[l]), "bif_col": b_if[l].reshape(-1, 1),
        "mlstm_g": row(mlstm_norm_g[l]), "mlstm_skip": row(mlstm_skip[l]),
        "wo_a": wo[:GROUP_W], "wo_b": wo[GROUP_W:], "post_g": row(norm_post_g[l]),
        "ple_gate": ple_gate_w[l].astype(BF16), "ple_w": ple_w[l].astype(BF16), "ple_g": row(ple_norm_g[l]),
    }


def _prompt_layer(x, p, w, lam_init):
    b, s, d = x.shape
    x2 = x.reshape(b * s, d)
    qt, k, v, rest, kb, vt = _inproj(x2, w["pre_g"], w["w_in"], tm=PROMPT_ROWS, kv_block=FLASH_BLOCK)
    rest3 = rest.reshape(b, s, 4 * GROUP_W)
    a = _flash(qt, kb.reshape(b, s, GROUP_W), vt, rest3, w["lam"], w["attn_g"], lam_init, tq=FLASH_BLOCK)
    hd = HEAD_DIM
    bo, c1, n1, m1 = _mlstm(rest3, jnp.zeros((b, CONV_W - 1, GROUP_W), F32),
                            jnp.zeros((b, N_HEADS, hd, hd), F32), jnp.zeros((b, N_HEADS, 1, hd), F32),
                            jnp.zeros((b, N_HEADS, 1, 1), F32), w, t=PROMPT_ROWS, chunk=PROMPT_CHUNK,
                            n_valid=PROMPT_CHUNK, group=PROMPT_GROUP)
    y = _outproj(a.reshape(b * s, GROUP_W), bo.reshape(b * s, GROUP_W), x2, p.reshape(b * s, -1), w,
                 tm=PROMPT_ROWS)
    state = (k.reshape(b, s, N_HEADS, hd), v.reshape(b, s, N_HEADS, hd),
             rest3[:, s - (CONV_W - 1):, GROUP_W:2 * GROUP_W],
             c1, n1.reshape(b, N_HEADS, hd), m1.reshape(b, N_HEADS))
    return y.reshape(b, s, d), state


def _sample_layer(x, p, n_valid, cache_k, cache_v, page_table, conv0, c0, n0, m0, w, lam_init):
    db, r, d = x.shape
    hd = HEAD_DIM
    assert CONV_W - 1 <= n_valid <= r
    x2 = x.reshape(db * r, d)
    q, k, v, rest = _inproj(x2, w["pre_g"], w["w_in"], tm=db * r)
    rest3 = rest.reshape(db, r, 4 * GROUP_W)
    k3 = k.reshape(db, r, GROUP_W)
    v3 = v.reshape(db, r, GROUP_W)
    a = _paged(q.reshape(db, r, GROUP_W), k3, v3, rest3, cache_k, cache_v, page_table, w["lam"], w["attn_g"],
               lam_init, n_valid)
    bo, c1, n1, m1 = _mlstm(rest3, conv0, c0, n0.reshape(db, N_HEADS, 1, hd), m0.reshape(db, N_HEADS, 1, 1),
                            w, t=r, chunk=r, n_valid=n_valid, group=SAMPLE_GROUP)
    y = _outproj(a.reshape(db * r, GROUP_W), bo.reshape(db * r, GROUP_W), x2, p.reshape(db * r, -1), w, tm=db * r)
    state = (k3[:, :n_valid].reshape(db, n_valid, N_HEADS, hd), v3[:, :n_valid].reshape(db, n_valid, N_HEADS, hd),
             rest3[:, n_valid - (CONV_W - 1):n_valid, GROUP_W:2 * GROUP_W],
             c1, n1.reshape(db, N_HEADS, hd), m1.reshape(db, N_HEADS))
    return y.reshape(db, r, d), state


def kernel(x_prompt, x_sample, cache_k, cache_v, state_conv, state_C, state_n, state_m, page_table,
           p_prompt, p_sample, norm_pre_g, norm_post_g, w_in, lam_q1, lam_k1, lam_q2, lam_k2,
           attn_norm_g, conv_w, conv_b, w_q_b, w_k_b, w_v_b, w_if, b_if, mlstm_norm_g, mlstm_skip,
           w_out, ple_w, ple_gate_w, ple_norm_g):
    depth = w_in.shape[0]
    t_dec = x_sample.shape[1]
    pad_rows = lambda a: jnp.pad(a, ((0, 0), (0, SAMPLE_ROWS - t_dec), (0, 0)))
    n_pool, page = cache_k.shape[1], cache_k.shape[2]
    cache_k2 = cache_k.reshape(depth * n_pool, page * N_HEADS, HEAD_DIM)
    cache_v2 = cache_v.reshape(depth * n_pool, page * N_HEADS, HEAD_DIM)
    xp = x_prompt
    xs = pad_rows(x_sample)
    outs_p, outs_s = [], []
    for l in range(depth):
        w = _layer_weights(l, norm_pre_g, norm_post_g, w_in, lam_q1, lam_k1, lam_q2, lam_k2, attn_norm_g,
                           conv_w, conv_b, w_q_b, w_k_b, w_v_b, w_if, b_if, mlstm_norm_g, mlstm_skip,
                           w_out, ple_w, ple_gate_w, ple_norm_g)
        lam_init = 0.8 - 0.6 * math.exp(-0.3 * l)
        xp, st_p = _prompt_layer(xp, p_prompt[l], w, lam_init)
        outs_p.append(st_p)
        xs, st_s = _sample_layer(xs, pad_rows(p_sample[l]), t_dec, cache_k2, cache_v2, page_table + l * n_pool,
                                 state_conv[l], state_C[l], state_n[l], state_m[l], w, lam_init)
        outs_s.append(st_s)
    stack = lambda outs, i: jnp.stack([o[i] for o in outs])
    return ((xp, xs[:, :t_dec]) + tuple(stack(outs_p, i) for i in range(6))
            + tuple(stack(outs_s, i) for i in range(6)))
```

```python
import functools
import math

import jax
import jax.numpy as jnp
from jax import lax
from jax.experimental import pallas as pl
from jax.experimental.pallas import tpu as pltpu

F32 = jnp.float32
BF16 = jnp.bfloat16
EPS = 1e-6
NEG_INF = float("-inf")

N_HEADS = 4
HEAD_DIM = 128
QK_DIM = 64
GROUP_W = N_HEADS * HEAD_DIM
CONV_W = 4
F32_SUBLANES = 8
SAMPLE_ROWS = F32_SUBLANES
HALO = F32_SUBLANES
PROMPT_CHUNK = 128
PROMPT_ROWS = 512
FLASH_BLOCK = 512
FLASH_UNROLL = 4
OUTPROJ_SLABS = 2
PROMPT_GROUP = 2
SAMPLE_GROUP = 4
VT_ROWS = HEAD_DIM + 16
Q_SCALE = (QK_DIM ** -0.5) * math.log2(math.e)
PAGES_PER_STEP = 32
V7X_VMEM_LIMIT = 48 * 1024 * 1024


def _cparams(sem):
    return pltpu.CompilerParams(dimension_semantics=sem, vmem_limit_bytes=V7X_VMEM_LIMIT)


def _sigmoid(x):
    return 0.5 * jnp.tanh(0.5 * x) + 0.5


def _silu(x):
    return x * _sigmoid(x)


def _rms(x):
    return x * lax.rsqrt(jnp.mean(x * x, axis=-1, keepdims=True) + EPS)


def _inproj_kernel(x_ref, g_ref, w_ref, q_ref, k_ref, v_ref, rest_ref, *maybe_flash_refs, kv_block):
    hb = (_rms(x_ref[...]) * g_ref[...]).astype(BF16)

    def proj(c):
        return jnp.dot(hb, w_ref[:, c * GROUP_W:(c + 1) * GROUP_W], preferred_element_type=F32)

    q = proj(0) * Q_SCALE
    k = proj(1)
    v = proj(2)
    if not kv_block:
        q_ref[...] = q.astype(BF16)
        k_ref[...] = k
        v_ref[...] = v
    else:
        for h in range(N_HEADS):
            for c in range(q.shape[0] // kv_block):
                blk = q[c * kv_block:(c + 1) * kv_block, h * HEAD_DIM:(h + 1) * HEAD_DIM]
                q_ref[h, c] = _split_maps(blk).T.astype(BF16)
        for h in range(N_HEADS):
            cols = slice(h * HEAD_DIM, (h + 1) * HEAD_DIM)
            k_ref[pl.ds(h, k.shape[0], stride=N_HEADS), :] = k[:, cols]
            v_ref[pl.ds(h, v.shape[0], stride=N_HEADS), :] = v[:, cols]
        kb_ref, vt_ref = maybe_flash_refs
        kb_ref[...] = k.astype(BF16)
        for h in range(N_HEADS):
            for c in range(v.shape[0] // kv_block):
                blk = v[c * kv_block:(c + 1) * kv_block, h * HEAD_DIM:(h + 1) * HEAD_DIM]
                vt_ref[h, c, 0:HEAD_DIM, :] = blk.T.astype(BF16)
                vt_ref[h, c, HEAD_DIM:VT_ROWS, :] = jnp.ones((VT_ROWS - HEAD_DIM, kv_block), BF16)
    for c, act in zip(range(3, 7), (_silu, None, _sigmoid, _silu)):
        u = proj(c)
        rest_ref[:, (c - 3) * GROUP_W:(c - 2) * GROUP_W] = u if act is None else act(u)


def _inproj(x, g, w_bf, tm, kv_block=0):
    m, d = x.shape
    n_in = w_bf.shape[1]
    row = lambda w: pl.BlockSpec((tm, w), lambda i: (i, 0))
    out_specs = [row(GROUP_W)] * 3 + [row(4 * GROUP_W)]
    out_shape = [jax.ShapeDtypeStruct((m, GROUP_W), BF16),
                 jax.ShapeDtypeStruct((m, GROUP_W), F32),
                 jax.ShapeDtypeStruct((m, GROUP_W), F32),
                 jax.ShapeDtypeStruct((m, 4 * GROUP_W), F32)]
    if kv_block:
        out_specs[0] = pl.BlockSpec((N_HEADS, tm // kv_block, HEAD_DIM, 2 * kv_block), lambda i: (0, i, 0, 0))
        out_shape[0] = jax.ShapeDtypeStruct((N_HEADS, m // kv_block, HEAD_DIM, 2 * kv_block), BF16)
        for i_out in (1, 2):
            out_specs[i_out] = pl.BlockSpec((tm * N_HEADS, HEAD_DIM), lambda i: (i, 0))
            out_shape[i_out] = jax.ShapeDtypeStruct((m * N_HEADS, HEAD_DIM), F32)
        out_specs += [row(GROUP_W),
                      pl.BlockSpec((N_HEADS, tm // kv_block, VT_ROWS, kv_block), lambda i: (0, i, 0, 0))]
        out_shape += [jax.ShapeDtypeStruct((m, GROUP_W), BF16),
                      jax.ShapeDtypeStruct((N_HEADS, m // kv_block, VT_ROWS, kv_block), BF16)]
    return pl.pallas_call(
        functools.partial(_inproj_kernel, kv_block=kv_block),
        grid=(m // tm,),
        in_specs=[row(d), pl.BlockSpec((1, d), lambda i: (0, 0)),
                  pl.BlockSpec((d, n_in), lambda i: (0, 0))],
        out_specs=out_specs,
        out_shape=out_shape,
        compiler_params=_cparams(("parallel",)),
        name="inproj",
    )(x, g, w_bf)


def _split_maps(q):
    lane = lax.broadcasted_iota(jnp.int32, q.shape, 1)
    zero = jnp.zeros_like(q)
    return jnp.concatenate([jnp.where(lane < QK_DIM, q, zero), jnp.where(lane >= QK_DIM, q, zero)], axis=0)


def _lambda(lam_ref, lam_init):
    l1 = jnp.sum(lam_ref[0:1, :] * lam_ref[1:2, :], axis=-1, keepdims=True)
    l2 = jnp.sum(lam_ref[2:3, :] * lam_ref[3:4, :], axis=-1, keepdims=True)
    return jnp.exp(l1) - jnp.exp(l2) + lam_init


def _flash_kernel(qt_ref, k_ref, vt_ref, za_ref, lam_ref, g_ref, o_ref, m_scr, acc_scr,
                  s0_scr, s1_scr, *, tq, lam_init):
    i = pl.program_id(2)

    def scores(j, dst):
        k0 = pl.multiple_of(j * tq, tq)
        dst[...] = jnp.dot(k_ref[pl.ds(k0, tq), :], qt_ref[...], preferred_element_type=F32)

    def consume(src, j, masked):
        s = src[...]
        if masked:
            key = lax.broadcasted_iota(jnp.int32, s.shape, 0)
            qry = lax.broadcasted_iota(jnp.int32, s.shape, 1)
            s = jnp.where(key <= jnp.where(qry >= tq, qry - tq, qry), s, NEG_INF)
        m_prev = m_scr[...]
        m_new = jnp.maximum(m_prev, jnp.max(s, axis=0, keepdims=True))
        p = jnp.exp2(s - m_new)
        alpha = jnp.exp2(m_prev - m_new)
        acc_scr[...] = alpha * acc_scr[...] + jnp.dot(vt_ref[j], p.astype(BF16), preferred_element_type=F32)
        m_scr[...] = m_new

    bufs = (s0_scr, s1_scr)
    scores(0, s0_scr)
    m_scr[...] = jnp.full(m_scr.shape, NEG_INF, F32)
    acc_scr[...] = jnp.zeros(acc_scr.shape, F32)

    def run(first, count, masked_last):
        for u in range(count):
            last = u == count - 1
            if not (last and masked_last):
                scores(first + u + 1, bufs[(u + 1) % 2])
            consume(bufs[u % 2], first + u, last and masked_last)

    def full_group(jj, carry):
        run(FLASH_UNROLL * jj, FLASH_UNROLL, False)
        return carry

    lax.fori_loop(0, i // FLASH_UNROLL, full_group, 0)

    for rem in range(FLASH_UNROLL):
        @pl.when(i % FLASH_UNROLL == rem)
        def _tail(rem=rem):
            run(i - rem, rem + 1, True)

    lam = _lambda(lam_ref, lam_init)
    o_t = acc_scr[0:HEAD_DIM, :] * (1.0 / acc_scr[HEAD_DIM:HEAD_DIM + 1, :])
    o_t = o_t[:, 0:tq] - lam * o_t[:, tq:2 * tq]
    o_t = o_t * lax.rsqrt(jnp.mean(o_t * o_t, axis=0, keepdims=True) + EPS)
    o = o_t.T * g_ref[...] * (1.0 - lam_init)
    o_ref[...] = (o * za_ref[...]).astype(BF16)


def _flash(qt, kb, vt, rest, lamp, attn_g, lam_init, tq):
    b, s, _ = kb.shape
    nblk = s // tq
    blk = lambda: pl.BlockSpec((None, tq, HEAD_DIM), lambda bi, h, i: (bi, i, h))
    return pl.pallas_call(
        functools.partial(_flash_kernel, tq=tq, lam_init=lam_init),
        grid=(b, N_HEADS, nblk),
        in_specs=[pl.BlockSpec((None, None, HEAD_DIM, 2 * tq), lambda bi, h, i: (h, bi * nblk + i, 0, 0)),
                  pl.BlockSpec((None, s, HEAD_DIM), lambda bi, h, i: (bi, 0, h)),
                  pl.BlockSpec((None, nblk, VT_ROWS, tq), lambda bi, h, i: (h, bi, 0, 0)),
                  blk(),
                  pl.BlockSpec((4, QK_DIM), lambda bi, h, i: (0, 0)),
                  pl.BlockSpec((1, HEAD_DIM), lambda bi, h, i: (0, h))],
        out_specs=blk(),
        out_shape=jax.ShapeDtypeStruct((b, s, GROUP_W), BF16),
        scratch_shapes=[pltpu.VMEM((1, 2 * tq), F32),
                        pltpu.VMEM((VT_ROWS, 2 * tq), F32),
                        pltpu.VMEM((tq, 2 * tq), F32),
                        pltpu.VMEM((tq, 2 * tq), F32)],
        compiler_params=_cparams(("parallel", "parallel", "arbitrary")),
        name="flash_prompt",
    )(qt, kb, vt, rest, lamp, attn_g)


def _paged_kernel(pt_ref, q_ref, *refs, n_pages, n_valid, lam_init):
    del pt_ref
    kp = refs[:n_pages]
    vp = refs[n_pages:2 * n_pages]
    kn_ref, vn_ref, za_ref, lam_ref, g_ref, o_ref, qq_scr, bias_scr, m_scr, l_scr, acc_scr = refs[2 * n_pages:]
    r = SAMPLE_ROWS
    half = r // 2
    pw = bias_scr.shape[1]
    step = pl.program_id(1)

    @pl.when(step == 0)
    def _init():
        row = lax.broadcasted_iota(jnp.int32, (r, HEAD_DIM), 0)
        lane = lax.broadcasted_iota(jnp.int32, (r, HEAD_DIM), 1)
        for h in range(N_HEADS):
            q = q_ref[:, h * HEAD_DIM:(h + 1) * HEAD_DIM].astype(F32)
            q_map1 = jnp.where(lane < QK_DIM, q, 0.0)
            q_map2 = pltpu.roll(jnp.where(lane >= QK_DIM, q, 0.0), half, axis=0)
            qq_scr[h * r:(h + 1) * r, :] = jnp.where(row < half, q_map1, q_map2)
        row_head = lax.broadcasted_iota(jnp.int32, bias_scr.shape, 0) // r
        col_head = lax.broadcasted_iota(jnp.int32, bias_scr.shape, 1) % N_HEADS
        bias_scr[...] = jnp.where(row_head == col_head, 0.0, NEG_INF).astype(F32)
        m_scr[...] = jnp.full(m_scr.shape, NEG_INF, F32)
        l_scr[...] = jnp.zeros(l_scr.shape, F32)
        acc_scr[...] = jnp.zeros(acc_scr.shape, F32)

    qq = qq_scr[...].astype(BF16)
    bias = bias_scr[...]
    s = jnp.concatenate(
        [lax.dot_general(qq, kp[j][...].astype(BF16), (((1,), (1,)), ((), ())),
                         preferred_element_type=F32) + bias for j in range(n_pages)], axis=1)
    m_prev = m_scr[...]
    m_new = jnp.maximum(m_prev, jnp.max(s, axis=-1, keepdims=True))
    p = jnp.exp2(s - m_new)
    alpha = jnp.exp2(m_prev - m_new)
    l_scr[...] = alpha * l_scr[...] + jnp.sum(p, axis=-1, keepdims=True)
    p = p.astype(BF16)
    pv = jnp.dot(p[:, 0:pw], vp[0][...].astype(BF16), preferred_element_type=F32)
    for j in range(1, n_pages):
        pv = pv + jnp.dot(p[:, j * pw:(j + 1) * pw], vp[j][...].astype(BF16), preferred_element_type=F32)
    acc_scr[...] = alpha * acc_scr[...] + pv
    m_scr[...] = m_new

    @pl.when(step == pl.num_programs(1) - 1)
    def _finish():
        lam = _lambda(lam_ref, lam_init)
        row = lax.broadcasted_iota(jnp.int32, (r, 1), 0)
        tok = jnp.where(row >= half, row - half, row)
        for h in range(N_HEADS):
            cols = slice(h * HEAD_DIM, (h + 1) * HEAD_DIM)
            rows = slice(h * r, (h + 1) * r)
            qf = qq_scr[rows, :]
            kn = kn_ref[:, cols]
            vn = vn_ref[:, cols]
            s_new = [jnp.where(tok >= j, jnp.sum(qf * kn[j:j + 1, :], axis=-1, keepdims=True), NEG_INF)
                     for j in range(n_valid)]
            m_prev = m_scr[rows, :]
            m_new = m_prev
            for sj in s_new:
                m_new = jnp.maximum(m_new, sj)
            alpha = jnp.exp2(m_prev - m_new)
            l = alpha * l_scr[rows, :]
            acc = alpha * acc_scr[rows, :]
            for j, sj in enumerate(s_new):
                pj = jnp.exp2(sj - m_new)
                l = l + pj
                acc = acc + pj * vn[j:j + 1, :]
            o_maps = acc / l
            o = o_maps - lam * pltpu.roll(o_maps, half, axis=0)
            o = _rms(o) * g_ref[:, cols] * (1.0 - lam_init)
            o_ref[:, cols] = (o * za_ref[:, cols]).astype(BF16)


def _paged(q, k_new, v_new, rest, cache_k, cache_v, page_table, lamp, attn_g, lam_init, n_valid):
    db = q.shape[0]
    r = SAMPLE_ROWS
    assert 2 * n_valid <= r, "each head packs the tokens of both maps into SAMPLE_ROWS query rows"
    n_pages_total = page_table.shape[1]
    page_rows = cache_k.shape[1]
    p = PAGES_PER_STEP
    seq = lambda w=GROUP_W: pl.BlockSpec((None, r, w), lambda b, s, pt: (b, 0, 0))

    def page_spec(j):
        return pl.BlockSpec((None, page_rows, HEAD_DIM), lambda b, s, pt, j=j: (pt[b, s * p + j], 0, 0))

    grid_spec = pltpu.PrefetchScalarGridSpec(
        num_scalar_prefetch=1,
        grid=(db, n_pages_total // p),
        in_specs=[seq()] + [page_spec(j) for j in range(p)] * 2 + [seq(), seq(), seq(),
                  pl.BlockSpec((4, QK_DIM), lambda b, s, pt: (0, 0)),
                  pl.BlockSpec((1, GROUP_W), lambda b, s, pt: (0, 0))],
        out_specs=seq(),
        scratch_shapes=[pltpu.VMEM((N_HEADS * r, HEAD_DIM), F32),
                        pltpu.VMEM((N_HEADS * r, page_rows), F32),
                        pltpu.VMEM((N_HEADS * r, 1), F32),
                        pltpu.VMEM((N_HEADS * r, 1), F32),
                        pltpu.VMEM((N_HEADS * r, HEAD_DIM), F32)])
    return pl.pallas_call(
        functools.partial(_paged_kernel, n_pages=p, n_valid=n_valid, lam_init=lam_init),
        grid_spec=grid_spec,
        out_shape=jax.ShapeDtypeStruct((db, r, GROUP_W), BF16),
        compiler_params=_cparams(("parallel", "arbitrary")),
        name="paged_attn",
    )(page_table, q, *([cache_k] * p), *([cache_v] * p), k_new, v_new, rest, lamp, attn_g)


def _log_sigmoid(x):
    return jnp.minimum(x, 0.0) - jnp.log1p(jnp.exp(-jnp.abs(x)))


def _bf16_terms(x):
    hi = x.astype(BF16)
    rest = x - hi.astype(F32)
    mid = rest.astype(BF16)
    lo = (rest - mid.astype(F32)).astype(BF16)
    return hi, mid, lo


def _mlstm_kernel(xb_ref, ob_ref, zb_ref, conv0_ref, c0_ref, n0_ref, m0_ref,
                  cw_ref, cb_ref, wq_ref, wk_ref, wv_ref, wif_ref, wift_ref, bifr_ref, bifc_ref,
                  mg_ref, skip_ref,
                  bo_ref, c1_ref, n1_ref, m1_ref,
                  xpad, xch_scr, q_scr, k_scr, v_scr, h_scr, gc_scr, gr_scr, bc_scr, br_scr, c_scr, n_scr, m_scr,
                  *, t, chunk, n_valid, group):
    step = pl.program_id(1)
    n_chunks = t // chunk
    hd = HEAD_DIM
    ri = lax.broadcasted_iota(jnp.int32, (chunk, chunk), 0)
    ci = lax.broadcasted_iota(jnp.int32, (chunk, chunk), 1)
    causal = ci <= ri
    tril = jnp.where(causal, 1.0, 0.0).astype(BF16)
    triu = jnp.where(ri <= ci, 1.0, 0.0).astype(BF16)

    @pl.when(step == 0)
    def _init():
        for g in range(group):
            xpad[g, 0:HALO, :] = jnp.zeros((HALO, GROUP_W), F32)
            xpad[g, HALO - (CONV_W - 1):HALO, :] = conv0_ref[g]
        c_scr[...] = c0_ref[...]
        n_scr[...] = n0_ref[...]
        m_scr[...] = m0_ref[...]

    for g in range(group):
        x = xb_ref[g]
        xpad[g, HALO:HALO + t, :] = x
        xc = cb_ref[...] + cw_ref[CONV_W - 1:CONV_W, :] * x
        for j in range(CONV_W - 1):
            first = HALO - (CONV_W - 1) + j
            xc = xc + cw_ref[j:j + 1, :] * xpad[g, first:first + t, :]
        tail = xpad[g, t:t + HALO, :]
        xpad[g, 0:HALO, :] = tail
        xch = _silu(xc)
        xch_scr[g] = xch

        gcol = jnp.zeros((t, 2 * N_HEADS), F32) + bifr_ref[...]
        grow = jnp.zeros((2 * N_HEADS, t), F32) + bifc_ref[...]
        for h in range(N_HEADS):
            cols = slice(h * hd, (h + 1) * hd)
            xh = xch[:, cols].astype(BF16)
            qh = jnp.dot(xh, wq_ref[h], preferred_element_type=F32)
            kh = jnp.dot(xh, wk_ref[h], preferred_element_type=F32) * (hd ** -0.5)
            vh = jnp.dot(x[:, cols].astype(BF16), wv_ref[h], preferred_element_type=F32)
            q_scr[g, :, cols] = qh
            k_scr[g, :, cols] = kh
            v_scr[g, :, cols] = vh
            for part, val in enumerate((qh, kh, vh)):
                off = (3 * h + part) * hd
                vb = val.astype(BF16)
                gcol = gcol + jnp.dot(vb, wif_ref[off:off + hd, :], preferred_element_type=F32)
                grow = grow + lax.dot_general(wift_ref[:, off:off + hd], vb, (((1,), (1,)), ((), ())),
                                              preferred_element_type=F32)

        lane = lax.broadcasted_iota(jnp.int32, gcol.shape, 1)
        gcol = jnp.where(lane < N_HEADS, gcol, _log_sigmoid(gcol))
        sub = lax.broadcasted_iota(jnp.int32, grow.shape, 0)
        grow = jnp.where(sub < N_HEADS, grow, _log_sigmoid(grow))
        if n_valid < chunk:
            pos_c = lax.broadcasted_iota(jnp.int32, gcol.shape, 0) % chunk
            gcol = jnp.where(pos_c >= n_valid, jnp.where(lane < N_HEADS, NEG_INF, 0.0), gcol)
            pos_r = lax.broadcasted_iota(jnp.int32, grow.shape, 1) % chunk
            grow = jnp.where(pos_r >= n_valid, jnp.where(sub < N_HEADS, NEG_INF, 0.0), grow)
        gc_scr[g] = gcol
        lf_col = jnp.where(lane >= N_HEADS, gcol, 0.0)
        lf_row = jnp.where(sub >= N_HEADS, grow, 0.0)
        for c in range(n_chunks):
            span = slice(c * chunk, (c + 1) * chunk)
            gr_scr[g, c] = grow[:, span]
            bc_scr[g, span, :] = sum(jnp.dot(tril, part, preferred_element_type=F32)
                                     for part in _bf16_terms(lf_col[span, :]))
            br_scr[g, c] = sum(jnp.dot(part, triu, preferred_element_type=F32)
                               for part in _bf16_terms(lf_row[:, span]))

    def scan_chunk(c, carry):
        r0 = pl.multiple_of(c * chunk, chunk)
        rows = pl.ds(r0, chunk)
        chains = [(g, h) for g in range(group) for h in range(N_HEADS)]
        gates = {g: (gc_scr[g, rows, :], gr_scr[g, c], bc_scr[g, rows, :], br_scr[g, c]) for g in range(group)}
        st = []
        for g, h in chains:
            gc, gr, b_c, b_r = gates[g]
            bc = b_c[:, N_HEADS + h:N_HEADS + h + 1]
            br = b_r[N_HEADS + h:N_HEADS + h + 1, :]
            ic = gc[:, h:h + 1]
            ir = gr[h:h + 1, :]
            m = m_scr[g, h]
            dm = jnp.where(causal, bc - br + ir, NEG_INF)
            st.append(dict(bc=bc, br=br, ic=ic, ir=ir, m=m, dm=dm, inter=bc + m,
                           cols=slice(h * hd, (h + 1) * hd)))
        for s in st:
            s["m_t"] = jnp.maximum(s["inter"], jnp.max(s["dm"], axis=-1, keepdims=True))
        for s in st:
            s["w_d"] = jnp.exp(s["dm"] - s["m_t"])
            s["w_in"] = jnp.exp(s["inter"] - s["m_t"])
        for (g, h), s in zip(chains, st):
            s["q"] = q_scr[g, rows, s["cols"]]
            s["k"] = k_scr[g, rows, s["cols"]]
            s["v"] = v_scr[g, rows, s["cols"]]
            s["qb"] = s["q"].astype(BF16)
            s["kb"] = s["k"].astype(BF16)
            s["sqk"] = lax.dot_general(s["qb"], s["kb"], (((1,), (1,)), ((), ())),
                                       preferred_element_type=F32) * s["w_d"]
        for (g, h), s in zip(chains, st):
            s["cmat"] = c_scr[g, h]
            s["nvec"] = n_scr[g, h]
            s["num"] = (s["w_in"] * jnp.dot(s["qb"], s["cmat"].astype(BF16), preferred_element_type=F32)
                        + jnp.dot(s["sqk"].astype(BF16), s["v"].astype(BF16), preferred_element_type=F32))
        for s in st:
            s["den"] = (s["w_in"] * jnp.sum(s["q"] * s["nvec"], axis=-1, keepdims=True)
                        + jnp.sum(s["sqk"], axis=-1, keepdims=True))
        for (g, h), s in zip(chains, st):
            h_scr[g, rows, s["cols"]] = s["num"] / jnp.maximum(jnp.abs(s["den"]), jnp.exp(-s["m_t"]))
        for (g, h), s in zip(chains, st):
            b_l = s["bc"][chunk - 1:chunk, :]
            g_r = b_l - s["br"] + s["ir"]
            m_new = jnp.maximum(b_l + s["m"], jnp.max(g_r, axis=-1, keepdims=True))
            decay = jnp.exp(b_l + s["m"] - m_new)
            w_g = jnp.exp(b_l - s["bc"] + s["ic"] - m_new)
            c_scr[g, h] = decay * s["cmat"] + lax.dot_general(
                s["kb"], (w_g * s["v"]).astype(BF16), (((0,), (0,)), ((), ())), preferred_element_type=F32)
            n_scr[g, h] = decay * s["nvec"] + jnp.sum(w_g * s["k"], axis=0, keepdims=True)
            m_scr[g, h] = m_new
        return carry

    if n_chunks == 1:
        scan_chunk(0, 0)
    else:
        lax.fori_loop(0, n_chunks, scan_chunk, 0)

    for g in range(group):
        for h in range(N_HEADS):
            cols = slice(h * hd, (h + 1) * hd)
            hb = h_scr[g, :, cols] * ob_ref[g, :, cols]
            hb = _rms(hb) * mg_ref[:, cols] + skip_ref[:, cols] * xch_scr[g, :, cols]
            bo_ref[g, :, cols] = (hb * zb_ref[g, :, cols]).astype(BF16)

    @pl.when(step == pl.num_programs(1) - 1)
    def _state_out():
        c1_ref[...] = c_scr[...]
        n1_ref[...] = n_scr[...]
        m1_ref[...] = m_scr[...]


def _mlstm(rest, conv0, c0, n0, m0, w, t, chunk, n_valid, group):
    bn, s, _ = rest.shape
    hd = HEAD_DIM
    tok = lambda col: pl.BlockSpec((group, t, GROUP_W), lambda b, j, col=col: (b, j, col))
    per_b = lambda *shape: pl.BlockSpec((group,) + shape, lambda b, j: (b,) + (0,) * len(shape))
    full = lambda *shape: pl.BlockSpec(shape, lambda b, j: (0,) * len(shape))
    state_shapes = [(N_HEADS, hd, hd), (N_HEADS, 1, hd), (N_HEADS, 1, 1)]
    state_specs = [per_b(*shape) for shape in state_shapes]
    return pl.pallas_call(
        functools.partial(_mlstm_kernel, t=t, chunk=chunk, n_valid=n_valid, group=group),
        grid=(bn // group, s // t),
        in_specs=[tok(1), tok(2), tok(3), per_b(CONV_W - 1, GROUP_W)] + state_specs + [
            full(CONV_W, GROUP_W), full(1, GROUP_W),
            full(N_HEADS, hd, hd), full(N_HEADS, hd, hd), full(N_HEADS, hd, hd),
            full(3 * GROUP_W, 2 * N_HEADS), full(2 * N_HEADS, 3 * GROUP_W),
            full(1, 2 * N_HEADS), full(2 * N_HEADS, 1),
            full(1, GROUP_W), full(1, GROUP_W)],
        out_specs=[tok(0)] + state_specs,
        out_shape=[jax.ShapeDtypeStruct((bn, s, GROUP_W), BF16)]
                  + [jax.ShapeDtypeStruct((bn,) + shape, F32) for shape in state_shapes],
        scratch_shapes=[pltpu.VMEM((group, t + HALO, GROUP_W), F32)]
                       + [pltpu.VMEM((group, t, GROUP_W), F32)] * 5
                       + [pltpu.VMEM((group, t, 2 * N_HEADS), F32),
                          pltpu.VMEM((group, t // chunk, 2 * N_HEADS, chunk), F32)] * 2
                       + [pltpu.VMEM((group,) + shape, F32) for shape in state_shapes],
        compiler_params=_cparams(("parallel", "arbitrary")),
        name="mlstm",
    )(rest, rest, rest, conv0, c0, n0, m0,
      w["conv_w"], w["conv_b"], w["wq"], w["wk"], w["wv"], w["wif"], w["wif_t"], w["bif_row"], w["bif_col"],
      w["mlstm_g"], w["mlstm_skip"])


def _outproj_kernel(a_ref, bo_ref, x_ref, p_ref, woa_ref, wob_ref, pg_ref, wgate_ref, wple_ref, plg_ref, y_ref):
    tm = x_ref.shape[0]
    n_slabs = OUTPROJ_SLABS if tm % (F32_SUBLANES * OUTPROJ_SLABS) == 0 else 1
    slabs = [slice(c * tm // n_slabs, (c + 1) * tm // n_slabs) for c in range(n_slabs)]
    y = [jnp.dot(a_ref[r, :], woa_ref[...], preferred_element_type=F32)
         + jnp.dot(bo_ref[r, :], wob_ref[...], preferred_element_type=F32) for r in slabs]
    pe = [jnp.dot(p_ref[r, :].astype(BF16), wple_ref[...], preferred_element_type=F32) for r in slabs]
    x1 = [x_ref[r, :] + _rms(y_r) * pg_ref[...] for r, y_r in zip(slabs, y)]
    gate = [_sigmoid(jnp.dot(x.astype(BF16), wgate_ref[...], preferred_element_type=F32)) for x in x1]
    for r, x, g, e in zip(slabs, x1, gate, pe):
        y_ref[r, :] = x + _rms(g * e) * plg_ref[...]


def _outproj(a, bo, x, p, w, tm):
    m, d = x.shape
    row = lambda width: pl.BlockSpec((tm, width), lambda i: (i, 0))
    full = lambda arr: pl.BlockSpec(arr.shape, lambda i: (0,) * arr.ndim)
    consts = [w["wo_a"], w["wo_b"], w["post_g"], w["ple_gate"], w["ple_w"], w["ple_g"]]
    return pl.pallas_call(
        _outproj_kernel,
        grid=(m // tm,),
        in_specs=[row(GROUP_W), row(GROUP_W), row(d), row(p.shape[1])] + [full(c) for c in consts],
        out_specs=row(d),
        out_shape=jax.ShapeDtypeStruct((m, d), F32),
        compiler_params=_cparams(("parallel",)),
        name="outproj",
    )(a, bo, x, p, *consts)


def _layer_weights(l, norm_pre_g, norm_post_g, w_in, lam_q1, lam_k1, lam_q2, lam_k2, attn_norm_g, conv_w, conv_b,
                   w_q_b, w_k_b, w_v_b, w_if, b_if, mlstm_norm_g, mlstm_skip, w_out, ple_w, ple_gate_w, ple_norm_g):
    row = lambda v: v.reshape(1, -1)
    wo = w_out[l].astype(BF16)
    return {
        "pre_g": row(norm_pre_g[l]), "w_in": w_in[l].astype(BF16),
        "lam": jnp.stack([lam_q1[l], lam_k1[l], lam_q2[l], lam_k2[l]]).astype(F32),
        "attn_g": row(attn_norm_g[l]),
        "conv_w": conv_w[l], "conv_b": row(conv_b[l]),
        "wq": w_q_b[l].astype(BF16), "wk": w_k_b[l].astype(BF16), "wv": w_v_b[l].astype(BF16),
        "wif": w_if[l].astype(BF16), "wif_t": w_if[l].T.astype(BF16),
        "bif_row": row(b_if[l]), "bif_col": b_if[l].reshape(-1, 1),
        "mlstm_g": row(mlstm_norm_g[l]), "mlstm_skip": row(mlstm_skip[l]),
        "wo_a": wo[:GROUP_W], "wo_b": wo[GROUP_W:], "post_g": row(norm_post_g[l]),
        "ple_gate": ple_gate_w[l].astype(BF16), "ple_w": ple_w[l].astype(BF16), "ple_g": row(ple_norm_g[l]),
    }


def _prompt_layer(x, p, w, lam_init):
    b, s, d = x.shape
    x2 = x.reshape(b * s, d)
    qt, k, v, rest, kb, vt = _inproj(x2, w["pre_g"], w["w_in"], tm=PROMPT_ROWS, kv_block=FLASH_BLOCK)
    rest3 = rest.reshape(b, s, 4 * GROUP_W)
    a = _flash(qt, kb.reshape(b, s, GROUP_W), vt, rest3, w["lam"], w["attn_g"], lam_init, tq=FLASH_BLOCK)
    hd = HEAD_DIM
    bo, c1, n1, m1 = _mlstm(rest3, jnp.zeros((b, CONV_W - 1, GROUP_W), F32),
                            jnp.zeros((b, N_HEADS, hd, hd), F32), jnp.zeros((b, N_HEADS, 1, hd), F32),
                            jnp.zeros((b, N_HEADS, 1, 1), F32), w, t=PROMPT_ROWS, chunk=PROMPT_CHUNK,
                            n_valid=PROMPT_CHUNK, group=PROMPT_GROUP)
    y = _outproj(a.reshape(b * s, GROUP_W), bo.reshape(b * s, GROUP_W), x2, p.reshape(b * s, -1), w,
                 tm=PROMPT_ROWS)
    state = (k.reshape(b, s, N_HEADS, hd), v.reshape(b, s, N_HEADS, hd),
             rest3[:, s - (CONV_W - 1):, GROUP_W:2 * GROUP_W],
             c1, n1.reshape(b, N_HEADS, hd), m1.reshape(b, N_HEADS))
    return y.reshape(b, s, d), state


def _sample_layer(x, p, n_valid, cache_k, cache_v, page_table, conv0, c0, n0, m0, w, lam_init):
    db, r, d = x.shape
    hd = HEAD_DIM
    assert CONV_W - 1 <= n_valid <= r
    x2 = x.reshape(db * r, d)
    q, k, v, rest = _inproj(x2, w["pre_g"], w["w_in"], tm=db * r)
    rest3 = rest.reshape(db, r, 4 * GROUP_W)
    k3 = k.reshape(db, r, GROUP_W)
    v3 = v.reshape(db, r, GROUP_W)
    a = _paged(q.reshape(db, r, GROUP_W), k3, v3, rest3, cache_k, cache_v, page_table, w["lam"], w["attn_g"],
               lam_init, n_valid)
    bo, c1, n1, m1 = _mlstm(rest3, conv0, c0, n0.reshape(db, N_HEADS, 1, hd), m0.reshape(db, N_HEADS, 1, 1),
                            w, t=r, chunk=r, n_valid=n_valid, group=SAMPLE_GROUP)
    y = _outproj(a.reshape(db * r, GROUP_W), bo.reshape(db * r, GROUP_W), x2, p.reshape(db * r, -1), w, tm=db * r)
    state = (k3[:, :n_valid].reshape(db, n_valid, N_HEADS, hd), v3[:, :n_valid].reshape(db, n_valid, N_HEADS, hd),
             rest3[:, n_valid - (CONV_W - 1):n_valid, GROUP_W:2 * GROUP_W],
             c1, n1.reshape(db, N_HEADS, hd), m1.reshape(db, N_HEADS))
    return y.reshape(db, r, d), state


def kernel(x_prompt, x_sample, cache_k, cache_v, state_conv, state_C, state_n, state_m, page_table,
           p_prompt, p_sample, norm_pre_g, norm_post_g, w_in, lam_q1, lam_k1, lam_q2, lam_k2,
           attn_norm_g, conv_w, conv_b, w_q_b, w_k_b, w_v_b, w_if, b_if, mlstm_norm_g, mlstm_skip,
           w_out, ple_w, ple_gate_w, ple_norm_g):
    depth = w_in.shape[0]
    t_dec = x_sample.shape[1]
    pad_rows = lambda a: jnp.pad(a, ((0, 0), (0, SAMPLE_ROWS - t_dec), (0, 0)))
    n_pool, page = cache_k.shape[1], cache_k.shape[2]
    cache_k2 = cache_k.reshape(depth * n_pool, page * N_HEADS, HEAD_DIM)
    cache_v2 = cache_v.reshape(depth * n_pool, page * N_HEADS, HEAD_DIM)
    xp = x_prompt
    xs = pad_rows(x_sample)
    outs_p, outs_s = [], []
    for l in range(depth):
        w = _layer_weights(l, norm_pre_g, norm_post_g, w_in, lam_q1, lam_k1, lam_q2, lam_k2, attn_norm_g,
                           conv_w, conv_b, w_q_b, w_k_b, w_v_b, w_if, b_if, mlstm_norm_g, mlstm_skip,
                           w_out, ple_w, ple_gate_w, ple_norm_g)
        lam_init = 0.8 - 0.6 * math.exp(-0.3 * l)
        xp, st_p = _prompt_layer(xp, p_prompt[l], w, lam_init)
        outs_p.append(st_p)
        xs, st_s = _sample_layer(xs, pad_rows(p_sample[l]), t_dec, cache_k2, cache_v2, page_table + l * n_pool,
                                 state_conv[l], state_C[l], state_n[l], state_m[l], w, lam_init)
        outs_s.append(st_s)
    stack = lambda outs, i: jnp.stack([o[i] for o in outs])
    return ((xp, xs[:, :t_dec]) + tuple(stack(outs_p, i) for i in range(6))
            + tuple(stack(outs_s, i) for i in range(6)))
```

```python
import functools
import math

import jax
import jax.numpy as jnp
from jax import lax
from jax.experimental import pallas as pl
from jax.experimental.pallas import tpu as pltpu

F32 = jnp.float32
BF16 = jnp.bfloat16
EPS = 1e-6
NEG_INF = float("-inf")

N_HEADS = 4
HEAD_DIM = 128
QK_DIM = 64
GROUP_W = N_HEADS * HEAD_DIM
CONV_W = 4
F32_SUBLANES = 8
SAMPLE_ROWS = F32_SUBLANES
HALO = F32_SUBLANES
PROMPT_CHUNK = 128
PROMPT_ROWS = 512
OUTPROJ_ROWS = 1024
FLASH_BLOCK = 512
FLASH_UNROLL = 4
OUTPROJ_SLABS = 4
PROMPT_GROUP = 2
SAMPLE_GROUP = 4
VT_ROWS = HEAD_DIM + 16
Q_SCALE = (QK_DIM ** -0.5) * math.log2(math.e)
PAGES_PER_STEP = 32
V7X_VMEM_LIMIT = 48 * 1024 * 1024


def _cparams(sem):
    return pltpu.CompilerParams(dimension_semantics=sem, vmem_limit_bytes=V7X_VMEM_LIMIT)


def _sigmoid(x):
    return 0.5 * jnp.tanh(0.5 * x) + 0.5


def _silu(x):
    return x * _sigmoid(x)


def _rms(x):
    return x * lax.rsqrt(jnp.mean(x * x, axis=-1, keepdims=True) + EPS)


def _inproj_kernel(x_ref, g_ref, w_ref, q_ref, k_ref, v_ref, rest_ref, *maybe_flash_refs, kv_block):
    hb = (_rms(x_ref[...]) * g_ref[...]).astype(BF16)

    def proj(c):
        return jnp.dot(hb, w_ref[:, c * GROUP_W:(c + 1) * GROUP_W], preferred_element_type=F32)

    q = proj(0) * Q_SCALE
    k = proj(1)
    v = proj(2)
    if not kv_block:
        q_ref[...] = q.astype(BF16)
        k_ref[...] = k
        v_ref[...] = v
    else:
        for h in range(N_HEADS):
            for c in range(q.shape[0] // kv_block):
                blk = q[c * kv_block:(c + 1) * kv_block, h * HEAD_DIM:(h + 1) * HEAD_DIM]
                q_ref[h, c] = _split_maps(blk).T.astype(BF16)
        for h in range(N_HEADS):
            cols = slice(h * HEAD_DIM, (h + 1) * HEAD_DIM)
            k_ref[pl.ds(h, k.shape[0], stride=N_HEADS), :] = k[:, cols]
            v_ref[pl.ds(h, v.shape[0], stride=N_HEADS), :] = v[:, cols]
        kb_ref, vt_ref = maybe_flash_refs
        kb_ref[...] = k.astype(BF16)
        for h in range(N_HEADS):
            for c in range(v.shape[0] // kv_block):
                blk = v[c * kv_block:(c + 1) * kv_block, h * HEAD_DIM:(h + 1) * HEAD_DIM]
                vt_ref[h, c, 0:HEAD_DIM, :] = blk.T.astype(BF16)
                vt_ref[h, c, HEAD_DIM:VT_ROWS, :] = jnp.ones((VT_ROWS - HEAD_DIM, kv_block), BF16)
    for c, act in zip(range(3, 7), (_silu, None, _sigmoid, _silu)):
        u = proj(c)
        rest_ref[:, (c - 3) * GROUP_W:(c - 2) * GROUP_W] = u if act is None else act(u)


def _inproj(x, g, w_bf, tm, kv_block=0):
    m, d = x.shape
    n_in = w_bf.shape[1]
    row = lambda w: pl.BlockSpec((tm, w), lambda i: (i, 0))
    out_specs = [row(GROUP_W)] * 3 + [row(4 * GROUP_W)]
    out_shape = [jax.ShapeDtypeStruct((m, GROUP_W), BF16),
                 jax.ShapeDtypeStruct((m, GROUP_W), F32),
                 jax.ShapeDtypeStruct((m, GROUP_W), F32),
                 jax.ShapeDtypeStruct((m, 4 * GROUP_W), F32)]
    if kv_block:
        out_specs[0] = pl.BlockSpec((N_HEADS, tm // kv_block, HEAD_DIM, 2 * kv_block), lambda i: (0, i, 0, 0))
        out_shape[0] = jax.ShapeDtypeStruct((N_HEADS, m // kv_block, HEAD_DIM, 2 * kv_block), BF16)
        for i_out in (1, 2):
            out_specs[i_out] = pl.BlockSpec((tm * N_HEADS, HEAD_DIM), lambda i: (i, 0))
            out_shape[i_out] = jax.ShapeDtypeStruct((m * N_HEADS, HEAD_DIM), F32)
        out_specs += [row(GROUP_W),
                      pl.BlockSpec((N_HEADS, tm // kv_block, VT_ROWS, kv_block), lambda i: (0, i, 0, 0))]
        out_shape += [jax.ShapeDtypeStruct((m, GROUP_W), BF16),
                      jax.ShapeDtypeStruct((N_HEADS, m // kv_block, VT_ROWS, kv_block), BF16)]
    return pl.pallas_call(
        functools.partial(_inproj_kernel, kv_block=kv_block),
        grid=(m // tm,),
        in_specs=[row(d), pl.BlockSpec((1, d), lambda i: (0, 0)),
                  pl.BlockSpec((d, n_in), lambda i: (0, 0))],
        out_specs=out_specs,
        out_shape=out_shape,
        compiler_params=_cparams(("parallel",)),
        name="inproj",
    )(x, g, w_bf)


def _split_maps(q):
    lane = lax.broadcasted_iota(jnp.int32, q.shape, 1)
    zero = jnp.zeros_like(q)
    return jnp.concatenate([jnp.where(lane < QK_DIM, q, zero), jnp.where(lane >= QK_DIM, q, zero)], axis=0)


def _lambda(lam_ref, lam_init):
    l1 = jnp.sum(lam_ref[0:1, :] * lam_ref[1:2, :], axis=-1, keepdims=True)
    l2 = jnp.sum(lam_ref[2:3, :] * lam_ref[3:4, :], axis=-1, keepdims=True)
    return jnp.exp(l1) - jnp.exp(l2) + lam_init


def _flash_kernel(qt_ref, k_ref, vt_ref, za_ref, lam_ref, g_ref, o_ref, m_scr, acc_scr,
                  s0_scr, s1_scr, *, tq, lam_init):
    i = pl.program_id(2)

    def scores(j, dst):
        k0 = pl.multiple_of(j * tq, tq)
        dst[...] = jnp.dot(k_ref[pl.ds(k0, tq), :], qt_ref[...], preferred_element_type=F32)

    def consume(src, j, masked):
        s = src[...]
        if masked:
            key = lax.broadcasted_iota(jnp.int32, s.shape, 0)
            qry = lax.broadcasted_iota(jnp.int32, s.shape, 1)
            s = jnp.where(key <= jnp.where(qry >= tq, qry - tq, qry), s, NEG_INF)
        m_prev = m_scr[...]
        m_new = jnp.maximum(m_prev, jnp.max(s, axis=0, keepdims=True))
        p = jnp.exp2(s - m_new)
        alpha = jnp.exp2(m_prev - m_new)
        acc_scr[...] = alpha * acc_scr[...] + jnp.dot(vt_ref[j], p.astype(BF16), preferred_element_type=F32)
        m_scr[...] = m_new

    bufs = (s0_scr, s1_scr)
    scores(0, s0_scr)
    m_scr[...] = jnp.full(m_scr.shape, NEG_INF, F32)
    acc_scr[...] = jnp.zeros(acc_scr.shape, F32)

    def run(first, count, masked_last):
        for u in range(count):
            last = u == count - 1
            if not (last and masked_last):
                scores(first + u + 1, bufs[(u + 1) % 2])
            consume(bufs[u % 2], first + u, last and masked_last)

    def full_group(jj, carry):
        run(FLASH_UNROLL * jj, FLASH_UNROLL, False)
        return carry

    lax.fori_loop(0, i // FLASH_UNROLL, full_group, 0)

    for rem in range(FLASH_UNROLL):
        @pl.when(i % FLASH_UNROLL == rem)
        def _tail(rem=rem):
            run(i - rem, rem + 1, True)

    lam = _lambda(lam_ref, lam_init)
    o_t = acc_scr[0:HEAD_DIM, :] * (1.0 / acc_scr[HEAD_DIM:HEAD_DIM + 1, :])
    o_t = o_t[:, 0:tq] - lam * o_t[:, tq:2 * tq]
    o_t = o_t * lax.rsqrt(jnp.mean(o_t * o_t, axis=0, keepdims=True) + EPS)
    o = o_t.T * g_ref[...] * (1.0 - lam_init)
    o_ref[...] = (o * za_ref[...]).astype(BF16)


def _flash(qt, kb, vt, rest, lamp, attn_g, lam_init, tq):
    b, s, _ = kb.shape
    nblk = s // tq
    blk = lambda: pl.BlockSpec((None, tq, HEAD_DIM), lambda bi, h, i: (bi, i, h))
    return pl.pallas_call(
        functools.partial(_flash_kernel, tq=tq, lam_init=lam_init),
        grid=(b, N_HEADS, nblk),
        in_specs=[pl.BlockSpec((None, None, HEAD_DIM, 2 * tq), lambda bi, h, i: (h, bi * nblk + i, 0, 0)),
                  pl.BlockSpec((None, s, HEAD_DIM), lambda bi, h, i: (bi, 0, h)),
                  pl.BlockSpec((None, nblk, VT_ROWS, tq), lambda bi, h, i: (h, bi, 0, 0)),
                  blk(),
                  pl.BlockSpec((4, QK_DIM), lambda bi, h, i: (0, 0)),
                  pl.BlockSpec((1, HEAD_DIM), lambda bi, h, i: (0, h))],
        out_specs=blk(),
        out_shape=jax.ShapeDtypeStruct((b, s, GROUP_W), BF16),
        scratch_shapes=[pltpu.VMEM((1, 2 * tq), F32),
                        pltpu.VMEM((VT_ROWS, 2 * tq), F32),
                        pltpu.VMEM((tq, 2 * tq), F32),
                        pltpu.VMEM((tq, 2 * tq), F32)],
        compiler_params=_cparams(("parallel", "parallel", "arbitrary")),
        name="flash_prompt",
    )(qt, kb, vt, rest, lamp, attn_g)


def _paged_kernel(pt_ref, q_ref, *refs, n_pages, n_valid, lam_init):
    del pt_ref
    kp = refs[:n_pages]
    vp = refs[n_pages:2 * n_pages]
    kn_ref, vn_ref, za_ref, lam_ref, g_ref, o_ref, qq_scr, bias_scr, m_scr, l_scr, acc_scr = refs[2 * n_pages:]
    r = SAMPLE_ROWS
    half = r // 2
    pw = bias_scr.shape[1]
    step = pl.program_id(1)

    @pl.when(step == 0)
    def _init():
        row = lax.broadcasted_iota(jnp.int32, (r, HEAD_DIM), 0)
        lane = lax.broadcasted_iota(jnp.int32, (r, HEAD_DIM), 1)
        for h in range(N_HEADS):
            q = q_ref[:, h * HEAD_DIM:(h + 1) * HEAD_DIM].astype(F32)
            q_map1 = jnp.where(lane < QK_DIM, q, 0.0)
            q_map2 = pltpu.roll(jnp.where(lane >= QK_DIM, q, 0.0), half, axis=0)
            qq_scr[h * r:(h + 1) * r, :] = jnp.where(row < half, q_map1, q_map2)
        row_head = lax.broadcasted_iota(jnp.int32, bias_scr.shape, 0) // r
        col_head = lax.broadcasted_iota(jnp.int32, bias_scr.shape, 1) % N_HEADS
        bias_scr[...] = jnp.where(row_head == col_head, 0.0, NEG_INF).astype(F32)
        m_scr[...] = jnp.full(m_scr.shape, NEG_INF, F32)
        l_scr[...] = jnp.zeros(l_scr.shape, F32)
        acc_scr[...] = jnp.zeros(acc_scr.shape, F32)

    qq = qq_scr[...].astype(BF16)
    bias = bias_scr[...]
    s = jnp.concatenate(
        [lax.dot_general(qq, kp[j][...].astype(BF16), (((1,), (1,)), ((), ())),
                         preferred_element_type=F32) + bias for j in range(n_pages)], axis=1)
    m_prev = m_scr[...]
    m_new = jnp.maximum(m_prev, jnp.max(s, axis=-1, keepdims=True))
    p = jnp.exp2(s - m_new)
    alpha = jnp.exp2(m_prev - m_new)
    l_scr[...] = alpha * l_scr[...] + jnp.sum(p, axis=-1, keepdims=True)
    p = p.astype(BF16)
    pv = jnp.dot(p[:, 0:pw], vp[0][...].astype(BF16), preferred_element_type=F32)
    for j in range(1, n_pages):
        pv = pv + jnp.dot(p[:, j * pw:(j + 1) * pw], vp[j][...].astype(BF16), preferred_element_type=F32)
    acc_scr[...] = alpha * acc_scr[...] + pv
    m_scr[...] = m_new

    @pl.when(step == pl.num_programs(1) - 1)
    def _finish():
        lam = _lambda(lam_ref, lam_init)
        row = lax.broadcasted_iota(jnp.int32, (r, 1), 0)
        tok = jnp.where(row >= half, row - half, row)
        for h in range(N_HEADS):
            cols = slice(h * HEAD_DIM, (h + 1) * HEAD_DIM)
            rows = slice(h * r, (h + 1) * r)
            qf = qq_scr[rows, :]
            kn = kn_ref[:, cols]
            vn = vn_ref[:, cols]
            s_new = [jnp.where(tok >= j, jnp.sum(qf * kn[j:j + 1, :], axis=-1, keepdims=True), NEG_INF)
                     for j in range(n_valid)]
            m_prev = m_scr[rows, :]
            m_new = m_prev
            for sj in s_new:
                m_new = jnp.maximum(m_new, sj)
            alpha = jnp.exp2(m_prev - m_new)
            l = alpha * l_scr[rows, :]
            acc = alpha * acc_scr[rows, :]
            for j, sj in enumerate(s_new):
                pj = jnp.exp2(sj - m_new)
                l = l + pj
                acc = acc + pj * vn[j:j + 1, :]
            o_maps = acc / l
            o = o_maps - lam * pltpu.roll(o_maps, half, axis=0)
            o = _rms(o) * g_ref[:, cols] * (1.0 - lam_init)
            o_ref[:, cols] = (o * za_ref[:, cols]).astype(BF16)


def _paged(q, k_new, v_new, rest, cache_k, cache_v, page_table, lamp, attn_g, lam_init, n_valid):
    db = q.shape[0]
    r = SAMPLE_ROWS
    assert 2 * n_valid <= r, "each head packs the tokens of both maps into SAMPLE_ROWS query rows"
    n_pages_total = page_table.shape[1]
    page_rows = cache_k.shape[1]
    p = PAGES_PER_STEP
    seq = lambda w=GROUP_W: pl.BlockSpec((None, r, w), lambda b, s, pt: (b, 0, 0))

    def page_spec(j):
        return pl.BlockSpec((None, page_rows, HEAD_DIM), lambda b, s, pt, j=j: (pt[b, s * p + j], 0, 0))

    grid_spec = pltpu.PrefetchScalarGridSpec(
        num_scalar_prefetch=1,
        grid=(db, n_pages_total // p),
        in_specs=[seq()] + [page_spec(j) for j in range(p)] * 2 + [seq(), seq(), seq(),
                  pl.BlockSpec((4, QK_DIM), lambda b, s, pt: (0, 0)),
                  pl.BlockSpec((1, GROUP_W), lambda b, s, pt: (0, 0))],
        out_specs=seq(),
        scratch_shapes=[pltpu.VMEM((N_HEADS * r, HEAD_DIM), F32),
                        pltpu.VMEM((N_HEADS * r, page_rows), F32),
                        pltpu.VMEM((N_HEADS * r, 1), F32),
                        pltpu.VMEM((N_HEADS * r, 1), F32),
                        pltpu.VMEM((N_HEADS * r, HEAD_DIM), F32)])
    return pl.pallas_call(
        functools.partial(_paged_kernel, n_pages=p, n_valid=n_valid, lam_init=lam_init),
        grid_spec=grid_spec,
        out_shape=jax.ShapeDtypeStruct((db, r, GROUP_W), BF16),
        compiler_params=_cparams(("parallel", "arbitrary")),
        name="paged_attn",
    )(page_table, q, *([cache_k] * p), *([cache_v] * p), k_new, v_new, rest, lamp, attn_g)


def _log_sigmoid(x):
    return jnp.minimum(x, 0.0) - jnp.log1p(jnp.exp(-jnp.abs(x)))


def _bf16_terms(x):
    hi = x.astype(BF16)
    rest = x - hi.astype(F32)
    mid = rest.astype(BF16)
    lo = (rest - mid.astype(F32)).astype(BF16)
    return hi, mid, lo


def _mlstm_kernel(xb_ref, ob_ref, zb_ref, conv0_ref, c0_ref, n0_ref, m0_ref,
                  cw_ref, cb_ref, wq_ref, wk_ref, wv_ref, wif_ref, wift_ref, bifr_ref, bifc_ref,
                  mg_ref, skip_ref,
                  bo_ref, c1_ref, n1_ref, m1_ref,
                  xpad, xch_scr, q_scr, k_scr, v_scr, h_scr, gc_scr, gr_scr, bc_scr, br_scr, c_scr, n_scr, m_scr,
                  *, t, chunk, n_valid, group):
    step = pl.program_id(1)
    n_chunks = t // chunk
    hd = HEAD_DIM
    ri = lax.broadcasted_iota(jnp.int32, (chunk, chunk), 0)
    ci = lax.broadcasted_iota(jnp.int32, (chunk, chunk), 1)
    causal = ci <= ri
    tril = jnp.where(causal, 1.0, 0.0).astype(BF16)
    triu = jnp.where(ri <= ci, 1.0, 0.0).astype(BF16)

    @pl.when(step == 0)
    def _init():
        for g in range(group):
            xpad[g, 0:HALO, :] = jnp.zeros((HALO, GROUP_W), F32)
            xpad[g, HALO - (CONV_W - 1):HALO, :] = conv0_ref[g]
        c_scr[...] = c0_ref[...]
        n_scr[...] = n0_ref[...]
        m_scr[...] = m0_ref[...]

    for g in range(group):
        x = xb_ref[g]
        xpad[g, HALO:HALO + t, :] = x
        xc = cb_ref[...] + cw_ref[CONV_W - 1:CONV_W, :] * x
        for j in range(CONV_W - 1):
            first = HALO - (CONV_W - 1) + j
            xc = xc + cw_ref[j:j + 1, :] * xpad[g, first:first + t, :]
        tail = xpad[g, t:t + HALO, :]
        xpad[g, 0:HALO, :] = tail
        xch = _silu(xc)
        xch_scr[g] = xch

        gcol = jnp.zeros((t, 2 * N_HEADS), F32) + bifr_ref[...]
        grow = jnp.zeros((2 * N_HEADS, t), F32) + bifc_ref[...]
        for h in range(N_HEADS):
            cols = slice(h * hd, (h + 1) * hd)
            xh = xch[:, cols].astype(BF16)
            qh = jnp.dot(xh, wq_ref[h], preferred_element_type=F32)
            kh = jnp.dot(xh, wk_ref[h], preferred_element_type=F32) * (hd ** -0.5)
            vh = jnp.dot(x[:, cols].astype(BF16), wv_ref[h], preferred_element_type=F32)
            q_scr[g, :, cols] = qh
            k_scr[g, :, cols] = kh
            v_scr[g, :, cols] = vh
            for part, val in enumerate((qh, kh, vh)):
                off = (3 * h + part) * hd
                vb = val.astype(BF16)
                gcol = gcol + jnp.dot(vb, wif_ref[off:off + hd, :], preferred_element_type=F32)
                grow = grow + lax.dot_general(wift_ref[:, off:off + hd], vb, (((1,), (1,)), ((), ())),
                                              preferred_element_type=F32)

        lane = lax.broadcasted_iota(jnp.int32, gcol.shape, 1)
        gcol = jnp.where(lane < N_HEADS, gcol, _log_sigmoid(gcol))
        sub = lax.broadcasted_iota(jnp.int32, grow.shape, 0)
        grow = jnp.where(sub < N_HEADS, grow, _log_sigmoid(grow))
        if n_valid < chunk:
            pos_c = lax.broadcasted_iota(jnp.int32, gcol.shape, 0) % chunk
            gcol = jnp.where(pos_c >= n_valid, jnp.where(lane < N_HEADS, NEG_INF, 0.0), gcol)
            pos_r = lax.broadcasted_iota(jnp.int32, grow.shape, 1) % chunk
            grow = jnp.where(pos_r >= n_valid, jnp.where(sub < N_HEADS, NEG_INF, 0.0), grow)
        gc_scr[g] = gcol
        lf_col = jnp.where(lane >= N_HEADS, gcol, 0.0)
        lf_row = jnp.where(sub >= N_HEADS, grow, 0.0)
        for c in range(n_chunks):
            span = slice(c * chunk, (c + 1) * chunk)
            gr_scr[g, c] = grow[:, span]
            bc_scr[g, span, :] = sum(jnp.dot(tril, part, preferred_element_type=F32)
                                     for part in _bf16_terms(lf_col[span, :]))
            br_scr[g, c] = sum(jnp.dot(part, triu, preferred_element_type=F32)
                               for part in _bf16_terms(lf_row[:, span]))

    def scan_chunk(c, carry):
        r0 = pl.multiple_of(c * chunk, chunk)
        rows = pl.ds(r0, chunk)
        chains = [(g, h) for g in range(group) for h in range(N_HEADS)]
        gates = {g: (gc_scr[g, rows, :], gr_scr[g, c], bc_scr[g, rows, :], br_scr[g, c]) for g in range(group)}
        st = []
        for g, h in chains:
            gc, gr, b_c, b_r = gates[g]
            bc = b_c[:, N_HEADS + h:N_HEADS + h + 1]
            br = b_r[N_HEADS + h:N_HEADS + h + 1, :]
            ic = gc[:, h:h + 1]
            ir = gr[h:h + 1, :]
            m = m_scr[g, h]
            dm = jnp.where(causal, bc - br + ir, NEG_INF)
            st.append(dict(bc=bc, br=br, ic=ic, ir=ir, m=m, dm=dm, inter=bc + m,
                           cols=slice(h * hd, (h + 1) * hd)))
        for s in st:
            s["m_t"] = jnp.maximum(s["inter"], jnp.max(s["dm"], axis=-1, keepdims=True))
        for s in st:
            s["w_d"] = jnp.exp(s["dm"] - s["m_t"])
            s["w_in"] = jnp.exp(s["inter"] - s["m_t"])
        for (g, h), s in zip(chains, st):
            s["q"] = q_scr[g, rows, s["cols"]]
            s["k"] = k_scr[g, rows, s["cols"]]
            s["v"] = v_scr[g, rows, s["cols"]]
            s["qb"] = s["q"].astype(BF16)
            s["kb"] = s["k"].astype(BF16)
            s["sqk"] = lax.dot_general(s["qb"], s["kb"], (((1,), (1,)), ((), ())),
                                       preferred_element_type=F32) * s["w_d"]
        for (g, h), s in zip(chains, st):
            s["cmat"] = c_scr[g, h]
            s["nvec"] = n_scr[g, h]
            s["num"] = (s["w_in"] * jnp.dot(s["qb"], s["cmat"].astype(BF16), preferred_element_type=F32)
                        + jnp.dot(s["sqk"].astype(BF16), s["v"].astype(BF16), preferred_element_type=F32))
        for s in st:
            s["den"] = (s["w_in"] * jnp.sum(s["q"] * s["nvec"], axis=-1, keepdims=True)
                        + jnp.sum(s["sqk"], axis=-1, keepdims=True))
        for (g, h), s in zip(chains, st):
            h_scr[g, rows, s["cols"]] = s["num"] / jnp.maximum(jnp.abs(s["den"]), jnp.exp(-s["m_t"]))
        for (g, h), s in zip(chains, st):
            b_l = s["bc"][chunk - 1:chunk, :]
            g_r = b_l - s["br"] + s["ir"]
            m_new = jnp.maximum(b_l + s["m"], jnp.max(g_r, axis=-1, keepdims=True))
            decay = jnp.exp(b_l + s["m"] - m_new)
            w_g = jnp.exp(b_l - s["bc"] + s["ic"] - m_new)
            c_scr[g, h] = decay * s["cmat"] + lax.dot_general(
                s["kb"], (w_g * s["v"]).astype(BF16), (((0,), (0,)), ((), ())), preferred_element_type=F32)
            n_scr[g, h] = decay * s["nvec"] + jnp.sum(w_g * s["k"], axis=0, keepdims=True)
            m_scr[g, h] = m_new
        return carry

    if n_chunks == 1:
        scan_chunk(0, 0)
    else:
        lax.fori_loop(0, n_chunks, scan_chunk, 0)

    for g in range(group):
        for h in range(N_HEADS):
            cols = slice(h * hd, (h + 1) * hd)
            hb = h_scr[g, :, cols] * ob_ref[g, :, cols]
            hb = _rms(hb) * mg_ref[:, cols] + skip_ref[:, cols] * xch_scr[g, :, cols]
            bo_ref[g, :, cols] = (hb * zb_ref[g, :, cols]).astype(BF16)

    @pl.when(step == pl.num_programs(1) - 1)
    def _state_out():
        c1_ref[...] = c_scr[...]
        n1_ref[...] = n_scr[...]
        m1_ref[...] = m_scr[...]


def _mlstm(rest, conv0, c0, n0, m0, w, t, chunk, n_valid, group):
    bn, s, _ = rest.shape
    hd = HEAD_DIM
    tok = lambda col: pl.BlockSpec((group, t, GROUP_W), lambda b, j, col=col: (b, j, col))
    per_b = lambda *shape: pl.BlockSpec((group,) + shape, lambda b, j: (b,) + (0,) * len(shape))
    full = lambda *shape: pl.BlockSpec(shape, lambda b, j: (0,) * len(shape))
    state_shapes = [(N_HEADS, hd, hd), (N_HEADS, 1, hd), (N_HEADS, 1, 1)]
    state_specs = [per_b(*shape) for shape in state_shapes]
    return pl.pallas_call(
        functools.partial(_mlstm_kernel, t=t, chunk=chunk, n_valid=n_valid, group=group),
        grid=(bn // group, s // t),
        in_specs=[tok(1), tok(2), tok(3), per_b(CONV_W - 1, GROUP_W)] + state_specs + [
            full(CONV_W, GROUP_W), full(1, GROUP_W),
            full(N_HEADS, hd, hd), full(N_HEADS, hd, hd), full(N_HEADS, hd, hd),
            full(3 * GROUP_W, 2 * N_HEADS), full(2 * N_HEADS, 3 * GROUP_W),
            full(1, 2 * N_HEADS), full(2 * N_HEADS, 1),
            full(1, GROUP_W), full(1, GROUP_W)],
        out_specs=[tok(0)] + state_specs,
        out_shape=[jax.ShapeDtypeStruct((bn, s, GROUP_W), BF16)]
                  + [jax.ShapeDtypeStruct((bn,) + shape, F32) for shape in state_shapes],
        scratch_shapes=[pltpu.VMEM((group, t + HALO, GROUP_W), F32)]
                       + [pltpu.VMEM((group, t, GROUP_W), F32)] * 5
                       + [pltpu.VMEM((group, t, 2 * N_HEADS), F32),
                          pltpu.VMEM((group, t // chunk, 2 * N_HEADS, chunk), F32)] * 2
                       + [pltpu.VMEM((group,) + shape, F32) for shape in state_shapes],
        compiler_params=_cparams(("parallel", "arbitrary")),
        name="mlstm",
    )(rest, rest, rest, conv0, c0, n0, m0,
      w["conv_w"], w["conv_b"], w["wq"], w["wk"], w["wv"], w["wif"], w["wif_t"], w["bif_row"], w["bif_col"],
      w["mlstm_g"], w["mlstm_skip"])


def _outproj_kernel(a_ref, bo_ref, x_ref, p_ref, woa_ref, wob_ref, pg_ref, wgate_ref, wple_ref, plg_ref, y_ref):
    tm = x_ref.shape[0]
    n_slabs = OUTPROJ_SLABS if tm % (F32_SUBLANES * OUTPROJ_SLABS) == 0 else 1
    slabs = [slice(c * tm // n_slabs, (c + 1) * tm // n_slabs) for c in range(n_slabs)]
    y = [jnp.dot(a_ref[r, :], woa_ref[...], preferred_element_type=F32)
         + jnp.dot(bo_ref[r, :], wob_ref[...], preferred_element_type=F32) for r in slabs]
    pe = [jnp.dot(p_ref[r, :].astype(BF16), wple_ref[...], preferred_element_type=F32) for r in slabs]
    x1 = [x_ref[r, :] + _rms(y_r) * pg_ref[...] for r, y_r in zip(slabs, y)]
    gate = [_sigmoid(jnp.dot(x.astype(BF16), wgate_ref[...], preferred_element_type=F32)) for x in x1]
    for r, x, g, e in zip(slabs, x1, gate, pe):
        y_ref[r, :] = x + _rms(g * e) * plg_ref[...]


def _outproj(a, bo, x, p, w, tm):
    m, d = x.shape
    row = lambda width: pl.BlockSpec((tm, width), lambda i: (i, 0))
    full = lambda arr: pl.BlockSpec(arr.shape, lambda i: (0,) * arr.ndim)
    consts = [w["wo_a"], w["wo_b"], w["post_g"], w["ple_gate"], w["ple_w"], w["ple_g"]]
    return pl.pallas_call(
        _outproj_kernel,
        grid=(m // tm,),
        in_specs=[row(GROUP_W), row(GROUP_W), row(d), row(p.shape[1])] + [full(c) for c in consts],
        out_specs=row(d),
        out_shape=jax.ShapeDtypeStruct((m, d), F32),
        compiler_params=_cparams(("parallel",)),
        name="outproj",
    )(a, bo, x, p, *consts)


def _layer_weights(l, norm_pre_g, norm_post_g, w_in, lam_q1, lam_k1, lam_q2, lam_k2, attn_norm_g, conv_w, conv_b,
                   w_q_b, w_k_b, w_v_b, w_if, b_if, mlstm_norm_g, mlstm_skip, w_out, ple_w, ple_gate_w, ple_norm_g):
    row = lambda v: v.reshape(1, -1)
    wo = w_out[l].astype(BF16)
    return {
        "pre_g": row(norm_pre_g[l]), "w_in": w_in[l].astype(BF16),
        "lam": jnp.stack([lam_q1[l], lam_k1[l], lam_q2[l], lam_k2[l]]).astype(F32),
        "attn_g": row(attn_norm_g[l]),
        "conv_w": conv_w[l], "conv_b": row(conv_b[l]),
        "wq": w_q_b[l].astype(BF16), "wk": w_k_b[l].astype(BF16), "wv": w_v_b[l].astype(BF16),
        "wif": w_if[l].astype(BF16), "wif_t": w_if[l].T.astype(BF16),
        "bif_row": row(b_if[l]), "bif_col": b_if[l].reshape(-1, 1),
        "mlstm_g": row(mlstm_norm_g[l]), "mlstm_skip": row(mlstm_skip[l]),
        "wo_a": wo[:GROUP_W], "wo_b": wo[GROUP_W:], "post_g": row(norm_post_g[l]),
        "ple_gate": ple_gate_w[l].astype(BF16), "ple_w": ple_w[l].astype(BF16), "ple_g": row(ple_norm_g[l]),
    }


def _prompt_layer(x, p, w, lam_init):
    b, s, d = x.shape
    x2 = x.reshape(b * s, d)
    qt, k, v, rest, kb, vt = _inproj(x2, w["pre_g"], w["w_in"], tm=PROMPT_ROWS, kv_block=FLASH_BLOCK)
    rest3 = rest.reshape(b, s, 4 * GROUP_W)
    a = _flash(qt, kb.reshape(b, s, GROUP_W), vt, rest3, w["lam"], w["attn_g"], lam_init, tq=FLASH_BLOCK)
    hd = HEAD_DIM
    bo, c1, n1, m1 = _mlstm(rest3, jnp.zeros((b, CONV_W - 1, GROUP_W), F32),
                            jnp.zeros((b, N_HEADS, hd, hd), F32), jnp.zeros((b, N_HEADS, 1, hd), F32),
                            jnp.zeros((b, N_HEADS, 1, 1), F32), w, t=PROMPT_ROWS, chunk=PROMPT_CHUNK,
                            n_valid=PROMPT_CHUNK, group=PROMPT_GROUP)
    y = _outproj(a.reshape(b * s, GROUP_W), bo.reshape(b * s, GROUP_W), x2, p.reshape(b * s, -1), w,
                 tm=OUTPROJ_ROWS)
    state = (k.reshape(b, s, N_HEADS, hd), v.reshape(b, s, N_HEADS, hd),
             rest3[:, s - (CONV_W - 1):, GROUP_W:2 * GROUP_W],
             c1, n1.reshape(b, N_HEADS, hd), m1.reshape(b, N_HEADS))
    return y.reshape(b, s, d), state


def _sample_layer(x, p, n_valid, cache_k, cache_v, page_table, conv0, c0, n0, m0, w, lam_init):
    db, r, d = x.shape
    hd = HEAD_DIM
    assert CONV_W - 1 <= n_valid <= r
    x2 = x.reshape(db * r, d)
    q, k, v, rest = _inproj(x2, w["pre_g"], w["w_in"], tm=db * r)
    rest3 = rest.reshape(db, r, 4 * GROUP_W)
    k3 = k.reshape(db, r, GROUP_W)
    v3 = v.reshape(db, r, GROUP_W)
    a = _paged(q.reshape(db, r, GROUP_W), k3, v3, rest3, cache_k, cache_v, page_table, w["lam"], w["attn_g"],
               lam_init, n_valid)
    bo, c1, n1, m1 = _mlstm(rest3, conv0, c0, n0.reshape(db, N_HEADS, 1, hd), m0.reshape(db, N_HEADS, 1, 1),
                            w, t=r, chunk=r, n_valid=n_valid, group=SAMPLE_GROUP)
    y = _outproj(a.reshape(db * r, GROUP_W), bo.reshape(db * r, GROUP_W), x2, p.reshape(db * r, -1), w, tm=db * r)
    state = (k3[:, :n_valid].reshape(db, n_valid, N_HEADS, hd), v3[:, :n_valid].reshape(db, n_valid, N_HEADS, hd),
             rest3[:, n_valid - (CONV_W - 1):n_valid, GROUP_W:2 * GROUP_W],
             c1, n1.reshape(db, N_HEADS, hd), m1.reshape(db, N_HEADS))
    return y.reshape(db, r, d), state


def kernel(x_prompt, x_sample, cache_k, cache_v, state_conv, state_C, state_n, state_m, page_table,
           p_prompt, p_sample, norm_pre_g, norm_post_g, w_in, lam_q1, lam_k1, lam_q2, lam_k2,
           attn_norm_g, conv_w, conv_b, w_q_b, w_k_b, w_v_b, w_if, b_if, mlstm_norm_g, mlstm_skip,
           w_out, ple_w, ple_gate_w, ple_norm_g):
    depth = w_in.shape[0]
    t_dec = x_sample.shape[1]
    pad_rows = lambda a: jnp.pad(a, ((0, 0), (0, SAMPLE_ROWS - t_dec), (0, 0)))
    n_pool, page = cache_k.shape[1], cache_k.shape[2]
    cache_k2 = cache_k.reshape(depth * n_pool, page * N_HEADS, HEAD_DIM)
    cache_v2 = cache_v.reshape(depth * n_pool, page * N_HEADS, HEAD_DIM)
    xp = x_prompt
    xs = pad_rows(x_sample)
    outs_p, outs_s = [], []
    for l in range(depth):
        w = _layer_weights(l, norm_pre_g, norm_post_g, w_in, lam_q1, lam_k1, lam_q2, lam_k2, attn_norm_g,
                           conv_w, conv_b, w_q_b, w_k_b, w_v_b, w_if, b_if, mlstm_norm_g, mlstm_skip,
                           w_out, ple_w, ple_gate_w, ple_norm_g)
        lam_init = 0.8 - 0.6 * math.exp(-0.3 * l)
        xp, st_p = _prompt_layer(xp, p_prompt[l], w, lam_init)
        outs_p.append(st_p)
        xs, st_s = _sample_layer(xs, pad_rows(p_sample[l]), t_dec, cache_k2, cache_v2, page_table + l * n_pool,
                                 state_conv[l], state_C[l], state_n[l], state_m[l], w, lam_init)
        outs_s.append(st_s)
    stack = lambda outs, i: jnp.stack([o[i] for o in outs])
    return ((xp, xs[:, :t_dec]) + tuple(stack(outs_p, i) for i in range(6))
            + tuple(stack(outs_s, i) for i in range(6)))
```

```python
import functools
import math

import jax
import jax.numpy as jnp
from jax import lax
from jax.experimental import pallas as pl
from jax.experimental.pallas import tpu as pltpu

F32 = jnp.float32
BF16 = jnp.bfloat16
EPS = 1e-6
NEG_INF = float("-inf")

N_HEADS = 4
HEAD_DIM = 128
QK_DIM = 64
GROUP_W = N_HEADS * HEAD_DIM
CONV_W = 4
F32_SUBLANES = 8
SAMPLE_ROWS = F32_SUBLANES
HALO = F32_SUBLANES
PROMPT_CHUNK = 128
PROMPT_ROWS = 512
OUTPROJ_ROWS = 1024
FLASH_BLOCK = 512
FLASH_UNROLL = 4
OUTPROJ_SLABS = 4
PROMPT_GROUP = 2
SAMPLE_GROUP = 4
VT_ROWS = HEAD_DIM + 16
Q_SCALE = (QK_DIM ** -0.5) * math.log2(math.e)
PAGES_PER_STEP = 32
V7X_VMEM_LIMIT = 48 * 1024 * 1024


def _cparams(sem):
    return pltpu.CompilerParams(dimension_semantics=sem, vmem_limit_bytes=V7X_VMEM_LIMIT)


def _sigmoid(x):
    return 0.5 * jnp.tanh(0.5 * x) + 0.5


def _silu(x):
    return x * _sigmoid(x)


def _rms(x):
    return x * lax.rsqrt(jnp.mean(x * x, axis=-1, keepdims=True) + EPS)


def _inproj_kernel(x_ref, g_ref, w_ref, q_ref, k_ref, v_ref, rest_ref, *maybe_flash_refs, kv_block):
    hb = (_rms(x_ref[...]) * g_ref[...]).astype(BF16)

    def proj(c):
        return jnp.dot(hb, w_ref[:, c * GROUP_W:(c + 1) * GROUP_W], preferred_element_type=F32)

    q = proj(0) * Q_SCALE
    k = proj(1)
    v = proj(2)
    if not kv_block:
        q_ref[...] = q.astype(BF16)
        k_ref[...] = k
        v_ref[...] = v
    else:
        for h in range(N_HEADS):
            for c in range(q.shape[0] // kv_block):
                blk = q[c * kv_block:(c + 1) * kv_block, h * HEAD_DIM:(h + 1) * HEAD_DIM]
                q_ref[h, c] = _split_maps(blk).T.astype(BF16)
        for h in range(N_HEADS):
            cols = slice(h * HEAD_DIM, (h + 1) * HEAD_DIM)
            k_ref[pl.ds(h, k.shape[0], stride=N_HEADS), :] = k[:, cols]
            v_ref[pl.ds(h, v.shape[0], stride=N_HEADS), :] = v[:, cols]
        kb_ref, vt_ref = maybe_flash_refs
        kb_ref[...] = k.astype(BF16)
        for h in range(N_HEADS):
            for c in range(v.shape[0] // kv_block):
                blk = v[c * kv_block:(c + 1) * kv_block, h * HEAD_DIM:(h + 1) * HEAD_DIM]
                vt_ref[h, c, 0:HEAD_DIM, :] = blk.T.astype(BF16)
                vt_ref[h, c, HEAD_DIM:VT_ROWS, :] = jnp.ones((VT_ROWS - HEAD_DIM, kv_block), BF16)
    for c, act in zip(range(3, 7), (_silu, None, _sigmoid, _silu)):
        u = proj(c)
        rest_ref[:, (c - 3) * GROUP_W:(c - 2) * GROUP_W] = u if act is None else act(u)


def _inproj(x, g, w_bf, tm, kv_block=0):
    m, d = x.shape
    n_in = w_bf.shape[1]
    row = lambda w: pl.BlockSpec((tm, w), lambda i: (i, 0))
    out_specs = [row(GROUP_W)] * 3 + [row(4 * GROUP_W)]
    out_shape = [jax.ShapeDtypeStruct((m, GROUP_W), BF16),
                 jax.ShapeDtypeStruct((m, GROUP_W), F32),
                 jax.ShapeDtypeStruct((m, GROUP_W), F32),
                 jax.ShapeDtypeStruct((m, 4 * GROUP_W), F32)]
    if kv_block:
        out_specs[0] = pl.BlockSpec((N_HEADS, tm // kv_block, HEAD_DIM, 2 * kv_block), lambda i: (0, i, 0, 0))
        out_shape[0] = jax.ShapeDtypeStruct((N_HEADS, m // kv_block, HEAD_DIM, 2 * kv_block), BF16)
        for i_out in (1, 2):
            out_specs[i_out] = pl.BlockSpec((tm * N_HEADS, HEAD_DIM), lambda i: (i, 0))
            out_shape[i_out] = jax.ShapeDtypeStruct((m * N_HEADS, HEAD_DIM), F32)
        out_specs += [row(GROUP_W),
                      pl.BlockSpec((N_HEADS, tm // kv_block, VT_ROWS, kv_block), lambda i: (0, i, 0, 0))]
        out_shape += [jax.ShapeDtypeStruct((m, GROUP_W), BF16),
                      jax.ShapeDtypeStruct((N_HEADS, m // kv_block, VT_ROWS, kv_block), BF16)]
    return pl.pallas_call(
        functools.partial(_inproj_kernel, kv_block=kv_block),
        grid=(m // tm,),
        in_specs=[row(d), pl.BlockSpec((1, d), lambda i: (0, 0)),
                  pl.BlockSpec((d, n_in), lambda i: (0, 0))],
        out_specs=out_specs,
        out_shape=out_shape,
        compiler_params=_cparams(("parallel",)),
        name="inproj",
    )(x, g, w_bf)


def _split_maps(q):
    lane = lax.broadcasted_iota(jnp.int32, q.shape, 1)
    zero = jnp.zeros_like(q)
    return jnp.concatenate([jnp.where(lane < QK_DIM, q, zero), jnp.where(lane >= QK_DIM, q, zero)], axis=0)


def _lambda(lam_ref, lam_init):
    l1 = jnp.sum(lam_ref[0:1, :] * lam_ref[1:2, :], axis=-1, keepdims=True)
    l2 = jnp.sum(lam_ref[2:3, :] * lam_ref[3:4, :], axis=-1, keepdims=True)
    return jnp.exp(l1) - jnp.exp(l2) + lam_init


def _flash_kernel(qt_ref, k_ref, vt_ref, za_ref, lam_ref, g_ref, o_ref, m_scr, acc_scr,
                  s0_scr, s1_scr, *, tq, lam_init):
    nblk = qt_ref.shape[0]
    lam = _lambda(lam_ref, lam_init)

    def scores(i, j, dst):
        k0 = pl.multiple_of(j * tq, tq)
        dst[...] = jnp.dot(k_ref[pl.ds(k0, tq), :], qt_ref[i], preferred_element_type=F32)

    def consume(src, j, masked):
        s = src[...]
        if masked:
            key = lax.broadcasted_iota(jnp.int32, s.shape, 0)
            qry = lax.broadcasted_iota(jnp.int32, s.shape, 1)
            s = jnp.where(key <= jnp.where(qry >= tq, qry - tq, qry), s, NEG_INF)
        m_prev = m_scr[...]
        m_new = jnp.maximum(m_prev, jnp.max(s, axis=0, keepdims=True))
        p = jnp.exp2(s - m_new)
        alpha = jnp.exp2(m_prev - m_new)
        acc_scr[...] = alpha * acc_scr[...] + jnp.dot(vt_ref[j], p.astype(BF16), preferred_element_type=F32)
        m_scr[...] = m_new

    bufs = (s0_scr, s1_scr)

    def run(i, first, count, masked_last):
        for u in range(count):
            last = u == count - 1
            if not (last and masked_last):
                scores(i, first + u + 1, bufs[(u + 1) % 2])
            consume(bufs[u % 2], first + u, last and masked_last)

    def finalize(i):
        rows = pl.ds(pl.multiple_of(i * tq, tq), tq)
        o_t = acc_scr[0:HEAD_DIM, :] * (1.0 / acc_scr[HEAD_DIM:HEAD_DIM + 1, :])
        o_t = o_t[:, 0:tq] - lam * o_t[:, tq:2 * tq]
        o_t = o_t * lax.rsqrt(jnp.mean(o_t * o_t, axis=0, keepdims=True) + EPS)
        o = o_t.T * g_ref[...] * (1.0 - lam_init)
        o_ref[rows, :] = (o * za_ref[rows, :]).astype(BF16)

    def query_block(i, carry):
        m_scr[...] = jnp.full(m_scr.shape, NEG_INF, F32)
        acc_scr[...] = jnp.zeros(acc_scr.shape, F32)

        def full_group(jj, c):
            run(i, FLASH_UNROLL * jj, FLASH_UNROLL, False)
            return c

        lax.fori_loop(0, i // FLASH_UNROLL, full_group, 0)
        for rem in range(FLASH_UNROLL):
            @pl.when(i % FLASH_UNROLL == rem)
            def _tail(rem=rem):
                run(i, i - rem, rem + 1, True)
        scores(jnp.minimum(i + 1, nblk - 1), 0, s0_scr)
        finalize(i)
        return carry

    scores(0, 0, s0_scr)
    lax.fori_loop(0, nblk, query_block, 0)


def _flash(qt, kb, vt, rest, lamp, attn_g, lam_init, tq):
    b, s, _ = kb.shape
    nblk = s // tq
    seq = lambda: pl.BlockSpec((None, s, HEAD_DIM), lambda bi, h: (bi, 0, h))
    return pl.pallas_call(
        functools.partial(_flash_kernel, tq=tq, lam_init=lam_init),
        grid=(b, N_HEADS),
        in_specs=[pl.BlockSpec((None, nblk, HEAD_DIM, 2 * tq), lambda bi, h: (h, bi, 0, 0)),
                  seq(),
                  pl.BlockSpec((None, nblk, VT_ROWS, tq), lambda bi, h: (h, bi, 0, 0)),
                  seq(),
                  pl.BlockSpec((4, QK_DIM), lambda bi, h: (0, 0)),
                  pl.BlockSpec((1, HEAD_DIM), lambda bi, h: (0, h))],
        out_specs=seq(),
        out_shape=jax.ShapeDtypeStruct((b, s, GROUP_W), BF16),
        scratch_shapes=[pltpu.VMEM((1, 2 * tq), F32),
                        pltpu.VMEM((VT_ROWS, 2 * tq), F32),
                        pltpu.VMEM((tq, 2 * tq), F32),
                        pltpu.VMEM((tq, 2 * tq), F32)],
        compiler_params=_cparams(("parallel", "parallel")),
        name="flash_prompt",
    )(qt, kb, vt, rest, lamp, attn_g)


def _paged_kernel(pt_ref, q_ref, *refs, n_pages, n_valid, lam_init):
    del pt_ref
    kp = refs[:n_pages]
    vp = refs[n_pages:2 * n_pages]
    kn_ref, vn_ref, za_ref, lam_ref, g_ref, o_ref, qq_scr, bias_scr, m_scr, l_scr, acc_scr = refs[2 * n_pages:]
    r = SAMPLE_ROWS
    half = r // 2
    pw = bias_scr.shape[1]
    step = pl.program_id(1)

    @pl.when(step == 0)
    def _init():
        row = lax.broadcasted_iota(jnp.int32, (r, HEAD_DIM), 0)
        lane = lax.broadcasted_iota(jnp.int32, (r, HEAD_DIM), 1)
        for h in range(N_HEADS):
            q = q_ref[:, h * HEAD_DIM:(h + 1) * HEAD_DIM].astype(F32)
            q_map1 = jnp.where(lane < QK_DIM, q, 0.0)
            q_map2 = pltpu.roll(jnp.where(lane >= QK_DIM, q, 0.0), half, axis=0)
            qq_scr[h * r:(h + 1) * r, :] = jnp.where(row < half, q_map1, q_map2)
        row_head = lax.broadcasted_iota(jnp.int32, bias_scr.shape, 0) // r
        col_head = lax.broadcasted_iota(jnp.int32, bias_scr.shape, 1) % N_HEADS
        bias_scr[...] = jnp.where(row_head == col_head, 0.0, NEG_INF).astype(F32)
        m_scr[...] = jnp.full(m_scr.shape, NEG_INF, F32)
        l_scr[...] = jnp.zeros(l_scr.shape, F32)
        acc_scr[...] = jnp.zeros(acc_scr.shape, F32)

    qq = qq_scr[...].astype(BF16)
    bias = bias_scr[...]
    s = jnp.concatenate(
        [lax.dot_general(qq, kp[j][...].astype(BF16), (((1,), (1,)), ((), ())),
                         preferred_element_type=F32) + bias for j in range(n_pages)], axis=1)
    m_prev = m_scr[...]
    m_new = jnp.maximum(m_prev, jnp.max(s, axis=-1, keepdims=True))
    p = jnp.exp2(s - m_new)
    alpha = jnp.exp2(m_prev - m_new)
    l_scr[...] = alpha * l_scr[...] + jnp.sum(p, axis=-1, keepdims=True)
    p = p.astype(BF16)
    pv = jnp.dot(p[:, 0:pw], vp[0][...].astype(BF16), preferred_element_type=F32)
    for j in range(1, n_pages):
        pv = pv + jnp.dot(p[:, j * pw:(j + 1) * pw], vp[j][...].astype(BF16), preferred_element_type=F32)
    acc_scr[...] = alpha * acc_scr[...] + pv
    m_scr[...] = m_new

    @pl.when(step == pl.num_programs(1) - 1)
    def _finish():
        lam = _lambda(lam_ref, lam_init)
        row = lax.broadcasted_iota(jnp.int32, (r, 1), 0)
        tok = jnp.where(row >= half, row - half, row)
        for h in range(N_HEADS):
            cols = slice(h * HEAD_DIM, (h + 1) * HEAD_DIM)
            rows = slice(h * r, (h + 1) * r)
            qf = qq_scr[rows, :]
            kn = kn_ref[:, cols]
            vn = vn_ref[:, cols]
            s_new = [jnp.where(tok >= j, jnp.sum(qf * kn[j:j + 1, :], axis=-1, keepdims=True), NEG_INF)
                     for j in range(n_valid)]
            m_prev = m_scr[rows, :]
            m_new = m_prev
            for sj in s_new:
                m_new = jnp.maximum(m_new, sj)
            alpha = jnp.exp2(m_prev - m_new)
            l = alpha * l_scr[rows, :]
            acc = alpha * acc_scr[rows, :]
            for j, sj in enumerate(s_new):
                pj = jnp.exp2(sj - m_new)
                l = l + pj
                acc = acc + pj * vn[j:j + 1, :]
            o_maps = acc / l
            o = o_maps - lam * pltpu.roll(o_maps, half, axis=0)
            o = _rms(o) * g_ref[:, cols] * (1.0 - lam_init)
            o_ref[:, cols] = (o * za_ref[:, cols]).astype(BF16)


def _paged(q, k_new, v_new, rest, cache_k, cache_v, page_table, lamp, attn_g, lam_init, n_valid):
    db = q.shape[0]
    r = SAMPLE_ROWS
    assert 2 * n_valid <= r, "each head packs the tokens of both maps into SAMPLE_ROWS query rows"
    n_pages_total = page_table.shape[1]
    page_rows = cache_k.shape[1]
    p = PAGES_PER_STEP
    seq = lambda w=GROUP_W: pl.BlockSpec((None, r, w), lambda b, s, pt: (b, 0, 0))

    def page_spec(j):
        return pl.BlockSpec((None, page_rows, HEAD_DIM), lambda b, s, pt, j=j: (pt[b, s * p + j], 0, 0))

    grid_spec = pltpu.PrefetchScalarGridSpec(
        num_scalar_prefetch=1,
        grid=(db, n_pages_total // p),
        in_specs=[seq()] + [page_spec(j) for j in range(p)] * 2 + [seq(), seq(), seq(),
                  pl.BlockSpec((4, QK_DIM), lambda b, s, pt: (0, 0)),
                  pl.BlockSpec((1, GROUP_W), lambda b, s, pt: (0, 0))],
        out_specs=seq(),
        scratch_shapes=[pltpu.VMEM((N_HEADS * r, HEAD_DIM), F32),
                        pltpu.VMEM((N_HEADS * r, page_rows), F32),
                        pltpu.VMEM((N_HEADS * r, 1), F32),
                        pltpu.VMEM((N_HEADS * r, 1), F32),
                        pltpu.VMEM((N_HEADS * r, HEAD_DIM), F32)])
    return pl.pallas_call(
        functools.partial(_paged_kernel, n_pages=p, n_valid=n_valid, lam_init=lam_init),
        grid_spec=grid_spec,
        out_shape=jax.ShapeDtypeStruct((db, r, GROUP_W), BF16),
        compiler_params=_cparams(("parallel", "arbitrary")),
        name="paged_attn",
    )(page_table, q, *([cache_k] * p), *([cache_v] * p), k_new, v_new, rest, lamp, attn_g)


def _log_sigmoid(x):
    return jnp.minimum(x, 0.0) - jnp.log1p(jnp.exp(-jnp.abs(x)))


def _bf16_terms(x):
    hi = x.astype(BF16)
    rest = x - hi.astype(F32)
    mid = rest.astype(BF16)
    lo = (rest - mid.astype(F32)).astype(BF16)
    return hi, mid, lo


def _mlstm_kernel(xb_ref, ob_ref, zb_ref, conv0_ref, c0_ref, n0_ref, m0_ref,
                  cw_ref, cb_ref, wq_ref, wk_ref, wv_ref, wif_ref, wift_ref, bifr_ref, bifc_ref,
                  mg_ref, skip_ref,
                  bo_ref, c1_ref, n1_ref, m1_ref,
                  xpad, xch_scr, q_scr, k_scr, v_scr, h_scr, gc_scr, gr_scr, bc_scr, br_scr, c_scr, n_scr, m_scr,
                  *, t, chunk, n_valid, group):
    step = pl.program_id(1)
    n_chunks = t // chunk
    hd = HEAD_DIM
    ri = lax.broadcasted_iota(jnp.int32, (chunk, chunk), 0)
    ci = lax.broadcasted_iota(jnp.int32, (chunk, chunk), 1)
    causal = ci <= ri
    tril = jnp.where(causal, 1.0, 0.0).astype(BF16)
    triu = jnp.where(ri <= ci, 1.0, 0.0).astype(BF16)

    @pl.when(step == 0)
    def _init():
        for g in range(group):
            xpad[g, 0:HALO, :] = jnp.zeros((HALO, GROUP_W), F32)
            xpad[g, HALO - (CONV_W - 1):HALO, :] = conv0_ref[g]
        c_scr[...] = c0_ref[...]
        n_scr[...] = n0_ref[...]
        m_scr[...] = m0_ref[...]

    for g in range(group):
        x = xb_ref[g]
        xpad[g, HALO:HALO + t, :] = x
        xc = cb_ref[...] + cw_ref[CONV_W - 1:CONV_W, :] * x
        for j in range(CONV_W - 1):
            first = HALO - (CONV_W - 1) + j
            xc = xc + cw_ref[j:j + 1, :] * xpad[g, first:first + t, :]
        tail = xpad[g, t:t + HALO, :]
        xpad[g, 0:HALO, :] = tail
        xch = _silu(xc)
        xch_scr[g] = xch

        gcol = jnp.zeros((t, 2 * N_HEADS), F32) + bifr_ref[...]
        grow = jnp.zeros((2 * N_HEADS, t), F32) + bifc_ref[...]
        for h in range(N_HEADS):
            cols = slice(h * hd, (h + 1) * hd)
            xh = xch[:, cols].astype(BF16)
            qh = jnp.dot(xh, wq_ref[h], preferred_element_type=F32)
            kh = jnp.dot(xh, wk_ref[h], preferred_element_type=F32) * (hd ** -0.5)
            vh = jnp.dot(x[:, cols].astype(BF16), wv_ref[h], preferred_element_type=F32)
            q_scr[g, :, cols] = qh
            k_scr[g, :, cols] = kh
            v_scr[g, :, cols] = vh
            for part, val in enumerate((qh, kh, vh)):
                off = (3 * h + part) * hd
                vb = val.astype(BF16)
                gcol = gcol + jnp.dot(vb, wif_ref[off:off + hd, :], preferred_element_type=F32)
                grow = grow + lax.dot_general(wift_ref[:, off:off + hd], vb, (((1,), (1,)), ((), ())),
                                              preferred_element_type=F32)

        lane = lax.broadcasted_iota(jnp.int32, gcol.shape, 1)
        gcol = jnp.where(lane < N_HEADS, gcol, _log_sigmoid(gcol))
        sub = lax.broadcasted_iota(jnp.int32, grow.shape, 0)
        grow = jnp.where(sub < N_HEADS, grow, _log_sigmoid(grow))
        if n_valid < chunk:
            pos_c = lax.broadcasted_iota(jnp.int32, gcol.shape, 0) % chunk
            gcol = jnp.where(pos_c >= n_valid, jnp.where(lane < N_HEADS, NEG_INF, 0.0), gcol)
            pos_r = lax.broadcasted_iota(jnp.int32, grow.shape, 1) % chunk
            grow = jnp.where(pos_r >= n_valid, jnp.where(sub < N_HEADS, NEG_INF, 0.0), grow)
        gc_scr[g] = gcol
        lf_col = jnp.where(lane >= N_HEADS, gcol, 0.0)
        lf_row = jnp.where(sub >= N_HEADS, grow, 0.0)
        for c in range(n_chunks):
            span = slice(c * chunk, (c + 1) * chunk)
            gr_scr[g, c] = grow[:, span]
            bc_scr[g, span, :] = sum(jnp.dot(tril, part, preferred_element_type=F32)
                                     for part in _bf16_terms(lf_col[span, :]))
            br_scr[g, c] = sum(jnp.dot(part, triu, preferred_element_type=F32)
                               for part in _bf16_terms(lf_row[:, span]))

    def scan_chunk(c, carry):
        r0 = pl.multiple_of(c * chunk, chunk)
        rows = pl.ds(r0, chunk)
        chains = [(g, h) for g in range(group) for h in range(N_HEADS)]
        gates = {g: (gc_scr[g, rows, :], gr_scr[g, c], bc_scr[g, rows, :], br_scr[g, c]) for g in range(group)}
        st = []
        for g, h in chains:
            gc, gr, b_c, b_r = gates[g]
            bc = b_c[:, N_HEADS + h:N_HEADS + h + 1]
            br = b_r[N_HEADS + h:N_HEADS + h + 1, :]
            ic = gc[:, h:h + 1]
            ir = gr[h:h + 1, :]
            m = m_scr[g, h]
            dm = jnp.where(causal, bc - br + ir, NEG_INF)
            st.append(dict(bc=bc, br=br, ic=ic, ir=ir, m=m, dm=dm, inter=bc + m,
                           cols=slice(h * hd, (h + 1) * hd)))
        for s in st:
            s["m_t"] = jnp.maximum(s["inter"], jnp.max(s["dm"], axis=-1, keepdims=True))
        for s in st:
            s["w_d"] = jnp.exp(s["dm"] - s["m_t"])
            s["w_in"] = jnp.exp(s["inter"] - s["m_t"])
        for (g, h), s in zip(chains, st):
            s["q"] = q_scr[g, rows, s["cols"]]
            s["k"] = k_scr[g, rows, s["cols"]]
            s["v"] = v_scr[g, rows, s["cols"]]
            s["qb"] = s["q"].astype(BF16)
            s["kb"] = s["k"].astype(BF16)
            s["sqk"] = lax.dot_general(s["qb"], s["kb"], (((1,), (1,)), ((), ())),
                                       preferred_element_type=F32) * s["w_d"]
        for (g, h), s in zip(chains, st):
            s["cmat"] = c_scr[g, h]
            s["nvec"] = n_scr[g, h]
            s["num"] = (s["w_in"] * jnp.dot(s["qb"], s["cmat"].astype(BF16), preferred_element_type=F32)
                        + jnp.dot(s["sqk"].astype(BF16), s["v"].astype(BF16), preferred_element_type=F32))
        for s in st:
            s["den"] = (s["w_in"] * jnp.sum(s["q"] * s["nvec"], axis=-1, keepdims=True)
                        + jnp.sum(s["sqk"], axis=-1, keepdims=True))
        for (g, h), s in zip(chains, st):
            h_scr[g, rows, s["cols"]] = s["num"] / jnp.maximum(jnp.abs(s["den"]), jnp.exp(-s["m_t"]))
        for (g, h), s in zip(chains, st):
            b_l = s["bc"][chunk - 1:chunk, :]
            g_r = b_l - s["br"] + s["ir"]
            m_new = jnp.maximum(b_l + s["m"], jnp.max(g_r, axis=-1, keepdims=True))
            decay = jnp.exp(b_l + s["m"] - m_new)
            w_g = jnp.exp(b_l - s["bc"] + s["ic"] - m_new)
            c_scr[g, h] = decay * s["cmat"] + lax.dot_general(
                s["kb"], (w_g * s["v"]).astype(BF16), (((0,), (0,)), ((), ())), preferred_element_type=F32)
            n_scr[g, h] = decay * s["nvec"] + jnp.sum(w_g * s["k"], axis=0, keepdims=True)
            m_scr[g, h] = m_new
        return carry

    if n_chunks == 1:
        scan_chunk(0, 0)
    else:
        lax.fori_loop(0, n_chunks, scan_chunk, 0)

    for g in range(group):
        for h in range(N_HEADS):
            cols = slice(h * hd, (h + 1) * hd)
            hb = h_scr[g, :, cols] * ob_ref[g, :, cols]
            hb = _rms(hb) * mg_ref[:, cols] + skip_ref[:, cols] * xch_scr[g, :, cols]
            bo_ref[g, :, cols] = (hb * zb_ref[g, :, cols]).astype(BF16)

    @pl.when(step == pl.num_programs(1) - 1)
    def _state_out():
        c1_ref[...] = c_scr[...]
        n1_ref[...] = n_scr[...]
        m1_ref[...] = m_scr[...]


def _mlstm(rest, conv0, c0, n0, m0, w, t, chunk, n_valid, group):
    bn, s, _ = rest.shape
    hd = HEAD_DIM
    tok = lambda col: pl.BlockSpec((group, t, GROUP_W), lambda b, j, col=col: (b, j, col))
    per_b = lambda *shape: pl.BlockSpec((group,) + shape, lambda b, j: (b,) + (0,) * len(shape))
    full = lambda *shape: pl.BlockSpec(shape, lambda b, j: (0,) * len(shape))
    state_shapes = [(N_HEADS, hd, hd), (N_HEADS, 1, hd), (N_HEADS, 1, 1)]
    state_specs = [per_b(*shape) for shape in state_shapes]
    return pl.pallas_call(
        functools.partial(_mlstm_kernel, t=t, chunk=chunk, n_valid=n_valid, group=group),
        grid=(bn // group, s // t),
        in_specs=[tok(1), tok(2), tok(3), per_b(CONV_W - 1, GROUP_W)] + state_specs + [
            full(CONV_W, GROUP_W), full(1, GROUP_W),
            full(N_HEADS, hd, hd), full(N_HEADS, hd, hd), full(N_HEADS, hd, hd),
            full(3 * GROUP_W, 2 * N_HEADS), full(2 * N_HEADS, 3 * GROUP_W),
            full(1, 2 * N_HEADS), full(2 * N_HEADS, 1),
            full(1, GROUP_W), full(1, GROUP_W)],
        out_specs=[tok(0)] + state_specs,
        out_shape=[jax.ShapeDtypeStruct((bn, s, GROUP_W), BF16)]
                  + [jax.ShapeDtypeStruct((bn,) + shape, F32) for shape in state_shapes],
        scratch_shapes=[pltpu.VMEM((group, t + HALO, GROUP_W), F32)]
                       + [pltpu.VMEM((group, t, GROUP_W), F32)] * 5
                       + [pltpu.VMEM((group, t, 2 * N_HEADS), F32),
                          pltpu.VMEM((group, t // chunk, 2 * N_HEADS, chunk), F32)] * 2
                       + [pltpu.VMEM((group,) + shape, F32) for shape in state_shapes],
        compiler_params=_cparams(("parallel", "arbitrary")),
        name="mlstm",
    )(rest, rest, rest, conv0, c0, n0, m0,
      w["conv_w"], w["conv_b"], w["wq"], w["wk"], w["wv"], w["wif"], w["wif_t"], w["bif_row"], w["bif_col"],
      w["mlstm_g"], w["mlstm_skip"])


def _outproj_kernel(a_ref, bo_ref, x_ref, p_ref, woa_ref, wob_ref, pg_ref, wgate_ref, wple_ref, plg_ref, y_ref):
    tm = x_ref.shape[0]
    n_slabs = OUTPROJ_SLABS if tm % (F32_SUBLANES * OUTPROJ_SLABS) == 0 else 1
    slabs = [slice(c * tm // n_slabs, (c + 1) * tm // n_slabs) for c in range(n_slabs)]
    y = [jnp.dot(a_ref[r, :], woa_ref[...], preferred_element_type=F32)
         + jnp.dot(bo_ref[r, :], wob_ref[...], preferred_element_type=F32) for r in slabs]
    pe = [jnp.dot(p_ref[r, :].astype(BF16), wple_ref[...], preferred_element_type=F32) for r in slabs]
    x1 = [x_ref[r, :] + _rms(y_r) * pg_ref[...] for r, y_r in zip(slabs, y)]
    gate = [_sigmoid(jnp.dot(x.astype(BF16), wgate_ref[...], preferred_element_type=F32)) for x in x1]
    for r, x, g, e in zip(slabs, x1, gate, pe):
        y_ref[r, :] = x + _rms(g * e) * plg_ref[...]


def _outproj(a, bo, x, p, w, tm):
    m, d = x.shape
    row = lambda width: pl.BlockSpec((tm, width), lambda i: (i, 0))
    full = lambda arr: pl.BlockSpec(arr.shape, lambda i: (0,) * arr.ndim)
    consts = [w["wo_a"], w["wo_b"], w["post_g"], w["ple_gate"], w["ple_w"], w["ple_g"]]
    return pl.pallas_call(
        _outproj_kernel,
        grid=(m // tm,),
        in_specs=[row(GROUP_W), row(GROUP_W), row(d), row(p.shape[1])] + [full(c) for c in consts],
        out_specs=row(d),
        out_shape=jax.ShapeDtypeStruct((m, d), F32),
        compiler_params=_cparams(("parallel",)),
        name="outproj",
    )(a, bo, x, p, *consts)


def _layer_weights(l, norm_pre_g, norm_post_g, w_in, lam_q1, lam_k1, lam_q2, lam_k2, attn_norm_g, conv_w, conv_b,
                   w_q_b, w_k_b, w_v_b, w_if, b_if, mlstm_norm_g, mlstm_skip, w_out, ple_w, ple_gate_w, ple_norm_g):
    row = lambda v: v.reshape(1, -1)
    wo = w_out[l].astype(BF16)
    return {
        "pre_g": row(norm_pre_g[l]), "w_in": w_in[l].astype(BF16),
        "lam": jnp.stack([lam_q1[l], lam_k1[l], lam_q2[l], lam_k2[l]]).astype(F32),
        "attn_g": row(attn_norm_g[l]),
        "conv_w": conv_w[l], "conv_b": row(conv_b[l]),
        "wq": w_q_b[l].astype(BF16), "wk": w_k_b[l].astype(BF16), "wv": w_v_b[l].astype(BF16),
        "wif": w_if[l].astype(BF16), "wif_t": w_if[l].T.astype(BF16),
        "bif_row": row(b_if[l]), "bif_col": b_if[l].reshape(-1, 1),
        "mlstm_g": row(mlstm_norm_g[l]), "mlstm_skip": row(mlstm_skip[l]),
        "wo_a": wo[:GROUP_W], "wo_b": wo[GROUP_W:], "post_g": row(norm_post_g[l]),
        "ple_gate": ple_gate_w[l].astype(BF16), "ple_w": ple_w[l].astype(BF16), "ple_g": row(ple_norm_g[l]),
    }


def _prompt_layer(x, p, w, lam_init):
    b, s, d = x.shape
    x2 = x.reshape(b * s, d)
    qt, k, v, rest, kb, vt = _inproj(x2, w["pre_g"], w["w_in"], tm=PROMPT_ROWS, kv_block=FLASH_BLOCK)
    rest3 = rest.reshape(b, s, 4 * GROUP_W)
    a = _flash(qt, kb.reshape(b, s, GROUP_W), vt, rest3, w["lam"], w["attn_g"], lam_init, tq=FLASH_BLOCK)
    hd = HEAD_DIM
    bo, c1, n1, m1 = _mlstm(rest3, jnp.zeros((b, CONV_W - 1, GROUP_W), F32),
                            jnp.zeros((b, N_HEADS, hd, hd), F32), jnp.zeros((b, N_HEADS, 1, hd), F32),
                            jnp.zeros((b, N_HEADS, 1, 1), F32), w, t=PROMPT_ROWS, chunk=PROMPT_CHUNK,
                            n_valid=PROMPT_CHUNK, group=PROMPT_GROUP)
    y = _outproj(a.reshape(b * s, GROUP_W), bo.reshape(b * s, GROUP_W), x2, p.reshape(b * s, -1), w,
                 tm=OUTPROJ_ROWS)
    state = (k.reshape(b, s, N_HEADS, hd), v.reshape(b, s, N_HEADS, hd),
             rest3[:, s - (CONV_W - 1):, GROUP_W:2 * GROUP_W],
             c1, n1.reshape(b, N_HEADS, hd), m1.reshape(b, N_HEADS))
    return y.reshape(b, s, d), state


def _sample_layer(x, p, n_valid, cache_k, cache_v, page_table, conv0, c0, n0, m0, w, lam_init):
    db, r, d = x.shape
    hd = HEAD_DIM
    assert CONV_W - 1 <= n_valid <= r
    x2 = x.reshape(db * r, d)
    q, k, v, rest = _inproj(x2, w["pre_g"], w["w_in"], tm=db * r)
    rest3 = rest.reshape(db, r, 4 * GROUP_W)
    k3 = k.reshape(db, r, GROUP_W)
    v3 = v.reshape(db, r, GROUP_W)
    a = _paged(q.reshape(db, r, GROUP_W), k3, v3, rest3, cache_k, cache_v, page_table, w["lam"], w["attn_g"],
               lam_init, n_valid)
    bo, c1, n1, m1 = _mlstm(rest3, conv0, c0, n0.reshape(db, N_HEADS, 1, hd), m0.reshape(db, N_HEADS, 1, 1),
                            w, t=r, chunk=r, n_valid=n_valid, group=SAMPLE_GROUP)
    y = _outproj(a.reshape(db * r, GROUP_W), bo.reshape(db * r, GROUP_W), x2, p.reshape(db * r, -1), w, tm=db * r)
    state = (k3[:, :n_valid].reshape(db, n_valid, N_HEADS, hd), v3[:, :n_valid].reshape(db, n_valid, N_HEADS, hd),
             rest3[:, n_valid - (CONV_W - 1):n_valid, GROUP_W:2 * GROUP_W],
             c1, n1.reshape(db, N_HEADS, hd), m1.reshape(db, N_HEADS))
    return y.reshape(db, r, d), state


def kernel(x_prompt, x_sample, cache_k, cache_v, state_conv, state_C, state_n, state_m, page_table,
           p_prompt, p_sample, norm_pre_g, norm_post_g, w_in, lam_q1, lam_k1, lam_q2, lam_k2,
           attn_norm_g, conv_w, conv_b, w_q_b, w_k_b, w_v_b, w_if, b_if, mlstm_norm_g, mlstm_skip,
           w_out, ple_w, ple_gate_w, ple_norm_g):
    depth = w_in.shape[0]
    t_dec = x_sample.shape[1]
    pad_rows = lambda a: jnp.pad(a, ((0, 0), (0, SAMPLE_ROWS - t_dec), (0, 0)))
    n_pool, page = cache_k.shape[1], cache_k.shape[2]
    cache_k2 = cache_k.reshape(depth * n_pool, page * N_HEADS, HEAD_DIM)
    cache_v2 = cache_v.reshape(depth * n_pool, page * N_HEADS, HEAD_DIM)
    xp = x_prompt
    xs = pad_rows(x_sample)
    outs_p, outs_s = [], []
    for l in range(depth):
        w = _layer_weights(l, norm_pre_g, norm_post_g, w_in, lam_q1, lam_k1, lam_q2, lam_k2, attn_norm_g,
                           conv_w, conv_b, w_q_b, w_k_b, w_v_b, w_if, b_if, mlstm_norm_g, mlstm_skip,
                           w_out, ple_w, ple_gate_w, ple_norm_g)
        lam_init = 0.8 - 0.6 * math.exp(-0.3 * l)
        xp, st_p = _prompt_layer(xp, p_prompt[l], w, lam_init)
        outs_p.append(st_p)
        xs, st_s = _sample_layer(xs, pad_rows(p_sample[l]), t_dec, cache_k2, cache_v2, page_table + l * n_pool,
                                 state_conv[l], state_C[l], state_n[l], state_m[l], w, lam_init)
        outs_s.append(st_s)
    stack = lambda outs, i: jnp.stack([o[i] for o in outs])
    return ((xp, xs[:, :t_dec]) + tuple(stack(outs_p, i) for i in range(6))
            + tuple(stack(outs_s, i) for i in range(6)))
```

```python
import functools
import math

import jax
import jax.numpy as jnp
from jax import lax
from jax.experimental import pallas as pl
from jax.experimental.pallas import tpu as pltpu

F32 = jnp.float32
BF16 = jnp.bfloat16
EPS = 1e-6
NEG_INF = float("-inf")

N_HEADS = 4
HEAD_DIM = 128
QK_DIM = 64
GROUP_W = N_HEADS * HEAD_DIM
CONV_W = 4
F32_SUBLANES = 8
SAMPLE_ROWS = F32_SUBLANES
HALO = F32_SUBLANES
PROMPT_CHUNK = 128
PROMPT_ROWS = 512
OUTPROJ_ROWS = 1024
FLASH_BLOCK = 512
FLASH_UNROLL = 4
OUTPROJ_SLABS = 4
PROMPT_GROUP = 2
SAMPLE_GROUP = 4
BF16_SUBLANES = 16
VT_ROWS = HEAD_DIM + BF16_SUBLANES
Q_SCALE = (QK_DIM ** -0.5) * math.log2(math.e)
PAGES_PER_STEP = 32
V7X_VMEM_LIMIT = 48 * 1024 * 1024


def _cparams(sem):
    return pltpu.CompilerParams(dimension_semantics=sem, vmem_limit_bytes=V7X_VMEM_LIMIT)


def _sigmoid(x):
    return 0.5 * jnp.tanh(0.5 * x) + 0.5


def _silu(x):
    return x * _sigmoid(x)


def _rms(x):
    return x * lax.rsqrt(jnp.mean(x * x, axis=-1, keepdims=True) + EPS)


def _inproj_kernel(x_ref, g_ref, w_ref, q_ref, k_ref, v_ref, rest_ref, *maybe_flash_refs, kv_block):
    hb = (_rms(x_ref[...]) * g_ref[...]).astype(BF16)

    def proj(c):
        return jnp.dot(hb, w_ref[:, c * GROUP_W:(c + 1) * GROUP_W], preferred_element_type=F32)

    q = proj(0) * Q_SCALE
    k = proj(1)
    v = proj(2)
    if not kv_block:
        q_ref[...] = q.astype(BF16)
        k_ref[...] = k
        v_ref[...] = v
    else:
        for h in range(N_HEADS):
            for c in range(q.shape[0] // kv_block):
                blk = q[c * kv_block:(c + 1) * kv_block, h * HEAD_DIM:(h + 1) * HEAD_DIM]
                q_ref[h, c] = _split_maps(blk).T.astype(BF16)
        for h in range(N_HEADS):
            cols = slice(h * HEAD_DIM, (h + 1) * HEAD_DIM)
            k_ref[pl.ds(h, k.shape[0], stride=N_HEADS), :] = k[:, cols]
            v_ref[pl.ds(h, v.shape[0], stride=N_HEADS), :] = v[:, cols]
        kb_ref, vt_ref = maybe_flash_refs
        kb_ref[...] = k.astype(BF16)
        for h in range(N_HEADS):
            for c in range(v.shape[0] // kv_block):
                blk = v[c * kv_block:(c + 1) * kv_block, h * HEAD_DIM:(h + 1) * HEAD_DIM]
                vt_ref[h, c, 0:HEAD_DIM, :] = blk.T.astype(BF16)
                vt_ref[h, c, HEAD_DIM:VT_ROWS, :] = jnp.ones((VT_ROWS - HEAD_DIM, kv_block), BF16)
    for c, act in zip(range(3, 7), (_silu, None, _sigmoid, _silu)):
        u = proj(c)
        rest_ref[:, (c - 3) * GROUP_W:(c - 2) * GROUP_W] = u if act is None else act(u)


def _inproj(x, g, w_bf, tm, kv_block=0):
    m, d = x.shape
    n_in = w_bf.shape[1]
    row = lambda w: pl.BlockSpec((tm, w), lambda i: (i, 0))
    out_specs = [row(GROUP_W)] * 3 + [row(4 * GROUP_W)]
    out_shape = [jax.ShapeDtypeStruct((m, GROUP_W), BF16),
                 jax.ShapeDtypeStruct((m, GROUP_W), F32),
                 jax.ShapeDtypeStruct((m, GROUP_W), F32),
                 jax.ShapeDtypeStruct((m, 4 * GROUP_W), F32)]
    if kv_block:
        out_specs[0] = pl.BlockSpec((N_HEADS, tm // kv_block, HEAD_DIM, 2 * kv_block), lambda i: (0, i, 0, 0))
        out_shape[0] = jax.ShapeDtypeStruct((N_HEADS, m // kv_block, HEAD_DIM, 2 * kv_block), BF16)
        for i_out in (1, 2):
            out_specs[i_out] = pl.BlockSpec((tm * N_HEADS, HEAD_DIM), lambda i: (i, 0))
            out_shape[i_out] = jax.ShapeDtypeStruct((m * N_HEADS, HEAD_DIM), F32)
        out_specs += [row(GROUP_W),
                      pl.BlockSpec((N_HEADS, tm // kv_block, VT_ROWS, kv_block), lambda i: (0, i, 0, 0))]
        out_shape += [jax.ShapeDtypeStruct((m, GROUP_W), BF16),
                      jax.ShapeDtypeStruct((N_HEADS, m // kv_block, VT_ROWS, kv_block), BF16)]
    return pl.pallas_call(
        functools.partial(_inproj_kernel, kv_block=kv_block),
        grid=(m // tm,),
        in_specs=[row(d), pl.BlockSpec((1, d), lambda i: (0, 0)),
                  pl.BlockSpec((d, n_in), lambda i: (0, 0))],
        out_specs=out_specs,
        out_shape=out_shape,
        compiler_params=_cparams(("parallel",)),
        name="inproj",
    )(x, g, w_bf)


def _split_maps(q):
    lane = lax.broadcasted_iota(jnp.int32, q.shape, 1)
    zero = jnp.zeros_like(q)
    return jnp.concatenate([jnp.where(lane < QK_DIM, q, zero), jnp.where(lane >= QK_DIM, q, zero)], axis=0)


def _lambda(lam_ref, lam_init):
    l1 = jnp.sum(lam_ref[0:1, :] * lam_ref[1:2, :], axis=-1, keepdims=True)
    l2 = jnp.sum(lam_ref[2:3, :] * lam_ref[3:4, :], axis=-1, keepdims=True)
    return jnp.exp(l1) - jnp.exp(l2) + lam_init


def _flash_kernel(qt_ref, k_ref, vt_ref, za_ref, lam_ref, g_ref, o_ref, m_scr, acc_scr,
                  s0_scr, s1_scr, *, tq, lam_init):
    nblk = qt_ref.shape[0]
    lam = _lambda(lam_ref, lam_init)

    def scores(i, j, dst):
        k0 = pl.multiple_of(j * tq, tq)
        dst[...] = jnp.dot(k_ref[pl.ds(k0, tq), :], qt_ref[i], preferred_element_type=F32)

    def consume(src, j, masked):
        s = src[...]
        if masked:
            key = lax.broadcasted_iota(jnp.int32, s.shape, 0)
            qry = lax.broadcasted_iota(jnp.int32, s.shape, 1)
            s = jnp.where(key <= jnp.where(qry >= tq, qry - tq, qry), s, NEG_INF)
        m_prev = m_scr[...]
        m_new = jnp.maximum(m_prev, jnp.max(s, axis=0, keepdims=True))
        p = jnp.exp2(s - m_new)
        alpha = jnp.exp2(m_prev - m_new)
        acc_scr[...] = alpha * acc_scr[...] + jnp.dot(vt_ref[j], p.astype(BF16), preferred_element_type=F32)
        m_scr[...] = m_new

    bufs = (s0_scr, s1_scr)

    def run(i, first, count, masked_last):
        for u in range(count):
            last = u == count - 1
            if not (last and masked_last):
                scores(i, first + u + 1, bufs[(u + 1) % 2])
            consume(bufs[u % 2], first + u, last and masked_last)

    def finalize(i):
        rows = pl.ds(pl.multiple_of(i * tq, tq), tq)
        o_t = acc_scr[0:HEAD_DIM, :] * (1.0 / acc_scr[HEAD_DIM:HEAD_DIM + 1, :])
        o_t = o_t[:, 0:tq] - lam * o_t[:, tq:2 * tq]
        o_t = o_t * lax.rsqrt(jnp.mean(o_t * o_t, axis=0, keepdims=True) + EPS)
        o = o_t.T * g_ref[...] * (1.0 - lam_init)
        o_ref[rows, :] = (o * za_ref[rows, :]).astype(BF16)

    def query_block(i, carry):
        m_scr[...] = jnp.full(m_scr.shape, NEG_INF, F32)
        acc_scr[...] = jnp.zeros(acc_scr.shape, F32)

        def full_group(jj, c):
            run(i, FLASH_UNROLL * jj, FLASH_UNROLL, False)
            return c

        lax.fori_loop(0, i // FLASH_UNROLL, full_group, 0)
        for rem in range(FLASH_UNROLL):
            @pl.when(i % FLASH_UNROLL == rem)
            def _tail(rem=rem):
                run(i, i - rem, rem + 1, True)
        scores(jnp.minimum(i + 1, nblk - 1), 0, s0_scr)
        finalize(i)
        return carry

    scores(0, 0, s0_scr)
    lax.fori_loop(0, nblk, query_block, 0)


def _flash(qt, kb, vt, rest, lamp, attn_g, lam_init, tq):
    b, s, _ = kb.shape
    nblk = s // tq
    seq = lambda: pl.BlockSpec((None, s, HEAD_DIM), lambda bi, h: (bi, 0, h))
    return pl.pallas_call(
        functools.partial(_flash_kernel, tq=tq, lam_init=lam_init),
        grid=(b, N_HEADS),
        in_specs=[pl.BlockSpec((None, nblk, HEAD_DIM, 2 * tq), lambda bi, h: (h, bi, 0, 0)),
                  seq(),
                  pl.BlockSpec((None, nblk, VT_ROWS, tq), lambda bi, h: (h, bi, 0, 0)),
                  seq(),
                  pl.BlockSpec((4, QK_DIM), lambda bi, h: (0, 0)),
                  pl.BlockSpec((1, HEAD_DIM), lambda bi, h: (0, h))],
        out_specs=seq(),
        out_shape=jax.ShapeDtypeStruct((b, s, GROUP_W), BF16),
        scratch_shapes=[pltpu.VMEM((1, 2 * tq), F32),
                        pltpu.VMEM((VT_ROWS, 2 * tq), F32),
                        pltpu.VMEM((tq, 2 * tq), F32),
                        pltpu.VMEM((tq, 2 * tq), F32)],
        compiler_params=_cparams(("parallel", "parallel")),
        name="flash_prompt",
    )(qt, kb, vt, rest, lamp, attn_g)


def _paged_kernel(pt_ref, q_ref, *refs, n_pages, n_valid, lam_init):
    del pt_ref
    kp = refs[:n_pages]
    vp = refs[n_pages:2 * n_pages]
    kn_ref, vn_ref, za_ref, lam_ref, g_ref, o_ref, qq_scr, bias_scr, m_scr, l_scr, acc_scr = refs[2 * n_pages:]
    r = SAMPLE_ROWS
    half = r // 2
    pw = bias_scr.shape[1]
    step = pl.program_id(1)

    @pl.when(step == 0)
    def _init():
        row = lax.broadcasted_iota(jnp.int32, (r, HEAD_DIM), 0)
        lane = lax.broadcasted_iota(jnp.int32, (r, HEAD_DIM), 1)
        for h in range(N_HEADS):
            q = q_ref[:, h * HEAD_DIM:(h + 1) * HEAD_DIM].astype(F32)
            q_map1 = jnp.where(lane < QK_DIM, q, 0.0)
            q_map2 = pltpu.roll(jnp.where(lane >= QK_DIM, q, 0.0), half, axis=0)
            qq_scr[h * r:(h + 1) * r, :] = jnp.where(row < half, q_map1, q_map2)
        row_head = lax.broadcasted_iota(jnp.int32, bias_scr.shape, 0) // r
        col_head = lax.broadcasted_iota(jnp.int32, bias_scr.shape, 1) % N_HEADS
        bias_scr[...] = jnp.where(row_head == col_head, 0.0, NEG_INF).astype(F32)
        m_scr[...] = jnp.full(m_scr.shape, NEG_INF, F32)
        l_scr[...] = jnp.zeros(l_scr.shape, F32)
        acc_scr[...] = jnp.zeros(acc_scr.shape, F32)

    qq = qq_scr[...].astype(BF16)
    bias = bias_scr[...]
    s = jnp.concatenate(
        [lax.dot_general(qq, kp[j][...].astype(BF16), (((1,), (1,)), ((), ())),
                         preferred_element_type=F32) + bias for j in range(n_pages)], axis=1)
    m_prev = m_scr[...]
    m_new = jnp.maximum(m_prev, jnp.max(s, axis=-1, keepdims=True))
    p = jnp.exp2(s - m_new)
    alpha = jnp.exp2(m_prev - m_new)
    l_scr[...] = alpha * l_scr[...] + jnp.sum(p, axis=-1, keepdims=True)
    p = p.astype(BF16)
    pv = jnp.dot(p[:, 0:pw], vp[0][...].astype(BF16), preferred_element_type=F32)
    for j in range(1, n_pages):
        pv = pv + jnp.dot(p[:, j * pw:(j + 1) * pw], vp[j][...].astype(BF16), preferred_element_type=F32)
    acc_scr[...] = alpha * acc_scr[...] + pv
    m_scr[...] = m_new

    @pl.when(step == pl.num_programs(1) - 1)
    def _finish():
        lam = _lambda(lam_ref, lam_init)
        row = lax.broadcasted_iota(jnp.int32, (r, 1), 0)
        tok = jnp.where(row >= half, row - half, row)
        for h in range(N_HEADS):
            cols = slice(h * HEAD_DIM, (h + 1) * HEAD_DIM)
            rows = slice(h * r, (h + 1) * r)
            qf = qq_scr[rows, :]
            kn = kn_ref[:, cols]
            vn = vn_ref[:, cols]
            s_new = [jnp.where(tok >= j, jnp.sum(qf * kn[j:j + 1, :], axis=-1, keepdims=True), NEG_INF)
                     for j in range(n_valid)]
            m_prev = m_scr[rows, :]
            m_new = m_prev
            for sj in s_new:
                m_new = jnp.maximum(m_new, sj)
            alpha = jnp.exp2(m_prev - m_new)
            l = alpha * l_scr[rows, :]
            acc = alpha * acc_scr[rows, :]
            for j, sj in enumerate(s_new):
                pj = jnp.exp2(sj - m_new)
                l = l + pj
                acc = acc + pj * vn[j:j + 1, :]
            o_maps = acc / l
            o = o_maps - lam * pltpu.roll(o_maps, half, axis=0)
            o = _rms(o) * g_ref[:, cols] * (1.0 - lam_init)
            o_ref[:, cols] = (o * za_ref[:, cols]).astype(BF16)


def _paged(q, k_new, v_new, rest, cache_k, cache_v, page_table, lamp, attn_g, lam_init, n_valid):
    db = q.shape[0]
    r = SAMPLE_ROWS
    assert 2 * n_valid <= r, "each head packs the tokens of both maps into SAMPLE_ROWS query rows"
    n_pages_total = page_table.shape[1]
    page_rows = cache_k.shape[1]
    p = PAGES_PER_STEP
    seq = lambda w=GROUP_W: pl.BlockSpec((None, r, w), lambda b, s, pt: (b, 0, 0))

    def page_spec(j):
        return pl.BlockSpec((None, page_rows, HEAD_DIM), lambda b, s, pt, j=j: (pt[b, s * p + j], 0, 0))

    grid_spec = pltpu.PrefetchScalarGridSpec(
        num_scalar_prefetch=1,
        grid=(db, n_pages_total // p),
        in_specs=[seq()] + [page_spec(j) for j in range(p)] * 2 + [seq(), seq(), seq(),
                  pl.BlockSpec((4, QK_DIM), lambda b, s, pt: (0, 0)),
                  pl.BlockSpec((1, GROUP_W), lambda b, s, pt: (0, 0))],
        out_specs=seq(),
        scratch_shapes=[pltpu.VMEM((N_HEADS * r, HEAD_DIM), F32),
                        pltpu.VMEM((N_HEADS * r, page_rows), F32),
                        pltpu.VMEM((N_HEADS * r, 1), F32),
                        pltpu.VMEM((N_HEADS * r, 1), F32),
                        pltpu.VMEM((N_HEADS * r, HEAD_DIM), F32)])
    return pl.pallas_call(
        functools.partial(_paged_kernel, n_pages=p, n_valid=n_valid, lam_init=lam_init),
        grid_spec=grid_spec,
        out_shape=jax.ShapeDtypeStruct((db, r, GROUP_W), BF16),
        compiler_params=_cparams(("parallel", "arbitrary")),
        name="paged_attn",
    )(page_table, q, *([cache_k] * p), *([cache_v] * p), k_new, v_new, rest, lamp, attn_g)


def _log_sigmoid(x):
    return jnp.minimum(x, 0.0) - jnp.log1p(jnp.exp(-jnp.abs(x)))


def _bf16_terms(x):
    hi = x.astype(BF16)
    rest = x - hi.astype(F32)
    mid = rest.astype(BF16)
    lo = (rest - mid.astype(F32)).astype(BF16)
    return hi, mid, lo


def _mlstm_kernel(xb_ref, ob_ref, zb_ref, conv0_ref, c0_ref, n0_ref, m0_ref,
                  cw_ref, cb_ref, wq_ref, wk_ref, wv_ref, wif_ref, wift_ref, bifr_ref, bifc_ref,
                  mg_ref, skip_ref,
                  bo_ref, c1_ref, n1_ref, m1_ref,
                  xpad, xch_scr, q_scr, k_scr, v_scr, h_scr, gc_scr, gr_scr, bc_scr, br_scr, c_scr, n_scr, m_scr,
                  *, t, chunk, n_valid, group):
    step = pl.program_id(1)
    n_chunks = t // chunk
    hd = HEAD_DIM
    ri = lax.broadcasted_iota(jnp.int32, (chunk, chunk), 0)
    ci = lax.broadcasted_iota(jnp.int32, (chunk, chunk), 1)
    causal = ci <= ri
    tril = jnp.where(causal, 1.0, 0.0).astype(BF16)
    triu = jnp.where(ri <= ci, 1.0, 0.0).astype(BF16)

    @pl.when(step == 0)
    def _init():
        for g in range(group):
            xpad[g, 0:HALO, :] = jnp.zeros((HALO, GROUP_W), F32)
            xpad[g, HALO - (CONV_W - 1):HALO, :] = conv0_ref[g]
        c_scr[...] = c0_ref[...]
        n_scr[...] = n0_ref[...]
        m_scr[...] = m0_ref[...]

    for g in range(group):
        x = xb_ref[g]
        xpad[g, HALO:HALO + t, :] = x
        xc = cb_ref[...] + cw_ref[CONV_W - 1:CONV_W, :] * x
        for j in range(CONV_W - 1):
            first = HALO - (CONV_W - 1) + j
            xc = xc + cw_ref[j:j + 1, :] * xpad[g, first:first + t, :]
        tail = xpad[g, t:t + HALO, :]
        xpad[g, 0:HALO, :] = tail
        xch = _silu(xc)
        xch_scr[g] = xch

        gcol = jnp.zeros((t, 2 * N_HEADS), F32) + bifr_ref[...]
        grow = jnp.zeros((2 * N_HEADS, t), F32) + bifc_ref[...]
        for h in range(N_HEADS):
            cols = slice(h * hd, (h + 1) * hd)
            xh = xch[:, cols].astype(BF16)
            qh = jnp.dot(xh, wq_ref[h], preferred_element_type=F32)
            kh = jnp.dot(xh, wk_ref[h], preferred_element_type=F32) * (hd ** -0.5)
            vh = jnp.dot(x[:, cols].astype(BF16), wv_ref[h], preferred_element_type=F32)
            q_scr[g, :, cols] = qh
            k_scr[g, :, cols] = kh
            v_scr[g, :, cols] = vh
            for part, val in enumerate((qh, kh, vh)):
                off = (3 * h + part) * hd
                vb = val.astype(BF16)
                gcol = gcol + jnp.dot(vb, wif_ref[off:off + hd, :], preferred_element_type=F32)
                grow = grow + lax.dot_general(wift_ref[:, off:off + hd], vb, (((1,), (1,)), ((), ())),
                                              preferred_element_type=F32)

        lane = lax.broadcasted_iota(jnp.int32, gcol.shape, 1)
        gcol = jnp.where(lane < N_HEADS, gcol, _log_sigmoid(gcol))
        sub = lax.broadcasted_iota(jnp.int32, grow.shape, 0)
        grow = jnp.where(sub < N_HEADS, grow, _log_sigmoid(grow))
        if n_valid < chunk:
            pos_c = lax.broadcasted_iota(jnp.int32, gcol.shape, 0) % chunk
            gcol = jnp.where(pos_c >= n_valid, jnp.where(lane < N_HEADS, NEG_INF, 0.0), gcol)
            pos_r = lax.broadcasted_iota(jnp.int32, grow.shape, 1) % chunk
            grow = jnp.where(pos_r >= n_valid, jnp.where(sub < N_HEADS, NEG_INF, 0.0), grow)
        gc_scr[g] = gcol
        lf_col = jnp.where(lane >= N_HEADS, gcol, 0.0)
        lf_row = jnp.where(sub >= N_HEADS, grow, 0.0)
        for c in range(n_chunks):
            span = slice(c * chunk, (c + 1) * chunk)
            gr_scr[g, c] = grow[:, span]
            bc_scr[g, span, :] = sum(jnp.dot(tril, part, preferred_element_type=F32)
                                     for part in _bf16_terms(lf_col[span, :]))
            br_scr[g, c] = sum(jnp.dot(part, triu, preferred_element_type=F32)
                               for part in _bf16_terms(lf_row[:, span]))

    def scan_chunk(c, carry):
        r0 = pl.multiple_of(c * chunk, chunk)
        rows = pl.ds(r0, chunk)
        chains = [(g, h) for g in range(group) for h in range(N_HEADS)]
        gates = {g: (gc_scr[g, rows, :], gr_scr[g, c], bc_scr[g, rows, :], br_scr[g, c]) for g in range(group)}
        st = []
        for g, h in chains:
            gc, gr, b_c, b_r = gates[g]
            bc = b_c[:, N_HEADS + h:N_HEADS + h + 1]
            br = b_r[N_HEADS + h:N_HEADS + h + 1, :]
            ic = gc[:, h:h + 1]
            ir = gr[h:h + 1, :]
            m = m_scr[g, h]
            dm = jnp.where(causal, bc - br + ir, NEG_INF)
            st.append(dict(bc=bc, br=br, ic=ic, ir=ir, m=m, dm=dm, inter=bc + m,
                           cols=slice(h * hd, (h + 1) * hd)))
        for s in st:
            s["m_t"] = jnp.maximum(s["inter"], jnp.max(s["dm"], axis=-1, keepdims=True))
        for s in st:
            s["w_d"] = jnp.exp(s["dm"] - s["m_t"])
            s["w_in"] = jnp.exp(s["inter"] - s["m_t"])
        for (g, h), s in zip(chains, st):
            s["q"] = q_scr[g, rows, s["cols"]]
            s["k"] = k_scr[g, rows, s["cols"]]
            s["v"] = v_scr[g, rows, s["cols"]]
            s["qb"] = s["q"].astype(BF16)
            s["kb"] = s["k"].astype(BF16)
            s["sqk"] = lax.dot_general(s["qb"], s["kb"], (((1,), (1,)), ((), ())),
                                       preferred_element_type=F32) * s["w_d"]
        for (g, h), s in zip(chains, st):
            s["cmat"] = c_scr[g, h]
            s["nvec"] = n_scr[g, h]
            s["num"] = (s["w_in"] * jnp.dot(s["qb"], s["cmat"].astype(BF16), preferred_element_type=F32)
                        + jnp.dot(s["sqk"].astype(BF16), s["v"].astype(BF16), preferred_element_type=F32))
        for s in st:
            s["den"] = (s["w_in"] * jnp.sum(s["q"] * s["nvec"], axis=-1, keepdims=True)
                        + jnp.sum(s["sqk"], axis=-1, keepdims=True))
        for (g, h), s in zip(chains, st):
            h_scr[g, rows, s["cols"]] = s["num"] / jnp.maximum(jnp.abs(s["den"]), jnp.exp(-s["m_t"]))
        for (g, h), s in zip(chains, st):
            b_l = s["bc"][chunk - 1:chunk, :]
            g_r = b_l - s["br"] + s["ir"]
            m_new = jnp.maximum(b_l + s["m"], jnp.max(g_r, axis=-1, keepdims=True))
            decay = jnp.exp(b_l + s["m"] - m_new)
            w_g = jnp.exp(b_l - s["bc"] + s["ic"] - m_new)
            c_scr[g, h] = decay * s["cmat"] + lax.dot_general(
                s["kb"], (w_g * s["v"]).astype(BF16), (((0,), (0,)), ((), ())), preferred_element_type=F32)
            n_scr[g, h] = decay * s["nvec"] + jnp.sum(w_g * s["k"], axis=0, keepdims=True)
            m_scr[g, h] = m_new
        return carry

    if n_chunks == 1:
        scan_chunk(0, 0)
    else:
        lax.fori_loop(0, n_chunks, scan_chunk, 0)

    for g in range(group):
        for h in range(N_HEADS):
            cols = slice(h * hd, (h + 1) * hd)
            hb = h_scr[g, :, cols] * ob_ref[g, :, cols]
            hb = _rms(hb) * mg_ref[:, cols] + skip_ref[:, cols] * xch_scr[g, :, cols]
            bo_ref[g, :, cols] = (hb * zb_ref[g, :, cols]).astype(BF16)

    @pl.when(step == pl.num_programs(1) - 1)
    def _state_out():
        c1_ref[...] = c_scr[...]
        n1_ref[...] = n_scr[...]
        m1_ref[...] = m_scr[...]


def _mlstm(rest, conv0, c0, n0, m0, w, t, chunk, n_valid, group):
    bn, s, _ = rest.shape
    hd = HEAD_DIM
    tok = lambda col: pl.BlockSpec((group, t, GROUP_W), lambda b, j, col=col: (b, j, col))
    per_b = lambda *shape: pl.BlockSpec((group,) + shape, lambda b, j: (b,) + (0,) * len(shape))
    full = lambda *shape: pl.BlockSpec(shape, lambda b, j: (0,) * len(shape))
    state_shapes = [(N_HEADS, hd, hd), (N_HEADS, 1, hd), (N_HEADS, 1, 1)]
    state_specs = [per_b(*shape) for shape in state_shapes]
    return pl.pallas_call(
        functools.partial(_mlstm_kernel, t=t, chunk=chunk, n_valid=n_valid, group=group),
        grid=(bn // group, s // t),
        in_specs=[tok(1), tok(2), tok(3), per_b(CONV_W - 1, GROUP_W)] + state_specs + [
            full(CONV_W, GROUP_W), full(1, GROUP_W),
            full(N_HEADS, hd, hd), full(N_HEADS, hd, hd), full(N_HEADS, hd, hd),
            full(3 * GROUP_W, 2 * N_HEADS), full(2 * N_HEADS, 3 * GROUP_W),
            full(1, 2 * N_HEADS), full(2 * N_HEADS, 1),
            full(1, GROUP_W), full(1, GROUP_W)],
        out_specs=[tok(0)] + state_specs,
        out_shape=[jax.ShapeDtypeStruct((bn, s, GROUP_W), BF16)]
                  + [jax.ShapeDtypeStruct((bn,) + shape, F32) for shape in state_shapes],
        scratch_shapes=[pltpu.VMEM((group, t + HALO, GROUP_W), F32)]
                       + [pltpu.VMEM((group, t, GROUP_W), F32)] * 5
                       + [pltpu.VMEM((group, t, 2 * N_HEADS), F32),
                          pltpu.VMEM((group, t // chunk, 2 * N_HEADS, chunk), F32)] * 2
                       + [pltpu.VMEM((group,) + shape, F32) for shape in state_shapes],
        compiler_params=_cparams(("parallel", "arbitrary")),
        name="mlstm",
    )(rest, rest, rest, conv0, c0, n0, m0,
      w["conv_w"], w["conv_b"], w["wq"], w["wk"], w["wv"], w["wif"], w["wif_t"], w["bif_row"], w["bif_col"],
      w["mlstm_g"], w["mlstm_skip"])


def _outproj_kernel(a_ref, bo_ref, x_ref, p_ref, woa_ref, wob_ref, pg_ref, wgate_ref, wple_ref, plg_ref, y_ref):
    tm = x_ref.shape[0]
    n_slabs = OUTPROJ_SLABS if tm % (F32_SUBLANES * OUTPROJ_SLABS) == 0 else 1
    slabs = [slice(c * tm // n_slabs, (c + 1) * tm // n_slabs) for c in range(n_slabs)]
    y = [jnp.dot(a_ref[r, :], woa_ref[...], preferred_element_type=F32)
         + jnp.dot(bo_ref[r, :], wob_ref[...], preferred_element_type=F32) for r in slabs]
    pe = [jnp.dot(p_ref[r, :].astype(BF16), wple_ref[...], preferred_element_type=F32) for r in slabs]
    x1 = [x_ref[r, :] + _rms(y_r) * pg_ref[...] for r, y_r in zip(slabs, y)]
    gate = [_sigmoid(jnp.dot(x.astype(BF16), wgate_ref[...], preferred_element_type=F32)) for x in x1]
    for r, x, g, e in zip(slabs, x1, gate, pe):
        y_ref[r, :] = x + _rms(g * e) * plg_ref[...]


def _outproj(a, bo, x, p, w, tm):
    m, d = x.shape
    row = lambda width: pl.BlockSpec((tm, width), lambda i: (i, 0))
    full = lambda arr: pl.BlockSpec(arr.shape, lambda i: (0,) * arr.ndim)
    consts = [w["wo_a"], w["wo_b"], w["post_g"], w["ple_gate"], w["ple_w"], w["ple_g"]]
    return pl.pallas_call(
        _outproj_kernel,
        grid=(m // tm,),
        in_specs=[row(GROUP_W), row(GROUP_W), row(d), row(p.shape[1])] + [full(c) for c in consts],
        out_specs=row(d),
        out_shape=jax.ShapeDtypeStruct((m, d), F32),
        compiler_params=_cparams(("parallel",)),
        name="outproj",
    )(a, bo, x, p, *consts)


def _layer_weights(l, norm_pre_g, norm_post_g, w_in, lam_q1, lam_k1, lam_q2, lam_k2, attn_norm_g, conv_w, conv_b,
                   w_q_b, w_k_b, w_v_b, w_if, b_if, mlstm_norm_g, mlstm_skip, w_out, ple_w, ple_gate_w, ple_norm_g):
    row = lambda v: v.reshape(1, -1)
    wo = w_out[l].astype(BF16)
    return {
        "pre_g": row(norm_pre_g[l]), "w_in": w_in[l].astype(BF16),
        "lam": jnp.stack([lam_q1[l], lam_k1[l], lam_q2[l], lam_k2[l]]).astype(F32),
        "attn_g": row(attn_norm_g[l]),
        "conv_w": conv_w[l], "conv_b": row(conv_b[l]),
        "wq": w_q_b[l].astype(BF16), "wk": w_k_b[l].astype(BF16), "wv": w_v_b[l].astype(BF16),
        "wif": w_if[l].astype(BF16), "wif_t": w_if[l].T.astype(BF16),
        "bif_row": row(b_if[l]), "bif_col": b_if[l].reshape(-1, 1),
        "mlstm_g": row(mlstm_norm_g[l]), "mlstm_skip": row(mlstm_skip[l]),
        "wo_a": wo[:GROUP_W], "wo_b": wo[GROUP_W:], "post_g": row(norm_post_g[l]),
        "ple_gate": ple_gate_w[l].astype(BF16), "ple_w": ple_w[l].astype(BF16), "ple_g": row(ple_norm_g[l]),
    }


def _prompt_layer(x, p, w, lam_init):
    b, s, d = x.shape
    x2 = x.reshape(b * s, d)
    qt, k, v, rest, kb, vt = _inproj(x2, w["pre_g"], w["w_in"], tm=PROMPT_ROWS, kv_block=FLASH_BLOCK)
    rest3 = rest.reshape(b, s, 4 * GROUP_W)
    a = _flash(qt, kb.reshape(b, s, GROUP_W), vt, rest3, w["lam"], w["attn_g"], lam_init, tq=FLASH_BLOCK)
    hd = HEAD_DIM
    bo, c1, n1, m1 = _mlstm(rest3, jnp.zeros((b, CONV_W - 1, GROUP_W), F32),
                            jnp.zeros((b, N_HEADS, hd, hd), F32), jnp.zeros((b, N_HEADS, 1, hd), F32),
                            jnp.zeros((b, N_HEADS, 1, 1), F32), w, t=PROMPT_ROWS, chunk=PROMPT_CHUNK,
                            n_valid=PROMPT_CHUNK, group=PROMPT_GROUP)
    y = _outproj(a.reshape(b * s, GROUP_W), bo.reshape(b * s, GROUP_W), x2, p.reshape(b * s, -1), w,
                 tm=OUTPROJ_ROWS)
    state = (k.reshape(b, s, N_HEADS, hd), v.reshape(b, s, N_HEADS, hd),
             rest3[:, s - (CONV_W - 1):, GROUP_W:2 * GROUP_W],
             c1, n1.reshape(b, N_HEADS, hd), m1.reshape(b, N_HEADS))
    return y.reshape(b, s, d), state


def _sample_layer(x, p, n_valid, cache_k, cache_v, page_table, conv0, c0, n0, m0, w, lam_init):
    db, r, d = x.shape
    hd = HEAD_DIM
    assert CONV_W - 1 <= n_valid <= r
    x2 = x.reshape(db * r, d)
    q, k, v, rest = _inproj(x2, w["pre_g"], w["w_in"], tm=db * r)
    rest3 = rest.reshape(db, r, 4 * GROUP_W)
    k3 = k.reshape(db, r, GROUP_W)
    v3 = v.reshape(db, r, GROUP_W)
    a = _paged(q.reshape(db, r, GROUP_W), k3, v3, rest3, cache_k, cache_v, page_table, w["lam"], w["attn_g"],
               lam_init, n_valid)
    bo, c1, n1, m1 = _mlstm(rest3, conv0, c0, n0.reshape(db, N_HEADS, 1, hd), m0.reshape(db, N_HEADS, 1, 1),
                            w, t=r, chunk=r, n_valid=n_valid, group=SAMPLE_GROUP)
    y = _outproj(a.reshape(db * r, GROUP_W), bo.reshape(db * r, GROUP_W), x2, p.reshape(db * r, -1), w, tm=db * r)
    state = (k3[:, :n_valid].reshape(db, n_valid, N_HEADS, hd), v3[:, :n_valid].reshape(db, n_valid, N_HEADS, hd),
             rest3[:, n_valid - (CONV_W - 1):n_valid, GROUP_W:2 * GROUP_W],
             c1, n1.reshape(db, N_HEADS, hd), m1.reshape(db, N_HEADS))
    return y.reshape(db, r, d), state


def kernel(x_prompt, x_sample, cache_k, cache_v, state_conv, state_C, state_n, state_m, page_table,
           p_prompt, p_sample, norm_pre_g, norm_post_g, w_in, lam_q1, lam_k1, lam_q2, lam_k2,
           attn_norm_g, conv_w, conv_b, w_q_b, w_k_b, w_v_b, w_if, b_if, mlstm_norm_g, mlstm_skip,
           w_out, ple_w, ple_gate_w, ple_norm_g):
    depth = w_in.shape[0]
    t_dec = x_sample.shape[1]
    pad_rows = lambda a: jnp.pad(a, ((0, 0), (0, SAMPLE_ROWS - t_dec), (0, 0)))
    n_pool, page = cache_k.shape[1], cache_k.shape[2]
    cache_k2 = cache_k.reshape(depth * n_pool, page * N_HEADS, HEAD_DIM)
    cache_v2 = cache_v.reshape(depth * n_pool, page * N_HEADS, HEAD_DIM)
    xp = x_prompt
    xs = pad_rows(x_sample)
    outs_p, outs_s = [], []
    for l in range(depth):
        w = _layer_weights(l, norm_pre_g, norm_post_g, w_in, lam_q1, lam_k1, lam_q2, lam_k2, attn_norm_g,
                           conv_w, conv_b, w_q_b, w_k_b, w_v_b, w_if, b_if, mlstm_norm_g, mlstm_skip,
                           w_out, ple_w, ple_gate_w, ple_norm_g)
        lam_init = 0.8 - 0.6 * math.exp(-0.3 * l)
        xp, st_p = _prompt_layer(xp, p_prompt[l], w, lam_init)
        outs_p.append(st_p)
        xs, st_s = _sample_layer(xs, pad_rows(p_sample[l]), t_dec, cache_k2, cache_v2, page_table + l * n_pool,
                                 state_conv[l], state_C[l], state_n[l], state_m[l], w, lam_init)
        outs_s.append(st_s)
    stack = lambda outs, i: jnp.stack([o[i] for o in outs])
    return ((xp, xs[:, :t_dec]) + tuple(stack(outs_p, i) for i in range(6))
            + tuple(stack(outs_s, i) for i in range(6)))
```

```python
import functools
import math

import jax
import jax.numpy as jnp
from jax import lax
from jax.experimental import pallas as pl
from jax.experimental.pallas import tpu as pltpu

F32 = jnp.float32
BF16 = jnp.bfloat16
EPS = 1e-6
NEG_INF = float("-inf")

N_HEADS = 4
HEAD_DIM = 128
QK_DIM = 64
GROUP_W = N_HEADS * HEAD_DIM
CONV_W = 4
F32_SUBLANES = 8
SAMPLE_ROWS = F32_SUBLANES
HALO = F32_SUBLANES
PROMPT_CHUNK = 128
PROMPT_ROWS = 512
OUTPROJ_ROWS = 1024
FLASH_BLOCK = 512
FLASH_UNROLL = 6
OUTPROJ_SLABS = 4
PROMPT_GROUP = 2
SAMPLE_GROUP = 4
BF16_SUBLANES = 16
VT_ROWS = HEAD_DIM + BF16_SUBLANES
Q_SCALE = (QK_DIM ** -0.5) * math.log2(math.e)
PAGES_PER_STEP = 32
V7X_VMEM_LIMIT = 48 * 1024 * 1024


def _cparams(sem):
    return pltpu.CompilerParams(dimension_semantics=sem, vmem_limit_bytes=V7X_VMEM_LIMIT)


def _sigmoid(x):
    return 0.5 * jnp.tanh(0.5 * x) + 0.5


def _silu(x):
    return x * _sigmoid(x)


def _rms(x):
    return x * lax.rsqrt(jnp.mean(x * x, axis=-1, keepdims=True) + EPS)


def _inproj_kernel(x_ref, g_ref, w_ref, q_ref, k_ref, v_ref, rest_ref, *maybe_flash_refs, kv_block):
    hb = (_rms(x_ref[...]) * g_ref[...]).astype(BF16)

    def proj(c):
        return jnp.dot(hb, w_ref[:, c * GROUP_W:(c + 1) * GROUP_W], preferred_element_type=F32)

    q = proj(0) * Q_SCALE
    k = proj(1)
    v = proj(2)
    if not kv_block:
        q_ref[...] = q.astype(BF16)
        k_ref[...] = k
        v_ref[...] = v
    else:
        for h in range(N_HEADS):
            for c in range(q.shape[0] // kv_block):
                blk = q[c * kv_block:(c + 1) * kv_block, h * HEAD_DIM:(h + 1) * HEAD_DIM]
                q_ref[h, c] = _split_maps(blk).T.astype(BF16)
        for h in range(N_HEADS):
            cols = slice(h * HEAD_DIM, (h + 1) * HEAD_DIM)
            k_ref[pl.ds(h, k.shape[0], stride=N_HEADS), :] = k[:, cols]
            v_ref[pl.ds(h, v.shape[0], stride=N_HEADS), :] = v[:, cols]
        kb_ref, vt_ref = maybe_flash_refs
        kb_ref[...] = k.astype(BF16)
        for h in range(N_HEADS):
            for c in range(v.shape[0] // kv_block):
                blk = v[c * kv_block:(c + 1) * kv_block, h * HEAD_DIM:(h + 1) * HEAD_DIM]
                vt_ref[h, c, 0:HEAD_DIM, :] = blk.T.astype(BF16)
                vt_ref[h, c, HEAD_DIM:VT_ROWS, :] = jnp.ones((VT_ROWS - HEAD_DIM, kv_block), BF16)
    for c, act in zip(range(3, 7), (_silu, None, _sigmoid, _silu)):
        u = proj(c)
        rest_ref[:, (c - 3) * GROUP_W:(c - 2) * GROUP_W] = u if act is None else act(u)


def _inproj(x, g, w_bf, tm, kv_block=0):
    m, d = x.shape
    n_in = w_bf.shape[1]
    row = lambda w: pl.BlockSpec((tm, w), lambda i: (i, 0))
    out_specs = [row(GROUP_W)] * 3 + [row(4 * GROUP_W)]
    out_shape = [jax.ShapeDtypeStruct((m, GROUP_W), BF16),
                 jax.ShapeDtypeStruct((m, GROUP_W), F32),
                 jax.ShapeDtypeStruct((m, GROUP_W), F32),
                 jax.ShapeDtypeStruct((m, 4 * GROUP_W), F32)]
    if kv_block:
        out_specs[0] = pl.BlockSpec((N_HEADS, tm // kv_block, HEAD_DIM, 2 * kv_block), lambda i: (0, i, 0, 0))
        out_shape[0] = jax.ShapeDtypeStruct((N_HEADS, m // kv_block, HEAD_DIM, 2 * kv_block), BF16)
        for i_out in (1, 2):
            out_specs[i_out] = pl.BlockSpec((tm * N_HEADS, HEAD_DIM), lambda i: (i, 0))
            out_shape[i_out] = jax.ShapeDtypeStruct((m * N_HEADS, HEAD_DIM), F32)
        out_specs += [row(GROUP_W),
                      pl.BlockSpec((N_HEADS, tm // kv_block, VT_ROWS, kv_block), lambda i: (0, i, 0, 0))]
        out_shape += [jax.ShapeDtypeStruct((m, GROUP_W), BF16),
                      jax.ShapeDtypeStruct((N_HEADS, m // kv_block, VT_ROWS, kv_block), BF16)]
    return pl.pallas_call(
        functools.partial(_inproj_kernel, kv_block=kv_block),
        grid=(m // tm,),
        in_specs=[row(d), pl.BlockSpec((1, d), lambda i: (0, 0)),
                  pl.BlockSpec((d, n_in), lambda i: (0, 0))],
        out_specs=out_specs,
        out_shape=out_shape,
        compiler_params=_cparams(("parallel",)),
        name="inproj",
    )(x, g, w_bf)


def _split_maps(q):
    lane = lax.broadcasted_iota(jnp.int32, q.shape, 1)
    zero = jnp.zeros_like(q)
    return jnp.concatenate([jnp.where(lane < QK_DIM, q, zero), jnp.where(lane >= QK_DIM, q, zero)], axis=0)


def _lambda(lam_ref, lam_init):
    l1 = jnp.sum(lam_ref[0:1, :] * lam_ref[1:2, :], axis=-1, keepdims=True)
    l2 = jnp.sum(lam_ref[2:3, :] * lam_ref[3:4, :], axis=-1, keepdims=True)
    return jnp.exp(l1) - jnp.exp(l2) + lam_init


def _flash_kernel(qt_ref, k_ref, vt_ref, za_ref, lam_ref, g_ref, o_ref, m_scr, acc_scr,
                  s0_scr, s1_scr, *, tq, lam_init):
    nblk = qt_ref.shape[0]
    lam = _lambda(lam_ref, lam_init)

    def scores(i, j, dst):
        k0 = pl.multiple_of(j * tq, tq)
        dst[...] = jnp.dot(k_ref[pl.ds(k0, tq), :], qt_ref[i], preferred_element_type=F32)

    def consume(src, j, masked):
        s = src[...]
        if masked:
            key = lax.broadcasted_iota(jnp.int32, s.shape, 0)
            qry = lax.broadcasted_iota(jnp.int32, s.shape, 1)
            s = jnp.where(key <= jnp.where(qry >= tq, qry - tq, qry), s, NEG_INF)
        m_prev = m_scr[...]
        m_new = jnp.maximum(m_prev, jnp.max(s, axis=0, keepdims=True))
        p = jnp.exp2(s - m_new)
        alpha = jnp.exp2(m_prev - m_new)
        acc_scr[...] = alpha * acc_scr[...] + jnp.dot(vt_ref[j], p.astype(BF16), preferred_element_type=F32)
        m_scr[...] = m_new

    bufs = (s0_scr, s1_scr)

    def run(i, first, count, masked_last):
        for u in range(count):
            last = u == count - 1
            if not (last and masked_last):
                scores(i, first + u + 1, bufs[(u + 1) % 2])
            consume(bufs[u % 2], first + u, last and masked_last)

    def finalize(i):
        rows = pl.ds(pl.multiple_of(i * tq, tq), tq)
        o_t = acc_scr[0:HEAD_DIM, :] * (1.0 / acc_scr[HEAD_DIM:HEAD_DIM + 1, :])
        o_t = o_t[:, 0:tq] - lam * o_t[:, tq:2 * tq]
        o_t = o_t * lax.rsqrt(jnp.mean(o_t * o_t, axis=0, keepdims=True) + EPS)
        o = o_t.T * g_ref[...] * (1.0 - lam_init)
        o_ref[rows, :] = (o * za_ref[rows, :]).astype(BF16)

    def query_block(i, carry):
        m_scr[...] = jnp.full(m_scr.shape, NEG_INF, F32)
        acc_scr[...] = jnp.zeros(acc_scr.shape, F32)

        def full_group(jj, c):
            run(i, FLASH_UNROLL * jj, FLASH_UNROLL, False)
            return c

        lax.fori_loop(0, i // FLASH_UNROLL, full_group, 0)
        for rem in range(FLASH_UNROLL):
            @pl.when(i % FLASH_UNROLL == rem)
            def _tail(rem=rem):
                run(i, i - rem, rem + 1, True)
        scores(jnp.minimum(i + 1, nblk - 1), 0, s0_scr)
        finalize(i)
        return carry

    scores(0, 0, s0_scr)
    lax.fori_loop(0, nblk, query_block, 0)


def _flash(qt, kb, vt, rest, lamp, attn_g, lam_init, tq):
    b, s, _ = kb.shape
    nblk = s // tq
    seq = lambda: pl.BlockSpec((None, s, HEAD_DIM), lambda bi, h: (bi, 0, h))
    return pl.pallas_call(
        functools.partial(_flash_kernel, tq=tq, lam_init=lam_init),
        grid=(b, N_HEADS),
        in_specs=[pl.BlockSpec((None, nblk, HEAD_DIM, 2 * tq), lambda bi, h: (h, bi, 0, 0)),
                  seq(),
                  pl.BlockSpec((None, nblk, VT_ROWS, tq), lambda bi, h: (h, bi, 0, 0)),
                  seq(),
                  pl.BlockSpec((4, QK_DIM), lambda bi, h: (0, 0)),
                  pl.BlockSpec((1, HEAD_DIM), lambda bi, h: (0, h))],
        out_specs=seq(),
        out_shape=jax.ShapeDtypeStruct((b, s, GROUP_W), BF16),
        scratch_shapes=[pltpu.VMEM((1, 2 * tq), F32),
                        pltpu.VMEM((VT_ROWS, 2 * tq), F32),
                        pltpu.VMEM((tq, 2 * tq), F32),
                        pltpu.VMEM((tq, 2 * tq), F32)],
        compiler_params=_cparams(("parallel", "parallel")),
        name="flash_prompt",
    )(qt, kb, vt, rest, lamp, attn_g)


def _paged_kernel(pt_ref, q_ref, *refs, n_pages, n_valid, lam_init):
    del pt_ref
    kp = refs[:n_pages]
    vp = refs[n_pages:2 * n_pages]
    kn_ref, vn_ref, za_ref, lam_ref, g_ref, o_ref, qq_scr, bias_scr, m_scr, l_scr, acc_scr = refs[2 * n_pages:]
    r = SAMPLE_ROWS
    half = r // 2
    pw = bias_scr.shape[1]
    step = pl.program_id(1)

    @pl.when(step == 0)
    def _init():
        row = lax.broadcasted_iota(jnp.int32, (r, HEAD_DIM), 0)
        lane = lax.broadcasted_iota(jnp.int32, (r, HEAD_DIM), 1)
        for h in range(N_HEADS):
            q = q_ref[:, h * HEAD_DIM:(h + 1) * HEAD_DIM].astype(F32)
            q_map1 = jnp.where(lane < QK_DIM, q, 0.0)
            q_map2 = pltpu.roll(jnp.where(lane >= QK_DIM, q, 0.0), half, axis=0)
            qq_scr[h * r:(h + 1) * r, :] = jnp.where(row < half, q_map1, q_map2)
        row_head = lax.broadcasted_iota(jnp.int32, bias_scr.shape, 0) // r
        col_head = lax.broadcasted_iota(jnp.int32, bias_scr.shape, 1) % N_HEADS
        bias_scr[...] = jnp.where(row_head == col_head, 0.0, NEG_INF).astype(F32)
        m_scr[...] = jnp.full(m_scr.shape, NEG_INF, F32)
        l_scr[...] = jnp.zeros(l_scr.shape, F32)
        acc_scr[...] = jnp.zeros(acc_scr.shape, F32)

    qq = qq_scr[...].astype(BF16)
    bias = bias_scr[...]
    s = jnp.concatenate(
        [lax.dot_general(qq, kp[j][...].astype(BF16), (((1,), (1,)), ((), ())),
                         preferred_element_type=F32) + bias for j in range(n_pages)], axis=1)
    m_prev = m_scr[...]
    m_new = jnp.maximum(m_prev, jnp.max(s, axis=-1, keepdims=True))
    p = jnp.exp2(s - m_new)
    alpha = jnp.exp2(m_prev - m_new)
    l_scr[...] = alpha * l_scr[...] + jnp.sum(p, axis=-1, keepdims=True)
    p = p.astype(BF16)
    pv = jnp.dot(p[:, 0:pw], vp[0][...].astype(BF16), preferred_element_type=F32)
    for j in range(1, n_pages):
        pv = pv + jnp.dot(p[:, j * pw:(j + 1) * pw], vp[j][...].astype(BF16), preferred_element_type=F32)
    acc_scr[...] = alpha * acc_scr[...] + pv
    m_scr[...] = m_new

    @pl.when(step == pl.num_programs(1) - 1)
    def _finish():
        lam = _lambda(lam_ref, lam_init)
        row = lax.broadcasted_iota(jnp.int32, (r, 1), 0)
        tok = jnp.where(row >= half, row - half, row)
        for h in range(N_HEADS):
            cols = slice(h * HEAD_DIM, (h + 1) * HEAD_DIM)
            rows = slice(h * r, (h + 1) * r)
            qf = qq_scr[rows, :]
            kn = kn_ref[:, cols]
            vn = vn_ref[:, cols]
            s_new = [jnp.where(tok >= j, jnp.sum(qf * kn[j:j + 1, :], axis=-1, keepdims=True), NEG_INF)
                     for j in range(n_valid)]
            m_prev = m_scr[rows, :]
            m_new = m_prev
            for sj in s_new:
                m_new = jnp.maximum(m_new, sj)
            alpha = jnp.exp2(m_prev - m_new)
            l = alpha * l_scr[rows, :]
            acc = alpha * acc_scr[rows, :]
            for j, sj in enumerate(s_new):
                pj = jnp.exp2(sj - m_new)
                l = l + pj
                acc = acc + pj * vn[j:j + 1, :]
            o_maps = acc / l
            o = o_maps - lam * pltpu.roll(o_maps, half, axis=0)
            o = _rms(o) * g_ref[:, cols] * (1.0 - lam_init)
            o_ref[:, cols] = (o * za_ref[:, cols]).astype(BF16)


def _paged(q, k_new, v_new, rest, cache_k, cache_v, page_table, lamp, attn_g, lam_init, n_valid):
    db = q.shape[0]
    r = SAMPLE_ROWS
    assert 2 * n_valid <= r, "each head packs the tokens of both maps into SAMPLE_ROWS query rows"
    n_pages_total = page_table.shape[1]
    page_rows = cache_k.shape[1]
    p = PAGES_PER_STEP
    seq = lambda w=GROUP_W: pl.BlockSpec((None, r, w), lambda b, s, pt: (b, 0, 0))

    def page_spec(j):
        return pl.BlockSpec((None, page_rows, HEAD_DIM), lambda b, s, pt, j=j: (pt[b, s * p + j], 0, 0))

    grid_spec = pltpu.PrefetchScalarGridSpec(
        num_scalar_prefetch=1,
        grid=(db, n_pages_total // p),
        in_specs=[seq()] + [page_spec(j) for j in range(p)] * 2 + [seq(), seq(), seq(),
                  pl.BlockSpec((4, QK_DIM), lambda b, s, pt: (0, 0)),
                  pl.BlockSpec((1, GROUP_W), lambda b, s, pt: (0, 0))],
        out_specs=seq(),
        scratch_shapes=[pltpu.VMEM((N_HEADS * r, HEAD_DIM), F32),
                        pltpu.VMEM((N_HEADS * r, page_rows), F32),
                        pltpu.VMEM((N_HEADS * r, 1), F32),
                        pltpu.VMEM((N_HEADS * r, 1), F32),
                        pltpu.VMEM((N_HEADS * r, HEAD_DIM), F32)])
    return pl.pallas_call(
        functools.partial(_paged_kernel, n_pages=p, n_valid=n_valid, lam_init=lam_init),
        grid_spec=grid_spec,
        out_shape=jax.ShapeDtypeStruct((db, r, GROUP_W), BF16),
        compiler_params=_cparams(("parallel", "arbitrary")),
        name="paged_attn",
    )(page_table, q, *([cache_k] * p), *([cache_v] * p), k_new, v_new, rest, lamp, attn_g)


def _log_sigmoid(x):
    return jnp.minimum(x, 0.0) - jnp.log1p(jnp.exp(-jnp.abs(x)))


def _bf16_terms(x):
    hi = x.astype(BF16)
    rest = x - hi.astype(F32)
    mid = rest.astype(BF16)
    lo = (rest - mid.astype(F32)).astype(BF16)
    return hi, mid, lo


def _mlstm_kernel(xb_ref, ob_ref, zb_ref, conv0_ref, c0_ref, n0_ref, m0_ref,
                  cw_ref, cb_ref, wq_ref, wk_ref, wv_ref, wif_ref, wift_ref, bifr_ref, bifc_ref,
                  mg_ref, skip_ref,
                  bo_ref, c1_ref, n1_ref, m1_ref,
                  xpad, xch_scr, q_scr, k_scr, v_scr, h_scr, gc_scr, gr_scr, bc_scr, br_scr, c_scr, n_scr, m_scr,
                  *, t, chunk, n_valid, group):
    step = pl.program_id(1)
    n_chunks = t // chunk
    hd = HEAD_DIM
    ri = lax.broadcasted_iota(jnp.int32, (chunk, chunk), 0)
    ci = lax.broadcasted_iota(jnp.int32, (chunk, chunk), 1)
    causal = ci <= ri
    tril = jnp.where(causal, 1.0, 0.0).astype(BF16)
    triu = jnp.where(ri <= ci, 1.0, 0.0).astype(BF16)

    @pl.when(step == 0)
    def _init():
        for g in range(group):
            xpad[g, 0:HALO, :] = jnp.zeros((HALO, GROUP_W), F32)
            xpad[g, HALO - (CONV_W - 1):HALO, :] = conv0_ref[g]
        c_scr[...] = c0_ref[...]
        n_scr[...] = n0_ref[...]
        m_scr[...] = m0_ref[...]

    for g in range(group):
        x = xb_ref[g]
        xpad[g, HALO:HALO + t, :] = x
        xc = cb_ref[...] + cw_ref[CONV_W - 1:CONV_W, :] * x
        for j in range(CONV_W - 1):
            first = HALO - (CONV_W - 1) + j
            xc = xc + cw_ref[j:j + 1, :] * xpad[g, first:first + t, :]
        tail = xpad[g, t:t + HALO, :]
        xpad[g, 0:HALO, :] = tail
        xch = _silu(xc)
        xch_scr[g] = xch

        gcol = jnp.zeros((t, 2 * N_HEADS), F32) + bifr_ref[...]
        grow = jnp.zeros((2 * N_HEADS, t), F32) + bifc_ref[...]
        for h in range(N_HEADS):
            cols = slice(h * hd, (h + 1) * hd)
            xh = xch[:, cols].astype(BF16)
            qh = jnp.dot(xh, wq_ref[h], preferred_element_type=F32)
            kh = jnp.dot(xh, wk_ref[h], preferred_element_type=F32) * (hd ** -0.5)
            vh = jnp.dot(x[:, cols].astype(BF16), wv_ref[h], preferred_element_type=F32)
            q_scr[g, :, cols] = qh
            k_scr[g, :, cols] = kh
            v_scr[g, :, cols] = vh
            for part, val in enumerate((qh, kh, vh)):
                off = (3 * h + part) * hd
                vb = val.astype(BF16)
                gcol = gcol + jnp.dot(vb, wif_ref[off:off + hd, :], preferred_element_type=F32)
                grow = grow + lax.dot_general(wift_ref[:, off:off + hd], vb, (((1,), (1,)), ((), ())),
                                              preferred_element_type=F32)

        lane = lax.broadcasted_iota(jnp.int32, gcol.shape, 1)
        gcol = jnp.where(lane < N_HEADS, gcol, _log_sigmoid(gcol))
        sub = lax.broadcasted_iota(jnp.int32, grow.shape, 0)
        grow = jnp.where(sub < N_HEADS, grow, _log_sigmoid(grow))
        if n_valid < chunk:
            pos_c = lax.broadcasted_iota(jnp.int32, gcol.shape, 0) % chunk
            gcol = jnp.where(pos_c >= n_valid, jnp.where(lane < N_HEADS, NEG_INF, 0.0), gcol)
            pos_r = lax.broadcasted_iota(jnp.int32, grow.shape, 1) % chunk
            grow = jnp.where(pos_r >= n_valid, jnp.where(sub < N_HEADS, NEG_INF, 0.0), grow)
        gc_scr[g] = gcol
        lf_col = jnp.where(lane >= N_HEADS, gcol, 0.0)
        lf_row = jnp.where(sub >= N_HEADS, grow, 0.0)
        for c in range(n_chunks):
            span = slice(c * chunk, (c + 1) * chunk)
            gr_scr[g, c] = grow[:, span]
            bc_scr[g, span, :] = sum(jnp.dot(tril, part, preferred_element_type=F32)
                                     for part in _bf16_terms(lf_col[span, :]))
            br_scr[g, c] = sum(jnp.dot(part, triu, preferred_element_type=F32)
                               for part in _bf16_terms(lf_row[:, span]))

    def scan_chunk(c, carry):
        r0 = pl.multiple_of(c * chunk, chunk)
        rows = pl.ds(r0, chunk)
        chains = [(g, h) for g in range(group) for h in range(N_HEADS)]
        gates = {g: (gc_scr[g, rows, :], gr_scr[g, c], bc_scr[g, rows, :], br_scr[g, c]) for g in range(group)}
        st = []
        for g, h in chains:
            gc, gr, b_c, b_r = gates[g]
            bc = b_c[:, N_HEADS + h:N_HEADS + h + 1]
            br = b_r[N_HEADS + h:N_HEADS + h + 1, :]
            ic = gc[:, h:h + 1]
            ir = gr[h:h + 1, :]
            m = m_scr[g, h]
            dm = jnp.where(causal, bc - br + ir, NEG_INF)
            st.append(dict(bc=bc, br=br, ic=ic, ir=ir, m=m, dm=dm, inter=bc + m,
                           cols=slice(h * hd, (h + 1) * hd)))
        for s in st:
            s["m_t"] = jnp.maximum(s["inter"], jnp.max(s["dm"], axis=-1, keepdims=True))
        for s in st:
            s["w_d"] = jnp.exp(s["dm"] - s["m_t"])
            s["w_in"] = jnp.exp(s["inter"] - s["m_t"])
        for (g, h), s in zip(chains, st):
            s["q"] = q_scr[g, rows, s["cols"]]
            s["k"] = k_scr[g, rows, s["cols"]]
            s["v"] = v_scr[g, rows, s["cols"]]
            s["qb"] = s["q"].astype(BF16)
            s["kb"] = s["k"].astype(BF16)
            s["sqk"] = lax.dot_general(s["qb"], s["kb"], (((1,), (1,)), ((), ())),
                                       preferred_element_type=F32) * s["w_d"]
        for (g, h), s in zip(chains, st):
            s["cmat"] = c_scr[g, h]
            s["nvec"] = n_scr[g, h]
            s["num"] = (s["w_in"] * jnp.dot(s["qb"], s["cmat"].astype(BF16), preferred_element_type=F32)
                        + jnp.dot(s["sqk"].astype(BF16), s["v"].astype(BF16), preferred_element_type=F32))
        for s in st:
            s["den"] = (s["w_in"] * jnp.sum(s["q"] * s["nvec"], axis=-1, keepdims=True)
                        + jnp.sum(s["sqk"], axis=-1, keepdims=True))
        for (g, h), s in zip(chains, st):
            h_scr[g, rows, s["cols"]] = s["num"] / jnp.maximum(jnp.abs(s["den"]), jnp.exp(-s["m_t"]))
        for (g, h), s in zip(chains, st):
            b_l = s["bc"][chunk - 1:chunk, :]
            g_r = b_l - s["br"] + s["ir"]
            m_new = jnp.maximum(b_l + s["m"], jnp.max(g_r, axis=-1, keepdims=True))
            decay = jnp.exp(b_l + s["m"] - m_new)
            w_g = jnp.exp(b_l - s["bc"] + s["ic"] - m_new)
            c_scr[g, h] = decay * s["cmat"] + lax.dot_general(
                s["kb"], (w_g * s["v"]).astype(BF16), (((0,), (0,)), ((), ())), preferred_element_type=F32)
            n_scr[g, h] = decay * s["nvec"] + jnp.sum(w_g * s["k"], axis=0, keepdims=True)
            m_scr[g, h] = m_new
        return carry

    if n_chunks == 1:
        scan_chunk(0, 0)
    else:
        lax.fori_loop(0, n_chunks, scan_chunk, 0)

    for g in range(group):
        for h in range(N_HEADS):
            cols = slice(h * hd, (h + 1) * hd)
            hb = h_scr[g, :, cols] * ob_ref[g, :, cols]
            hb = _rms(hb) * mg_ref[:, cols] + skip_ref[:, cols] * xch_scr[g, :, cols]
            bo_ref[g, :, cols] = (hb * zb_ref[g, :, cols]).astype(BF16)

    @pl.when(step == pl.num_programs(1) - 1)
    def _state_out():
        c1_ref[...] = c_scr[...]
        n1_ref[...] = n_scr[...]
        m1_ref[...] = m_scr[...]


def _mlstm(rest, conv0, c0, n0, m0, w, t, chunk, n_valid, group):
    bn, s, _ = rest.shape
    hd = HEAD_DIM
    tok = lambda col: pl.BlockSpec((group, t, GROUP_W), lambda b, j, col=col: (b, j, col))
    per_b = lambda *shape: pl.BlockSpec((group,) + shape, lambda b, j: (b,) + (0,) * len(shape))
    full = lambda *shape: pl.BlockSpec(shape, lambda b, j: (0,) * len(shape))
    state_shapes = [(N_HEADS, hd, hd), (N_HEADS, 1, hd), (N_HEADS, 1, 1)]
    state_specs = [per_b(*shape) for shape in state_shapes]
    return pl.pallas_call(
        functools.partial(_mlstm_kernel, t=t, chunk=chunk, n_valid=n_valid, group=group),
        grid=(bn // group, s // t),
        in_specs=[tok(1), tok(2), tok(3), per_b(CONV_W - 1, GROUP_W)] + state_specs + [
            full(CONV_W, GROUP_W), full(1, GROUP_W),
            full(N_HEADS, hd, hd), full(N_HEADS, hd, hd), full(N_HEADS, hd, hd),
            full(3 * GROUP_W, 2 * N_HEADS), full(2 * N_HEADS, 3 * GROUP_W),
            full(1, 2 * N_HEADS), full(2 * N_HEADS, 1),
            full(1, GROUP_W), full(1, GROUP_W)],
        out_specs=[tok(0)] + state_specs,
        out_shape=[jax.ShapeDtypeStruct((bn, s, GROUP_W), BF16)]
                  + [jax.ShapeDtypeStruct((bn,) + shape, F32) for shape in state_shapes],
        scratch_shapes=[pltpu.VMEM((group, t + HALO, GROUP_W), F32)]
                       + [pltpu.VMEM((group, t, GROUP_W), F32)] * 5
                       + [pltpu.VMEM((group, t, 2 * N_HEADS), F32),
                          pltpu.VMEM((group, t // chunk, 2 * N_HEADS, chunk), F32)] * 2
                       + [pltpu.VMEM((group,) + shape, F32) for shape in state_shapes],
        compiler_params=_cparams(("parallel", "arbitrary")),
        name="mlstm",
    )(rest, rest, rest, conv0, c0, n0, m0,
      w["conv_w"], w["conv_b"], w["wq"], w["wk"], w["wv"], w["wif"], w["wif_t"], w["bif_row"], w["bif_col"],
      w["mlstm_g"], w["mlstm_skip"])


def _outproj_kernel(a_ref, bo_ref, x_ref, p_ref, woa_ref, wob_ref, pg_ref, wgate_ref, wple_ref, plg_ref, y_ref):
    tm = x_ref.shape[0]
    n_slabs = OUTPROJ_SLABS if tm % (F32_SUBLANES * OUTPROJ_SLABS) == 0 else 1
    slabs = [slice(c * tm // n_slabs, (c + 1) * tm // n_slabs) for c in range(n_slabs)]
    y = [jnp.dot(a_ref[r, :], woa_ref[...], preferred_element_type=F32)
         + jnp.dot(bo_ref[r, :], wob_ref[...], preferred_element_type=F32) for r in slabs]
    pe = [jnp.dot(p_ref[r, :].astype(BF16), wple_ref[...], preferred_element_type=F32) for r in slabs]
    x1 = [x_ref[r, :] + _rms(y_r) * pg_ref[...] for r, y_r in zip(slabs, y)]
    gate = [_sigmoid(jnp.dot(x.astype(BF16), wgate_ref[...], preferred_element_type=F32)) for x in x1]
    for r, x, g, e in zip(slabs, x1, gate, pe):
        y_ref[r, :] = x + _rms(g * e) * plg_ref[...]


def _outproj(a, bo, x, p, w, tm):
    m, d = x.shape
    row = lambda width: pl.BlockSpec((tm, width), lambda i: (i, 0))
    full = lambda arr: pl.BlockSpec(arr.shape, lambda i: (0,) * arr.ndim)
    consts = [w["wo_a"], w["wo_b"], w["post_g"], w["ple_gate"], w["ple_w"], w["ple_g"]]
    return pl.pallas_call(
        _outproj_kernel,
        grid=(m // tm,),
        in_specs=[row(GROUP_W), row(GROUP_W), row(d), row(p.shape[1])] + [full(c) for c in consts],
        out_specs=row(d),
        out_shape=jax.ShapeDtypeStruct((m, d), F32),
        compiler_params=_cparams(("parallel",)),
        name="outproj",
    )(a, bo, x, p, *consts)


def _layer_weights(l, norm_pre_g, norm_post_g, w_in, lam_q1, lam_k1, lam_q2, lam_k2, attn_norm_g, conv_w, conv_b,
                   w_q_b, w_k_b, w_v_b, w_if, b_if, mlstm_norm_g, mlstm_skip, w_out, ple_w, ple_gate_w, ple_norm_g):
    row = lambda v: v.reshape(1, -1)
    wo = w_out[l].astype(BF16)
    return {
        "pre_g": row(norm_pre_g[l]), "w_in": w_in[l].astype(BF16),
        "lam": jnp.stack([lam_q1[l], lam_k1[l], lam_q2[l], lam_k2[l]]).astype(F32),
        "attn_g": row(attn_norm_g[l]),
        "conv_w": conv_w[l], "conv_b": row(conv_b[l]),
        "wq": w_q_b[l].astype(BF16), "wk": w_k_b[l].astype(BF16), "wv": w_v_b[l].astype(BF16),
        "wif": w_if[l].astype(BF16), "wif_t": w_if[l].T.astype(BF16),
        "bif_row": row(b_if[l]), "bif_col": b_if[l].reshape(-1, 1),
        "mlstm_g": row(mlstm_norm_g[l]), "mlstm_skip": row(mlstm_skip[l]),
        "wo_a": wo[:GROUP_W], "wo_b": wo[GROUP_W:], "post_g": row(norm_post_g[l]),
        "ple_gate": ple_gate_w[l].astype(BF16), "ple_w": ple_w[l].astype(BF16), "ple_g": row(ple_norm_g[l]),
    }


def _prompt_layer(x, p, w, lam_init):
    b, s, d = x.shape
    x2 = x.reshape(b * s, d)
    qt, k, v, rest, kb, vt = _inproj(x2, w["pre_g"], w["w_in"], tm=PROMPT_ROWS, kv_block=FLASH_BLOCK)
    rest3 = rest.reshape(b, s, 4 * GROUP_W)
    a = _flash(qt, kb.reshape(b, s, GROUP_W), vt, rest3, w["lam"], w["attn_g"], lam_init, tq=FLASH_BLOCK)
    hd = HEAD_DIM
    bo, c1, n1, m1 = _mlstm(rest3, jnp.zeros((b, CONV_W - 1, GROUP_W), F32),
                            jnp.zeros((b, N_HEADS, hd, hd), F32), jnp.zeros((b, N_HEADS, 1, hd), F32),
                            jnp.zeros((b, N_HEADS, 1, 1), F32), w, t=PROMPT_ROWS, chunk=PROMPT_CHUNK,
                            n_valid=PROMPT_CHUNK, group=PROMPT_GROUP)
    y = _outproj(a.reshape(b * s, GROUP_W), bo.reshape(b * s, GROUP_W), x2, p.reshape(b * s, -1), w,
                 tm=OUTPROJ_ROWS)
    state = (k.reshape(b, s, N_HEADS, hd), v.reshape(b, s, N_HEADS, hd),
             rest3[:, s - (CONV_W - 1):, GROUP_W:2 * GROUP_W],
             c1, n1.reshape(b, N_HEADS, hd), m1.reshape(b, N_HEADS))
    return y.reshape(b, s, d), state


def _sample_layer(x, p, n_valid, cache_k, cache_v, page_table, conv0, c0, n0, m0, w, lam_init):
    db, r, d = x.shape
    hd = HEAD_DIM
    assert CONV_W - 1 <= n_valid <= r
    x2 = x.reshape(db * r, d)
    q, k, v, rest = _inproj(x2, w["pre_g"], w["w_in"], tm=db * r)
    rest3 = rest.reshape(db, r, 4 * GROUP_W)
    k3 = k.reshape(db, r, GROUP_W)
    v3 = v.reshape(db, r, GROUP_W)
    a = _paged(q.reshape(db, r, GROUP_W), k3, v3, rest3, cache_k, cache_v, page_table, w["lam"], w["attn_g"],
               lam_init, n_valid)
    bo, c1, n1, m1 = _mlstm(rest3, conv0, c0, n0.reshape(db, N_HEADS, 1, hd), m0.reshape(db, N_HEADS, 1, 1),
                            w, t=r, chunk=r, n_valid=n_valid, group=SAMPLE_GROUP)
    y = _outproj(a.reshape(db * r, GROUP_W), bo.reshape(db * r, GROUP_W), x2, p.reshape(db * r, -1), w, tm=db * r)
    state = (k3[:, :n_valid].reshape(db, n_valid, N_HEADS, hd), v3[:, :n_valid].reshape(db, n_valid, N_HEADS, hd),
             rest3[:, n_valid - (CONV_W - 1):n_valid, GROUP_W:2 * GROUP_W],
             c1, n1.reshape(db, N_HEADS, hd), m1.reshape(db, N_HEADS))
    return y.reshape(db, r, d), state


def kernel(x_prompt, x_sample, cache_k, cache_v, state_conv, state_C, state_n, state_m, page_table,
           p_prompt, p_sample, norm_pre_g, norm_post_g, w_in, lam_q1, lam_k1, lam_q2, lam_k2,
           attn_norm_g, conv_w, conv_b, w_q_b, w_k_b, w_v_b, w_if, b_if, mlstm_norm_g, mlstm_skip,
           w_out, ple_w, ple_gate_w, ple_norm_g):
    depth = w_in.shape[0]
    t_dec = x_sample.shape[1]
    pad_rows = lambda a: jnp.pad(a, ((0, 0), (0, SAMPLE_ROWS - t_dec), (0, 0)))
    n_pool, page = cache_k.shape[1], cache_k.shape[2]
    cache_k2 = cache_k.reshape(depth * n_pool, page * N_HEADS, HEAD_DIM)
    cache_v2 = cache_v.reshape(depth * n_pool, page * N_HEADS, HEAD_DIM)
    xp = x_prompt
    xs = pad_rows(x_sample)
    outs_p, outs_s = [], []
    for l in range(depth):
        w = _layer_weights(l, norm_pre_g, norm_post_g, w_in, lam_q1, lam_k1, lam_q2, lam_k2, attn_norm_g,
                           conv_w, conv_b, w_q_b, w_k_b, w_v_b, w_if, b_if, mlstm_norm_g, mlstm_skip,
                           w_out, ple_w, ple_gate_w, ple_norm_g)
        lam_init = 0.8 - 0.6 * math.exp(-0.3 * l)
        xp, st_p = _prompt_layer(xp, p_prompt[l], w, lam_init)
        outs_p.append(st_p)
        xs, st_s = _sample_layer(xs, pad_rows(p_sample[l]), t_dec, cache_k2, cache_v2, page_table + l * n_pool,
                                 state_conv[l], state_C[l], state_n[l], state_m[l], w, lam_init)
        outs_s.append(st_s)
    stack = lambda outs, i: jnp.stack([o[i] for o in outs])
    return ((xp, xs[:, :t_dec]) + tuple(stack(outs_p, i) for i in range(6))
            + tuple(stack(outs_s, i) for i in range(6)))
```

```python
import functools
import math

import jax
import jax.numpy as jnp
from jax import lax
from jax.experimental import pallas as pl
from jax.experimental.pallas import tpu as pltpu

F32 = jnp.float32
BF16 = jnp.bfloat16
EPS = 1e-6
NEG_INF = float("-inf")

N_HEADS = 4
HEAD_DIM = 128
QK_DIM = 64
GROUP_W = N_HEADS * HEAD_DIM
CONV_W = 4
F32_SUBLANES = 8
SAMPLE_ROWS = F32_SUBLANES
HALO = F32_SUBLANES
PROMPT_CHUNK = 128
PROMPT_ROWS = 512
INPROJ_ROWS = 1024
OUTPROJ_ROWS = 1024
FLASH_BLOCK = 512
FLASH_UNROLL = 6
OUTPROJ_SLABS = 4
PROMPT_GROUP = 2
SAMPLE_GROUP = 4
BF16_SUBLANES = 16
VT_ROWS = HEAD_DIM + BF16_SUBLANES
Q_SCALE = (QK_DIM ** -0.5) * math.log2(math.e)
PAGES_PER_STEP = 32
V7X_VMEM_LIMIT = 48 * 1024 * 1024
V7X_VMEM_LIMIT_INPROJ = 54 * 1024 * 1024


def _cparams(sem, vmem_limit=V7X_VMEM_LIMIT):
    return pltpu.CompilerParams(dimension_semantics=sem, vmem_limit_bytes=vmem_limit)


def _sigmoid(x):
    return 0.5 * jnp.tanh(0.5 * x) + 0.5


def _silu(x):
    return x * _sigmoid(x)


def _rms(x):
    return x * lax.rsqrt(jnp.mean(x * x, axis=-1, keepdims=True) + EPS)


def _inproj_kernel(x_ref, g_ref, w_ref, q_ref, k_ref, v_ref, rest_ref, *maybe_flash_refs, kv_block):
    hb = (_rms(x_ref[...]) * g_ref[...]).astype(BF16)

    def proj(c):
        return jnp.dot(hb, w_ref[:, c * GROUP_W:(c + 1) * GROUP_W], preferred_element_type=F32)

    q = proj(0) * Q_SCALE
    k = proj(1)
    v = proj(2)
    if not kv_block:
        q_ref[...] = q.astype(BF16)
        k_ref[...] = k
        v_ref[...] = v
    else:
        for h in range(N_HEADS):
            for c in range(q.shape[0] // kv_block):
                blk = q[c * kv_block:(c + 1) * kv_block, h * HEAD_DIM:(h + 1) * HEAD_DIM]
                q_ref[h, c] = _split_maps(blk).T.astype(BF16)
        for h in range(N_HEADS):
            cols = slice(h * HEAD_DIM, (h + 1) * HEAD_DIM)
            k_ref[pl.ds(h, k.shape[0], stride=N_HEADS), :] = k[:, cols]
            v_ref[pl.ds(h, v.shape[0], stride=N_HEADS), :] = v[:, cols]
        kb_ref, vt_ref = maybe_flash_refs
        kb_ref[...] = k.astype(BF16)
        for h in range(N_HEADS):
            for c in range(v.shape[0] // kv_block):
                blk = v[c * kv_block:(c + 1) * kv_block, h * HEAD_DIM:(h + 1) * HEAD_DIM]
                vt_ref[h, c, 0:HEAD_DIM, :] = blk.T.astype(BF16)
                vt_ref[h, c, HEAD_DIM:VT_ROWS, :] = jnp.ones((VT_ROWS - HEAD_DIM, kv_block), BF16)
    for c, act in zip(range(3, 7), (_silu, None, _sigmoid, _silu)):
        u = proj(c)
        rest_ref[:, (c - 3) * GROUP_W:(c - 2) * GROUP_W] = u if act is None else act(u)


def _inproj(x, g, w_bf, tm, kv_block=0):
    m, d = x.shape
    n_in = w_bf.shape[1]
    row = lambda w: pl.BlockSpec((tm, w), lambda i: (i, 0))
    out_specs = [row(GROUP_W)] * 3 + [row(4 * GROUP_W)]
    out_shape = [jax.ShapeDtypeStruct((m, GROUP_W), BF16),
                 jax.ShapeDtypeStruct((m, GROUP_W), F32),
                 jax.ShapeDtypeStruct((m, GROUP_W), F32),
                 jax.ShapeDtypeStruct((m, 4 * GROUP_W), F32)]
    if kv_block:
        out_specs[0] = pl.BlockSpec((N_HEADS, tm // kv_block, HEAD_DIM, 2 * kv_block), lambda i: (0, i, 0, 0))
        out_shape[0] = jax.ShapeDtypeStruct((N_HEADS, m // kv_block, HEAD_DIM, 2 * kv_block), BF16)
        for i_out in (1, 2):
            out_specs[i_out] = pl.BlockSpec((tm * N_HEADS, HEAD_DIM), lambda i: (i, 0))
            out_shape[i_out] = jax.ShapeDtypeStruct((m * N_HEADS, HEAD_DIM), F32)
        out_specs += [row(GROUP_W),
                      pl.BlockSpec((N_HEADS, tm // kv_block, VT_ROWS, kv_block), lambda i: (0, i, 0, 0))]
        out_shape += [jax.ShapeDtypeStruct((m, GROUP_W), BF16),
                      jax.ShapeDtypeStruct((N_HEADS, m // kv_block, VT_ROWS, kv_block), BF16)]
    return pl.pallas_call(
        functools.partial(_inproj_kernel, kv_block=kv_block),
        grid=(m // tm,),
        in_specs=[row(d), pl.BlockSpec((1, d), lambda i: (0, 0)),
                  pl.BlockSpec((d, n_in), lambda i: (0, 0), pipeline_mode=pl.Buffered(1))],
        out_specs=out_specs,
        out_shape=out_shape,
        compiler_params=_cparams(("parallel",), V7X_VMEM_LIMIT_INPROJ),
        name="inproj",
    )(x, g, w_bf)


def _split_maps(q):
    lane = lax.broadcasted_iota(jnp.int32, q.shape, 1)
    zero = jnp.zeros_like(q)
    return jnp.concatenate([jnp.where(lane < QK_DIM, q, zero), jnp.where(lane >= QK_DIM, q, zero)], axis=0)


def _lambda(lam_ref, lam_init):
    l1 = jnp.sum(lam_ref[0:1, :] * lam_ref[1:2, :], axis=-1, keepdims=True)
    l2 = jnp.sum(lam_ref[2:3, :] * lam_ref[3:4, :], axis=-1, keepdims=True)
    return jnp.exp(l1) - jnp.exp(l2) + lam_init


def _flash_kernel(qt_ref, k_ref, vt_ref, za_ref, lam_ref, g_ref, o_ref, m_scr, acc_scr,
                  s0_scr, s1_scr, *, tq, lam_init):
    nblk = qt_ref.shape[0]
    lam = _lambda(lam_ref, lam_init)

    def scores(i, j, dst):
        k0 = pl.multiple_of(j * tq, tq)
        dst[...] = jnp.dot(k_ref[pl.ds(k0, tq), :], qt_ref[i], preferred_element_type=F32)

    def consume(src, j, masked):
        s = src[...]
        if masked:
            key = lax.broadcasted_iota(jnp.int32, s.shape, 0)
            qry = lax.broadcasted_iota(jnp.int32, s.shape, 1)
            s = jnp.where(key <= jnp.where(qry >= tq, qry - tq, qry), s, NEG_INF)
        m_prev = m_scr[...]
        m_new = jnp.maximum(m_prev, jnp.max(s, axis=0, keepdims=True))
        p = jnp.exp2(s - m_new)
        alpha = jnp.exp2(m_prev - m_new)
        acc_scr[...] = alpha * acc_scr[...] + jnp.dot(vt_ref[j], p.astype(BF16), preferred_element_type=F32)
        m_scr[...] = m_new

    bufs = (s0_scr, s1_scr)

    def run(i, first, count, masked_last):
        for u in range(count):
            last = u == count - 1
            if not (last and masked_last):
                scores(i, first + u + 1, bufs[(u + 1) % 2])
            consume(bufs[u % 2], first + u, last and masked_last)

    def finalize(i):
        rows = pl.ds(pl.multiple_of(i * tq, tq), tq)
        o_t = acc_scr[0:HEAD_DIM, :] * (1.0 / acc_scr[HEAD_DIM:HEAD_DIM + 1, :])
        o_t = o_t[:, 0:tq] - lam * o_t[:, tq:2 * tq]
        o_t = o_t * lax.rsqrt(jnp.mean(o_t * o_t, axis=0, keepdims=True) + EPS)
        o = o_t.T * g_ref[...] * (1.0 - lam_init)
        o_ref[rows, :] = (o * za_ref[rows, :]).astype(BF16)

    def query_block(i, carry):
        m_scr[...] = jnp.full(m_scr.shape, NEG_INF, F32)
        acc_scr[...] = jnp.zeros(acc_scr.shape, F32)

        def full_group(jj, c):
            run(i, FLASH_UNROLL * jj, FLASH_UNROLL, False)
            return c

        lax.fori_loop(0, i // FLASH_UNROLL, full_group, 0)
        for rem in range(FLASH_UNROLL):
            @pl.when(i % FLASH_UNROLL == rem)
            def _tail(rem=rem):
                run(i, i - rem, rem + 1, True)
        scores(jnp.minimum(i + 1, nblk - 1), 0, s0_scr)
        finalize(i)
        return carry

    scores(0, 0, s0_scr)
    lax.fori_loop(0, nblk, query_block, 0)


def _flash(qt, kb, vt, rest, lamp, attn_g, lam_init, tq):
    b, s, _ = kb.shape
    nblk = s // tq
    seq = lambda: pl.BlockSpec((None, s, HEAD_DIM), lambda bi, h: (bi, 0, h))
    return pl.pallas_call(
        functools.partial(_flash_kernel, tq=tq, lam_init=lam_init),
        grid=(b, N_HEADS),
        in_specs=[pl.BlockSpec((None, nblk, HEAD_DIM, 2 * tq), lambda bi, h: (h, bi, 0, 0)),
                  seq(),
                  pl.BlockSpec((None, nblk, VT_ROWS, tq), lambda bi, h: (h, bi, 0, 0)),
                  seq(),
                  pl.BlockSpec((4, QK_DIM), lambda bi, h: (0, 0)),
                  pl.BlockSpec((1, HEAD_DIM), lambda bi, h: (0, h))],
        out_specs=seq(),
        out_shape=jax.ShapeDtypeStruct((b, s, GROUP_W), BF16),
        scratch_shapes=[pltpu.VMEM((1, 2 * tq), F32),
                        pltpu.VMEM((VT_ROWS, 2 * tq), F32),
                        pltpu.VMEM((tq, 2 * tq), F32),
                        pltpu.VMEM((tq, 2 * tq), F32)],
        compiler_params=_cparams(("parallel", "parallel")),
        name="flash_prompt",
    )(qt, kb, vt, rest, lamp, attn_g)


def _paged_kernel(pt_ref, q_ref, *refs, n_pages, n_valid, lam_init):
    del pt_ref
    kp = refs[:n_pages]
    vp = refs[n_pages:2 * n_pages]
    kn_ref, vn_ref, za_ref, lam_ref, g_ref, o_ref, qq_scr, bias_scr, m_scr, l_scr, acc_scr = refs[2 * n_pages:]
    r = SAMPLE_ROWS
    half = r // 2
    pw = bias_scr.shape[1]
    step = pl.program_id(1)

    @pl.when(step == 0)
    def _init():
        row = lax.broadcasted_iota(jnp.int32, (r, HEAD_DIM), 0)
        lane = lax.broadcasted_iota(jnp.int32, (r, HEAD_DIM), 1)
        for h in range(N_HEADS):
            q = q_ref[:, h * HEAD_DIM:(h + 1) * HEAD_DIM].astype(F32)
            q_map1 = jnp.where(lane < QK_DIM, q, 0.0)
            q_map2 = pltpu.roll(jnp.where(lane >= QK_DIM, q, 0.0), half, axis=0)
            qq_scr[h * r:(h + 1) * r, :] = jnp.where(row < half, q_map1, q_map2)
        row_head = lax.broadcasted_iota(jnp.int32, bias_scr.shape, 0) // r
        col_head = lax.broadcasted_iota(jnp.int32, bias_scr.shape, 1) % N_HEADS
        bias_scr[...] = jnp.where(row_head == col_head, 0.0, NEG_INF).astype(F32)
        m_scr[...] = jnp.full(m_scr.shape, NEG_INF, F32)
        l_scr[...] = jnp.zeros(l_scr.shape, F32)
        acc_scr[...] = jnp.zeros(acc_scr.shape, F32)

    qq = qq_scr[...].astype(BF16)
    bias = bias_scr[...]
    s = jnp.concatenate(
        [lax.dot_general(qq, kp[j][...].astype(BF16), (((1,), (1,)), ((), ())),
                         preferred_element_type=F32) + bias for j in range(n_pages)], axis=1)
    m_prev = m_scr[...]
    m_new = jnp.maximum(m_prev, jnp.max(s, axis=-1, keepdims=True))
    p = jnp.exp2(s - m_new)
    alpha = jnp.exp2(m_prev - m_new)
    l_scr[...] = alpha * l_scr[...] + jnp.sum(p, axis=-1, keepdims=True)
    p = p.astype(BF16)
    pv = jnp.dot(p[:, 0:pw], vp[0][...].astype(BF16), preferred_element_type=F32)
    for j in range(1, n_pages):
        pv = pv + jnp.dot(p[:, j * pw:(j + 1) * pw], vp[j][...].astype(BF16), preferred_element_type=F32)
    acc_scr[...] = alpha * acc_scr[...] + pv
    m_scr[...] = m_new

    @pl.when(step == pl.num_programs(1) - 1)
    def _finish():
        lam = _lambda(lam_ref, lam_init)
        row = lax.broadcasted_iota(jnp.int32, (r, 1), 0)
        tok = jnp.where(row >= half, row - half, row)
        for h in range(N_HEADS):
            cols = slice(h * HEAD_DIM, (h + 1) * HEAD_DIM)
            rows = slice(h * r, (h + 1) * r)
            qf = qq_scr[rows, :]
            kn = kn_ref[:, cols]
            vn = vn_ref[:, cols]
            s_new = [jnp.where(tok >= j, jnp.sum(qf * kn[j:j + 1, :], axis=-1, keepdims=True), NEG_INF)
                     for j in range(n_valid)]
            m_prev = m_scr[rows, :]
            m_new = m_prev
            for sj in s_new:
                m_new = jnp.maximum(m_new, sj)
            alpha = jnp.exp2(m_prev - m_new)
            l = alpha * l_scr[rows, :]
            acc = alpha * acc_scr[rows, :]
            for j, sj in enumerate(s_new):
                pj = jnp.exp2(sj - m_new)
                l = l + pj
                acc = acc + pj * vn[j:j + 1, :]
            o_maps = acc / l
            o = o_maps - lam * pltpu.roll(o_maps, half, axis=0)
            o = _rms(o) * g_ref[:, cols] * (1.0 - lam_init)
            o_ref[:, cols] = (o * za_ref[:, cols]).astype(BF16)


def _paged(q, k_new, v_new, rest, cache_k, cache_v, page_table, lamp, attn_g, lam_init, n_valid):
    db = q.shape[0]
    r = SAMPLE_ROWS
    assert 2 * n_valid <= r, "each head packs the tokens of both maps into SAMPLE_ROWS query rows"
    n_pages_total = page_table.shape[1]
    page_rows = cache_k.shape[1]
    p = PAGES_PER_STEP
    seq = lambda w=GROUP_W: pl.BlockSpec((None, r, w), lambda b, s, pt: (b, 0, 0))

    def page_spec(j):
        return pl.BlockSpec((None, page_rows, HEAD_DIM), lambda b, s, pt, j=j: (pt[b, s * p + j], 0, 0))

    grid_spec = pltpu.PrefetchScalarGridSpec(
        num_scalar_prefetch=1,
        grid=(db, n_pages_total // p),
        in_specs=[seq()] + [page_spec(j) for j in range(p)] * 2 + [seq(), seq(), seq(),
                  pl.BlockSpec((4, QK_DIM), lambda b, s, pt: (0, 0)),
                  pl.BlockSpec((1, GROUP_W), lambda b, s, pt: (0, 0))],
        out_specs=seq(),
        scratch_shapes=[pltpu.VMEM((N_HEADS * r, HEAD_DIM), F32),
                        pltpu.VMEM((N_HEADS * r, page_rows), F32),
                        pltpu.VMEM((N_HEADS * r, 1), F32),
                        pltpu.VMEM((N_HEADS * r, 1), F32),
                        pltpu.VMEM((N_HEADS * r, HEAD_DIM), F32)])
    return pl.pallas_call(
        functools.partial(_paged_kernel, n_pages=p, n_valid=n_valid, lam_init=lam_init),
        grid_spec=grid_spec,
        out_shape=jax.ShapeDtypeStruct((db, r, GROUP_W), BF16),
        compiler_params=_cparams(("parallel", "arbitrary")),
        name="paged_attn",
    )(page_table, q, *([cache_k] * p), *([cache_v] * p), k_new, v_new, rest, lamp, attn_g)


def _log_sigmoid(x):
    return jnp.minimum(x, 0.0) - jnp.log1p(jnp.exp(-jnp.abs(x)))


def _bf16_terms(x):
    hi = x.astype(BF16)
    rest = x - hi.astype(F32)
    mid = rest.astype(BF16)
    lo = (rest - mid.astype(F32)).astype(BF16)
    return hi, mid, lo


def _mlstm_kernel(xb_ref, ob_ref, zb_ref, conv0_ref, c0_ref, n0_ref, m0_ref,
                  cw_ref, cb_ref, wq_ref, wk_ref, wv_ref, wif_ref, wift_ref, bifr_ref, bifc_ref,
                  mg_ref, skip_ref,
                  bo_ref, c1_ref, n1_ref, m1_ref,
                  xpad, xch_scr, q_scr, k_scr, v_scr, h_scr, gc_scr, gr_scr, bc_scr, br_scr, c_scr, n_scr, m_scr,
                  *, t, chunk, n_valid, group):
    step = pl.program_id(1)
    n_chunks = t // chunk
    hd = HEAD_DIM
    ri = lax.broadcasted_iota(jnp.int32, (chunk, chunk), 0)
    ci = lax.broadcasted_iota(jnp.int32, (chunk, chunk), 1)
    causal = ci <= ri
    tril = jnp.where(causal, 1.0, 0.0).astype(BF16)
    triu = jnp.where(ri <= ci, 1.0, 0.0).astype(BF16)

    @pl.when(step == 0)
    def _init():
        for g in range(group):
            xpad[g, 0:HALO, :] = jnp.zeros((HALO, GROUP_W), F32)
            xpad[g, HALO - (CONV_W - 1):HALO, :] = conv0_ref[g]
        c_scr[...] = c0_ref[...]
        n_scr[...] = n0_ref[...]
        m_scr[...] = m0_ref[...]

    for g in range(group):
        x = xb_ref[g]
        xpad[g, HALO:HALO + t, :] = x
        xc = cb_ref[...] + cw_ref[CONV_W - 1:CONV_W, :] * x
        for j in range(CONV_W - 1):
            first = HALO - (CONV_W - 1) + j
            xc = xc + cw_ref[j:j + 1, :] * xpad[g, first:first + t, :]
        tail = xpad[g, t:t + HALO, :]
        xpad[g, 0:HALO, :] = tail
        xch = _silu(xc)
        xch_scr[g] = xch

        gcol = jnp.zeros((t, 2 * N_HEADS), F32) + bifr_ref[...]
        grow = jnp.zeros((2 * N_HEADS, t), F32) + bifc_ref[...]
        for h in range(N_HEADS):
            cols = slice(h * hd, (h + 1) * hd)
            xh = xch[:, cols].astype(BF16)
            qh = jnp.dot(xh, wq_ref[h], preferred_element_type=F32)
            kh = jnp.dot(xh, wk_ref[h], preferred_element_type=F32) * (hd ** -0.5)
            vh = jnp.dot(x[:, cols].astype(BF16), wv_ref[h], preferred_element_type=F32)
            q_scr[g, :, cols] = qh
            k_scr[g, :, cols] = kh
            v_scr[g, :, cols] = vh
            for part, val in enumerate((qh, kh, vh)):
                off = (3 * h + part) * hd
                vb = val.astype(BF16)
                gcol = gcol + jnp.dot(vb, wif_ref[off:off + hd, :], preferred_element_type=F32)
                grow = grow + lax.dot_general(wift_ref[:, off:off + hd], vb, (((1,), (1,)), ((), ())),
                                              preferred_element_type=F32)

        lane = lax.broadcasted_iota(jnp.int32, gcol.shape, 1)
        gcol = jnp.where(lane < N_HEADS, gcol, _log_sigmoid(gcol))
        sub = lax.broadcasted_iota(jnp.int32, grow.shape, 0)
        grow = jnp.where(sub < N_HEADS, grow, _log_sigmoid(grow))
        if n_valid < chunk:
            pos_c = lax.broadcasted_iota(jnp.int32, gcol.shape, 0) % chunk
            gcol = jnp.where(pos_c >= n_valid, jnp.where(lane < N_HEADS, NEG_INF, 0.0), gcol)
            pos_r = lax.broadcasted_iota(jnp.int32, grow.shape, 1) % chunk
            grow = jnp.where(pos_r >= n_valid, jnp.where(sub < N_HEADS, NEG_INF, 0.0), grow)
        gc_scr[g] = gcol
        lf_col = jnp.where(lane >= N_HEADS, gcol, 0.0)
        lf_row = jnp.where(sub >= N_HEADS, grow, 0.0)
        for c in range(n_chunks):
            span = slice(c * chunk, (c + 1) * chunk)
            gr_scr[g, c] = grow[:, span]
            bc_scr[g, span, :] = sum(jnp.dot(tril, part, preferred_element_type=F32)
                                     for part in _bf16_terms(lf_col[span, :]))
            br_scr[g, c] = sum(jnp.dot(part, triu, preferred_element_type=F32)
                               for part in _bf16_terms(lf_row[:, span]))

    def scan_chunk(c, carry):
        r0 = pl.multiple_of(c * chunk, chunk)
        rows = pl.ds(r0, chunk)
        chains = [(g, h) for g in range(group) for h in range(N_HEADS)]
        gates = {g: (gc_scr[g, rows, :], gr_scr[g, c], bc_scr[g, rows, :], br_scr[g, c]) for g in range(group)}
        st = []
        for g, h in chains:
            gc, gr, b_c, b_r = gates[g]
            bc = b_c[:, N_HEADS + h:N_HEADS + h + 1]
            br = b_r[N_HEADS + h:N_HEADS + h + 1, :]
            ic = gc[:, h:h + 1]
            ir = gr[h:h + 1, :]
            m = m_scr[g, h]
            dm = jnp.where(causal, bc - br + ir, NEG_INF)
            st.append(dict(bc=bc, br=br, ic=ic, ir=ir, m=m, dm=dm, inter=bc + m,
                           cols=slice(h * hd, (h + 1) * hd)))
        for s in st:
            s["m_t"] = jnp.maximum(s["inter"], jnp.max(s["dm"], axis=-1, keepdims=True))
        for s in st:
            s["w_d"] = jnp.exp(s["dm"] - s["m_t"])
            s["w_in"] = jnp.exp(s["inter"] - s["m_t"])
        for (g, h), s in zip(chains, st):
            s["q"] = q_scr[g, rows, s["cols"]]
            s["k"] = k_scr[g, rows, s["cols"]]
            s["v"] = v_scr[g, rows, s["cols"]]
            s["qb"] = s["q"].astype(BF16)
            s["kb"] = s["k"].astype(BF16)
            s["sqk"] = lax.dot_general(s["qb"], s["kb"], (((1,), (1,)), ((), ())),
                                       preferred_element_type=F32) * s["w_d"]
        for (g, h), s in zip(chains, st):
            s["cmat"] = c_scr[g, h]
            s["nvec"] = n_scr[g, h]
            s["num"] = (s["w_in"] * jnp.dot(s["qb"], s["cmat"].astype(BF16), preferred_element_type=F32)
                        + jnp.dot(s["sqk"].astype(BF16), s["v"].astype(BF16), preferred_element_type=F32))
        for s in st:
            s["den"] = (s["w_in"] * jnp.sum(s["q"] * s["nvec"], axis=-1, keepdims=True)
                        + jnp.sum(s["sqk"], axis=-1, keepdims=True))
        for (g, h), s in zip(chains, st):
            h_scr[g, rows, s["cols"]] = s["num"] / jnp.maximum(jnp.abs(s["den"]), jnp.exp(-s["m_t"]))
        for (g, h), s in zip(chains, st):
            b_l = s["bc"][chunk - 1:chunk, :]
            g_r = b_l - s["br"] + s["ir"]
            m_new = jnp.maximum(b_l + s["m"], jnp.max(g_r, axis=-1, keepdims=True))
            decay = jnp.exp(b_l + s["m"] - m_new)
            w_g = jnp.exp(b_l - s["bc"] + s["ic"] - m_new)
            c_scr[g, h] = decay * s["cmat"] + lax.dot_general(
                s["kb"], (w_g * s["v"]).astype(BF16), (((0,), (0,)), ((), ())), preferred_element_type=F32)
            n_scr[g, h] = decay * s["nvec"] + jnp.sum(w_g * s["k"], axis=0, keepdims=True)
            m_scr[g, h] = m_new
        return carry

    if n_chunks == 1:
        scan_chunk(0, 0)
    else:
        lax.fori_loop(0, n_chunks, scan_chunk, 0)

    for g in range(group):
        for h in range(N_HEADS):
            cols = slice(h * hd, (h + 1) * hd)
            hb = h_scr[g, :, cols] * ob_ref[g, :, cols]
            hb = _rms(hb) * mg_ref[:, cols] + skip_ref[:, cols] * xch_scr[g, :, cols]
            bo_ref[g, :, cols] = (hb * zb_ref[g, :, cols]).astype(BF16)

    @pl.when(step == pl.num_programs(1) - 1)
    def _state_out():
        c1_ref[...] = c_scr[...]
        n1_ref[...] = n_scr[...]
        m1_ref[...] = m_scr[...]


def _mlstm(rest, conv0, c0, n0, m0, w, t, chunk, n_valid, group):
    bn, s, _ = rest.shape
    hd = HEAD_DIM
    tok = lambda col: pl.BlockSpec((group, t, GROUP_W), lambda b, j, col=col: (b, j, col))
    per_b = lambda *shape: pl.BlockSpec((group,) + shape, lambda b, j: (b,) + (0,) * len(shape))
    full = lambda *shape: pl.BlockSpec(shape, lambda b, j: (0,) * len(shape))
    state_shapes = [(N_HEADS, hd, hd), (N_HEADS, 1, hd), (N_HEADS, 1, 1)]
    state_specs = [per_b(*shape) for shape in state_shapes]
    return pl.pallas_call(
        functools.partial(_mlstm_kernel, t=t, chunk=chunk, n_valid=n_valid, group=group),
        grid=(bn // group, s // t),
        in_specs=[tok(1), tok(2), tok(3), per_b(CONV_W - 1, GROUP_W)] + state_specs + [
            full(CONV_W, GROUP_W), full(1, GROUP_W),
            full(N_HEADS, hd, hd), full(N_HEADS, hd, hd), full(N_HEADS, hd, hd),
            full(3 * GROUP_W, 2 * N_HEADS), full(2 * N_HEADS, 3 * GROUP_W),
            full(1, 2 * N_HEADS), full(2 * N_HEADS, 1),
            full(1, GROUP_W), full(1, GROUP_W)],
        out_specs=[tok(0)] + state_specs,
        out_shape=[jax.ShapeDtypeStruct((bn, s, GROUP_W), BF16)]
                  + [jax.ShapeDtypeStruct((bn,) + shape, F32) for shape in state_shapes],
        scratch_shapes=[pltpu.VMEM((group, t + HALO, GROUP_W), F32)]
                       + [pltpu.VMEM((group, t, GROUP_W), F32)] * 5
                       + [pltpu.VMEM((group, t, 2 * N_HEADS), F32),
                          pltpu.VMEM((group, t // chunk, 2 * N_HEADS, chunk), F32)] * 2
                       + [pltpu.VMEM((group,) + shape, F32) for shape in state_shapes],
        compiler_params=_cparams(("parallel", "arbitrary")),
        name="mlstm",
    )(rest, rest, rest, conv0, c0, n0, m0,
      w["conv_w"], w["conv_b"], w["wq"], w["wk"], w["wv"], w["wif"], w["wif_t"], w["bif_row"], w["bif_col"],
      w["mlstm_g"], w["mlstm_skip"])


def _outproj_kernel(a_ref, bo_ref, x_ref, p_ref, woa_ref, wob_ref, pg_ref, wgate_ref, wple_ref, plg_ref, y_ref):
    tm = x_ref.shape[0]
    n_slabs = OUTPROJ_SLABS if tm % (F32_SUBLANES * OUTPROJ_SLABS) == 0 else 1
    slabs = [slice(c * tm // n_slabs, (c + 1) * tm // n_slabs) for c in range(n_slabs)]
    y = [jnp.dot(a_ref[r, :], woa_ref[...], preferred_element_type=F32)
         + jnp.dot(bo_ref[r, :], wob_ref[...], preferred_element_type=F32) for r in slabs]
    pe = [jnp.dot(p_ref[r, :].astype(BF16), wple_ref[...], preferred_element_type=F32) for r in slabs]
    x1 = [x_ref[r, :] + _rms(y_r) * pg_ref[...] for r, y_r in zip(slabs, y)]
    gate = [_sigmoid(jnp.dot(x.astype(BF16), wgate_ref[...], preferred_element_type=F32)) for x in x1]
    for r, x, g, e in zip(slabs, x1, gate, pe):
        y_ref[r, :] = x + _rms(g * e) * plg_ref[...]


def _outproj(a, bo, x, p, w, tm):
    m, d = x.shape
    row = lambda width: pl.BlockSpec((tm, width), lambda i: (i, 0))
    full = lambda arr: pl.BlockSpec(arr.shape, lambda i: (0,) * arr.ndim)
    consts = [w["wo_a"], w["wo_b"], w["post_g"], w["ple_gate"], w["ple_w"], w["ple_g"]]
    return pl.pallas_call(
        _outproj_kernel,
        grid=(m // tm,),
        in_specs=[row(GROUP_W), row(GROUP_W), row(d), row(p.shape[1])] + [full(c) for c in consts],
        out_specs=row(d),
        out_shape=jax.ShapeDtypeStruct((m, d), F32),
        compiler_params=_cparams(("parallel",)),
        name="outproj",
    )(a, bo, x, p, *consts)


def _layer_weights(l, norm_pre_g, norm_post_g, w_in, lam_q1, lam_k1, lam_q2, lam_k2, attn_norm_g, conv_w, conv_b,
                   w_q_b, w_k_b, w_v_b, w_if, b_if, mlstm_norm_g, mlstm_skip, w_out, ple_w, ple_gate_w, ple_norm_g):
    row = lambda v: v.reshape(1, -1)
    wo = w_out[l].astype(BF16)
    return {
        "pre_g": row(norm_pre_g[l]), "w_in": w_in[l].astype(BF16),
        "lam": jnp.stack([lam_q1[l], lam_k1[l], lam_q2[l], lam_k2[l]]).astype(F32),
        "attn_g": row(attn_norm_g[l]),
        "conv_w": conv_w[l], "conv_b": row(conv_b[l]),
        "wq": w_q_b[l].astype(BF16), "wk": w_k_b[l].astype(BF16), "wv": w_v_b[l].astype(BF16),
        "wif": w_if[l].astype(BF16), "wif_t": w_if[l].T.astype(BF16),
        "bif_row": row(b_if[l]), "bif_col": b_if[l].reshape(-1, 1),
        "mlstm_g": row(mlstm_norm_g[l]), "mlstm_skip": row(mlstm_skip[l]),
        "wo_a": wo[:GROUP_W], "wo_b": wo[GROUP_W:], "post_g": row(norm_post_g[l]),
        "ple_gate": ple_gate_w[l].astype(BF16), "ple_w": ple_w[l].astype(BF16), "ple_g": row(ple_norm_g[l]),
    }


def _prompt_layer(x, p, w, lam_init):
    b, s, d = x.shape
    x2 = x.reshape(b * s, d)
    qt, k, v, rest, kb, vt = _inproj(x2, w["pre_g"], w["w_in"], tm=INPROJ_ROWS, kv_block=FLASH_BLOCK)
    rest3 = rest.reshape(b, s, 4 * GROUP_W)
    a = _flash(qt, kb.reshape(b, s, GROUP_W), vt, rest3, w["lam"], w["attn_g"], lam_init, tq=FLASH_BLOCK)
    hd = HEAD_DIM
    bo, c1, n1, m1 = _mlstm(rest3, jnp.zeros((b, CONV_W - 1, GROUP_W), F32),
                            jnp.zeros((b, N_HEADS, hd, hd), F32), jnp.zeros((b, N_HEADS, 1, hd), F32),
                            jnp.zeros((b, N_HEADS, 1, 1), F32), w, t=PROMPT_ROWS, chunk=PROMPT_CHUNK,
                            n_valid=PROMPT_CHUNK, group=PROMPT_GROUP)
    y = _outproj(a.reshape(b * s, GROUP_W), bo.reshape(b * s, GROUP_W), x2, p.reshape(b * s, -1), w,
                 tm=OUTPROJ_ROWS)
    state = (k.reshape(b, s, N_HEADS, hd), v.reshape(b, s, N_HEADS, hd),
             rest3[:, s - (CONV_W - 1):, GROUP_W:2 * GROUP_W],
             c1, n1.reshape(b, N_HEADS, hd), m1.reshape(b, N_HEADS))
    return y.reshape(b, s, d), state


def _sample_layer(x, p, n_valid, cache_k, cache_v, page_table, conv0, c0, n0, m0, w, lam_init):
    db, r, d = x.shape
    hd = HEAD_DIM
    assert CONV_W - 1 <= n_valid <= r
    x2 = x.reshape(db * r, d)
    q, k, v, rest = _inproj(x2, w["pre_g"], w["w_in"], tm=db * r)
    rest3 = rest.reshape(db, r, 4 * GROUP_W)
    k3 = k.reshape(db, r, GROUP_W)
    v3 = v.reshape(db, r, GROUP_W)
    a = _paged(q.reshape(db, r, GROUP_W), k3, v3, rest3, cache_k, cache_v, page_table, w["lam"], w["attn_g"],
               lam_init, n_valid)
    bo, c1, n1, m1 = _mlstm(rest3, conv0, c0, n0.reshape(db, N_HEADS, 1, hd), m0.reshape(db, N_HEADS, 1, 1),
                            w, t=r, chunk=r, n_valid=n_valid, group=SAMPLE_GROUP)
    y = _outproj(a.reshape(db * r, GROUP_W), bo.reshape(db * r, GROUP_W), x2, p.reshape(db * r, -1), w, tm=db * r)
    state = (k3[:, :n_valid].reshape(db, n_valid, N_HEADS, hd), v3[:, :n_valid].reshape(db, n_valid, N_HEADS, hd),
             rest3[:, n_valid - (CONV_W - 1):n_valid, GROUP_W:2 * GROUP_W],
             c1, n1.reshape(db, N_HEADS, hd), m1.reshape(db, N_HEADS))
    return y.reshape(db, r, d), state


def kernel(x_prompt, x_sample, cache_k, cache_v, state_conv, state_C, state_n, state_m, page_table,
           p_prompt, p_sample, norm_pre_g, norm_post_g, w_in, lam_q1, lam_k1, lam_q2, lam_k2,
           attn_norm_g, conv_w, conv_b, w_q_b, w_k_b, w_v_b, w_if, b_if, mlstm_norm_g, mlstm_skip,
           w_out, ple_w, ple_gate_w, ple_norm_g):
    depth = w_in.shape[0]
    t_dec = x_sample.shape[1]
    pad_rows = lambda a: jnp.pad(a, ((0, 0), (0, SAMPLE_ROWS - t_dec), (0, 0)))
    n_pool, page = cache_k.shape[1], cache_k.shape[2]
    cache_k2 = cache_k.reshape(depth * n_pool, page * N_HEADS, HEAD_DIM)
    cache_v2 = cache_v.reshape(depth * n_pool, page * N_HEADS, HEAD_DIM)
    xp = x_prompt
    xs = pad_rows(x_sample)
    outs_p, outs_s = [], []
    for l in range(depth):
        w = _layer_weights(l, norm_pre_g, norm_post_g, w_in, lam_q1, lam_k1, lam_q2, lam_k2, attn_norm_g,
                           conv_w, conv_b, w_q_b, w_k_b, w_v_b, w_if, b_if, mlstm_norm_g, mlstm_skip,
                           w_out, ple_w, ple_gate_w, ple_norm_g)
        lam_init = 0.8 - 0.6 * math.exp(-0.3 * l)
        xp, st_p = _prompt_layer(xp, p_prompt[l], w, lam_init)
        outs_p.append(st_p)
        xs, st_s = _sample_layer(xs, pad_rows(p_sample[l]), t_dec, cache_k2, cache_v2, page_table + l * n_pool,
                                 state_conv[l], state_C[l], state_n[l], state_m[l], w, lam_init)
        outs_s.append(st_s)
    stack = lambda outs, i: jnp.stack([o[i] for o in outs])
    return ((xp, xs[:, :t_dec]) + tuple(stack(outs_p, i) for i in range(6))
            + tuple(stack(outs_s, i) for i in range(6)))
```
